```python
import math
import jax
import jax.numpy as jnp
from jax import lax
import numpy as np

D_MODEL = 2048
BATCH = 1
SEQ = 16384
DEPTH = 2

CTX_LEN = 256
GRID_W = 64
BLOCK = 128
HEAD_DIM = 128
AXIS_DIM = HEAD_DIM // 2
ROPE_THETA = 10000.0
ATTN_SCALE = HEAD_DIM ** -0.5
NEG_INF = -1e30
EPS = 1e-6

S5_GROUPS = 48
S5_GROUP_CH = 16
S5_STATE = 64
S5_WIDTH = S5_GROUPS * S5_GROUP_CH
GA_HEADS = 8
GA_KV = 2
WA_HEADS = 8
WA_KV = 2
WINDOW = 128
SG_GROUPS = 6
SG_GROUP_CH = 128
SG_WIDTH = SG_GROUPS * SG_GROUP_CH
CHUNK = 128

N_BRANCH = 4
IN_SPLITS = (S5_WIDTH, GA_HEADS * HEAD_DIM, GA_KV * HEAD_DIM, GA_KV * HEAD_DIM, WA_HEADS * HEAD_DIM, WA_KV * HEAD_DIM, WA_KV * HEAD_DIM, SG_WIDTH, SG_WIDTH, N_BRANCH * D_MODEL)
IN_WIDTH = sum(IN_SPLITS)

PEER_HEADS = 8
PEER_NKEYS = 128
PEER_EXPERTS = PEER_NKEYS * PEER_NKEYS
PEER_QDIM = 256
PEER_HALF = PEER_QDIM // 2
PEER_TOPK = 16
PEER_BLOCK = 128

DN_ALPHA = (2 * DEPTH) ** 0.25
DN_BETA = (8 * DEPTH) ** -0.25

kernel_name = 'hybrid_diffusion_trunk'


def layer_norm(x, g, b):
    xf = x.astype(jnp.float32)
    mu = jnp.mean(xf, -1, keepdims=True)
    var = jnp.mean(jnp.square(xf - mu), -1, keepdims=True)
    return ((xf - mu) * lax.rsqrt(var + EPS) * g.astype(jnp.float32) + b.astype(jnp.float32)).astype(x.dtype)


def rms_norm(x, g):
    xf = x.astype(jnp.float32)
    return (xf * lax.rsqrt(jnp.mean(xf * xf, -1, keepdims=True) + EPS) * g.astype(jnp.float32)).astype(x.dtype)


def modulation(cond, w_ada, b_ada):
    m = jax.nn.silu(cond) @ w_ada + b_ada
    return jnp.split(m, 6, axis=-1)


def split_heads(t, n_heads):
    return t.reshape(*t.shape[:-1], n_heads, HEAD_DIM)


def axial_rope_tables(rows):
    row = jnp.repeat(jnp.arange(rows, dtype=jnp.float32), GRID_W)
    col = jnp.tile(jnp.arange(GRID_W, dtype=jnp.float32), rows)
    inv = jnp.power(ROPE_THETA, -jnp.arange(0, AXIS_DIM, 2, dtype=jnp.float32) / AXIS_DIM)
    ang = jnp.stack([row[:, None] * inv, col[:, None] * inv], axis=1)
    return jnp.cos(ang), jnp.sin(ang)


def apply_rope(t, cos, sin):
    bn, ln, hn, _ = t.shape
    tf = t.astype(jnp.float32).reshape(bn, ln, hn, 2, 2, AXIS_DIM // 2)
    t1, t2 = tf[..., 0, :], tf[..., 1, :]
    c_, s_ = cos[None, :, None], sin[None, :, None]
    out = jnp.stack([t1 * c_ - t2 * s_, t1 * s_ + t2 * c_], axis=-2)
    return out.reshape(t.shape).astype(t.dtype)


def s5_discretize(lam_re, lam_im, log_dt):
    dt = jnp.exp(log_dt.astype(jnp.float32))[:, None]
    lr, li = lam_re.astype(jnp.float32), lam_im.astype(jnp.float32)
    mag = jnp.exp(lr * dt)
    ar, ai = mag * jnp.cos(li * dt), mag * jnp.sin(li * dt)
    nr, ni = ar - 1.0, ai
    den = lr * lr + li * li
    cr = (nr * lr + ni * li) / den
    ci = (ni * lr - nr * li) / den
    return ar, ai, cr, ci


def _ssm_combine(e1, e2):
    a1r, a1i, b1r, b1i = e1
    a2r, a2i, b2r, b2i = e2
    return (a2r * a1r - a2i * a1i, a2r * a1i + a2i * a1r,
            a2r * b1r - a2i * b1i + b2r, a2r * b1i + a2i * b1r + b2i)


def s5_scan(u, ar, ai, cr, ci, b_re, b_im, h0=None):
    bu_r = jnp.einsum('btgh,gph->btgp', u, b_re)
    bu_i = jnp.einsum('btgh,gph->btgp', u, b_im)
    xr = cr * bu_r - ci * bu_i
    xi = cr * bu_i + ci * bu_r
    if h0 is not None:
        h0r, h0i = h0
        xr = xr.at[:, 0].add(ar * h0r - ai * h0i)
        xi = xi.at[:, 0].add(ar * h0i + ai * h0r)
    a_r = jnp.broadcast_to(ar, xr.shape)
    a_i = jnp.broadcast_to(ai, xr.shape)
    _, _, hr, hi = lax.associative_scan(_ssm_combine, (a_r, a_i, xr, xi), axis=1)
    return hr, hi


def s5_readout(hr, hi, c_re, c_im):
    return (jnp.einsum('btgp,ghp->btgh', hr, c_re.astype(jnp.float32))
            - jnp.einsum('btgp,ghp->btgh', hi, c_im.astype(jnp.float32)))


def _orient(t, direction):
    return t[:, ::-1] if direction == 1 else t


def _s5_glu(y, w_glu, b_glu):
    g = jax.nn.gelu(y, approximate=False)
    return g * jax.nn.sigmoid(g @ w_glu.astype(jnp.float32) + b_glu.astype(jnp.float32))


def s5_mixer(u_lat, u_ctx, lam_re, lam_im, log_dt, b_re, b_im, c_re, c_im, d_skip, w_glu, b_glu, need_ctx):
    f32 = jnp.float32
    bn, ln, _ = u_lat.shape
    n_ctx = u_ctx.shape[1]
    ul = u_lat.astype(f32).reshape(bn, ln, S5_GROUPS, S5_GROUP_CH)
    uc = u_ctx.astype(f32).reshape(bn, n_ctx, S5_GROUPS, S5_GROUP_CH)
    d = d_skip.astype(f32).reshape(S5_GROUPS, S5_GROUP_CH)
    y_lat = ul * d
    y_ctx = uc * d
    for direction in range(2):
        ar, ai, cr, ci = s5_discretize(lam_re[direction], lam_im[direction], log_dt[direction])
        br, bi = b_re[direction].astype(f32), b_im[direction].astype(f32)
        hr_c, hi_c = s5_scan(_orient(uc, direction), ar, ai, cr, ci, br, bi)
        hr_l, hi_l = s5_scan(_orient(ul, direction), ar, ai, cr, ci, br, bi, (hr_c[:, -1], hi_c[:, -1]))
        y_lat = y_lat + _orient(s5_readout(hr_l, hi_l, c_re[direction], c_im[direction]), direction)
        if need_ctx:
            y_ctx = y_ctx + _orient(s5_readout(hr_c, hi_c, c_re[direction], c_im[direction]), direction)
    out_lat = _s5_glu(y_lat.reshape(bn, ln, S5_WIDTH), w_glu, b_glu).astype(u_lat.dtype)
    out_ctx = _s5_glu(y_ctx.reshape(bn, n_ctx, S5_WIDTH), w_glu, b_glu).astype(u_ctx.dtype) if need_ctx else None
    return out_lat, out_ctx


def ctx_attn(q, k, v, sink=None):
    bn, n_ctx, hn, _ = q.shape
    kvn = k.shape[2]
    gn = hn // kvn
    qg = q.reshape(bn, n_ctx, kvn, gn, HEAD_DIM)
    s = jnp.einsum('bqkgd,bskd->bkgqs', qg, k).astype(jnp.float32) * ATTN_SCALE
    if sink is not None:
        col = jnp.broadcast_to(sink.astype(jnp.float32).reshape(kvn, gn)[None, :, :, None, None], (bn, kvn, gn, n_ctx, 1))
        p = jax.nn.softmax(jnp.concatenate([s, col], -1), axis=-1)[..., :-1]
    else:
        p = jax.nn.softmax(s, axis=-1)
    out = jnp.einsum('bkgqs,bskd->bqkgd', p.astype(v.dtype), v)
    return out.reshape(bn, n_ctx, hn * HEAD_DIM)


def global_attn_latent(q, k, v, k_ctx, v_ctx):
    bn, ln = q.shape[:2]
    nb = ln // BLOCK
    gn = GA_HEADS // GA_KV
    keys = jnp.concatenate([k_ctx, k], axis=1)
    vals = jnp.concatenate([v_ctx, v], axis=1)
    qb = jnp.moveaxis(q.reshape(bn, nb, BLOCK, GA_KV, gn, HEAD_DIM), 1, 0)

    def one_block(qblk):
        s = jnp.einsum('bqkgd,bskd->bkgqs', qblk, keys).astype(jnp.float32) * ATTN_SCALE
        p = jax.nn.softmax(s, axis=-1).astype(vals.dtype)
        return jnp.einsum('bkgqs,bskd->bqkgd', p, vals)

    out = lax.map(one_block, qb)
    return jnp.moveaxis(out, 0, 1).reshape(bn, ln, GA_HEADS * HEAD_DIM)


def window_attn_latent(q, k, v, k_ctx, v_ctx, sink):
    bn, ln = q.shape[:2]
    nb = ln // BLOCK
    gn = WA_HEADS // WA_KV
    n_ctx = k_ctx.shape[1]
    pad = ((0, 0), (BLOCK, BLOCK), (0, 0), (0, 0))
    kp, vp = jnp.pad(k, pad), jnp.pad(v, pad)
    qb = jnp.moveaxis(q.reshape(bn, nb, BLOCK, WA_KV, gn, HEAD_DIM), 1, 0)
    rel = jnp.arange(BLOCK)[:, None] - (jnp.arange(3 * BLOCK)[None, :] - BLOCK)
    in_band = jnp.abs(rel) <= WINDOW
    sink_col = jnp.broadcast_to(sink.astype(jnp.float32).reshape(WA_KV, gn)[None, :, :, None, None], (bn, WA_KV, gn, BLOCK, 1))

    def one_block(args):
        i, qblk = args
        start = i * BLOCK
        kw = lax.dynamic_slice_in_dim(kp, start, 3 * BLOCK, axis=1)
        vw = lax.dynamic_slice_in_dim(vp, start, 3 * BLOCK, axis=1)
        key_pos = start - BLOCK + jnp.arange(3 * BLOCK)
        valid = in_band & ((key_pos >= 0) & (key_pos < ln))[None, :]
        s_win = jnp.einsum('bqkgd,bskd->bkgqs', qblk, kw).astype(jnp.float32) * ATTN_SCALE
        s_win = jnp.where(valid, s_win, NEG_INF)
        s_ctx = jnp.einsum('bqkgd,bskd->bkgqs', qblk, k_ctx).astype(jnp.float32) * ATTN_SCALE
        p = jax.nn.softmax(jnp.concatenate([s_ctx, s_win, sink_col], -1), axis=-1).astype(v.dtype)
        return (jnp.einsum('bkgqs,bskd->bqkgd', p[..., :n_ctx], v_ctx)
                + jnp.einsum('bkgqs,bskd->bqkgd', p[..., n_ctx:n_ctx + 3 * BLOCK], vw))

    out = lax.map(one_block, (jnp.arange(nb), qb))
    return jnp.moveaxis(out, 0, 1).reshape(bn, ln, WA_HEADS * HEAD_DIM)


def spatial_gate(u, v, g_v, b_v, w_sp, b_sp):
    bn, tn, _ = v.shape
    vc = layer_norm(v, g_v, b_v).reshape(bn, tn // CHUNK, CHUNK, SG_GROUPS, SG_GROUP_CH)
    s = jnp.einsum('gpq,bnqgc->bnpgc', w_sp, vc) + b_sp.T[:, :, None]
    return u * s.reshape(bn, tn, SG_WIDTH)


def gated_merge(gate_logits, branches, w_brs, w_out):
    gl = gate_logits.reshape(*gate_logits.shape[:-1], N_BRANCH, D_MODEL)
    merged = jax.nn.sigmoid(gl[..., 0, :]) * (branches[0] @ w_brs[0])
    for i in range(1, N_BRANCH):
        merged = merged + jax.nn.sigmoid(gl[..., i, :]) * (branches[i] @ w_brs[i])
    return merged @ w_out


def token_mixers(h, hc, cos, sin, need_ctx, w_in, b_in, lam_re, lam_im, log_dt, sb_re, sb_im, sc_re, sc_im, s5_d,
                 w_glu, b_glu, qn_gain, kn_gain, sink, sg_ln_g, sg_ln_b, w_sp, b_sp, w_brs, w_out):
    cuts = [int(v) for v in np.cumsum(IN_SPLITS)[:-1]]
    a_l, qg_l, kg_l, vg_l, qw_l, kw_l, vw_l, du_l, dv_l, gate_l = jnp.split(h @ w_in + b_in, cuts, axis=-1)
    a_c, qg_c, kg_c, vg_c, qw_c, kw_c, vw_c, du_c, dv_c, gate_c = jnp.split(hc @ w_in + b_in, cuts, axis=-1)
    y_a, y_a_c = s5_mixer(a_l, a_c, lam_re, lam_im, log_dt, sb_re, sb_im, sc_re, sc_im, s5_d, w_glu, b_glu, need_ctx)
    qg = apply_rope(rms_norm(split_heads(qg_l, GA_HEADS), qn_gain), cos, sin)
    kg = apply_rope(rms_norm(split_heads(kg_l, GA_KV), kn_gain), cos, sin)
    vg = split_heads(vg_l, GA_KV)
    kgc = rms_norm(split_heads(kg_c, GA_KV), kn_gain)
    vgc = split_heads(vg_c, GA_KV)
    y_b = global_attn_latent(qg, kg, vg, kgc, vgc)
    qw = apply_rope(split_heads(qw_l, WA_HEADS), cos, sin)
    kw = apply_rope(split_heads(kw_l, WA_KV), cos, sin)
    vw = split_heads(vw_l, WA_KV)
    kwc, vwc = split_heads(kw_c, WA_KV), split_heads(vw_c, WA_KV)
    y_c = window_attn_latent(qw, kw, vw, kwc, vwc, sink)
    y_d = spatial_gate(du_l, dv_l, sg_ln_g, sg_ln_b, w_sp, b_sp)
    out_lat = gated_merge(gate_l, (y_a, y_b, y_c, y_d), w_brs, w_out)
    if not need_ctx:
        return out_lat, None
    y_b_c = ctx_attn(rms_norm(split_heads(qg_c, GA_HEADS), qn_gain), kgc, vgc)
    y_c_c = ctx_attn(split_heads(qw_c, WA_HEADS), kwc, vwc, sink)
    y_d_c = spatial_gate(du_c, dv_c, sg_ln_g, sg_ln_b, w_sp, b_sp)
    out_ctx = gated_merge(gate_c, (y_a_c, y_b_c, y_c_c, y_d_c), w_brs, w_out)
    return out_lat, out_ctx


def peer(h, w_pq, sub_keys, peer_u, peer_v):
    bn, tn, dn = h.shape
    tok = h.reshape(bn * tn, dn)
    q = (tok @ w_pq).reshape(-1, PEER_HEADS, 2, PEER_HALF)
    s = jnp.einsum('thsd,hsnd->thsn', q, sub_keys).astype(jnp.float32)
    top_s, top_i = lax.top_k(s, PEER_TOPK)
    cand = top_s[:, :, 0, :, None] + top_s[:, :, 1, None, :]
    best_s, best_f = lax.top_k(cand.reshape(*cand.shape[:2], PEER_TOPK * PEER_TOPK), PEER_TOPK)
    i1 = jnp.take_along_axis(top_i[:, :, 0], best_f // PEER_TOPK, axis=-1)
    i2 = jnp.take_along_axis(top_i[:, :, 1], best_f % PEER_TOPK, axis=-1)
    expert = i1 * PEER_NKEYS + i2
    gate = jax.nn.softmax(best_s, axis=-1).astype(h.dtype)
    nblk = tok.shape[0] // PEER_BLOCK

    def one_block(args):
        xb, eb, gb = args
        act = jax.nn.gelu(jnp.einsum('td,thkd->thk', xb, peer_u[eb]), approximate=False)
        return jnp.einsum('thk,thkd->td', gb * act, peer_v[eb])

    out = lax.map(one_block, (tok.reshape(nblk, PEER_BLOCK, dn),
                              expert.reshape(nblk, PEER_BLOCK, PEER_HEADS, PEER_TOPK),
                              gate.reshape(nblk, PEER_BLOCK, PEER_HEADS, PEER_TOPK)))
    return out.reshape(bn, tn, dn)


def setup_inputs(seed: int = 0) -> dict:
    key = jax.random.key(seed)
    ks = iter(jax.random.split(key, 64))
    f32 = jnp.float32
    D = D_MODEL
    L_ = DEPTH

    def nrm(shape, scale):
        return jax.random.normal(next(ks), shape, f32) * scale

    G, P, H = S5_GROUPS, S5_STATE, S5_GROUP_CH
    return {
        'x': nrm((BATCH, SEQ, D), 1.0),
        'c': nrm((BATCH, D), 1.0),
        'ctx': nrm((BATCH, CTX_LEN, D), 1.0),
        'c_ctx': nrm((D,), 1.0),
        'w_ada': nrm((L_, D, 6 * D), D ** -0.5),
        'b_ada': nrm((L_, 6 * D), 0.01),
        'w_in': nrm((L_, D, IN_WIDTH), D ** -0.5),
        'b_in': nrm((L_, IN_WIDTH), 0.01),
        's5_lam_re': -0.5 + nrm((L_, 2, G, P), 0.01),
        's5_lam_im': math.pi * jnp.arange(P, dtype=f32) + nrm((L_, 2, G, P), 0.01),
        's5_log_dt': jax.random.uniform(next(ks), (L_, 2, G), f32, math.log(1e-3), math.log(1e-1)),
        's5_b_re': nrm((L_, 2, G, P, H), (2 * H) ** -0.5),
        's5_b_im': nrm((L_, 2, G, P, H), (2 * H) ** -0.5),
        's5_c_re': nrm((L_, 2, G, H, P), 0.5),
        's5_c_im': nrm((L_, 2, G, H, P), 0.5),
        's5_d': nrm((L_, S5_WIDTH), 1.0),
        'w_glu': nrm((L_, S5_WIDTH, S5_WIDTH), S5_WIDTH ** -0.5),
        'b_glu': nrm((L_, S5_WIDTH), 0.01),
        'qn_gain': 1.0 + nrm((L_, HEAD_DIM), 0.02),
        'kn_gain': 1.0 + nrm((L_, HEAD_DIM), 0.02),
        'sink': nrm((L_, WA_HEADS), 0.5),
        'sg_ln_g': 1.0 + nrm((L_, SG_WIDTH), 0.02),
        'sg_ln_b': nrm((L_, SG_WIDTH), 0.01),
        'w_sp': nrm((L_, SG_GROUPS, CHUNK, CHUNK), CHUNK ** -0.5),
        'b_sp': 1.0 + nrm((L_, SG_GROUPS, CHUNK), 0.02),
        'w_br_a': nrm((L_, S5_WIDTH, D), S5_WIDTH ** -0.5),
        'w_br_b': nrm((L_, GA_HEADS * HEAD_DIM, D), (GA_HEADS * HEAD_DIM) ** -0.5),
        'w_br_c': nrm((L_, WA_HEADS * HEAD_DIM, D), (WA_HEADS * HEAD_DIM) ** -0.5),
        'w_br_d': nrm((L_, SG_WIDTH, D), SG_WIDTH ** -0.5),
        'w_out': nrm((L_, D, D), DN_BETA * D ** -0.5),
        'ln1_g': 1.0 + nrm((L_, D), 0.02),
        'ln1_b': nrm((L_, D), 0.01),
        'ln2_g': 1.0 + nrm((L_, D), 0.02),
        'ln2_b': nrm((L_, D), 0.01),
        'w_pq': nrm((L_, D, PEER_HEADS * PEER_QDIM), D ** -0.5),
        'peer_keys': nrm((L_, PEER_HEADS, 2, PEER_NKEYS, PEER_HALF), PEER_HALF ** -0.5),
        'peer_u': nrm((L_, PEER_EXPERTS, D), D ** -0.5),
        'peer_v': nrm((L_, PEER_EXPERTS, D), DN_BETA),
    }


def reference(x, c, ctx, c_ctx, w_ada, b_ada, w_in, b_in, s5_lam_re, s5_lam_im, s5_log_dt, s5_b_re, s5_b_im,
              s5_c_re, s5_c_im, s5_d, w_glu, b_glu, qn_gain, kn_gain, sink, sg_ln_g, sg_ln_b, w_sp, b_sp,
              w_br_a, w_br_b, w_br_c, w_br_d, w_out, ln1_g, ln1_b, ln2_g, ln2_b, w_pq, peer_keys, peer_u, peer_v):
    n_lat = x.shape[1]
    rows = n_lat // GRID_W
    cos, sin = axial_rope_tables(rows)
    for l in range(DEPTH):
        need_ctx = l < DEPTH - 1
        sh1, sc1, g1, sh2, sc2, g2 = modulation(c[:, None, :], w_ada[l], b_ada[l])
        sh1c, sc1c, g1c, sh2c, sc2c, g2c = modulation(c_ctx, w_ada[l], b_ada[l])
        h = x * (1.0 + sc1) + sh1
        hc = ctx * (1.0 + sc1c) + sh1c
        mix, mix_c = token_mixers(
            h, hc, cos, sin, need_ctx, w_in[l], b_in[l], s5_lam_re[l], s5_lam_im[l], s5_log_dt[l],
            s5_b_re[l], s5_b_im[l], s5_c_re[l], s5_c_im[l], s5_d[l], w_glu[l], b_glu[l], qn_gain[l], kn_gain[l],
            sink[l], sg_ln_g[l], sg_ln_b[l], w_sp[l], b_sp[l], (w_br_a[l], w_br_b[l], w_br_c[l], w_br_d[l]), w_out[l])
        x = layer_norm(DN_ALPHA * x + g1 * mix, ln1_g[l], ln1_b[l])
        ffn = peer(x * (1.0 + sc2) + sh2, w_pq[l], peer_keys[l], peer_u[l], peer_v[l])
        x = layer_norm(DN_ALPHA * x + g2 * ffn, ln2_g[l], ln2_b[l])
        if need_ctx:
            ctx = layer_norm(DN_ALPHA * ctx + g1c * mix_c, ln1_g[l], ln1_b[l])
            ffn_c = peer(ctx * (1.0 + sc2c) + sh2c, w_pq[l], peer_keys[l], peer_u[l], peer_v[l])
            ctx = layer_norm(DN_ALPHA * ctx + g2c * ffn_c, ln2_g[l], ln2_b[l])
    return x
```

```python
import functools
import math

import jax
import jax.numpy as jnp
import numpy as np
from jax import lax
from jax.experimental import pallas as pl
from jax.experimental.pallas import tpu as pltpu

F32, BF16 = jnp.float32, jnp.bfloat16

D_MODEL = 2048
GRID_W = 64
BLOCK = 128
HEAD_DIM = 128
AXIS_DIM = HEAD_DIM // 2
ROPE_THETA = 10000.0
ATTN_SCALE = HEAD_DIM ** -0.5
NEG_INF = -1e30
EPS = 1e-6
S5_GROUPS, S5_GROUP_CH, S5_STATE = 48, 16, 64
S5_WIDTH = S5_GROUPS * S5_GROUP_CH
S5_CH = S5_GROUPS * S5_STATE
S5_GB = 16
S5_NB = S5_GROUPS // S5_GB
S5_KB = S5_GB * S5_GROUP_CH
S5_PB = S5_GB * S5_STATE
GA_HEADS, GA_KV = 8, 2
WA_HEADS, WA_KV = 8, 2
KV_GROUP = GA_HEADS // GA_KV
SG_GROUPS, SG_GROUP_CH = 6, 128
SG_WIDTH = SG_GROUPS * SG_GROUP_CH
CHUNK = 128
N_BRANCH = 4
PEER_HEADS, PEER_NKEYS, PEER_QDIM, PEER_TOPK = 8, 128, 256, 16
PEER_HALF = PEER_QDIM // 2
PEER_EXPERTS = PEER_NKEYS * PEER_NKEYS

COL_A, COL_KG, COL_QG, COL_QW, COL_KW, COL_VG, COL_VW, COL_DU, COL_DV = 0, 768, 1024, 2048, 3072, 3328, 3584, 3840, 4608
SMALL_W = 5376
GATE_W = N_BRANCH * D_MODEL

VMEM_CAP_MB = 56


def _cp(ndims, vmem_mb=32):
    return pltpu.CompilerParams(dimension_semantics=("arbitrary",) * ndims,
                                vmem_limit_bytes=min(vmem_mb, VMEM_CAP_MB) << 20)


def _gelu(y):
    return 0.5 * y * (1.0 + lax.erf(y * (2.0 ** -0.5)))


def _ln(y, g, b):
    mu = jnp.mean(y, -1, keepdims=True)
    yc = y - mu
    var = jnp.mean(yc * yc, -1, keepdims=True)
    return yc * lax.rsqrt(var + EPS) * g + b


_NT = (((1,), (1,)), ((), ()))


def _ada_body(c_ref, w_ref, b_ref, o_ref):
    cnd = c_ref[...]
    s = (cnd * jax.nn.sigmoid(cnd)).astype(BF16)
    o_ref[...] = jnp.dot(s, w_ref[...].astype(BF16), preferred_element_type=F32) + b_ref[...]


def _ada(cond8, w_ada, b_ada, l):
    depth, d, n = w_ada.shape
    tn = 1024
    return pl.pallas_call(
        _ada_body, grid=(n // tn,),
        in_specs=[pl.BlockSpec((8, d), lambda j: (0, 0)),
                  pl.BlockSpec((None, d, tn), lambda j: (l, 0, j)),
                  pl.BlockSpec((None, 1, tn), lambda j: (l, 0, j))],
        out_specs=pl.BlockSpec((8, tn), lambda j: (0, j)),
        out_shape=jax.ShapeDtypeStruct((8, n), F32),
        compiler_params=_cp(1, 32), name="ada_mod")(cond8, w_ada, b_ada.reshape(depth, 1, n))


def _mod_body(x_ref, sc_ref, sh_ref, o_ref):
    o_ref[...] = (x_ref[...] * (1.0 + sc_ref[...]) + sh_ref[...]).astype(o_ref.dtype)


def _modulate(x, sc, sh):
    t, d = x.shape
    tm = min(512, t)
    return pl.pallas_call(
        _mod_body, grid=(t // tm,),
        in_specs=[pl.BlockSpec((tm, d), lambda i: (i, 0)),
                  pl.BlockSpec((1, d), lambda i: (0, 0)),
                  pl.BlockSpec((1, d), lambda i: (0, 0))],
        out_specs=pl.BlockSpec((tm, d), lambda i: (i, 0)),
        out_shape=jax.ShapeDtypeStruct((t, d), BF16),
        compiler_params=_cp(1, 24), name="modulate")(x, sc, sh)


def _mm_body(a_ref, w_ref, b_ref, o_ref):
    acc = jnp.dot(a_ref[...], w_ref[...], preferred_element_type=F32)
    o_ref[...] = (acc + b_ref[...]).astype(o_ref.dtype)


def _mm_nobias_body(a_ref, w_ref, o_ref):
    o_ref[...] = jnp.dot(a_ref[...], w_ref[...], preferred_element_type=F32).astype(o_ref.dtype)


def _matmul(a, w, b, *, tm, tn, out_dtype, name):
    m, k = a.shape
    n = w.shape[1]
    tm, tn = min(tm, m), min(tn, n)
    in_specs = [pl.BlockSpec((tm, k), lambda j, i: (i, 0)),
                pl.BlockSpec((k, tn), lambda j, i: (0, j))]
    args = [a, w]
    body = _mm_nobias_body
    if b is not None:
        in_specs.append(pl.BlockSpec((1, tn), lambda j, i: (0, j)))
        args.append(b)
        body = _mm_body
    osz = jnp.dtype(out_dtype).itemsize
    vmem = 2 * (tm * k * 2 + k * tn * 2 + tm * tn * osz) + tm * tn * 4
    return pl.pallas_call(
        body, grid=(n // tn, m // tm), in_specs=in_specs,
        out_specs=pl.BlockSpec((tm, tn), lambda j, i: (i, j)),
        out_shape=jax.ShapeDtypeStruct((m, n), out_dtype),
        compiler_params=_cp(2, (vmem >> 20) + 8), name=name)(*args)


def _s5_body(u_ref, bre_ref, bim_ref, cre_ref, cim_ref, a_ref, h0_ref, y_ref, hT_ref, xr, xi, carry, *, tc, rev):
    @pl.when(pl.program_id(0) == 0)
    def _():
        carry[...] = h0_ref[...]

    for j in range(S5_NB):
        uj = u_ref[:, j * S5_KB:(j + 1) * S5_KB]
        xr[:, j * S5_PB:(j + 1) * S5_PB] = jnp.dot(uj, bre_ref[j], preferred_element_type=F32)
        xi[:, j * S5_PB:(j + 1) * S5_PB] = jnp.dot(uj, bim_ref[j], preferred_element_type=F32)

    for j in range(S5_NB):
        cs = slice(j * S5_PB, (j + 1) * S5_PB)
        ar, ai = a_ref[0:1, cs], a_ref[1:2, cs]

        def step(k, c, cs=cs, ar=ar, ai=ai):
            hr, hi = c
            t = (tc - 1 - k) if rev else k
            nr = ar * hr - ai * hi + xr[pl.ds(t, 1), cs]
            ni = ar * hi + ai * hr + xi[pl.ds(t, 1), cs]
            xr[pl.ds(t, 1), cs] = nr
            xi[pl.ds(t, 1), cs] = ni
            return nr, ni

        hr, hi = lax.fori_loop(0, tc, step, (carry[0:1, cs], carry[1:2, cs]), unroll=8)
        carry[0:1, cs] = hr
        carry[1:2, cs] = hi

    for j in range(S5_NB):
        cs = slice(j * S5_PB, (j + 1) * S5_PB)
        y_ref[:, j * S5_KB:(j + 1) * S5_KB] = (
            jnp.dot(xr[:, cs].astype(BF16), cre_ref[j], preferred_element_type=F32)
            + jnp.dot(xi[:, cs].astype(BF16), cim_ref[j], preferred_element_type=F32))
    hT_ref[...] = carry[...]


def _s5_dir(proj, mats, h0, rev):
    t = proj.shape[0]
    tc = min(256, t)
    nt = t // tc
    bre, bim, cre, cim, a = mats
    tmap = (lambda i: (nt - 1 - i, 0)) if rev else (lambda i: (i, 0))
    full3 = lambda i: (0, 0, 0)
    full2 = lambda i: (0, 0)
    return pl.pallas_call(
        functools.partial(_s5_body, tc=tc, rev=rev), grid=(nt,),
        in_specs=[pl.BlockSpec((tc, S5_WIDTH), tmap),
                  pl.BlockSpec((S5_NB, S5_KB, S5_PB), full3), pl.BlockSpec((S5_NB, S5_KB, S5_PB), full3),
                  pl.BlockSpec((S5_NB, S5_PB, S5_KB), full3), pl.BlockSpec((S5_NB, S5_PB, S5_KB), full3),
                  pl.BlockSpec((2, S5_CH), full2), pl.BlockSpec((2, S5_CH), full2)],
        out_specs=[pl.BlockSpec((tc, S5_WIDTH), tmap), pl.BlockSpec((2, S5_CH), full2)],
        out_shape=[jax.ShapeDtypeStruct((t, S5_WIDTH), F32), jax.ShapeDtypeStruct((2, S5_CH), F32)],
        scratch_shapes=[pltpu.VMEM((tc, S5_CH), F32), pltpu.VMEM((tc, S5_CH), F32), pltpu.VMEM((2, S5_CH), F32)],
        compiler_params=_cp(1, 40), name="s5_scan_bwd" if rev else "s5_scan_fwd")(proj, bre, bim, cre, cim, a, h0)


def _s5_out_body(u_ref, yf_ref, yb_ref, d_ref, wg_ref, bg_ref, o_ref):
    y = u_ref[...].astype(F32) * d_ref[...] + yf_ref[...] + yb_ref[...]
    g = _gelu(y)
    z = jnp.dot(g.astype(BF16), wg_ref[...], preferred_element_type=F32) + bg_ref[...]
    o_ref[...] = (g * jax.nn.sigmoid(z)).astype(o_ref.dtype)


def _s5_out(proj, yf, yb, d, wg, bg):
    t = proj.shape[0]
    tm = min(512, t)
    row = lambda i: (i, 0)
    full = lambda i: (0, 0)
    return pl.pallas_call(
        _s5_out_body, grid=(t // tm,),
        in_specs=[pl.BlockSpec((tm, S5_WIDTH), row), pl.BlockSpec((tm, S5_WIDTH), row), pl.BlockSpec((tm, S5_WIDTH), row),
                  pl.BlockSpec((1, S5_WIDTH), full), pl.BlockSpec((S5_WIDTH, S5_WIDTH), full), pl.BlockSpec((1, S5_WIDTH), full)],
        out_specs=pl.BlockSpec((tm, S5_WIDTH), row),
        out_shape=jax.ShapeDtypeStruct((t, S5_WIDTH), BF16),
        compiler_params=_cp(1, 24), name="s5_glu")(proj, yf, yb, d, wg, bg)


def _s5_mats(lam_re, lam_im, log_dt, b_re, b_im, c_re, c_im):
    dt = jnp.exp(log_dt)[:, None]
    mag = jnp.exp(lam_re * dt)
    ar, ai = mag * jnp.cos(lam_im * dt), mag * jnp.sin(lam_im * dt)
    nr, ni = ar - 1.0, ai
    den = lam_re * lam_re + lam_im * lam_im
    cr = (nr * lam_re + ni * lam_im) / den
    ci = (ni * lam_re - nr * lam_im) / den
    fre = cr[:, :, None] * b_re - ci[:, :, None] * b_im
    fim = cr[:, :, None] * b_im + ci[:, :, None] * b_re
    eye = jnp.eye(S5_GB, dtype=F32)

    def blk_in(f):
        f = f.reshape(S5_NB, S5_GB, S5_STATE, S5_GROUP_CH)
        m = jnp.einsum('jgph,gk->jghkp', f, eye)
        return m.reshape(S5_NB, S5_KB, S5_PB).astype(BF16)

    def blk_out(c):
        c = c.reshape(S5_NB, S5_GB, S5_GROUP_CH, S5_STATE)
        m = jnp.einsum('jghp,gk->jgpkh', c, eye)
        return m.reshape(S5_NB, S5_PB, S5_KB).astype(BF16)

    is_re = lax.broadcasted_iota(jnp.int32, (2, S5_CH), 0) == 0
    a = jnp.where(is_re, ar.reshape(1, -1), ai.reshape(1, -1))
    return blk_in(fre), blk_in(fim), blk_out(c_re), blk_out(-c_im), a


def _prep_body(qg_ref, kg_ref, qw_ref, kw_ref, cos_ref, sin_ref, qn_ref, kn_ref, oqg, okg, oqw, okw):
    cosf, sinf = cos_ref[...], sin_ref[...]
    lane = lax.broadcasted_iota(jnp.int32, cosf.shape, 1)
    low = (lane % AXIS_DIM) < (AXIS_DIM // 2)

    def rope(t):
        partner = jnp.where(low, pltpu.roll(t, HEAD_DIM - AXIS_DIM // 2, 1), pltpu.roll(t, AXIS_DIM // 2, 1))
        return t * cosf + partner * sinf

    def rms(t, g):
        return t * lax.rsqrt(jnp.mean(t * t, -1, keepdims=True) + EPS) * g

    for h in range(GA_HEADS):
        cs = slice(h * HEAD_DIM, (h + 1) * HEAD_DIM)
        oqg[:, cs] = (rope(rms(qg_ref[:, cs].astype(F32), qn_ref[...])) * ATTN_SCALE).astype(BF16)
        oqw[:, cs] = (rope(qw_ref[:, cs].astype(F32)) * ATTN_SCALE).astype(BF16)
    for h in range(GA_KV):
        cs = slice(h * HEAD_DIM, (h + 1) * HEAD_DIM)
        okg[:, cs] = rope(rms(kg_ref[:, cs].astype(F32), kn_ref[...])).astype(BF16)
        okw[:, cs] = rope(kw_ref[:, cs].astype(F32)).astype(BF16)


def _prep(proj, cosf, sinf, qn, kn):
    t = proj.shape[0]
    tr = min(256, t)
    qw_, kw_ = GA_HEADS * HEAD_DIM, GA_KV * HEAD_DIM
    row = lambda i: (i, 0)
    full = lambda i: (0, 0)
    return pl.pallas_call(
        _prep_body, grid=(t // tr,),
        in_specs=[pl.BlockSpec((tr, qw_), lambda i: (i, COL_QG // qw_)),
                  pl.BlockSpec((tr, kw_), lambda i: (i, COL_KG // kw_)),
                  pl.BlockSpec((tr, qw_), lambda i: (i, COL_QW // qw_)),
                  pl.BlockSpec((tr, kw_), lambda i: (i, COL_KW // kw_)),
                  pl.BlockSpec((tr, HEAD_DIM), row), pl.BlockSpec((tr, HEAD_DIM), row),
                  pl.BlockSpec((1, HEAD_DIM), full), pl.BlockSpec((1, HEAD_DIM), full)],
        out_specs=[pl.BlockSpec((tr, qw_), row), pl.BlockSpec((tr, kw_), row),
                   pl.BlockSpec((tr, qw_), row), pl.BlockSpec((tr, kw_), row)],
        out_shape=[jax.ShapeDtypeStruct((t, qw_), BF16), jax.ShapeDtypeStruct((t, kw_), BF16),
                   jax.ShapeDtypeStruct((t, qw_), BF16), jax.ShapeDtypeStruct((t, kw_), BF16)],
        compiler_params=_cp(1, 24), name="qk_prep")(proj, proj, proj, proj, cosf, sinf, qn, kn)


def _rope_tables(n_lat):
    rows = n_lat // GRID_W
    row = jnp.repeat(jnp.arange(rows, dtype=F32), GRID_W)
    col = jnp.tile(jnp.arange(GRID_W, dtype=F32), rows)
    inv = jnp.power(ROPE_THETA, -jnp.arange(0, AXIS_DIM, 2, dtype=F32) / AXIS_DIM)
    ang_r, ang_c = row[:, None] * inv, col[:, None] * inv
    cr, sr, cc, sc = jnp.cos(ang_r), jnp.sin(ang_r), jnp.cos(ang_c), jnp.sin(ang_c)
    return jnp.concatenate([cr, cr, cc, cc], -1), jnp.concatenate([-sr, sr, -sc, sc], -1)


def _flash_body(q_ref, kc_ref, vc_ref, k_ref, v_ref, o_ref, qs, m_s, l_s, acc, *, tq, nk):
    kj = pl.program_id(2)

    def update(k, v):
        s = lax.dot_general(qs[...], k, _NT, preferred_element_type=F32)
        m_prev = m_s[...]
        m_new = jnp.maximum(m_prev, jnp.max(s, -1, keepdims=True))
        alpha = jnp.exp(m_prev - m_new)
        p = jnp.exp(s - m_new)
        l_s[...] = alpha * l_s[...] + jnp.sum(p, -1, keepdims=True)
        acc[...] = alpha * acc[...] + jnp.dot(p.astype(BF16), v, preferred_element_type=F32)
        m_s[...] = m_new

    @pl.when(kj == 0)
    def _():
        for g in range(KV_GROUP):
            qs[g * tq:(g + 1) * tq, :] = q_ref[:, g * HEAD_DIM:(g + 1) * HEAD_DIM]
        m_s[...] = jnp.full(m_s.shape, -jnp.inf, F32)
        l_s[...] = jnp.zeros(l_s.shape, F32)
        acc[...] = jnp.zeros(acc.shape, F32)
        update(kc_ref[...], vc_ref[...])

    @pl.when(kj > 0)
    def _():
        update(k_ref[...], v_ref[...])

    @pl.when(kj == nk)
    def _():
        out = acc[...] / l_s[...]
        for g in range(KV_GROUP):
            o_ref[:, g * HEAD_DIM:(g + 1) * HEAD_DIM] = out[g * tq:(g + 1) * tq, :].astype(o_ref.dtype)


def _flash(q, kg, proj, kgc, proj_c):
    t = q.shape[0]
    n_ctx = kgc.shape[0]
    tq = min(256, t)
    tk = min(512, t)
    nk = t // tk
    gw = KV_GROUP * HEAD_DIM
    vcol = COL_VG // HEAD_DIM
    lat = lambda h, i, j: (jnp.maximum(j - 1, 0), h)
    latv = lambda h, i, j: (jnp.maximum(j - 1, 0), vcol + h)
    return pl.pallas_call(
        functools.partial(_flash_body, tq=tq, nk=nk), grid=(GA_KV, t // tq, nk + 1),
        in_specs=[pl.BlockSpec((tq, gw), lambda h, i, j: (i, h)),
                  pl.BlockSpec((n_ctx, HEAD_DIM), lambda h, i, j: (0, h)),
                  pl.BlockSpec((n_ctx, HEAD_DIM), lambda h, i, j: (0, vcol + h)),
                  pl.BlockSpec((tk, HEAD_DIM), lat),
                  pl.BlockSpec((tk, HEAD_DIM), latv)],
        out_specs=pl.BlockSpec((tq, gw), lambda h, i, j: (i, h)),
        out_shape=jax.ShapeDtypeStruct((t, GA_HEADS * HEAD_DIM), BF16),
        scratch_shapes=[pltpu.VMEM((KV_GROUP * tq, HEAD_DIM), BF16), pltpu.VMEM((KV_GROUP * tq, 1), F32),
                        pltpu.VMEM((KV_GROUP * tq, 1), F32), pltpu.VMEM((KV_GROUP * tq, HEAD_DIM), F32)],
        compiler_params=_cp(3, 40), name="global_attn")(q, kgc, proj_c, kg, proj)


def _win_body(q_ref, km_ref, k0_ref, kp_ref, vm_ref, v0_ref, vp_ref, kc_ref, vc_ref, sink_ref, o_ref, *, nb):
    i = pl.program_id(1)
    q = jnp.concatenate([q_ref[:, g * HEAD_DIM:(g + 1) * HEAD_DIM] for g in range(KV_GROUP)], axis=0)
    kw = jnp.concatenate([km_ref[...], k0_ref[...], kp_ref[...]], axis=0)
    vw = jnp.concatenate([vm_ref[...], v0_ref[...], vp_ref[...]], axis=0)
    s_w = lax.dot_general(q, kw, _NT, preferred_element_type=F32)
    r = lax.broadcasted_iota(jnp.int32, s_w.shape, 0) % BLOCK
    c = lax.broadcasted_iota(jnp.int32, s_w.shape, 1)
    cc = c % BLOCK
    iv = jnp.full(s_w.shape, i, jnp.int32)
    valid = (((c < BLOCK) & (cc >= r) & (iv >= 1)) | ((c >= BLOCK) & (c < 2 * BLOCK))
             | ((c >= 2 * BLOCK) & (cc <= r) & (iv + 1 < nb)))
    s_w = jnp.where(valid, s_w, NEG_INF)
    s_c = lax.dot_general(q, kc_ref[...], _NT, preferred_element_type=F32)
    sink = sink_ref[0]
    m = jnp.maximum(jnp.maximum(jnp.max(s_w, -1, keepdims=True), jnp.max(s_c, -1, keepdims=True)), sink)
    pw, pc = jnp.exp(s_w - m), jnp.exp(s_c - m)
    l = jnp.sum(pw, -1, keepdims=True) + jnp.sum(pc, -1, keepdims=True) + jnp.exp(sink - m)
    inv = 1.0 / l
    out = (jnp.dot((pc * inv).astype(BF16), vc_ref[...], preferred_element_type=F32)
           + jnp.dot((pw * inv).astype(BF16), vw, preferred_element_type=F32))
    for g in range(KV_GROUP):
        o_ref[:, g * HEAD_DIM:(g + 1) * HEAD_DIM] = out[g * BLOCK:(g + 1) * BLOCK, :].astype(o_ref.dtype)


def _sink_rows(sink, rows_per_head):
    return jnp.repeat(sink.astype(F32).reshape(WA_KV, KV_GROUP), rows_per_head, axis=1)[:, :, None]


def _window(qw, kw, proj, kwc, proj_c, sink):
    t = qw.shape[0]
    n_ctx = kwc.shape[0]
    nb = t // BLOCK
    gw = KV_GROUP * HEAD_DIM
    vcol = COL_VW // HEAD_DIM
    kspec = lambda f: pl.BlockSpec((BLOCK, HEAD_DIM), f)
    return pl.pallas_call(
        functools.partial(_win_body, nb=nb), grid=(WA_KV, nb),
        in_specs=[pl.BlockSpec((BLOCK, gw), lambda h, i: (i, h)),
                  kspec(lambda h, i: (jnp.maximum(i - 1, 0), h)), kspec(lambda h, i: (i, h)),
                  kspec(lambda h, i: (jnp.minimum(i + 1, nb - 1), h)),
                  kspec(lambda h, i: (jnp.maximum(i - 1, 0), vcol + h)), kspec(lambda h, i: (i, vcol + h)),
                  kspec(lambda h, i: (jnp.minimum(i + 1, nb - 1), vcol + h)),
                  pl.BlockSpec((n_ctx, HEAD_DIM), lambda h, i: (0, h)),
                  pl.BlockSpec((n_ctx, HEAD_DIM), lambda h, i: (0, vcol + h)),
                  pl.BlockSpec((1, KV_GROUP * BLOCK, 1), lambda h, i: (h, 0, 0))],
        out_specs=pl.BlockSpec((BLOCK, gw), lambda h, i: (i, h)),
        out_shape=jax.ShapeDtypeStruct((t, WA_HEADS * HEAD_DIM), BF16),
        compiler_params=_cp(2, 24), name="window_attn")(
            qw, kw, kw, kw, proj, proj, proj, kwc, proj_c, _sink_rows(sink, BLOCK))


def _ctx_attn_body(q_ref, k_ref, v_ref, sink_ref, o_ref, *, n_ctx, use_sink):
    q = jnp.concatenate([q_ref[:, g * HEAD_DIM:(g + 1) * HEAD_DIM] for g in range(KV_GROUP)], axis=0)
    s = lax.dot_general(q, k_ref[...], _NT, preferred_element_type=F32)
    m = jnp.max(s, -1, keepdims=True)
    if use_sink:
        m = jnp.maximum(m, sink_ref[0])
    p = jnp.exp(s - m)
    l = jnp.sum(p, -1, keepdims=True)
    if use_sink:
        l = l + jnp.exp(sink_ref[0] - m)
    out = jnp.dot((p / l).astype(BF16), v_ref[...], preferred_element_type=F32)
    for g in range(KV_GROUP):
        o_ref[:, g * HEAD_DIM:(g + 1) * HEAD_DIM] = out[g * n_ctx:(g + 1) * n_ctx, :].astype(o_ref.dtype)


def _ctx_attn(q, k, proj_c, vcol0, sink, use_sink):
    n_ctx = q.shape[0]
    gw = KV_GROUP * HEAD_DIM
    vcol = vcol0 // HEAD_DIM
    return pl.pallas_call(
        functools.partial(_ctx_attn_body, n_ctx=n_ctx, use_sink=use_sink), grid=(GA_KV,),
        in_specs=[pl.BlockSpec((n_ctx, gw), lambda h: (0, h)),
                  pl.BlockSpec((n_ctx, HEAD_DIM), lambda h: (0, h)),
                  pl.BlockSpec((n_ctx, HEAD_DIM), lambda h: (0, vcol + h)),
                  pl.BlockSpec((1, KV_GROUP * n_ctx, 1), lambda h: (h, 0, 0))],
        out_specs=pl.BlockSpec((n_ctx, gw), lambda h: (0, h)),
        out_shape=jax.ShapeDtypeStruct((n_ctx, GA_HEADS * HEAD_DIM), BF16),
        compiler_params=_cp(1, 24), name="ctx_attn_sink" if use_sink else "ctx_attn")(
            q, k, proj_c, _sink_rows(sink, n_ctx))


def _sg_body(u_ref, v_ref, g_ref, b_ref, w_ref, bs_ref, o_ref, *, nch):
    for n in range(nch):
        rs = slice(n * CHUNK, (n + 1) * CHUNK)
        vn = _ln(v_ref[rs, :].astype(F32), g_ref[...], b_ref[...]).astype(BF16)
        for g in range(SG_GROUPS):
            cs = slice(g * SG_GROUP_CH, (g + 1) * SG_GROUP_CH)
            s = jnp.dot(w_ref[g], vn[:, cs], preferred_element_type=F32) + bs_ref[g]
            o_ref[rs, cs] = (u_ref[rs, cs].astype(F32) * s).astype(o_ref.dtype)


def _spatial_gate(proj, g_v, b_v, w_sp, b_sp):
    t = proj.shape[0]
    tm = min(512, t)
    full = lambda i: (0, 0)
    return pl.pallas_call(
        functools.partial(_sg_body, nch=tm // CHUNK), grid=(t // tm,),
        in_specs=[pl.BlockSpec((tm, SG_WIDTH), lambda i: (i, COL_DU // SG_WIDTH)),
                  pl.BlockSpec((tm, SG_WIDTH), lambda i: (i, COL_DV // SG_WIDTH)),
                  pl.BlockSpec((1, SG_WIDTH), full), pl.BlockSpec((1, SG_WIDTH), full),
                  pl.BlockSpec((SG_GROUPS, CHUNK, CHUNK), lambda i: (0, 0, 0)),
                  pl.BlockSpec((SG_GROUPS, CHUNK, 1), lambda i: (0, 0, 0))],
        out_specs=pl.BlockSpec((tm, SG_WIDTH), lambda i: (i, 0)),
        out_shape=jax.ShapeDtypeStruct((t, SG_WIDTH), BF16),
        compiler_params=_cp(1, 24), name="spatial_gate")(proj, proj, g_v, b_v, w_sp, b_sp)


def _merge_body(ya, yb, yc, yd, wa, wb, wc, wd, ga, gb, gc, gd, o_ref):
    def br(y, w, g):
        return jax.nn.sigmoid(g[...].astype(F32)) * jnp.dot(y[...], w[...], preferred_element_type=F32)
    o_ref[...] = (br(ya, wa, ga) + br(yb, wb, gb) + br(yc, wc, gc) + br(yd, wd, gd)).astype(o_ref.dtype)


def _merge(ys, ws, gates):
    t = ys[0].shape[0]
    tm, tn = min(512, t), 1024
    nn = D_MODEL // tn
    in_specs = [pl.BlockSpec((tm, y.shape[1]), lambda j, i: (i, 0)) for y in ys]
    in_specs += [pl.BlockSpec((w.shape[0], tn), lambda j, i: (0, j)) for w in ws]
    in_specs += [pl.BlockSpec((tm, tn), functools.partial(lambda j, i, b: (i, b * nn + j), b=b)) for b in range(N_BRANCH)]
    return pl.pallas_call(
        _merge_body, grid=(nn, t // tm), in_specs=in_specs,
        out_specs=pl.BlockSpec((tm, tn), lambda j, i: (i, j)),
        out_shape=jax.ShapeDtypeStruct((t, D_MODEL), BF16),
        compiler_params=_cp(2, 48), name="gated_merge")(*ys, *ws, gates, gates, gates, gates)


def _out_ln_body(mg_ref, w_ref, x_ref, g1_ref, lng_ref, lnb_ref, sc_ref, sh_ref, x1_ref, xmT_ref, *, alpha):
    o = jnp.dot(mg_ref[...], w_ref[...], preferred_element_type=F32)
    x1 = _ln(alpha * x_ref[...] + g1_ref[...] * o, lng_ref[...], lnb_ref[...])
    x1_ref[...] = x1
    xmT_ref[...] = (x1 * (1.0 + sc_ref[...]) + sh_ref[...]).T.astype(xmT_ref.dtype)


def _out_ln(merged, w_out, x, g1, lng, lnb, sc2, sh2, alpha):
    t, d = x.shape
    tm = min(256, t)
    row = lambda i: (i, 0)
    vec = pl.BlockSpec((1, d), lambda i: (0, 0))
    return pl.pallas_call(
        functools.partial(_out_ln_body, alpha=alpha), grid=(t // tm,),
        in_specs=[pl.BlockSpec((tm, d), row), pl.BlockSpec((d, d), lambda i: (0, 0)), pl.BlockSpec((tm, d), row),
                  vec, vec, vec, vec, vec],
        out_specs=[pl.BlockSpec((tm, d), row), pl.BlockSpec((d, tm), lambda i: (0, i))],
        out_shape=[jax.ShapeDtypeStruct((t, d), F32), jax.ShapeDtypeStruct((d, t), BF16)],
        compiler_params=_cp(1, 48), name="out_proj_ln")(merged, w_out, x, g1, lng, lnb, sc2, sh2)


def _peer_topk_body(q_ref, k_ref, s1_o, a1_o, s2_o, e2_o, tau_o):
    q = q_ref[...].astype(BF16)
    s1 = jnp.dot(k_ref[0], q[:PEER_HALF], preferred_element_type=F32)
    s2 = jnp.dot(k_ref[1], q[PEER_HALF:], preferred_element_type=F32)
    ninf = -jnp.inf

    def top_distinct(s):
        vals, tops, cnts = s, [], []
        for _ in range(PEER_TOPK):
            m = jnp.max(vals, axis=0, keepdims=True)
            eq = vals == m
            tops.append(m)
            cnts.append(jnp.sum(eq.astype(F32), axis=0, keepdims=True))
            vals = jnp.where(eq, ninf, vals)
        return jnp.concatenate(tops, 0), jnp.concatenate(cnts, 0)

    ta, na = top_distinct(s1)
    tb, nb = top_distinct(s2)
    ea, eb = jnp.exp(ta - ta[0:1]), jnp.exp(tb - tb[0:1])
    cand = jnp.concatenate([ta[k:k + 1] + tb for k in range(PEER_TOPK)], 0)
    mult = jnp.concatenate([na[k:k + 1] * nb for k in range(PEER_TOPK)], 0)
    ee = jnp.concatenate([ea[k:k + 1] * eb for k in range(PEER_TOPK)], 0)
    vals = cand
    cnt = jnp.zeros_like(cand[0:1])
    tau = jnp.full_like(cand[0:1], ninf)
    for _ in range(PEER_TOPK):
        m = jnp.max(vals, axis=0, keepdims=True)
        eq = vals == m
        tau = jnp.where(cnt < PEER_TOPK, m, tau)
        cnt = cnt + jnp.sum(jnp.where(eq, mult, 0.0), axis=0, keepdims=True)
        vals = jnp.where(eq, ninf, vals)
    z = jnp.sum(jnp.where(cand >= tau, mult * ee, 0.0), axis=0, keepdims=True)
    s1_o[...] = s1
    a1_o[...] = jnp.exp(s1 - ta[0:1]) * (1.0 / z)
    s2_o[...] = s2
    e2_o[...] = jnp.exp(s2 - tb[0:1])
    tau_o[...] = tau


def _peer_topk(qT, keys):
    t = qT.shape[1]
    tt = min(256, t)
    blk = pl.BlockSpec((None, PEER_NKEYS, tt), lambda j, h: (h, 0, j))
    shp = jax.ShapeDtypeStruct((PEER_HEADS, PEER_NKEYS, t), F32)
    return pl.pallas_call(
        _peer_topk_body, grid=(t // tt, PEER_HEADS),
        in_specs=[pl.BlockSpec((PEER_QDIM, tt), lambda j, h: (h, j)),
                  pl.BlockSpec((None, 2, PEER_NKEYS, PEER_HALF), lambda j, h: (h, 0, 0, 0))],
        out_specs=[blk, blk, blk, blk, pl.BlockSpec((None, 1, tt), lambda j, h: (h, 0, j))],
        out_shape=[shp, shp, shp, shp, jax.ShapeDtypeStruct((PEER_HEADS, 1, t), F32)],
        compiler_params=_cp(2, 32), name="peer_topk")(qT, keys)


def _peer_dense_body(xT_ref, u_ref, vT_ref, s1_ref, a1_ref, s2_ref, e2_ref, tau_ref, o_ref, aw, *, ek):
    k = pl.program_id(1)

    @pl.when(k == 0)
    def _():
        o_ref[...] = jnp.zeros(o_ref.shape, F32)

    hT = jnp.dot(u_ref[...], xT_ref[...], preferred_element_type=F32)
    nsub = ek // PEER_NKEYS
    for ii in range(nsub):
        i1 = k * nsub + ii
        w = None
        for h in range(PEER_HEADS):
            c = s2_ref[h] + s1_ref[h, pl.ds(i1, 1), :]
            wh = jnp.where(c >= tau_ref[h], e2_ref[h] * a1_ref[h, pl.ds(i1, 1), :], 0.0)
            w = wh if w is None else w + wh
        rs = slice(ii * PEER_NKEYS, (ii + 1) * PEER_NKEYS)
        aw[rs, :] = (_gelu(hT[rs, :]) * w).astype(BF16)
    o_ref[...] += jnp.dot(vT_ref[...], aw[...], preferred_element_type=F32)


def _peer_dense(xmT, u_bf, vT_bf, tk):
    d, t = xmT.shape
    tt = min(512, t)
    ek = 512
    s1, a1, s2, e2, tau = tk
    sblk = pl.BlockSpec((PEER_HEADS, PEER_NKEYS, tt), lambda j, k: (0, 0, j))
    return pl.pallas_call(
        functools.partial(_peer_dense_body, ek=ek), grid=(t // tt, PEER_EXPERTS // ek),
        in_specs=[pl.BlockSpec((d, tt), lambda j, k: (0, j)),
                  pl.BlockSpec((ek, d), lambda j, k: (k, 0)),
                  pl.BlockSpec((d, ek), lambda j, k: (0, k)),
                  sblk, sblk, sblk, sblk,
                  pl.BlockSpec((PEER_HEADS, 1, tt), lambda j, k: (0, 0, j))],
        out_specs=pl.BlockSpec((d, tt), lambda j, k: (0, j)),
        out_shape=jax.ShapeDtypeStruct((d, t), F32),
        scratch_shapes=[pltpu.VMEM((ek, tt), BF16)],
        compiler_params=_cp(2, 48), name="peer_dense")(xmT, u_bf, vT_bf, s1, a1, s2, e2, tau)


def _ffn_ln_body(fT_ref, x1_ref, g2_ref, lng_ref, lnb_ref, scn_ref, shn_ref, x2_ref, *rest, alpha):
    x2 = _ln(alpha * x1_ref[...] + g2_ref[...] * fT_ref[...].T, lng_ref[...], lnb_ref[...])
    x2_ref[...] = x2
    if rest:
        rest[0][...] = (x2 * (1.0 + scn_ref[...]) + shn_ref[...]).astype(rest[0].dtype)


def _ffn_ln(fT, x1, g2, lng, lnb, scn, shn, alpha, emit_h):
    t, d = x1.shape
    tm = min(256, t)
    row = pl.BlockSpec((tm, d), lambda i: (i, 0))
    vec = pl.BlockSpec((1, d), lambda i: (0, 0))
    out_specs = [row, row] if emit_h else [row]
    out_shape = [jax.ShapeDtypeStruct((t, d), F32)] + ([jax.ShapeDtypeStruct((t, d), BF16)] if emit_h else [])
    return pl.pallas_call(
        functools.partial(_ffn_ln_body, alpha=alpha), grid=(t // tm,),
        in_specs=[pl.BlockSpec((d, tm), lambda i: (0, i)), row, vec, vec, vec, vec, vec],
        out_specs=out_specs, out_shape=out_shape,
        compiler_params=_cp(1, 32), name="ffn_ln")(fT, x1, g2, lng, lnb, scn, shn)


def kernel(x, c, ctx, c_ctx, w_ada, b_ada, w_in, b_in, s5_lam_re, s5_lam_im, s5_log_dt, s5_b_re, s5_b_im, s5_c_re, s5_c_im, s5_d, w_glu, b_glu, qn_gain, kn_gain, sink, sg_ln_g, sg_ln_b, w_sp, b_sp, w_br_a, w_br_b, w_br_c, w_br_d, w_out, ln1_g, ln1_b, ln2_g, ln2_b, w_pq, peer_keys, peer_u, peer_v):
    depth = w_in.shape[0]
    bsz, n_lat, d = x.shape
    assert bsz == 1 and d == D_MODEL and n_lat % 512 == 0 and ctx.shape[1] % BLOCK == 0
    n_ctx = ctx.shape[1]
    alpha = (2 * depth) ** 0.25
    x, ctx = x[0], ctx[0]

    row8 = lax.broadcasted_iota(jnp.int32, (8, d), 0)
    cond8 = jnp.where(row8 == 0, c[0][None], jnp.where(row8 == 1, c_ctx[None], 0.0))
    ada = [_ada(cond8, w_ada, b_ada, l) for l in range(depth)]
    cosf, sinf = _rope_tables(n_lat)
    cos1, sin0 = jnp.ones((n_ctx, HEAD_DIM), F32), jnp.zeros((n_ctx, HEAD_DIM), F32)
    vec = lambda a: a.reshape(1, -1).astype(F32)

    ref_off = {'a': 0, 'qg': 768, 'kg': 1792, 'vg': 2048, 'qw': 2304, 'kw': 3328, 'vw': 3584, 'du': 3840, 'dv': 4608}
    ref_w = {'a': 768, 'qg': 1024, 'kg': 256, 'vg': 256, 'qw': 1024, 'kw': 256, 'vw': 256, 'du': 768, 'dv': 768}
    order = ['a', 'kg', 'qg', 'qw', 'kw', 'vg', 'vw', 'du', 'dv']
    perm = lambda a: jnp.concatenate([a[..., ref_off[n]:ref_off[n] + ref_w[n]] for n in order], -1)
    perm_idx = np.concatenate([np.arange(ref_off[n], ref_off[n] + ref_w[n]) for n in order]).astype(np.int32)

    h_lat = None
    for l in range(depth):
        need_ctx = l < depth - 1
        mods = [[ada[l][r:r + 1, i * d:(i + 1) * d] for i in range(6)] for r in range(2)]
        (sh1, sc1, g1, sh2, sc2, g2), (sh1c, sc1c, g1c, sh2c, sc2c, g2c) = mods
        if need_ctx:
            nxt = [ada[l + 1][r:r + 1, 0:2 * d] for r in range(2)]
            (shn, scn), (shnc, scnc) = [(m[:, :d], m[:, d:]) for m in nxt]
        else:
            shn = scn = shnc = scnc = jnp.zeros((1, d), F32)

        w_small, b_small = perm(w_in[l][:, :SMALL_W]).astype(BF16), vec(jnp.take(b_in[l], perm_idx))
        w_gate, b_gate = w_in[l][:, SMALL_W:].astype(BF16), vec(b_in[l][SMALL_W:])
        s5m = [_s5_mats(s5_lam_re[l, dr], s5_lam_im[l, dr], s5_log_dt[l, dr], s5_b_re[l, dr], s5_b_im[l, dr],
                        s5_c_re[l, dr], s5_c_im[l, dr]) for dr in range(2)]
        wg, bg, s5d = w_glu[l].astype(BF16), vec(b_glu[l]), vec(s5_d[l])
        qn, kn = vec(qn_gain[l]), vec(kn_gain[l])
        wsp, bsp = w_sp[l].astype(BF16), b_sp[l].astype(F32)[:, :, None]
        w_brs = [w[l].astype(BF16) for w in (w_br_a, w_br_b, w_br_c, w_br_d)]
        wo = w_out[l].astype(BF16)
        wpqT = w_pq[l].T.astype(BF16)
        keys = peer_keys[l].astype(BF16)
        u_bf, vT_bf = peer_u[l].astype(BF16), peer_v[l].T.astype(BF16)

        if h_lat is None:
            h_lat, h_ctx = _modulate(x, sc1, sh1), _modulate(ctx, sc1c, sh1c)

        proj = _matmul(h_lat, w_small, b_small, tm=512, tn=1792, out_dtype=BF16, name="in_proj")
        gates = _matmul(h_lat, w_gate, b_gate, tm=512, tn=2048, out_dtype=BF16, name="in_proj_gates")
        proj_c = _matmul(h_ctx, w_small, b_small, tm=512, tn=1792, out_dtype=BF16, name="in_proj_ctx")

        y_dir, y_dir_c = [], []
        for dr in range(2):
            yc_, hc_end = _s5_dir(proj_c, s5m[dr], jnp.zeros((2, S5_CH), F32), bool(dr))
            yl_, _ = _s5_dir(proj, s5m[dr], hc_end, bool(dr))
            y_dir.append(yl_)
            y_dir_c.append(yc_)
        y_a = _s5_out(proj, y_dir[0], y_dir[1], s5d, wg, bg)

        qg, kg, qw, kw = _prep(proj, cosf, sinf, qn, kn)
        qgc, kgc, qwc, kwc = _prep(proj_c, cos1, sin0, qn, kn)
        y_b = _flash(qg, kg, proj, kgc, proj_c)
        y_c = _window(qw, kw, proj, kwc, proj_c, sink[l])

        y_d = _spatial_gate(proj, vec(sg_ln_g[l]), vec(sg_ln_b[l]), wsp, bsp)

        merged = _merge((y_a, y_b, y_c, y_d), w_brs, gates)
        x1, xmT = _out_ln(merged, wo, x, g1, vec(ln1_g[l]), vec(ln1_b[l]), sc2, sh2, alpha)
        qT = _matmul(wpqT, xmT, None, tm=D_MODEL, tn=1024, out_dtype=F32, name="peer_query")
        ffnT = _peer_dense(xmT, u_bf, vT_bf, _peer_topk(qT, keys))
        res = _ffn_ln(ffnT, x1, g2, vec(ln2_g[l]), vec(ln2_b[l]), scn, shn, alpha, need_ctx)
        x = res[0]

        if need_ctx:
            h_lat = res[1]
            gates_c = _matmul(h_ctx, w_gate, b_gate, tm=512, tn=2048, out_dtype=BF16, name="in_proj_gates_ctx")
            y_a_c = _s5_out(proj_c, y_dir_c[0], y_dir_c[1], s5d, wg, bg)
            y_b_c = _ctx_attn(qgc, kgc, proj_c, COL_VG, sink[l], False)
            y_c_c = _ctx_attn(qwc, kwc, proj_c, COL_VW, sink[l], True)
            y_d_c = _spatial_gate(proj_c, vec(sg_ln_g[l]), vec(sg_ln_b[l]), wsp, bsp)
            merged_c = _merge((y_a_c, y_b_c, y_c_c, y_d_c), w_brs, gates_c)
            c1, cmT = _out_ln(merged_c, wo, ctx, g1c, vec(ln1_g[l]), vec(ln1_b[l]), sc2c, sh2c, alpha)
            qTc = _matmul(wpqT, cmT, None, tm=D_MODEL, tn=1024, out_dtype=F32, name="peer_query_ctx")
            ffnTc = _peer_dense(cmT, u_bf, vT_bf, _peer_topk(qTc, keys))
            ctx, h_ctx = _ffn_ln(ffnTc, c1, g2c, vec(ln2_g[l]), vec(ln2_b[l]), scnc, shnc, alpha, True)
    return x[None]
```

```python
import functools
import math

import jax
import jax.numpy as jnp
import numpy as np
from jax import lax
from jax.experimental import pallas as pl
from jax.experimental.pallas import tpu as pltpu

F32, BF16 = jnp.float32, jnp.bfloat16

D_MODEL = 2048
GRID_W = 64
BLOCK = 128
HEAD_DIM = 128
AXIS_DIM = HEAD_DIM // 2
ROPE_THETA = 10000.0
ATTN_SCALE = HEAD_DIM ** -0.5
NEG_INF = -1e30
EPS = 1e-6
S5_GROUPS, S5_GROUP_CH, S5_STATE = 48, 16, 64
S5_WIDTH = S5_GROUPS * S5_GROUP_CH
S5_CH = S5_GROUPS * S5_STATE
S5_GB = 16
S5_NB = S5_GROUPS // S5_GB
S5_KB = S5_GB * S5_GROUP_CH
S5_PB = S5_GB * S5_STATE
GA_HEADS, GA_KV = 8, 2
WA_HEADS, WA_KV = 8, 2
KV_GROUP = GA_HEADS // GA_KV
SG_GROUPS, SG_GROUP_CH = 6, 128
SG_WIDTH = SG_GROUPS * SG_GROUP_CH
CHUNK = 128
N_BRANCH = 4
PEER_HEADS, PEER_NKEYS, PEER_QDIM, PEER_TOPK = 8, 128, 256, 16
PEER_HALF = PEER_QDIM // 2
PEER_EXPERTS = PEER_NKEYS * PEER_NKEYS

COL_A, COL_KG, COL_QG, COL_QW, COL_KW, COL_VG, COL_VW, COL_DU, COL_DV = 0, 768, 1024, 2048, 3072, 3328, 3584, 3840, 4608
SMALL_W = 5376
GATE_W = N_BRANCH * D_MODEL

VMEM_CAP_MB = 56


def _cp(ndims, vmem_mb=32):
    return pltpu.CompilerParams(dimension_semantics=("arbitrary",) * ndims,
                                vmem_limit_bytes=min(vmem_mb, VMEM_CAP_MB) << 20)


def _gelu(y):
    return 0.5 * y * (1.0 + lax.erf(y * (2.0 ** -0.5)))


def _ln(y, g, b):
    mu = jnp.mean(y, -1, keepdims=True)
    yc = y - mu
    var = jnp.mean(yc * yc, -1, keepdims=True)
    return yc * lax.rsqrt(var + EPS) * g + b


_NT = (((1,), (1,)), ((), ()))


def _ada_body(c_ref, w_ref, b_ref, o_ref):
    cnd = c_ref[...]
    s = (cnd * jax.nn.sigmoid(cnd)).astype(BF16)
    o_ref[...] = jnp.dot(s, w_ref[...].astype(BF16), preferred_element_type=F32) + b_ref[...]


def _ada(cond8, w_ada, b_ada, l):
    depth, d, n = w_ada.shape
    tn = 1024
    return pl.pallas_call(
        _ada_body, grid=(n // tn,),
        in_specs=[pl.BlockSpec((8, d), lambda j: (0, 0)),
                  pl.BlockSpec((None, d, tn), lambda j: (l, 0, j)),
                  pl.BlockSpec((None, 1, tn), lambda j: (l, 0, j))],
        out_specs=pl.BlockSpec((8, tn), lambda j: (0, j)),
        out_shape=jax.ShapeDtypeStruct((8, n), F32),
        compiler_params=_cp(1, 32), name="ada_mod")(cond8, w_ada, b_ada.reshape(depth, 1, n))


def _mod_body(x_ref, sc_ref, sh_ref, o_ref):
    o_ref[...] = (x_ref[...] * (1.0 + sc_ref[...]) + sh_ref[...]).astype(o_ref.dtype)


def _modulate(x, sc, sh):
    t, d = x.shape
    tm = min(512, t)
    return pl.pallas_call(
        _mod_body, grid=(t // tm,),
        in_specs=[pl.BlockSpec((tm, d), lambda i: (i, 0)),
                  pl.BlockSpec((1, d), lambda i: (0, 0)),
                  pl.BlockSpec((1, d), lambda i: (0, 0))],
        out_specs=pl.BlockSpec((tm, d), lambda i: (i, 0)),
        out_shape=jax.ShapeDtypeStruct((t, d), BF16),
        compiler_params=_cp(1, 24), name="modulate")(x, sc, sh)


def _mm_body(a_ref, w_ref, b_ref, o_ref):
    acc = jnp.dot(a_ref[...], w_ref[...], preferred_element_type=F32)
    o_ref[...] = (acc + b_ref[...]).astype(o_ref.dtype)


def _mm_nobias_body(a_ref, w_ref, o_ref):
    o_ref[...] = jnp.dot(a_ref[...], w_ref[...], preferred_element_type=F32).astype(o_ref.dtype)


def _matmul(a, w, b, *, tm, tn, out_dtype, name):
    m, k = a.shape
    n = w.shape[1]
    tm, tn = min(tm, m), min(tn, n)
    in_specs = [pl.BlockSpec((tm, k), lambda j, i: (i, 0)),
                pl.BlockSpec((k, tn), lambda j, i: (0, j))]
    args = [a, w]
    body = _mm_nobias_body
    if b is not None:
        in_specs.append(pl.BlockSpec((1, tn), lambda j, i: (0, j)))
        args.append(b)
        body = _mm_body
    osz = jnp.dtype(out_dtype).itemsize
    vmem = 2 * (tm * k * 2 + k * tn * 2 + tm * tn * osz) + tm * tn * 4
    return pl.pallas_call(
        body, grid=(n // tn, m // tm), in_specs=in_specs,
        out_specs=pl.BlockSpec((tm, tn), lambda j, i: (i, j)),
        out_shape=jax.ShapeDtypeStruct((m, n), out_dtype),
        compiler_params=_cp(2, (vmem >> 20) + 8), name=name)(*args)


def _s5_body(u_ref, bre_ref, bim_ref, cre_ref, cim_ref, a_ref, h0_ref, y_ref, hT_ref, xr, xi, carry, *, tc, rev):
    @pl.when(pl.program_id(0) == 0)
    def _():
        carry[...] = h0_ref[...]

    for j in range(S5_NB):
        uj = u_ref[:, j * S5_KB:(j + 1) * S5_KB]
        xr[:, j * S5_PB:(j + 1) * S5_PB] = jnp.dot(uj, bre_ref[j], preferred_element_type=F32)
        xi[:, j * S5_PB:(j + 1) * S5_PB] = jnp.dot(uj, bim_ref[j], preferred_element_type=F32)

    for j in range(S5_NB):
        cs = slice(j * S5_PB, (j + 1) * S5_PB)
        ar, ai = a_ref[0:1, cs], a_ref[1:2, cs]

        def step(k, c, cs=cs, ar=ar, ai=ai):
            hr, hi = c
            t = (tc - 1 - k) if rev else k
            nr = ar * hr - ai * hi + xr[pl.ds(t, 1), cs]
            ni = ar * hi + ai * hr + xi[pl.ds(t, 1), cs]
            xr[pl.ds(t, 1), cs] = nr
            xi[pl.ds(t, 1), cs] = ni
            return nr, ni

        hr, hi = lax.fori_loop(0, tc, step, (carry[0:1, cs], carry[1:2, cs]), unroll=8)
        carry[0:1, cs] = hr
        carry[1:2, cs] = hi

    for j in range(S5_NB):
        cs = slice(j * S5_PB, (j + 1) * S5_PB)
        y_ref[:, j * S5_KB:(j + 1) * S5_KB] = (
            jnp.dot(xr[:, cs].astype(BF16), cre_ref[j], preferred_element_type=F32)
            + jnp.dot(xi[:, cs].astype(BF16), cim_ref[j], preferred_element_type=F32))
    hT_ref[...] = carry[...]


def _s5_dir(proj, mats, h0, rev):
    t = proj.shape[0]
    tc = min(256, t)
    nt = t // tc
    bre, bim, cre, cim, a = mats
    tmap = (lambda i: (nt - 1 - i, 0)) if rev else (lambda i: (i, 0))
    full3 = lambda i: (0, 0, 0)
    full2 = lambda i: (0, 0)
    return pl.pallas_call(
        functools.partial(_s5_body, tc=tc, rev=rev), grid=(nt,),
        in_specs=[pl.BlockSpec((tc, S5_WIDTH), tmap),
                  pl.BlockSpec((S5_NB, S5_KB, S5_PB), full3), pl.BlockSpec((S5_NB, S5_KB, S5_PB), full3),
                  pl.BlockSpec((S5_NB, S5_PB, S5_KB), full3), pl.BlockSpec((S5_NB, S5_PB, S5_KB), full3),
                  pl.BlockSpec((2, S5_CH), full2), pl.BlockSpec((2, S5_CH), full2)],
        out_specs=[pl.BlockSpec((tc, S5_WIDTH), tmap), pl.BlockSpec((2, S5_CH), full2)],
        out_shape=[jax.ShapeDtypeStruct((t, S5_WIDTH), F32), jax.ShapeDtypeStruct((2, S5_CH), F32)],
        scratch_shapes=[pltpu.VMEM((tc, S5_CH), F32), pltpu.VMEM((tc, S5_CH), F32), pltpu.VMEM((2, S5_CH), F32)],
        compiler_params=_cp(1, 40), name="s5_scan_bwd" if rev else "s5_scan_fwd")(proj, bre, bim, cre, cim, a, h0)


def _s5_out_body(u_ref, yf_ref, yb_ref, d_ref, wg_ref, bg_ref, o_ref):
    y = u_ref[...].astype(F32) * d_ref[...] + yf_ref[...] + yb_ref[...]
    g = _gelu(y)
    z = jnp.dot(g.astype(BF16), wg_ref[...], preferred_element_type=F32) + bg_ref[...]
    o_ref[...] = (g * jax.nn.sigmoid(z)).astype(o_ref.dtype)


def _s5_out(proj, yf, yb, d, wg, bg):
    t = proj.shape[0]
    tm = min(512, t)
    row = lambda i: (i, 0)
    full = lambda i: (0, 0)
    return pl.pallas_call(
        _s5_out_body, grid=(t // tm,),
        in_specs=[pl.BlockSpec((tm, S5_WIDTH), row), pl.BlockSpec((tm, S5_WIDTH), row), pl.BlockSpec((tm, S5_WIDTH), row),
                  pl.BlockSpec((1, S5_WIDTH), full), pl.BlockSpec((S5_WIDTH, S5_WIDTH), full), pl.BlockSpec((1, S5_WIDTH), full)],
        out_specs=pl.BlockSpec((tm, S5_WIDTH), row),
        out_shape=jax.ShapeDtypeStruct((t, S5_WIDTH), BF16),
        compiler_params=_cp(1, 24), name="s5_glu")(proj, yf, yb, d, wg, bg)


def _s5_mats(lam_re, lam_im, log_dt, b_re, b_im, c_re, c_im):
    dt = jnp.exp(log_dt)[:, None]
    mag = jnp.exp(lam_re * dt)
    ar, ai = mag * jnp.cos(lam_im * dt), mag * jnp.sin(lam_im * dt)
    nr, ni = ar - 1.0, ai
    den = lam_re * lam_re + lam_im * lam_im
    cr = (nr * lam_re + ni * lam_im) / den
    ci = (ni * lam_re - nr * lam_im) / den
    fre = cr[:, :, None] * b_re - ci[:, :, None] * b_im
    fim = cr[:, :, None] * b_im + ci[:, :, None] * b_re
    eye = jnp.eye(S5_GB, dtype=F32)

    def blk_in(f):
        f = f.reshape(S5_NB, S5_GB, S5_STATE, S5_GROUP_CH)
        m = jnp.einsum('jgph,gk->jghkp', f, eye)
        return m.reshape(S5_NB, S5_KB, S5_PB).astype(BF16)

    def blk_out(c):
        c = c.reshape(S5_NB, S5_GB, S5_GROUP_CH, S5_STATE)
        m = jnp.einsum('jghp,gk->jgpkh', c, eye)
        return m.reshape(S5_NB, S5_PB, S5_KB).astype(BF16)

    is_re = lax.broadcasted_iota(jnp.int32, (2, S5_CH), 0) == 0
    a = jnp.where(is_re, ar.reshape(1, -1), ai.reshape(1, -1))
    return blk_in(fre), blk_in(fim), blk_out(c_re), blk_out(-c_im), a


def _prep_body(qg_ref, kg_ref, vg_ref, qw_ref, kw_ref, cos_ref, sin_ref, qn_ref, kn_ref, oqg, okg, ovg, oqw, okw, *,
               qg_scale):
    cosf, sinf = cos_ref[...], sin_ref[...]
    lane = lax.broadcasted_iota(jnp.int32, cosf.shape, 1)
    low = (lane % AXIS_DIM) < (AXIS_DIM // 2)

    def rope(t):
        partner = jnp.where(low, pltpu.roll(t, HEAD_DIM - AXIS_DIM // 2, 1), pltpu.roll(t, AXIS_DIM // 2, 1))
        return t * cosf + partner * sinf

    def rms(t, g):
        return t * lax.rsqrt(jnp.mean(t * t, -1, keepdims=True) + EPS) * g

    for h in range(GA_HEADS):
        cs = slice(h * HEAD_DIM, (h + 1) * HEAD_DIM)
        oqg[:, cs] = (rope(rms(qg_ref[:, cs].astype(F32), qn_ref[...])) * qg_scale).astype(BF16)
        oqw[:, cs] = (rope(qw_ref[:, cs].astype(F32)) * ATTN_SCALE).astype(BF16)
    for h in range(GA_KV):
        cs = slice(h * HEAD_DIM, (h + 1) * HEAD_DIM)
        okg[:, cs] = rope(rms(kg_ref[:, cs].astype(F32), kn_ref[...])).astype(BF16)
        okw[:, cs] = rope(kw_ref[:, cs].astype(F32)).astype(BF16)
        ovg[:, 2 * h * HEAD_DIM:(2 * h + 1) * HEAD_DIM] = vg_ref[:, cs]
        ovg[:, (2 * h + 1) * HEAD_DIM:(2 * h + 2) * HEAD_DIM] = jnp.ones((vg_ref.shape[0], HEAD_DIM), BF16)


def _prep(proj, cosf, sinf, qn, kn, qg_scale):
    t = proj.shape[0]
    tr = min(256, t)
    qw_, kw_ = GA_HEADS * HEAD_DIM, GA_KV * HEAD_DIM
    row = lambda i: (i, 0)
    full = lambda i: (0, 0)
    return pl.pallas_call(
        functools.partial(_prep_body, qg_scale=qg_scale), grid=(t // tr,),
        in_specs=[pl.BlockSpec((tr, qw_), lambda i: (i, COL_QG // qw_)),
                  pl.BlockSpec((tr, kw_), lambda i: (i, COL_KG // kw_)),
                  pl.BlockSpec((tr, kw_), lambda i: (i, COL_VG // kw_)),
                  pl.BlockSpec((tr, qw_), lambda i: (i, COL_QW // qw_)),
                  pl.BlockSpec((tr, kw_), lambda i: (i, COL_KW // kw_)),
                  pl.BlockSpec((tr, HEAD_DIM), row), pl.BlockSpec((tr, HEAD_DIM), row),
                  pl.BlockSpec((1, HEAD_DIM), full), pl.BlockSpec((1, HEAD_DIM), full)],
        out_specs=[pl.BlockSpec((tr, qw_), row), pl.BlockSpec((tr, kw_), row), pl.BlockSpec((tr, 2 * kw_), row),
                   pl.BlockSpec((tr, qw_), row), pl.BlockSpec((tr, kw_), row)],
        out_shape=[jax.ShapeDtypeStruct((t, qw_), BF16), jax.ShapeDtypeStruct((t, kw_), BF16),
                   jax.ShapeDtypeStruct((t, 2 * kw_), BF16),
                   jax.ShapeDtypeStruct((t, qw_), BF16), jax.ShapeDtypeStruct((t, kw_), BF16)],
        compiler_params=_cp(1, 24), name="qk_prep")(proj, proj, proj, proj, proj, cosf, sinf, qn, kn)


def _rope_tables(n_lat):
    rows = n_lat // GRID_W
    row = jnp.repeat(jnp.arange(rows, dtype=F32), GRID_W)
    col = jnp.tile(jnp.arange(GRID_W, dtype=F32), rows)
    inv = jnp.power(ROPE_THETA, -jnp.arange(0, AXIS_DIM, 2, dtype=F32) / AXIS_DIM)
    ang_r, ang_c = row[:, None] * inv, col[:, None] * inv
    cr, sr, cc, sc = jnp.cos(ang_r), jnp.sin(ang_r), jnp.cos(ang_c), jnp.sin(ang_c)
    return jnp.concatenate([cr, cr, cc, cc], -1), jnp.concatenate([-sr, sr, -sc, sc], -1)


def _flash_body(q_ref, kc_ref, vc_ref, k_ref, v_ref, o_ref, qs, m_s, acc, *, tq, tk, nk):
    for g in range(KV_GROUP):
        qs[g * tq:(g + 1) * tq, :] = q_ref[:, g * HEAD_DIM:(g + 1) * HEAD_DIM]
    m_s[...] = jnp.full(m_s.shape, -jnp.inf, F32)
    acc[...] = jnp.zeros(acc.shape, F32)

    def update(k, v):
        s = lax.dot_general(qs[...], k, _NT, preferred_element_type=F32)
        m_prev = m_s[...]
        m_next = jnp.maximum(m_prev, jnp.max(s, axis=1, keepdims=True))
        alpha = jnp.exp2(m_prev - m_next)
        p = jnp.exp2(s - jnp.concatenate([m_next] * (s.shape[1] // HEAD_DIM), axis=1))
        acc[...] = (jnp.concatenate([alpha, alpha], axis=1) * acc[...]
                    + jnp.dot(p.astype(BF16), v, preferred_element_type=F32))
        m_s[...] = m_next

    update(kc_ref[...], vc_ref[...])

    def body(j, carry):
        rows = pl.ds(pl.multiple_of(j * tk, tk), tk)
        update(k_ref[rows, :], v_ref[rows, :])
        return carry

    lax.fori_loop(0, nk, body, 0)
    out = acc[:, :HEAD_DIM] / acc[:, HEAD_DIM:]
    for g in range(KV_GROUP):
        o_ref[:, g * HEAD_DIM:(g + 1) * HEAD_DIM] = out[g * tq:(g + 1) * tq, :].astype(o_ref.dtype)


def _flash(q, kg, vge, kgc, vgce):
    t = q.shape[0]
    n_ctx = kgc.shape[0]
    tq = min(256, t)
    tk = min(512, t)
    gw = KV_GROUP * HEAD_DIM
    head = lambda h, i: (0, h)
    return pl.pallas_call(
        functools.partial(_flash_body, tq=tq, tk=tk, nk=t // tk), grid=(GA_KV, t // tq),
        in_specs=[pl.BlockSpec((tq, gw), lambda h, i: (i, h)),
                  pl.BlockSpec((n_ctx, HEAD_DIM), head), pl.BlockSpec((n_ctx, 2 * HEAD_DIM), head),
                  pl.BlockSpec((t, HEAD_DIM), head), pl.BlockSpec((t, 2 * HEAD_DIM), head)],
        out_specs=pl.BlockSpec((tq, gw), lambda h, i: (i, h)),
        out_shape=jax.ShapeDtypeStruct((t, GA_HEADS * HEAD_DIM), BF16),
        scratch_shapes=[pltpu.VMEM((KV_GROUP * tq, HEAD_DIM), BF16), pltpu.VMEM((KV_GROUP * tq, HEAD_DIM), F32),
                        pltpu.VMEM((KV_GROUP * tq, 2 * HEAD_DIM), F32)],
        compiler_params=_cp(2, 48), name="global_attn")(q, kgc, vgce, kg, vge)


def _win_body(q_ref, km_ref, k0_ref, kp_ref, vm_ref, v0_ref, vp_ref, kc_ref, vc_ref, sink_ref, o_ref, *, nb):
    i = pl.program_id(1)
    q = jnp.concatenate([q_ref[:, g * HEAD_DIM:(g + 1) * HEAD_DIM] for g in range(KV_GROUP)], axis=0)
    kw = jnp.concatenate([km_ref[...], k0_ref[...], kp_ref[...]], axis=0)
    vw = jnp.concatenate([vm_ref[...], v0_ref[...], vp_ref[...]], axis=0)
    s_w = lax.dot_general(q, kw, _NT, preferred_element_type=F32)
    r = lax.broadcasted_iota(jnp.int32, s_w.shape, 0) % BLOCK
    c = lax.broadcasted_iota(jnp.int32, s_w.shape, 1)
    cc = c % BLOCK
    iv = jnp.full(s_w.shape, i, jnp.int32)
    valid = (((c < BLOCK) & (cc >= r) & (iv >= 1)) | ((c >= BLOCK) & (c < 2 * BLOCK))
             | ((c >= 2 * BLOCK) & (cc <= r) & (iv + 1 < nb)))
    s_w = jnp.where(valid, s_w, NEG_INF)
    s_c = lax.dot_general(q, kc_ref[...], _NT, preferred_element_type=F32)
    sink = sink_ref[0]
    m = jnp.maximum(jnp.maximum(jnp.max(s_w, -1, keepdims=True), jnp.max(s_c, -1, keepdims=True)), sink)
    pw, pc = jnp.exp(s_w - m), jnp.exp(s_c - m)
    l = jnp.sum(pw, -1, keepdims=True) + jnp.sum(pc, -1, keepdims=True) + jnp.exp(sink - m)
    inv = 1.0 / l
    out = (jnp.dot((pc * inv).astype(BF16), vc_ref[...], preferred_element_type=F32)
           + jnp.dot((pw * inv).astype(BF16), vw, preferred_element_type=F32))
    for g in range(KV_GROUP):
        o_ref[:, g * HEAD_DIM:(g + 1) * HEAD_DIM] = out[g * BLOCK:(g + 1) * BLOCK, :].astype(o_ref.dtype)


def _sink_rows(sink, rows_per_head):
    return jnp.repeat(sink.astype(F32).reshape(WA_KV, KV_GROUP), rows_per_head, axis=1)[:, :, None]


def _window(qw, kw, proj, kwc, proj_c, sink):
    t = qw.shape[0]
    n_ctx = kwc.shape[0]
    nb = t // BLOCK
    gw = KV_GROUP * HEAD_DIM
    vcol = COL_VW // HEAD_DIM
    kspec = lambda f: pl.BlockSpec((BLOCK, HEAD_DIM), f)
    return pl.pallas_call(
        functools.partial(_win_body, nb=nb), grid=(WA_KV, nb),
        in_specs=[pl.BlockSpec((BLOCK, gw), lambda h, i: (i, h)),
                  kspec(lambda h, i: (jnp.maximum(i - 1, 0), h)), kspec(lambda h, i: (i, h)),
                  kspec(lambda h, i: (jnp.minimum(i + 1, nb - 1), h)),
                  kspec(lambda h, i: (jnp.maximum(i - 1, 0), vcol + h)), kspec(lambda h, i: (i, vcol + h)),
                  kspec(lambda h, i: (jnp.minimum(i + 1, nb - 1), vcol + h)),
                  pl.BlockSpec((n_ctx, HEAD_DIM), lambda h, i: (0, h)),
                  pl.BlockSpec((n_ctx, HEAD_DIM), lambda h, i: (0, vcol + h)),
                  pl.BlockSpec((1, KV_GROUP * BLOCK, 1), lambda h, i: (h, 0, 0))],
        out_specs=pl.BlockSpec((BLOCK, gw), lambda h, i: (i, h)),
        out_shape=jax.ShapeDtypeStruct((t, WA_HEADS * HEAD_DIM), BF16),
        compiler_params=_cp(2, 24), name="window_attn")(
            qw, kw, kw, kw, proj, proj, proj, kwc, proj_c, _sink_rows(sink, BLOCK))


def _ctx_attn_body(q_ref, k_ref, v_ref, sink_ref, o_ref, *, n_ctx, use_sink):
    q = jnp.concatenate([q_ref[:, g * HEAD_DIM:(g + 1) * HEAD_DIM] for g in range(KV_GROUP)], axis=0)
    s = lax.dot_general(q, k_ref[...], _NT, preferred_element_type=F32)
    m = jnp.max(s, -1, keepdims=True)
    if use_sink:
        m = jnp.maximum(m, sink_ref[0])
    p = jnp.exp(s - m)
    l = jnp.sum(p, -1, keepdims=True)
    if use_sink:
        l = l + jnp.exp(sink_ref[0] - m)
    out = jnp.dot((p / l).astype(BF16), v_ref[...], preferred_element_type=F32)
    for g in range(KV_GROUP):
        o_ref[:, g * HEAD_DIM:(g + 1) * HEAD_DIM] = out[g * n_ctx:(g + 1) * n_ctx, :].astype(o_ref.dtype)


def _ctx_attn(q, k, proj_c, vcol0, sink, use_sink):
    n_ctx = q.shape[0]
    gw = KV_GROUP * HEAD_DIM
    vcol = vcol0 // HEAD_DIM
    return pl.pallas_call(
        functools.partial(_ctx_attn_body, n_ctx=n_ctx, use_sink=use_sink), grid=(GA_KV,),
        in_specs=[pl.BlockSpec((n_ctx, gw), lambda h: (0, h)),
                  pl.BlockSpec((n_ctx, HEAD_DIM), lambda h: (0, h)),
                  pl.BlockSpec((n_ctx, HEAD_DIM), lambda h: (0, vcol + h)),
                  pl.BlockSpec((1, KV_GROUP * n_ctx, 1), lambda h: (h, 0, 0))],
        out_specs=pl.BlockSpec((n_ctx, gw), lambda h: (0, h)),
        out_shape=jax.ShapeDtypeStruct((n_ctx, GA_HEADS * HEAD_DIM), BF16),
        compiler_params=_cp(1, 24), name="ctx_attn_sink" if use_sink else "ctx_attn")(
            q, k, proj_c, _sink_rows(sink, n_ctx))


def _sg_body(u_ref, v_ref, g_ref, b_ref, w_ref, bs_ref, o_ref, *, nch):
    for n in range(nch):
        rs = slice(n * CHUNK, (n + 1) * CHUNK)
        vn = _ln(v_ref[rs, :].astype(F32), g_ref[...], b_ref[...]).astype(BF16)
        for g in range(SG_GROUPS):
            cs = slice(g * SG_GROUP_CH, (g + 1) * SG_GROUP_CH)
            s = jnp.dot(w_ref[g], vn[:, cs], preferred_element_type=F32) + bs_ref[g]
            o_ref[rs, cs] = (u_ref[rs, cs].astype(F32) * s).astype(o_ref.dtype)


def _spatial_gate(proj, g_v, b_v, w_sp, b_sp):
    t = proj.shape[0]
    tm = min(512, t)
    full = lambda i: (0, 0)
    return pl.pallas_call(
        functools.partial(_sg_body, nch=tm // CHUNK), grid=(t // tm,),
        in_specs=[pl.BlockSpec((tm, SG_WIDTH), lambda i: (i, COL_DU // SG_WIDTH)),
                  pl.BlockSpec((tm, SG_WIDTH), lambda i: (i, COL_DV // SG_WIDTH)),
                  pl.BlockSpec((1, SG_WIDTH), full), pl.BlockSpec((1, SG_WIDTH), full),
                  pl.BlockSpec((SG_GROUPS, CHUNK, CHUNK), lambda i: (0, 0, 0)),
                  pl.BlockSpec((SG_GROUPS, CHUNK, 1), lambda i: (0, 0, 0))],
        out_specs=pl.BlockSpec((tm, SG_WIDTH), lambda i: (i, 0)),
        out_shape=jax.ShapeDtypeStruct((t, SG_WIDTH), BF16),
        compiler_params=_cp(1, 24), name="spatial_gate")(proj, proj, g_v, b_v, w_sp, b_sp)


def _merge_body(ya, yb, yc, yd, wa, wb, wc, wd, ga, gb, gc, gd, o_ref):
    def br(y, w, g):
        return jax.nn.sigmoid(g[...].astype(F32)) * jnp.dot(y[...], w[...], preferred_element_type=F32)
    o_ref[...] = (br(ya, wa, ga) + br(yb, wb, gb) + br(yc, wc, gc) + br(yd, wd, gd)).astype(o_ref.dtype)


def _merge(ys, ws, gates):
    t = ys[0].shape[0]
    tm, tn = min(512, t), 1024
    nn = D_MODEL // tn
    in_specs = [pl.BlockSpec((tm, y.shape[1]), lambda j, i: (i, 0)) for y in ys]
    in_specs += [pl.BlockSpec((w.shape[0], tn), lambda j, i: (0, j)) for w in ws]
    in_specs += [pl.BlockSpec((tm, tn), functools.partial(lambda j, i, b: (i, b * nn + j), b=b)) for b in range(N_BRANCH)]
    return pl.pallas_call(
        _merge_body, grid=(nn, t // tm), in_specs=in_specs,
        out_specs=pl.BlockSpec((tm, tn), lambda j, i: (i, j)),
        out_shape=jax.ShapeDtypeStruct((t, D_MODEL), BF16),
        compiler_params=_cp(2, 48), name="gated_merge")(*ys, *ws, gates, gates, gates, gates)


def _out_ln_body(mg_ref, w_ref, x_ref, g1_ref, lng_ref, lnb_ref, sc_ref, sh_ref, x1_ref, xmT_ref, *, alpha):
    o = jnp.dot(mg_ref[...], w_ref[...], preferred_element_type=F32)
    x1 = _ln(alpha * x_ref[...] + g1_ref[...] * o, lng_ref[...], lnb_ref[...])
    x1_ref[...] = x1
    xmT_ref[...] = (x1 * (1.0 + sc_ref[...]) + sh_ref[...]).T.astype(xmT_ref.dtype)


def _out_ln(merged, w_out, x, g1, lng, lnb, sc2, sh2, alpha):
    t, d = x.shape
    tm = min(256, t)
    row = lambda i: (i, 0)
    vec = pl.BlockSpec((1, d), lambda i: (0, 0))
    return pl.pallas_call(
        functools.partial(_out_ln_body, alpha=alpha), grid=(t // tm,),
        in_specs=[pl.BlockSpec((tm, d), row), pl.BlockSpec((d, d), lambda i: (0, 0)), pl.BlockSpec((tm, d), row),
                  vec, vec, vec, vec, vec],
        out_specs=[pl.BlockSpec((tm, d), row), pl.BlockSpec((d, tm), lambda i: (0, i))],
        out_shape=[jax.ShapeDtypeStruct((t, d), F32), jax.ShapeDtypeStruct((d, t), BF16)],
        compiler_params=_cp(1, 48), name="out_proj_ln")(merged, w_out, x, g1, lng, lnb, sc2, sh2)


def _peer_topk_body(q_ref, k_ref, s1_o, a1_o, s2_o, e2_o, tau_o):
    q = q_ref[...].astype(BF16)
    s1 = jnp.dot(k_ref[0], q[:PEER_HALF], preferred_element_type=F32)
    s2 = jnp.dot(k_ref[1], q[PEER_HALF:], preferred_element_type=F32)
    ninf = -jnp.inf

    def top_distinct(s):
        vals, tops, cnts = s, [], []
        for _ in range(PEER_TOPK):
            m = jnp.max(vals, axis=0, keepdims=True)
            eq = vals == m
            tops.append(m)
            cnts.append(jnp.sum(eq.astype(F32), axis=0, keepdims=True))
            vals = jnp.where(eq, ninf, vals)
        return jnp.concatenate(tops, 0), jnp.concatenate(cnts, 0)

    ta, na = top_distinct(s1)
    tb, nb = top_distinct(s2)
    ea, eb = jnp.exp(ta - ta[0:1]), jnp.exp(tb - tb[0:1])
    cand = jnp.concatenate([ta[k:k + 1] + tb for k in range(PEER_TOPK)], 0)
    mult = jnp.concatenate([na[k:k + 1] * nb for k in range(PEER_TOPK)], 0)
    ee = jnp.concatenate([ea[k:k + 1] * eb for k in range(PEER_TOPK)], 0)
    vals = cand
    cnt = jnp.zeros_like(cand[0:1])
    tau = jnp.full_like(cand[0:1], ninf)
    for _ in range(PEER_TOPK):
        m = jnp.max(vals, axis=0, keepdims=True)
        eq = vals == m
        tau = jnp.where(cnt < PEER_TOPK, m, tau)
        cnt = cnt + jnp.sum(jnp.where(eq, mult, 0.0), axis=0, keepdims=True)
        vals = jnp.where(eq, ninf, vals)
    z = jnp.sum(jnp.where(cand >= tau, mult * ee, 0.0), axis=0, keepdims=True)
    s1_o[...] = s1
    a1_o[...] = jnp.exp(s1 - ta[0:1]) * (1.0 / z)
    s2_o[...] = s2
    e2_o[...] = jnp.exp(s2 - tb[0:1])
    tau_o[...] = tau


def _peer_topk(qT, keys):
    t = qT.shape[1]
    tt = min(256, t)
    blk = pl.BlockSpec((None, PEER_NKEYS, tt), lambda j, h: (h, 0, j))
    shp = jax.ShapeDtypeStruct((PEER_HEADS, PEER_NKEYS, t), F32)
    return pl.pallas_call(
        _peer_topk_body, grid=(t // tt, PEER_HEADS),
        in_specs=[pl.BlockSpec((PEER_QDIM, tt), lambda j, h: (h, j)),
                  pl.BlockSpec((None, 2, PEER_NKEYS, PEER_HALF), lambda j, h: (h, 0, 0, 0))],
        out_specs=[blk, blk, blk, blk, pl.BlockSpec((None, 1, tt), lambda j, h: (h, 0, j))],
        out_shape=[shp, shp, shp, shp, jax.ShapeDtypeStruct((PEER_HEADS, 1, t), F32)],
        compiler_params=_cp(2, 32), name="peer_topk")(qT, keys)


def _peer_dense_body(xT_ref, u_ref, vT_ref, s1_ref, a1_ref, s2_ref, e2_ref, tau_ref, o_ref, aw, *, ek):
    k = pl.program_id(1)

    @pl.when(k == 0)
    def _():
        o_ref[...] = jnp.zeros(o_ref.shape, F32)

    hT = jnp.dot(u_ref[...], xT_ref[...], preferred_element_type=F32)
    nsub = ek // PEER_NKEYS
    for ii in range(nsub):
        i1 = k * nsub + ii
        w = None
        for h in range(PEER_HEADS):
            c = s2_ref[h] + s1_ref[h, pl.ds(i1, 1), :]
            wh = jnp.where(c >= tau_ref[h], e2_ref[h] * a1_ref[h, pl.ds(i1, 1), :], 0.0)
            w = wh if w is None else w + wh
        rs = slice(ii * PEER_NKEYS, (ii + 1) * PEER_NKEYS)
        aw[rs, :] = (_gelu(hT[rs, :]) * w).astype(BF16)
    o_ref[...] += jnp.dot(vT_ref[...], aw[...], preferred_element_type=F32)


def _peer_dense(xmT, u_bf, vT_bf, tk):
    d, t = xmT.shape
    tt = min(512, t)
    ek = 512
    s1, a1, s2, e2, tau = tk
    sblk = pl.BlockSpec((PEER_HEADS, PEER_NKEYS, tt), lambda j, k: (0, 0, j))
    return pl.pallas_call(
        functools.partial(_peer_dense_body, ek=ek), grid=(t // tt, PEER_EXPERTS // ek),
        in_specs=[pl.BlockSpec((d, tt), lambda j, k: (0, j)),
                  pl.BlockSpec((ek, d), lambda j, k: (k, 0)),
                  pl.BlockSpec((d, ek), lambda j, k: (0, k)),
                  sblk, sblk, sblk, sblk,
                  pl.BlockSpec((PEER_HEADS, 1, tt), lambda j, k: (0, 0, j))],
        out_specs=pl.BlockSpec((d, tt), lambda j, k: (0, j)),
        out_shape=jax.ShapeDtypeStruct((d, t), F32),
        scratch_shapes=[pltpu.VMEM((ek, tt), BF16)],
        compiler_params=_cp(2, 48), name="peer_dense")(xmT, u_bf, vT_bf, s1, a1, s2, e2, tau)


def _ffn_ln_body(fT_ref, x1_ref, g2_ref, lng_ref, lnb_ref, scn_ref, shn_ref, x2_ref, *rest, alpha):
    x2 = _ln(alpha * x1_ref[...] + g2_ref[...] * fT_ref[...].T, lng_ref[...], lnb_ref[...])
    x2_ref[...] = x2
    if rest:
        rest[0][...] = (x2 * (1.0 + scn_ref[...]) + shn_ref[...]).astype(rest[0].dtype)


def _ffn_ln(fT, x1, g2, lng, lnb, scn, shn, alpha, emit_h):
    t, d = x1.shape
    tm = min(256, t)
    row = pl.BlockSpec((tm, d), lambda i: (i, 0))
    vec = pl.BlockSpec((1, d), lambda i: (0, 0))
    out_specs = [row, row] if emit_h else [row]
    out_shape = [jax.ShapeDtypeStruct((t, d), F32)] + ([jax.ShapeDtypeStruct((t, d), BF16)] if emit_h else [])
    return pl.pallas_call(
        functools.partial(_ffn_ln_body, alpha=alpha), grid=(t // tm,),
        in_specs=[pl.BlockSpec((d, tm), lambda i: (0, i)), row, vec, vec, vec, vec, vec],
        out_specs=out_specs, out_shape=out_shape,
        compiler_params=_cp(1, 32), name="ffn_ln")(fT, x1, g2, lng, lnb, scn, shn)


def kernel(x, c, ctx, c_ctx, w_ada, b_ada, w_in, b_in, s5_lam_re, s5_lam_im, s5_log_dt, s5_b_re, s5_b_im, s5_c_re, s5_c_im, s5_d, w_glu, b_glu, qn_gain, kn_gain, sink, sg_ln_g, sg_ln_b, w_sp, b_sp, w_br_a, w_br_b, w_br_c, w_br_d, w_out, ln1_g, ln1_b, ln2_g, ln2_b, w_pq, peer_keys, peer_u, peer_v):
    depth = w_in.shape[0]
    bsz, n_lat, d = x.shape
    assert bsz == 1 and d == D_MODEL and n_lat % 512 == 0 and ctx.shape[1] % BLOCK == 0
    n_ctx = ctx.shape[1]
    alpha = (2 * depth) ** 0.25
    x, ctx = x[0], ctx[0]

    row8 = lax.broadcasted_iota(jnp.int32, (8, d), 0)
    cond8 = jnp.where(row8 == 0, c[0][None], jnp.where(row8 == 1, c_ctx[None], 0.0))
    ada = [_ada(cond8, w_ada, b_ada, l) for l in range(depth)]
    cosf, sinf = _rope_tables(n_lat)
    cos1, sin0 = jnp.ones((n_ctx, HEAD_DIM), F32), jnp.zeros((n_ctx, HEAD_DIM), F32)
    vec = lambda a: a.reshape(1, -1).astype(F32)

    ref_off = {'a': 0, 'qg': 768, 'kg': 1792, 'vg': 2048, 'qw': 2304, 'kw': 3328, 'vw': 3584, 'du': 3840, 'dv': 4608}
    ref_w = {'a': 768, 'qg': 1024, 'kg': 256, 'vg': 256, 'qw': 1024, 'kw': 256, 'vw': 256, 'du': 768, 'dv': 768}
    order = ['a', 'kg', 'qg', 'qw', 'kw', 'vg', 'vw', 'du', 'dv']
    perm = lambda a: jnp.concatenate([a[..., ref_off[n]:ref_off[n] + ref_w[n]] for n in order], -1)
    perm_idx = np.concatenate([np.arange(ref_off[n], ref_off[n] + ref_w[n]) for n in order]).astype(np.int32)

    h_lat = None
    for l in range(depth):
        need_ctx = l < depth - 1
        mods = [[ada[l][r:r + 1, i * d:(i + 1) * d] for i in range(6)] for r in range(2)]
        (sh1, sc1, g1, sh2, sc2, g2), (sh1c, sc1c, g1c, sh2c, sc2c, g2c) = mods
        if need_ctx:
            nxt = [ada[l + 1][r:r + 1, 0:2 * d] for r in range(2)]
            (shn, scn), (shnc, scnc) = [(m[:, :d], m[:, d:]) for m in nxt]
        else:
            shn = scn = shnc = scnc = jnp.zeros((1, d), F32)

        w_small, b_small = perm(w_in[l][:, :SMALL_W]).astype(BF16), vec(jnp.take(b_in[l], perm_idx))
        w_gate, b_gate = w_in[l][:, SMALL_W:].astype(BF16), vec(b_in[l][SMALL_W:])
        s5m = [_s5_mats(s5_lam_re[l, dr], s5_lam_im[l, dr], s5_log_dt[l, dr], s5_b_re[l, dr], s5_b_im[l, dr],
                        s5_c_re[l, dr], s5_c_im[l, dr]) for dr in range(2)]
        wg, bg, s5d = w_glu[l].astype(BF16), vec(b_glu[l]), vec(s5_d[l])
        qn, kn = vec(qn_gain[l]), vec(kn_gain[l])
        wsp, bsp = w_sp[l].astype(BF16), b_sp[l].astype(F32)[:, :, None]
        w_brs = [w[l].astype(BF16) for w in (w_br_a, w_br_b, w_br_c, w_br_d)]
        wo = w_out[l].astype(BF16)
        wpqT = w_pq[l].T.astype(BF16)
        keys = peer_keys[l].astype(BF16)
        u_bf, vT_bf = peer_u[l].astype(BF16), peer_v[l].T.astype(BF16)

        if h_lat is None:
            h_lat, h_ctx = _modulate(x, sc1, sh1), _modulate(ctx, sc1c, sh1c)

        proj = _matmul(h_lat, w_small, b_small, tm=512, tn=1792, out_dtype=BF16, name="in_proj")
        gates = _matmul(h_lat, w_gate, b_gate, tm=512, tn=2048, out_dtype=BF16, name="in_proj_gates")
        proj_c = _matmul(h_ctx, w_small, b_small, tm=512, tn=1792, out_dtype=BF16, name="in_proj_ctx")

        y_dir, y_dir_c = [], []
        for dr in range(2):
            yc_, hc_end = _s5_dir(proj_c, s5m[dr], jnp.zeros((2, S5_CH), F32), bool(dr))
            yl_, _ = _s5_dir(proj, s5m[dr], hc_end, bool(dr))
            y_dir.append(yl_)
            y_dir_c.append(yc_)
        y_a = _s5_out(proj, y_dir[0], y_dir[1], s5d, wg, bg)

        qg, kg, vge, qw, kw = _prep(proj, cosf, sinf, qn, kn, ATTN_SCALE * math.log2(math.e))
        qgc, kgc, vgce, qwc, kwc = _prep(proj_c, cos1, sin0, qn, kn, ATTN_SCALE)
        y_b = _flash(qg, kg, vge, kgc, vgce)
        y_c = _window(qw, kw, proj, kwc, proj_c, sink[l])

        y_d = _spatial_gate(proj, vec(sg_ln_g[l]), vec(sg_ln_b[l]), wsp, bsp)

        merged = _merge((y_a, y_b, y_c, y_d), w_brs, gates)
        x1, xmT = _out_ln(merged, wo, x, g1, vec(ln1_g[l]), vec(ln1_b[l]), sc2, sh2, alpha)
        qT = _matmul(wpqT, xmT, None, tm=D_MODEL, tn=1024, out_dtype=F32, name="peer_query")
        ffnT = _peer_dense(xmT, u_bf, vT_bf, _peer_topk(qT, keys))
        res = _ffn_ln(ffnT, x1, g2, vec(ln2_g[l]), vec(ln2_b[l]), scn, shn, alpha, need_ctx)
        x = res[0]

        if need_ctx:
            h_lat = res[1]
            gates_c = _matmul(h_ctx, w_gate, b_gate, tm=512, tn=2048, out_dtype=BF16, name="in_proj_gates_ctx")
            y_a_c = _s5_out(proj_c, y_dir_c[0], y_dir_c[1], s5d, wg, bg)
            y_b_c = _ctx_attn(qgc, kgc, proj_c, COL_VG, sink[l], False)
            y_c_c = _ctx_attn(qwc, kwc, proj_c, COL_VW, sink[l], True)
            y_d_c = _spatial_gate(proj_c, vec(sg_ln_g[l]), vec(sg_ln_b[l]), wsp, bsp)
            merged_c = _merge((y_a_c, y_b_c, y_c_c, y_d_c), w_brs, gates_c)
            c1, cmT = _out_ln(merged_c, wo, ctx, g1c, vec(ln1_g[l]), vec(ln1_b[l]), sc2c, sh2c, alpha)
            qTc = _matmul(wpqT, cmT, None, tm=D_MODEL, tn=1024, out_dtype=F32, name="peer_query_ctx")
            ffnTc = _peer_dense(cmT, u_bf, vT_bf, _peer_topk(qTc, keys))
            ctx, h_ctx = _ffn_ln(ffnTc, c1, g2c, vec(ln2_g[l]), vec(ln2_b[l]), scnc, shnc, alpha, True)
    return x[None]
```

```python
import functools
import math

import jax
import jax.numpy as jnp
import numpy as np
from jax import lax
from jax.experimental import pallas as pl
from jax.experimental.pallas import tpu as pltpu

F32, BF16 = jnp.float32, jnp.bfloat16

D_MODEL = 2048
GRID_W = 64
BLOCK = 128
HEAD_DIM = 128
AXIS_DIM = HEAD_DIM // 2
ROPE_THETA = 10000.0
ATTN_SCALE = HEAD_DIM ** -0.5
NEG_INF = -1e30
EPS = 1e-6
S5_GROUPS, S5_GROUP_CH, S5_STATE = 48, 16, 64
S5_WIDTH = S5_GROUPS * S5_GROUP_CH
S5_CH = S5_GROUPS * S5_STATE
S5_GB = 16
S5_NB = S5_GROUPS // S5_GB
S5_KB = S5_GB * S5_GROUP_CH
S5_PB = S5_GB * S5_STATE
GA_HEADS, GA_KV = 8, 2
WA_HEADS, WA_KV = 8, 2
KV_GROUP = GA_HEADS // GA_KV
SG_GROUPS, SG_GROUP_CH = 6, 128
SG_WIDTH = SG_GROUPS * SG_GROUP_CH
CHUNK = 128
N_BRANCH = 4
PEER_HEADS, PEER_NKEYS, PEER_QDIM, PEER_TOPK = 8, 128, 256, 16
PEER_HALF = PEER_QDIM // 2
PEER_EXPERTS = PEER_NKEYS * PEER_NKEYS

COL_A, COL_KG, COL_QG, COL_QW, COL_KW, COL_VG, COL_VW, COL_DU, COL_DV = 0, 768, 1024, 2048, 3072, 3328, 3584, 3840, 4608
SMALL_W = 5376
GATE_W = N_BRANCH * D_MODEL

VMEM_CAP_MB = 56


def _cp(ndims, vmem_mb=32):
    return pltpu.CompilerParams(dimension_semantics=("arbitrary",) * ndims,
                                vmem_limit_bytes=min(vmem_mb, VMEM_CAP_MB) << 20)


def _gelu(y):
    return 0.5 * y * (1.0 + lax.erf(y * (2.0 ** -0.5)))


def _ln(y, g, b):
    mu = jnp.mean(y, -1, keepdims=True)
    yc = y - mu
    var = jnp.mean(yc * yc, -1, keepdims=True)
    return yc * lax.rsqrt(var + EPS) * g + b


_NT = (((1,), (1,)), ((), ()))


def _ada_body(c_ref, w_ref, b_ref, o_ref):
    cnd = c_ref[...]
    s = (cnd * jax.nn.sigmoid(cnd)).astype(BF16)
    o_ref[...] = jnp.dot(s, w_ref[...].astype(BF16), preferred_element_type=F32) + b_ref[...]


def _ada(cond8, w_ada, b_ada, l):
    depth, d, n = w_ada.shape
    tn = 1024
    return pl.pallas_call(
        _ada_body, grid=(n // tn,),
        in_specs=[pl.BlockSpec((8, d), lambda j: (0, 0)),
                  pl.BlockSpec((None, d, tn), lambda j: (l, 0, j)),
                  pl.BlockSpec((None, 1, tn), lambda j: (l, 0, j))],
        out_specs=pl.BlockSpec((8, tn), lambda j: (0, j)),
        out_shape=jax.ShapeDtypeStruct((8, n), F32),
        compiler_params=_cp(1, 32), name="ada_mod")(cond8, w_ada, b_ada.reshape(depth, 1, n))


def _mod_body(x_ref, sc_ref, sh_ref, o_ref):
    o_ref[...] = (x_ref[...] * (1.0 + sc_ref[...]) + sh_ref[...]).astype(o_ref.dtype)


def _modulate(x, sc, sh):
    t, d = x.shape
    tm = min(512, t)
    return pl.pallas_call(
        _mod_body, grid=(t // tm,),
        in_specs=[pl.BlockSpec((tm, d), lambda i: (i, 0)),
                  pl.BlockSpec((1, d), lambda i: (0, 0)),
                  pl.BlockSpec((1, d), lambda i: (0, 0))],
        out_specs=pl.BlockSpec((tm, d), lambda i: (i, 0)),
        out_shape=jax.ShapeDtypeStruct((t, d), BF16),
        compiler_params=_cp(1, 24), name="modulate")(x, sc, sh)


def _mm_body(a_ref, w_ref, b_ref, o_ref):
    acc = jnp.dot(a_ref[...], w_ref[...], preferred_element_type=F32)
    o_ref[...] = (acc + b_ref[...]).astype(o_ref.dtype)


def _mm_nobias_body(a_ref, w_ref, o_ref):
    o_ref[...] = jnp.dot(a_ref[...], w_ref[...], preferred_element_type=F32).astype(o_ref.dtype)


def _matmul(a, w, b, *, tm, tn, out_dtype, name):
    m, k = a.shape
    n = w.shape[1]
    tm, tn = min(tm, m), min(tn, n)
    in_specs = [pl.BlockSpec((tm, k), lambda j, i: (i, 0)),
                pl.BlockSpec((k, tn), lambda j, i: (0, j))]
    args = [a, w]
    body = _mm_nobias_body
    if b is not None:
        in_specs.append(pl.BlockSpec((1, tn), lambda j, i: (0, j)))
        args.append(b)
        body = _mm_body
    osz = jnp.dtype(out_dtype).itemsize
    vmem = 2 * (tm * k * 2 + k * tn * 2 + tm * tn * osz) + tm * tn * 4
    return pl.pallas_call(
        body, grid=(n // tn, m // tm), in_specs=in_specs,
        out_specs=pl.BlockSpec((tm, tn), lambda j, i: (i, j)),
        out_shape=jax.ShapeDtypeStruct((m, n), out_dtype),
        compiler_params=_cp(2, (vmem >> 20) + 8), name=name)(*args)


def _s5_body(u_ref, bre_ref, bim_ref, cre_ref, cim_ref, a_ref, h0_ref, y_ref, hT_ref, xr, xi, carry, *, tc, rev):
    @pl.when(pl.program_id(0) == 0)
    def _():
        carry[...] = h0_ref[...]

    for j in range(S5_NB):
        uj = u_ref[:, j * S5_KB:(j + 1) * S5_KB]
        xr[:, j * S5_PB:(j + 1) * S5_PB] = jnp.dot(uj, bre_ref[j], preferred_element_type=F32)
        xi[:, j * S5_PB:(j + 1) * S5_PB] = jnp.dot(uj, bim_ref[j], preferred_element_type=F32)

    for j in range(S5_NB):
        cs = slice(j * S5_PB, (j + 1) * S5_PB)
        ar, ai = a_ref[0:1, cs], a_ref[1:2, cs]

        def step(k, c, cs=cs, ar=ar, ai=ai):
            hr, hi = c
            t = (tc - 1 - k) if rev else k
            nr = ar * hr - ai * hi + xr[pl.ds(t, 1), cs]
            ni = ar * hi + ai * hr + xi[pl.ds(t, 1), cs]
            xr[pl.ds(t, 1), cs] = nr
            xi[pl.ds(t, 1), cs] = ni
            return nr, ni

        hr, hi = lax.fori_loop(0, tc, step, (carry[0:1, cs], carry[1:2, cs]), unroll=8)
        carry[0:1, cs] = hr
        carry[1:2, cs] = hi

    for j in range(S5_NB):
        cs = slice(j * S5_PB, (j + 1) * S5_PB)
        y_ref[:, j * S5_KB:(j + 1) * S5_KB] = (
            jnp.dot(xr[:, cs].astype(BF16), cre_ref[j], preferred_element_type=F32)
            + jnp.dot(xi[:, cs].astype(BF16), cim_ref[j], preferred_element_type=F32))
    hT_ref[...] = carry[...]


def _s5_dir(proj, mats, h0, rev):
    t = proj.shape[0]
    tc = min(256, t)
    nt = t // tc
    bre, bim, cre, cim, a = mats
    tmap = (lambda i: (nt - 1 - i, 0)) if rev else (lambda i: (i, 0))
    full3 = lambda i: (0, 0, 0)
    full2 = lambda i: (0, 0)
    return pl.pallas_call(
        functools.partial(_s5_body, tc=tc, rev=rev), grid=(nt,),
        in_specs=[pl.BlockSpec((tc, S5_WIDTH), tmap),
                  pl.BlockSpec((S5_NB, S5_KB, S5_PB), full3), pl.BlockSpec((S5_NB, S5_KB, S5_PB), full3),
                  pl.BlockSpec((S5_NB, S5_PB, S5_KB), full3), pl.BlockSpec((S5_NB, S5_PB, S5_KB), full3),
                  pl.BlockSpec((2, S5_CH), full2), pl.BlockSpec((2, S5_CH), full2)],
        out_specs=[pl.BlockSpec((tc, S5_WIDTH), tmap), pl.BlockSpec((2, S5_CH), full2)],
        out_shape=[jax.ShapeDtypeStruct((t, S5_WIDTH), F32), jax.ShapeDtypeStruct((2, S5_CH), F32)],
        scratch_shapes=[pltpu.VMEM((tc, S5_CH), F32), pltpu.VMEM((tc, S5_CH), F32), pltpu.VMEM((2, S5_CH), F32)],
        compiler_params=_cp(1, 40), name="s5_scan_bwd" if rev else "s5_scan_fwd")(proj, bre, bim, cre, cim, a, h0)


def _s5_out_body(u_ref, yf_ref, yb_ref, d_ref, wg_ref, bg_ref, o_ref):
    y = u_ref[...].astype(F32) * d_ref[...] + yf_ref[...] + yb_ref[...]
    g = _gelu(y)
    z = jnp.dot(g.astype(BF16), wg_ref[...], preferred_element_type=F32) + bg_ref[...]
    o_ref[...] = (g * jax.nn.sigmoid(z)).astype(o_ref.dtype)


def _s5_out(proj, yf, yb, d, wg, bg):
    t = proj.shape[0]
    tm = min(512, t)
    row = lambda i: (i, 0)
    full = lambda i: (0, 0)
    return pl.pallas_call(
        _s5_out_body, grid=(t // tm,),
        in_specs=[pl.BlockSpec((tm, S5_WIDTH), row), pl.BlockSpec((tm, S5_WIDTH), row), pl.BlockSpec((tm, S5_WIDTH), row),
                  pl.BlockSpec((1, S5_WIDTH), full), pl.BlockSpec((S5_WIDTH, S5_WIDTH), full), pl.BlockSpec((1, S5_WIDTH), full)],
        out_specs=pl.BlockSpec((tm, S5_WIDTH), row),
        out_shape=jax.ShapeDtypeStruct((t, S5_WIDTH), BF16),
        compiler_params=_cp(1, 24), name="s5_glu")(proj, yf, yb, d, wg, bg)


def _s5_mats(lam_re, lam_im, log_dt, b_re, b_im, c_re, c_im):
    dt = jnp.exp(log_dt)[:, None]
    mag = jnp.exp(lam_re * dt)
    ar, ai = mag * jnp.cos(lam_im * dt), mag * jnp.sin(lam_im * dt)
    nr, ni = ar - 1.0, ai
    den = lam_re * lam_re + lam_im * lam_im
    cr = (nr * lam_re + ni * lam_im) / den
    ci = (ni * lam_re - nr * lam_im) / den
    fre = cr[:, :, None] * b_re - ci[:, :, None] * b_im
    fim = cr[:, :, None] * b_im + ci[:, :, None] * b_re
    eye = jnp.eye(S5_GB, dtype=F32)

    def blk_in(f):
        f = f.reshape(S5_NB, S5_GB, S5_STATE, S5_GROUP_CH)
        m = jnp.einsum('jgph,gk->jghkp', f, eye)
        return m.reshape(S5_NB, S5_KB, S5_PB).astype(BF16)

    def blk_out(c):
        c = c.reshape(S5_NB, S5_GB, S5_GROUP_CH, S5_STATE)
        m = jnp.einsum('jghp,gk->jgpkh', c, eye)
        return m.reshape(S5_NB, S5_PB, S5_KB).astype(BF16)

    is_re = lax.broadcasted_iota(jnp.int32, (2, S5_CH), 0) == 0
    a = jnp.where(is_re, ar.reshape(1, -1), ai.reshape(1, -1))
    return blk_in(fre), blk_in(fim), blk_out(c_re), blk_out(-c_im), a


def _prep_body(qg_ref, kg_ref, vg_ref, qw_ref, kw_ref, cos_ref, sin_ref, qn_ref, kn_ref, oqg, okg, ovg, oqw, okw, *,
               qg_scale):
    cosf, sinf = cos_ref[...], sin_ref[...]
    lane = lax.broadcasted_iota(jnp.int32, cosf.shape, 1)
    low = (lane % AXIS_DIM) < (AXIS_DIM // 2)

    def rope(t):
        partner = jnp.where(low, pltpu.roll(t, HEAD_DIM - AXIS_DIM // 2, 1), pltpu.roll(t, AXIS_DIM // 2, 1))
        return t * cosf + partner * sinf

    def rms(t, g):
        return t * lax.rsqrt(jnp.mean(t * t, -1, keepdims=True) + EPS) * g

    for h in range(GA_HEADS):
        cs = slice(h * HEAD_DIM, (h + 1) * HEAD_DIM)
        oqg[:, cs] = (rope(rms(qg_ref[:, cs].astype(F32), qn_ref[...])) * qg_scale).astype(BF16)
        oqw[:, cs] = (rope(qw_ref[:, cs].astype(F32)) * ATTN_SCALE).astype(BF16)
    for h in range(GA_KV):
        cs = slice(h * HEAD_DIM, (h + 1) * HEAD_DIM)
        okg[:, cs] = rope(rms(kg_ref[:, cs].astype(F32), kn_ref[...])).astype(BF16)
        okw[:, cs] = rope(kw_ref[:, cs].astype(F32)).astype(BF16)
        ovg[:, 2 * h * HEAD_DIM:(2 * h + 1) * HEAD_DIM] = vg_ref[:, cs]
        ovg[:, (2 * h + 1) * HEAD_DIM:(2 * h + 2) * HEAD_DIM] = jnp.ones((vg_ref.shape[0], HEAD_DIM), BF16)


def _prep(proj, cosf, sinf, qn, kn, qg_scale):
    t = proj.shape[0]
    tr = min(256, t)
    qw_, kw_ = GA_HEADS * HEAD_DIM, GA_KV * HEAD_DIM
    row = lambda i: (i, 0)
    full = lambda i: (0, 0)
    return pl.pallas_call(
        functools.partial(_prep_body, qg_scale=qg_scale), grid=(t // tr,),
        in_specs=[pl.BlockSpec((tr, qw_), lambda i: (i, COL_QG // qw_)),
                  pl.BlockSpec((tr, kw_), lambda i: (i, COL_KG // kw_)),
                  pl.BlockSpec((tr, kw_), lambda i: (i, COL_VG // kw_)),
                  pl.BlockSpec((tr, qw_), lambda i: (i, COL_QW // qw_)),
                  pl.BlockSpec((tr, kw_), lambda i: (i, COL_KW // kw_)),
                  pl.BlockSpec((tr, HEAD_DIM), row), pl.BlockSpec((tr, HEAD_DIM), row),
                  pl.BlockSpec((1, HEAD_DIM), full), pl.BlockSpec((1, HEAD_DIM), full)],
        out_specs=[pl.BlockSpec((tr, qw_), row), pl.BlockSpec((tr, kw_), row), pl.BlockSpec((tr, 2 * kw_), row),
                   pl.BlockSpec((tr, qw_), row), pl.BlockSpec((tr, kw_), row)],
        out_shape=[jax.ShapeDtypeStruct((t, qw_), BF16), jax.ShapeDtypeStruct((t, kw_), BF16),
                   jax.ShapeDtypeStruct((t, 2 * kw_), BF16),
                   jax.ShapeDtypeStruct((t, qw_), BF16), jax.ShapeDtypeStruct((t, kw_), BF16)],
        compiler_params=_cp(1, 24), name="qk_prep")(proj, proj, proj, proj, proj, cosf, sinf, qn, kn)


def _rope_tables(n_lat):
    rows = n_lat // GRID_W
    row = jnp.repeat(jnp.arange(rows, dtype=F32), GRID_W)
    col = jnp.tile(jnp.arange(GRID_W, dtype=F32), rows)
    inv = jnp.power(ROPE_THETA, -jnp.arange(0, AXIS_DIM, 2, dtype=F32) / AXIS_DIM)
    ang_r, ang_c = row[:, None] * inv, col[:, None] * inv
    cr, sr, cc, sc = jnp.cos(ang_r), jnp.sin(ang_r), jnp.cos(ang_c), jnp.sin(ang_c)
    return jnp.concatenate([cr, cr, cc, cc], -1), jnp.concatenate([-sr, sr, -sc, sc], -1)


def _flash_body(q_ref, kc_ref, vc_ref, k_ref, v_ref, o_ref, qs, m_s, acc, *, tq, tk, nk):
    for g in range(KV_GROUP):
        qs[g * tq:(g + 1) * tq, :] = q_ref[:, g * HEAD_DIM:(g + 1) * HEAD_DIM]
    m_s[...] = jnp.full(m_s.shape, -jnp.inf, F32)
    acc[...] = jnp.zeros(acc.shape, F32)

    def update(k, v):
        for g in range(KV_GROUP):
            rs = slice(g * tq, (g + 1) * tq)
            s = lax.dot_general(qs[rs, :], k, _NT, preferred_element_type=F32)
            m_prev = m_s[rs, :]
            m_next = jnp.maximum(m_prev, jnp.max(s, axis=1, keepdims=True))
            alpha = jnp.exp2(m_prev - m_next)
            p = jnp.exp2(s - jnp.concatenate([m_next] * (s.shape[1] // HEAD_DIM), axis=1))
            acc[rs, :] = (jnp.concatenate([alpha, alpha], axis=1) * acc[rs, :]
                          + jnp.dot(p.astype(BF16), v, preferred_element_type=F32))
            m_s[rs, :] = m_next

    update(kc_ref[...], vc_ref[...])

    def body(j, carry):
        rows = pl.ds(pl.multiple_of(j * tk, tk), tk)
        update(k_ref[rows, :], v_ref[rows, :])
        return carry

    lax.fori_loop(0, nk, body, 0)
    out = acc[:, :HEAD_DIM] / acc[:, HEAD_DIM:]
    for g in range(KV_GROUP):
        o_ref[:, g * HEAD_DIM:(g + 1) * HEAD_DIM] = out[g * tq:(g + 1) * tq, :].astype(o_ref.dtype)


def _flash(q, kg, vge, kgc, vgce):
    t = q.shape[0]
    n_ctx = kgc.shape[0]
    tq = min(256, t)
    tk = min(512, t)
    gw = KV_GROUP * HEAD_DIM
    head = lambda h, i: (0, h)
    return pl.pallas_call(
        functools.partial(_flash_body, tq=tq, tk=tk, nk=t // tk), grid=(GA_KV, t // tq),
        in_specs=[pl.BlockSpec((tq, gw), lambda h, i: (i, h)),
                  pl.BlockSpec((n_ctx, HEAD_DIM), head), pl.BlockSpec((n_ctx, 2 * HEAD_DIM), head),
                  pl.BlockSpec((t, HEAD_DIM), head), pl.BlockSpec((t, 2 * HEAD_DIM), head)],
        out_specs=pl.BlockSpec((tq, gw), lambda h, i: (i, h)),
        out_shape=jax.ShapeDtypeStruct((t, GA_HEADS * HEAD_DIM), BF16),
        scratch_shapes=[pltpu.VMEM((KV_GROUP * tq, HEAD_DIM), BF16), pltpu.VMEM((KV_GROUP * tq, HEAD_DIM), F32),
                        pltpu.VMEM((KV_GROUP * tq, 2 * HEAD_DIM), F32)],
        compiler_params=_cp(2, 48), name="global_attn")(q, kgc, vgce, kg, vge)


def _win_body(q_ref, km_ref, k0_ref, kp_ref, vm_ref, v0_ref, vp_ref, kc_ref, vc_ref, sink_ref, o_ref, *, nb):
    i = pl.program_id(1)
    q = jnp.concatenate([q_ref[:, g * HEAD_DIM:(g + 1) * HEAD_DIM] for g in range(KV_GROUP)], axis=0)
    kw = jnp.concatenate([km_ref[...], k0_ref[...], kp_ref[...]], axis=0)
    vw = jnp.concatenate([vm_ref[...], v0_ref[...], vp_ref[...]], axis=0)
    s_w = lax.dot_general(q, kw, _NT, preferred_element_type=F32)
    r = lax.broadcasted_iota(jnp.int32, s_w.shape, 0) % BLOCK
    c = lax.broadcasted_iota(jnp.int32, s_w.shape, 1)
    cc = c % BLOCK
    iv = jnp.full(s_w.shape, i, jnp.int32)
    valid = (((c < BLOCK) & (cc >= r) & (iv >= 1)) | ((c >= BLOCK) & (c < 2 * BLOCK))
             | ((c >= 2 * BLOCK) & (cc <= r) & (iv + 1 < nb)))
    s_w = jnp.where(valid, s_w, NEG_INF)
    s_c = lax.dot_general(q, kc_ref[...], _NT, preferred_element_type=F32)
    sink = sink_ref[0]
    m = jnp.maximum(jnp.maximum(jnp.max(s_w, -1, keepdims=True), jnp.max(s_c, -1, keepdims=True)), sink)
    pw, pc = jnp.exp(s_w - m), jnp.exp(s_c - m)
    l = jnp.sum(pw, -1, keepdims=True) + jnp.sum(pc, -1, keepdims=True) + jnp.exp(sink - m)
    inv = 1.0 / l
    out = (jnp.dot((pc * inv).astype(BF16), vc_ref[...], preferred_element_type=F32)
           + jnp.dot((pw * inv).astype(BF16), vw, preferred_element_type=F32))
    for g in range(KV_GROUP):
        o_ref[:, g * HEAD_DIM:(g + 1) * HEAD_DIM] = out[g * BLOCK:(g + 1) * BLOCK, :].astype(o_ref.dtype)


def _sink_rows(sink, rows_per_head):
    return jnp.repeat(sink.astype(F32).reshape(WA_KV, KV_GROUP), rows_per_head, axis=1)[:, :, None]


def _window(qw, kw, proj, kwc, proj_c, sink):
    t = qw.shape[0]
    n_ctx = kwc.shape[0]
    nb = t // BLOCK
    gw = KV_GROUP * HEAD_DIM
    vcol = COL_VW // HEAD_DIM
    kspec = lambda f: pl.BlockSpec((BLOCK, HEAD_DIM), f)
    return pl.pallas_call(
        functools.partial(_win_body, nb=nb), grid=(WA_KV, nb),
        in_specs=[pl.BlockSpec((BLOCK, gw), lambda h, i: (i, h)),
                  kspec(lambda h, i: (jnp.maximum(i - 1, 0), h)), kspec(lambda h, i: (i, h)),
                  kspec(lambda h, i: (jnp.minimum(i + 1, nb - 1), h)),
                  kspec(lambda h, i: (jnp.maximum(i - 1, 0), vcol + h)), kspec(lambda h, i: (i, vcol + h)),
                  kspec(lambda h, i: (jnp.minimum(i + 1, nb - 1), vcol + h)),
                  pl.BlockSpec((n_ctx, HEAD_DIM), lambda h, i: (0, h)),
                  pl.BlockSpec((n_ctx, HEAD_DIM), lambda h, i: (0, vcol + h)),
                  pl.BlockSpec((1, KV_GROUP * BLOCK, 1), lambda h, i: (h, 0, 0))],
        out_specs=pl.BlockSpec((BLOCK, gw), lambda h, i: (i, h)),
        out_shape=jax.ShapeDtypeStruct((t, WA_HEADS * HEAD_DIM), BF16),
        compiler_params=_cp(2, 24), name="window_attn")(
            qw, kw, kw, kw, proj, proj, proj, kwc, proj_c, _sink_rows(sink, BLOCK))


def _ctx_attn_body(q_ref, k_ref, v_ref, sink_ref, o_ref, *, n_ctx, use_sink):
    q = jnp.concatenate([q_ref[:, g * HEAD_DIM:(g + 1) * HEAD_DIM] for g in range(KV_GROUP)], axis=0)
    s = lax.dot_general(q, k_ref[...], _NT, preferred_element_type=F32)
    m = jnp.max(s, -1, keepdims=True)
    if use_sink:
        m = jnp.maximum(m, sink_ref[0])
    p = jnp.exp(s - m)
    l = jnp.sum(p, -1, keepdims=True)
    if use_sink:
        l = l + jnp.exp(sink_ref[0] - m)
    out = jnp.dot((p / l).astype(BF16), v_ref[...], preferred_element_type=F32)
    for g in range(KV_GROUP):
        o_ref[:, g * HEAD_DIM:(g + 1) * HEAD_DIM] = out[g * n_ctx:(g + 1) * n_ctx, :].astype(o_ref.dtype)


def _ctx_attn(q, k, proj_c, vcol0, sink, use_sink):
    n_ctx = q.shape[0]
    gw = KV_GROUP * HEAD_DIM
    vcol = vcol0 // HEAD_DIM
    return pl.pallas_call(
        functools.partial(_ctx_attn_body, n_ctx=n_ctx, use_sink=use_sink), grid=(GA_KV,),
        in_specs=[pl.BlockSpec((n_ctx, gw), lambda h: (0, h)),
                  pl.BlockSpec((n_ctx, HEAD_DIM), lambda h: (0, h)),
                  pl.BlockSpec((n_ctx, HEAD_DIM), lambda h: (0, vcol + h)),
                  pl.BlockSpec((1, KV_GROUP * n_ctx, 1), lambda h: (h, 0, 0))],
        out_specs=pl.BlockSpec((n_ctx, gw), lambda h: (0, h)),
        out_shape=jax.ShapeDtypeStruct((n_ctx, GA_HEADS * HEAD_DIM), BF16),
        compiler_params=_cp(1, 24), name="ctx_attn_sink" if use_sink else "ctx_attn")(
            q, k, proj_c, _sink_rows(sink, n_ctx))


def _sg_body(u_ref, v_ref, g_ref, b_ref, w_ref, bs_ref, o_ref, *, nch):
    for n in range(nch):
        rs = slice(n * CHUNK, (n + 1) * CHUNK)
        vn = _ln(v_ref[rs, :].astype(F32), g_ref[...], b_ref[...]).astype(BF16)
        for g in range(SG_GROUPS):
            cs = slice(g * SG_GROUP_CH, (g + 1) * SG_GROUP_CH)
            s = jnp.dot(w_ref[g], vn[:, cs], preferred_element_type=F32) + bs_ref[g]
            o_ref[rs, cs] = (u_ref[rs, cs].astype(F32) * s).astype(o_ref.dtype)


def _spatial_gate(proj, g_v, b_v, w_sp, b_sp):
    t = proj.shape[0]
    tm = min(512, t)
    full = lambda i: (0, 0)
    return pl.pallas_call(
        functools.partial(_sg_body, nch=tm // CHUNK), grid=(t // tm,),
        in_specs=[pl.BlockSpec((tm, SG_WIDTH), lambda i: (i, COL_DU // SG_WIDTH)),
                  pl.BlockSpec((tm, SG_WIDTH), lambda i: (i, COL_DV // SG_WIDTH)),
                  pl.BlockSpec((1, SG_WIDTH), full), pl.BlockSpec((1, SG_WIDTH), full),
                  pl.BlockSpec((SG_GROUPS, CHUNK, CHUNK), lambda i: (0, 0, 0)),
                  pl.BlockSpec((SG_GROUPS, CHUNK, 1), lambda i: (0, 0, 0))],
        out_specs=pl.BlockSpec((tm, SG_WIDTH), lambda i: (i, 0)),
        out_shape=jax.ShapeDtypeStruct((t, SG_WIDTH), BF16),
        compiler_params=_cp(1, 24), name="spatial_gate")(proj, proj, g_v, b_v, w_sp, b_sp)


def _merge_body(ya, yb, yc, yd, wa, wb, wc, wd, ga, gb, gc, gd, o_ref):
    def br(y, w, g):
        return jax.nn.sigmoid(g[...].astype(F32)) * jnp.dot(y[...], w[...], preferred_element_type=F32)
    o_ref[...] = (br(ya, wa, ga) + br(yb, wb, gb) + br(yc, wc, gc) + br(yd, wd, gd)).astype(o_ref.dtype)


def _merge(ys, ws, gates):
    t = ys[0].shape[0]
    tm, tn = min(512, t), 1024
    nn = D_MODEL // tn
    in_specs = [pl.BlockSpec((tm, y.shape[1]), lambda j, i: (i, 0)) for y in ys]
    in_specs += [pl.BlockSpec((w.shape[0], tn), lambda j, i: (0, j)) for w in ws]
    in_specs += [pl.BlockSpec((tm, tn), functools.partial(lambda j, i, b: (i, b * nn + j), b=b)) for b in range(N_BRANCH)]
    return pl.pallas_call(
        _merge_body, grid=(nn, t // tm), in_specs=in_specs,
        out_specs=pl.BlockSpec((tm, tn), lambda j, i: (i, j)),
        out_shape=jax.ShapeDtypeStruct((t, D_MODEL), BF16),
        compiler_params=_cp(2, 48), name="gated_merge")(*ys, *ws, gates, gates, gates, gates)


def _out_ln_body(mg_ref, w_ref, x_ref, g1_ref, lng_ref, lnb_ref, sc_ref, sh_ref, x1_ref, xmT_ref, *, alpha):
    o = jnp.dot(mg_ref[...], w_ref[...], preferred_element_type=F32)
    x1 = _ln(alpha * x_ref[...] + g1_ref[...] * o, lng_ref[...], lnb_ref[...])
    x1_ref[...] = x1
    xmT_ref[...] = (x1 * (1.0 + sc_ref[...]) + sh_ref[...]).T.astype(xmT_ref.dtype)


def _out_ln(merged, w_out, x, g1, lng, lnb, sc2, sh2, alpha):
    t, d = x.shape
    tm = min(256, t)
    row = lambda i: (i, 0)
    vec = pl.BlockSpec((1, d), lambda i: (0, 0))
    return pl.pallas_call(
        functools.partial(_out_ln_body, alpha=alpha), grid=(t // tm,),
        in_specs=[pl.BlockSpec((tm, d), row), pl.BlockSpec((d, d), lambda i: (0, 0)), pl.BlockSpec((tm, d), row),
                  vec, vec, vec, vec, vec],
        out_specs=[pl.BlockSpec((tm, d), row), pl.BlockSpec((d, tm), lambda i: (0, i))],
        out_shape=[jax.ShapeDtypeStruct((t, d), F32), jax.ShapeDtypeStruct((d, t), BF16)],
        compiler_params=_cp(1, 48), name="out_proj_ln")(merged, w_out, x, g1, lng, lnb, sc2, sh2)


def _peer_topk_body(q_ref, k_ref, s1_o, s2_o, s1l_o, s2l_o, tau_o):
    q = q_ref[...].astype(BF16)
    s1 = jnp.dot(k_ref[0], q[:PEER_HALF], preferred_element_type=F32)
    s2 = jnp.dot(k_ref[1], q[PEER_HALF:], preferred_element_type=F32)
    ninf = -jnp.inf

    def top_distinct(s):
        vals, tops, cnts = s, [], []
        for _ in range(PEER_TOPK):
            m = jnp.max(vals, axis=0, keepdims=True)
            eq = vals == m
            tops.append(m)
            cnts.append(jnp.sum(eq.astype(F32), axis=0, keepdims=True))
            vals = jnp.where(eq, ninf, vals)
        return jnp.concatenate(tops, 0), jnp.concatenate(cnts, 0)

    ta, na = top_distinct(s1)
    tb, nb = top_distinct(s2)
    nrow = [PEER_TOPK // (k + 1) for k in range(PEER_TOPK)]
    pad = -sum(nrow) % 8
    cand = jnp.concatenate([ta[k:k + 1] + tb[:nrow[k]] for k in range(PEER_TOPK)]
                           + [jnp.full((pad, ta.shape[1]), ninf, F32)], 0)
    mult = jnp.concatenate([na[k:k + 1] * nb[:nrow[k]] for k in range(PEER_TOPK)]
                           + [jnp.zeros((pad, ta.shape[1]), F32)], 0)
    vals = cand
    cnt = jnp.zeros_like(cand[0:1])
    tau = jnp.full_like(cand[0:1], ninf)
    for _ in range(PEER_TOPK):
        m = jnp.max(vals, axis=0, keepdims=True)
        eq = vals == m
        tau = jnp.where(cnt < PEER_TOPK, m, tau)
        cnt = cnt + jnp.sum(jnp.where(eq, mult, 0.0), axis=0, keepdims=True)
        vals = jnp.where(eq, ninf, vals)
    cmax = cand[0:1]
    z = jnp.sum(jnp.where(cand >= tau, mult * jnp.exp(cand - cmax), 0.0), axis=0, keepdims=True)
    log2e = math.log2(math.e)
    s2l = s2 * log2e - (cmax * log2e + jnp.log2(z))
    for lc in range(s1.shape[1] // PEER_LANES):
        ls = slice(lc * PEER_LANES, (lc + 1) * PEER_LANES)
        s1_o[lc] = s1[:, ls]
        s2_o[lc] = s2[:, ls]
        s1l_o[lc] = s1[:, ls] * log2e
        s2l_o[lc] = s2l[:, ls]
        tau_o[lc] = tau[:, ls]


PEER_LANES = 128


def _peer_topk(qT, keys):
    t = qT.shape[1]
    tt = min(512, t)
    nl = tt // PEER_LANES
    blk = pl.BlockSpec((None, nl, PEER_NKEYS, PEER_LANES), lambda j, h: (h, j, 0, 0))
    shp = jax.ShapeDtypeStruct((PEER_HEADS, t // PEER_LANES, PEER_NKEYS, PEER_LANES), F32)
    return pl.pallas_call(
        _peer_topk_body, grid=(t // tt, PEER_HEADS),
        in_specs=[pl.BlockSpec((PEER_QDIM, tt), lambda j, h: (h, j)),
                  pl.BlockSpec((None, 2, PEER_NKEYS, PEER_HALF), lambda j, h: (h, 0, 0, 0))],
        out_specs=[blk, blk, blk, blk, pl.BlockSpec((None, nl, 1, PEER_LANES), lambda j, h: (h, j, 0, 0))],
        out_shape=[shp, shp, shp, shp, jax.ShapeDtypeStruct((PEER_HEADS, t // PEER_LANES, 1, PEER_LANES), F32)],
        compiler_params=_cp(2, 32), name="peer_topk")(qT, keys)


PEER_EK = 512
PEER_SUBS = PEER_EK // PEER_NKEYS


def _peer_dense_body(xT_ref, u_ref, vT_ref, s1_ref, s2_ref, s1l_ref, s2l_ref, tau_ref, o_ref, hbuf, awbuf):
    k = pl.program_id(1)

    @pl.when(k == 0)
    def _():
        o_ref[...] = jnp.zeros(o_ref.shape, F32)

    hbuf[...] = jnp.dot(u_ref[...], xT_ref[...], preferred_element_type=F32)
    for ii in range(PEER_SUBS):
        i1 = pl.ds(k * PEER_SUBS + ii, 1)
        rs = slice(ii * PEER_NKEYS, (ii + 1) * PEER_NKEYS)
        for lc in range(o_ref.shape[1] // PEER_LANES):
            w = None
            for h in range(PEER_HEADS):
                c = s2_ref[h, lc] + s1_ref[h, lc, i1, :]
                wh = jnp.where(c >= tau_ref[h, lc], jnp.exp2(s2l_ref[h, lc] + s1l_ref[h, lc, i1, :]), 0.0)
                w = wh if w is None else w + wh
            ls = slice(lc * PEER_LANES, (lc + 1) * PEER_LANES)
            awbuf[rs, ls] = (_gelu(hbuf[rs, ls]) * w).astype(BF16)
    o_ref[...] += jnp.dot(vT_ref[...], awbuf[...], preferred_element_type=F32)


def _peer_dense(xmT, u_bf, vT_bf, tk):
    d, t = xmT.shape
    tt = min(512, t)
    nl = tt // PEER_LANES
    s1, s2, s1l, s2l, tau = tk
    sblk = pl.BlockSpec((PEER_HEADS, nl, PEER_NKEYS, PEER_LANES), lambda j, k: (0, j, 0, 0))
    return pl.pallas_call(
        _peer_dense_body, grid=(t // tt, PEER_EXPERTS // PEER_EK),
        in_specs=[pl.BlockSpec((d, tt), lambda j, k: (0, j)),
                  pl.BlockSpec((PEER_EK, d), lambda j, k: (k, 0)),
                  pl.BlockSpec((d, PEER_EK), lambda j, k: (0, k)),
                  sblk, sblk, sblk, sblk,
                  pl.BlockSpec((PEER_HEADS, nl, 1, PEER_LANES), lambda j, k: (0, j, 0, 0))],
        out_specs=pl.BlockSpec((d, tt), lambda j, k: (0, j)),
        out_shape=jax.ShapeDtypeStruct((d, t), F32),
        scratch_shapes=[pltpu.VMEM((PEER_EK, tt), F32), pltpu.VMEM((PEER_EK, tt), BF16)],
        compiler_params=_cp(2, 48), name="peer_dense")(xmT, u_bf, vT_bf, s1, s2, s1l, s2l, tau)


def _ffn_ln_body(fT_ref, x1_ref, g2_ref, lng_ref, lnb_ref, scn_ref, shn_ref, x2_ref, *rest, alpha):
    x2 = _ln(alpha * x1_ref[...] + g2_ref[...] * fT_ref[...].T, lng_ref[...], lnb_ref[...])
    x2_ref[...] = x2
    if rest:
        rest[0][...] = (x2 * (1.0 + scn_ref[...]) + shn_ref[...]).astype(rest[0].dtype)


def _ffn_ln(fT, x1, g2, lng, lnb, scn, shn, alpha, emit_h):
    t, d = x1.shape
    tm = min(256, t)
    row = pl.BlockSpec((tm, d), lambda i: (i, 0))
    vec = pl.BlockSpec((1, d), lambda i: (0, 0))
    out_specs = [row, row] if emit_h else [row]
    out_shape = [jax.ShapeDtypeStruct((t, d), F32)] + ([jax.ShapeDtypeStruct((t, d), BF16)] if emit_h else [])
    return pl.pallas_call(
        functools.partial(_ffn_ln_body, alpha=alpha), grid=(t // tm,),
        in_specs=[pl.BlockSpec((d, tm), lambda i: (0, i)), row, vec, vec, vec, vec, vec],
        out_specs=out_specs, out_shape=out_shape,
        compiler_params=_cp(1, 32), name="ffn_ln")(fT, x1, g2, lng, lnb, scn, shn)


def kernel(x, c, ctx, c_ctx, w_ada, b_ada, w_in, b_in, s5_lam_re, s5_lam_im, s5_log_dt, s5_b_re, s5_b_im, s5_c_re, s5_c_im, s5_d, w_glu, b_glu, qn_gain, kn_gain, sink, sg_ln_g, sg_ln_b, w_sp, b_sp, w_br_a, w_br_b, w_br_c, w_br_d, w_out, ln1_g, ln1_b, ln2_g, ln2_b, w_pq, peer_keys, peer_u, peer_v):
    depth = w_in.shape[0]
    bsz, n_lat, d = x.shape
    assert bsz == 1 and d == D_MODEL and n_lat % 512 == 0 and ctx.shape[1] % BLOCK == 0
    n_ctx = ctx.shape[1]
    alpha = (2 * depth) ** 0.25
    x, ctx = x[0], ctx[0]

    row8 = lax.broadcasted_iota(jnp.int32, (8, d), 0)
    cond8 = jnp.where(row8 == 0, c[0][None], jnp.where(row8 == 1, c_ctx[None], 0.0))
    ada = [_ada(cond8, w_ada, b_ada, l) for l in range(depth)]
    cosf, sinf = _rope_tables(n_lat)
    cos1, sin0 = jnp.ones((n_ctx, HEAD_DIM), F32), jnp.zeros((n_ctx, HEAD_DIM), F32)
    vec = lambda a: a.reshape(1, -1).astype(F32)

    ref_off = {'a': 0, 'qg': 768, 'kg': 1792, 'vg': 2048, 'qw': 2304, 'kw': 3328, 'vw': 3584, 'du': 3840, 'dv': 4608}
    ref_w = {'a': 768, 'qg': 1024, 'kg': 256, 'vg': 256, 'qw': 1024, 'kw': 256, 'vw': 256, 'du': 768, 'dv': 768}
    order = ['a', 'kg', 'qg', 'qw', 'kw', 'vg', 'vw', 'du', 'dv']
    perm = lambda a: jnp.concatenate([a[..., ref_off[n]:ref_off[n] + ref_w[n]] for n in order], -1)
    perm_idx = np.concatenate([np.arange(ref_off[n], ref_off[n] + ref_w[n]) for n in order]).astype(np.int32)

    h_lat = None
    for l in range(depth):
        need_ctx = l < depth - 1
        mods = [[ada[l][r:r + 1, i * d:(i + 1) * d] for i in range(6)] for r in range(2)]
        (sh1, sc1, g1, sh2, sc2, g2), (sh1c, sc1c, g1c, sh2c, sc2c, g2c) = mods
        if need_ctx:
            nxt = [ada[l + 1][r:r + 1, 0:2 * d] for r in range(2)]
            (shn, scn), (shnc, scnc) = [(m[:, :d], m[:, d:]) for m in nxt]
        else:
            shn = scn = shnc = scnc = jnp.zeros((1, d), F32)

        w_small, b_small = perm(w_in[l][:, :SMALL_W]).astype(BF16), vec(jnp.take(b_in[l], perm_idx))
        w_gate, b_gate = w_in[l][:, SMALL_W:].astype(BF16), vec(b_in[l][SMALL_W:])
        s5m = [_s5_mats(s5_lam_re[l, dr], s5_lam_im[l, dr], s5_log_dt[l, dr], s5_b_re[l, dr], s5_b_im[l, dr],
                        s5_c_re[l, dr], s5_c_im[l, dr]) for dr in range(2)]
        wg, bg, s5d = w_glu[l].astype(BF16), vec(b_glu[l]), vec(s5_d[l])
        qn, kn = vec(qn_gain[l]), vec(kn_gain[l])
        wsp, bsp = w_sp[l].astype(BF16), b_sp[l].astype(F32)[:, :, None]
        w_brs = [w[l].astype(BF16) for w in (w_br_a, w_br_b, w_br_c, w_br_d)]
        wo = w_out[l].astype(BF16)
        wpqT = w_pq[l].T.astype(BF16)
        keys = peer_keys[l].astype(BF16)
        u_bf, vT_bf = peer_u[l].astype(BF16), peer_v[l].T.astype(BF16)

        if h_lat is None:
            h_lat, h_ctx = _modulate(x, sc1, sh1), _modulate(ctx, sc1c, sh1c)

        proj = _matmul(h_lat, w_small, b_small, tm=512, tn=1792, out_dtype=BF16, name="in_proj")
        gates = _matmul(h_lat, w_gate, b_gate, tm=512, tn=2048, out_dtype=BF16, name="in_proj_gates")
        proj_c = _matmul(h_ctx, w_small, b_small, tm=512, tn=1792, out_dtype=BF16, name="in_proj_ctx")

        y_dir, y_dir_c = [], []
        for dr in range(2):
            yc_, hc_end = _s5_dir(proj_c, s5m[dr], jnp.zeros((2, S5_CH), F32), bool(dr))
            yl_, _ = _s5_dir(proj, s5m[dr], hc_end, bool(dr))
            y_dir.append(yl_)
            y_dir_c.append(yc_)
        y_a = _s5_out(proj, y_dir[0], y_dir[1], s5d, wg, bg)

        qg, kg, vge, qw, kw = _prep(proj, cosf, sinf, qn, kn, ATTN_SCALE * math.log2(math.e))
        qgc, kgc, vgce, qwc, kwc = _prep(proj_c, cos1, sin0, qn, kn, ATTN_SCALE)
        y_b = _flash(qg, kg, vge, kgc, vgce)
        y_c = _window(qw, kw, proj, kwc, proj_c, sink[l])

        y_d = _spatial_gate(proj, vec(sg_ln_g[l]), vec(sg_ln_b[l]), wsp, bsp)

        merged = _merge((y_a, y_b, y_c, y_d), w_brs, gates)
        x1, xmT = _out_ln(merged, wo, x, g1, vec(ln1_g[l]), vec(ln1_b[l]), sc2, sh2, alpha)
        qT = _matmul(wpqT, xmT, None, tm=D_MODEL, tn=1024, out_dtype=F32, name="peer_query")
        ffnT = _peer_dense(xmT, u_bf, vT_bf, _peer_topk(qT, keys))
        res = _ffn_ln(ffnT, x1, g2, vec(ln2_g[l]), vec(ln2_b[l]), scn, shn, alpha, need_ctx)
        x = res[0]

        if need_ctx:
            h_lat = res[1]
            gates_c = _matmul(h_ctx, w_gate, b_gate, tm=512, tn=2048, out_dtype=BF16, name="in_proj_gates_ctx")
            y_a_c = _s5_out(proj_c, y_dir_c[0], y_dir_c[1], s5d, wg, bg)
            y_b_c = _ctx_attn(qgc, kgc, proj_c, COL_VG, sink[l], False)
            y_c_c = _ctx_attn(qwc, kwc, proj_c, COL_VW, sink[l], True)
            y_d_c = _spatial_gate(proj_c, vec(sg_ln_g[l]), vec(sg_ln_b[l]), wsp, bsp)
            merged_c = _merge((y_a_c, y_b_c, y_c_c, y_d_c), w_brs, gates_c)
            c1, cmT = _out_ln(merged_c, wo, ctx, g1c, vec(ln1_g[l]), vec(ln1_b[l]), sc2c, sh2c, alpha)
            qTc = _matmul(wpqT, cmT, None, tm=D_MODEL, tn=1024, out_dtype=F32, name="peer_query_ctx")
            ffnTc = _peer_dense(cmT, u_bf, vT_bf, _peer_topk(qTc, keys))
            ctx, h_ctx = _ffn_ln(ffnTc, c1, g2c, vec(ln2_g[l]), vec(ln2_b[l]), scnc, shnc, alpha, True)
    return x[None]
```

```python
import functools
import math

import jax
import jax.numpy as jnp
import numpy as np
from jax import lax
from jax.experimental import pallas as pl
from jax.experimental.pallas import tpu as pltpu

F32, BF16 = jnp.float32, jnp.bfloat16

D_MODEL = 2048
GRID_W = 64
BLOCK = 128
HEAD_DIM = 128
AXIS_DIM = HEAD_DIM // 2
ROPE_THETA = 10000.0
ATTN_SCALE = HEAD_DIM ** -0.5
NEG_INF = -1e30
EPS = 1e-6
S5_GROUPS, S5_GROUP_CH, S5_STATE = 48, 16, 64
S5_WIDTH = S5_GROUPS * S5_GROUP_CH
S5_CH = S5_GROUPS * S5_STATE
S5_GB = 16
S5_NB = S5_GROUPS // S5_GB
S5_KB = S5_GB * S5_GROUP_CH
S5_PB = S5_GB * S5_STATE
GA_HEADS, GA_KV = 8, 2
WA_HEADS, WA_KV = 8, 2
KV_GROUP = GA_HEADS // GA_KV
SG_GROUPS, SG_GROUP_CH = 6, 128
SG_WIDTH = SG_GROUPS * SG_GROUP_CH
CHUNK = 128
N_BRANCH = 4
PEER_HEADS, PEER_NKEYS, PEER_QDIM, PEER_TOPK = 8, 128, 256, 16
PEER_HALF = PEER_QDIM // 2
PEER_EXPERTS = PEER_NKEYS * PEER_NKEYS

COL_A, COL_KG, COL_QG, COL_QW, COL_KW, COL_VG, COL_VW, COL_DU, COL_DV = 0, 768, 1024, 2048, 3072, 3328, 3584, 3840, 4608
SMALL_W = 5376
GATE_W = N_BRANCH * D_MODEL

VMEM_CAP_MB = 56


def _cp(ndims, vmem_mb=32):
    return pltpu.CompilerParams(dimension_semantics=("arbitrary",) * ndims,
                                vmem_limit_bytes=min(vmem_mb, VMEM_CAP_MB) << 20)


def _gelu(y):
    return 0.5 * y * (1.0 + lax.erf(y * (2.0 ** -0.5)))


def _ln(y, g, b):
    mu = jnp.mean(y, -1, keepdims=True)
    yc = y - mu
    var = jnp.mean(yc * yc, -1, keepdims=True)
    return yc * lax.rsqrt(var + EPS) * g + b


_NT = (((1,), (1,)), ((), ()))


def _ada_body(c_ref, w_ref, b_ref, o_ref):
    cnd = c_ref[...]
    s = (cnd * jax.nn.sigmoid(cnd)).astype(BF16)
    o_ref[...] = jnp.dot(s, w_ref[...].astype(BF16), preferred_element_type=F32) + b_ref[...]


def _ada(cond8, w_ada, b_ada, l):
    depth, d, n = w_ada.shape
    tn = 1024
    return pl.pallas_call(
        _ada_body, grid=(n // tn,),
        in_specs=[pl.BlockSpec((8, d), lambda j: (0, 0)),
                  pl.BlockSpec((None, d, tn), lambda j: (l, 0, j)),
                  pl.BlockSpec((None, 1, tn), lambda j: (l, 0, j))],
        out_specs=pl.BlockSpec((8, tn), lambda j: (0, j)),
        out_shape=jax.ShapeDtypeStruct((8, n), F32),
        compiler_params=_cp(1, 32), name="ada_mod")(cond8, w_ada, b_ada.reshape(depth, 1, n))


def _mod_body(x_ref, sc_ref, sh_ref, o_ref):
    o_ref[...] = (x_ref[...] * (1.0 + sc_ref[...]) + sh_ref[...]).astype(o_ref.dtype)


def _modulate(x, sc, sh):
    t, d = x.shape
    tm = min(512, t)
    return pl.pallas_call(
        _mod_body, grid=(t // tm,),
        in_specs=[pl.BlockSpec((tm, d), lambda i: (i, 0)),
                  pl.BlockSpec((1, d), lambda i: (0, 0)),
                  pl.BlockSpec((1, d), lambda i: (0, 0))],
        out_specs=pl.BlockSpec((tm, d), lambda i: (i, 0)),
        out_shape=jax.ShapeDtypeStruct((t, d), BF16),
        compiler_params=_cp(1, 24), name="modulate")(x, sc, sh)


def _mm_body(a_ref, w_ref, b_ref, o_ref):
    acc = jnp.dot(a_ref[...], w_ref[...], preferred_element_type=F32)
    o_ref[...] = (acc + b_ref[...]).astype(o_ref.dtype)


def _mm_nobias_body(a_ref, w_ref, o_ref):
    o_ref[...] = jnp.dot(a_ref[...], w_ref[...], preferred_element_type=F32).astype(o_ref.dtype)


def _matmul(a, w, b, *, tm, tn, out_dtype, name):
    m, k = a.shape
    n = w.shape[1]
    tm, tn = min(tm, m), min(tn, n)
    in_specs = [pl.BlockSpec((tm, k), lambda j, i: (i, 0)),
                pl.BlockSpec((k, tn), lambda j, i: (0, j))]
    args = [a, w]
    body = _mm_nobias_body
    if b is not None:
        in_specs.append(pl.BlockSpec((1, tn), lambda j, i: (0, j)))
        args.append(b)
        body = _mm_body
    osz = jnp.dtype(out_dtype).itemsize
    vmem = 2 * (tm * k * 2 + k * tn * 2 + tm * tn * osz) + tm * tn * 4
    return pl.pallas_call(
        body, grid=(n // tn, m // tm), in_specs=in_specs,
        out_specs=pl.BlockSpec((tm, tn), lambda j, i: (i, j)),
        out_shape=jax.ShapeDtypeStruct((m, n), out_dtype),
        compiler_params=_cp(2, (vmem >> 20) + 8), name=name)(*args)


def _s5_body(u_ref, bre_ref, bim_ref, cre_ref, cim_ref, a_ref, h0_ref, y_ref, hT_ref, xr, xi, carry, *, tc, rev):
    @pl.when(pl.program_id(0) == 0)
    def _():
        carry[...] = h0_ref[...]

    for j in range(S5_NB):
        uj = u_ref[:, j * S5_KB:(j + 1) * S5_KB]
        xr[:, j * S5_PB:(j + 1) * S5_PB] = jnp.dot(uj, bre_ref[j], preferred_element_type=F32)
        xi[:, j * S5_PB:(j + 1) * S5_PB] = jnp.dot(uj, bim_ref[j], preferred_element_type=F32)

    for j in range(S5_NB):
        cs = slice(j * S5_PB, (j + 1) * S5_PB)
        ar, ai = a_ref[0:1, cs], a_ref[1:2, cs]

        def step(k, c, cs=cs, ar=ar, ai=ai):
            hr, hi = c
            t = (tc - 1 - k) if rev else k
            nr = ar * hr - ai * hi + xr[pl.ds(t, 1), cs]
            ni = ar * hi + ai * hr + xi[pl.ds(t, 1), cs]
            xr[pl.ds(t, 1), cs] = nr
            xi[pl.ds(t, 1), cs] = ni
            return nr, ni

        hr, hi = lax.fori_loop(0, tc, step, (carry[0:1, cs], carry[1:2, cs]), unroll=8)
        carry[0:1, cs] = hr
        carry[1:2, cs] = hi

    for j in range(S5_NB):
        cs = slice(j * S5_PB, (j + 1) * S5_PB)
        y_ref[:, j * S5_KB:(j + 1) * S5_KB] = (
            jnp.dot(xr[:, cs].astype(BF16), cre_ref[j], preferred_element_type=F32)
            + jnp.dot(xi[:, cs].astype(BF16), cim_ref[j], preferred_element_type=F32))
    hT_ref[...] = carry[...]


def _s5_dir(proj, mats, h0, rev):
    t = proj.shape[0]
    tc = min(256, t)
    nt = t // tc
    bre, bim, cre, cim, a = mats
    tmap = (lambda i: (nt - 1 - i, 0)) if rev else (lambda i: (i, 0))
    full3 = lambda i: (0, 0, 0)
    full2 = lambda i: (0, 0)
    return pl.pallas_call(
        functools.partial(_s5_body, tc=tc, rev=rev), grid=(nt,),
        in_specs=[pl.BlockSpec((tc, S5_WIDTH), tmap),
                  pl.BlockSpec((S5_NB, S5_KB, S5_PB), full3), pl.BlockSpec((S5_NB, S5_KB, S5_PB), full3),
                  pl.BlockSpec((S5_NB, S5_PB, S5_KB), full3), pl.BlockSpec((S5_NB, S5_PB, S5_KB), full3),
                  pl.BlockSpec((2, S5_CH), full2), pl.BlockSpec((2, S5_CH), full2)],
        out_specs=[pl.BlockSpec((tc, S5_WIDTH), tmap), pl.BlockSpec((2, S5_CH), full2)],
        out_shape=[jax.ShapeDtypeStruct((t, S5_WIDTH), F32), jax.ShapeDtypeStruct((2, S5_CH), F32)],
        scratch_shapes=[pltpu.VMEM((tc, S5_CH), F32), pltpu.VMEM((tc, S5_CH), F32), pltpu.VMEM((2, S5_CH), F32)],
        compiler_params=_cp(1, 40), name="s5_scan_bwd" if rev else "s5_scan_fwd")(proj, bre, bim, cre, cim, a, h0)


def _s5_out_body(u_ref, yf_ref, yb_ref, d_ref, wg_ref, bg_ref, o_ref):
    y = u_ref[...].astype(F32) * d_ref[...] + yf_ref[...] + yb_ref[...]
    g = _gelu(y)
    z = jnp.dot(g.astype(BF16), wg_ref[...], preferred_element_type=F32) + bg_ref[...]
    o_ref[...] = (g * jax.nn.sigmoid(z)).astype(o_ref.dtype)


def _s5_out(proj, yf, yb, d, wg, bg):
    t = proj.shape[0]
    tm = min(512, t)
    row = lambda i: (i, 0)
    full = lambda i: (0, 0)
    return pl.pallas_call(
        _s5_out_body, grid=(t // tm,),
        in_specs=[pl.BlockSpec((tm, S5_WIDTH), row), pl.BlockSpec((tm, S5_WIDTH), row), pl.BlockSpec((tm, S5_WIDTH), row),
                  pl.BlockSpec((1, S5_WIDTH), full), pl.BlockSpec((S5_WIDTH, S5_WIDTH), full), pl.BlockSpec((1, S5_WIDTH), full)],
        out_specs=pl.BlockSpec((tm, S5_WIDTH), row),
        out_shape=jax.ShapeDtypeStruct((t, S5_WIDTH), BF16),
        compiler_params=_cp(1, 24), name="s5_glu")(proj, yf, yb, d, wg, bg)


def _s5_mats(lam_re, lam_im, log_dt, b_re, b_im, c_re, c_im):
    dt = jnp.exp(log_dt)[:, None]
    mag = jnp.exp(lam_re * dt)
    ar, ai = mag * jnp.cos(lam_im * dt), mag * jnp.sin(lam_im * dt)
    nr, ni = ar - 1.0, ai
    den = lam_re * lam_re + lam_im * lam_im
    cr = (nr * lam_re + ni * lam_im) / den
    ci = (ni * lam_re - nr * lam_im) / den
    fre = cr[:, :, None] * b_re - ci[:, :, None] * b_im
    fim = cr[:, :, None] * b_im + ci[:, :, None] * b_re
    eye = jnp.eye(S5_GB, dtype=F32)

    def blk_in(f):
        f = f.reshape(S5_NB, S5_GB, S5_STATE, S5_GROUP_CH)
        m = jnp.einsum('jgph,gk->jghkp', f, eye)
        return m.reshape(S5_NB, S5_KB, S5_PB).astype(BF16)

    def blk_out(c):
        c = c.reshape(S5_NB, S5_GB, S5_GROUP_CH, S5_STATE)
        m = jnp.einsum('jghp,gk->jgpkh', c, eye)
        return m.reshape(S5_NB, S5_PB, S5_KB).astype(BF16)

    is_re = lax.broadcasted_iota(jnp.int32, (2, S5_CH), 0) == 0
    a = jnp.where(is_re, ar.reshape(1, -1), ai.reshape(1, -1))
    return blk_in(fre), blk_in(fim), blk_out(c_re), blk_out(-c_im), a


def _prep_body(qg_ref, kg_ref, vg_ref, qw_ref, kw_ref, cos_ref, sin_ref, qn_ref, kn_ref, oqg, okg, ovg, oqw, okw, *,
               qg_scale):
    cosf, sinf = cos_ref[...], sin_ref[...]
    lane = lax.broadcasted_iota(jnp.int32, cosf.shape, 1)
    low = (lane % AXIS_DIM) < (AXIS_DIM // 2)

    def rope(t):
        partner = jnp.where(low, pltpu.roll(t, HEAD_DIM - AXIS_DIM // 2, 1), pltpu.roll(t, AXIS_DIM // 2, 1))
        return t * cosf + partner * sinf

    def rms(t, g):
        return t * lax.rsqrt(jnp.mean(t * t, -1, keepdims=True) + EPS) * g

    for h in range(GA_HEADS):
        cs = slice(h * HEAD_DIM, (h + 1) * HEAD_DIM)
        oqg[:, cs] = (rope(rms(qg_ref[:, cs].astype(F32), qn_ref[...])) * qg_scale).astype(BF16)
        oqw[:, cs] = (rope(qw_ref[:, cs].astype(F32)) * ATTN_SCALE).astype(BF16)
    for h in range(GA_KV):
        cs = slice(h * HEAD_DIM, (h + 1) * HEAD_DIM)
        okg[:, cs] = rope(rms(kg_ref[:, cs].astype(F32), kn_ref[...])).astype(BF16)
        okw[:, cs] = rope(kw_ref[:, cs].astype(F32)).astype(BF16)
        ovg[:, 2 * h * HEAD_DIM:(2 * h + 1) * HEAD_DIM] = vg_ref[:, cs]
        ovg[:, (2 * h + 1) * HEAD_DIM:(2 * h + 2) * HEAD_DIM] = jnp.ones((vg_ref.shape[0], HEAD_DIM), BF16)


def _prep(proj, cosf, sinf, qn, kn, qg_scale):
    t = proj.shape[0]
    tr = min(256, t)
    qw_, kw_ = GA_HEADS * HEAD_DIM, GA_KV * HEAD_DIM
    row = lambda i: (i, 0)
    full = lambda i: (0, 0)
    return pl.pallas_call(
        functools.partial(_prep_body, qg_scale=qg_scale), grid=(t // tr,),
        in_specs=[pl.BlockSpec((tr, qw_), lambda i: (i, COL_QG // qw_)),
                  pl.BlockSpec((tr, kw_), lambda i: (i, COL_KG // kw_)),
                  pl.BlockSpec((tr, kw_), lambda i: (i, COL_VG // kw_)),
                  pl.BlockSpec((tr, qw_), lambda i: (i, COL_QW // qw_)),
                  pl.BlockSpec((tr, kw_), lambda i: (i, COL_KW // kw_)),
                  pl.BlockSpec((tr, HEAD_DIM), row), pl.BlockSpec((tr, HEAD_DIM), row),
                  pl.BlockSpec((1, HEAD_DIM), full), pl.BlockSpec((1, HEAD_DIM), full)],
        out_specs=[pl.BlockSpec((tr, qw_), row), pl.BlockSpec((tr, kw_), row), pl.BlockSpec((tr, 2 * kw_), row),
                   pl.BlockSpec((tr, qw_), row), pl.BlockSpec((tr, kw_), row)],
        out_shape=[jax.ShapeDtypeStruct((t, qw_), BF16), jax.ShapeDtypeStruct((t, kw_), BF16),
                   jax.ShapeDtypeStruct((t, 2 * kw_), BF16),
                   jax.ShapeDtypeStruct((t, qw_), BF16), jax.ShapeDtypeStruct((t, kw_), BF16)],
        compiler_params=_cp(1, 24), name="qk_prep")(proj, proj, proj, proj, proj, cosf, sinf, qn, kn)


def _rope_tables(n_lat):
    rows = n_lat // GRID_W
    row = jnp.repeat(jnp.arange(rows, dtype=F32), GRID_W)
    col = jnp.tile(jnp.arange(GRID_W, dtype=F32), rows)
    inv = jnp.power(ROPE_THETA, -jnp.arange(0, AXIS_DIM, 2, dtype=F32) / AXIS_DIM)
    ang_r, ang_c = row[:, None] * inv, col[:, None] * inv
    cr, sr, cc, sc = jnp.cos(ang_r), jnp.sin(ang_r), jnp.cos(ang_c), jnp.sin(ang_c)
    return jnp.concatenate([cr, cr, cc, cc], -1), jnp.concatenate([-sr, sr, -sc, sc], -1)


def _flash_body(q_ref, kc_ref, vc_ref, k_ref, v_ref, o_ref, qs, m_s, acc, *, tq, tk, nk):
    for g in range(KV_GROUP):
        qs[g * tq:(g + 1) * tq, :] = q_ref[:, g * HEAD_DIM:(g + 1) * HEAD_DIM]
    m_s[...] = jnp.full(m_s.shape, -jnp.inf, F32)
    acc[...] = jnp.zeros(acc.shape, F32)

    def update(k, v):
        for g in range(KV_GROUP):
            rs = slice(g * tq, (g + 1) * tq)
            s = lax.dot_general(qs[rs, :], k, _NT, preferred_element_type=F32)
            m_prev = m_s[rs, :]
            m_next = jnp.maximum(m_prev, jnp.max(s, axis=1, keepdims=True))
            alpha = jnp.exp2(m_prev - m_next)
            p = jnp.exp2(s - jnp.concatenate([m_next] * (s.shape[1] // HEAD_DIM), axis=1))
            acc[rs, :] = (jnp.concatenate([alpha, alpha], axis=1) * acc[rs, :]
                          + jnp.dot(p.astype(BF16), v, preferred_element_type=F32))
            m_s[rs, :] = m_next

    update(kc_ref[...], vc_ref[...])

    def body(j, carry):
        rows = pl.ds(pl.multiple_of(j * tk, tk), tk)
        update(k_ref[rows, :], v_ref[rows, :])
        return carry

    lax.fori_loop(0, nk, body, 0)
    out = acc[:, :HEAD_DIM] / acc[:, HEAD_DIM:]
    for g in range(KV_GROUP):
        o_ref[:, g * HEAD_DIM:(g + 1) * HEAD_DIM] = out[g * tq:(g + 1) * tq, :].astype(o_ref.dtype)


def _flash(q, kg, vge, kgc, vgce):
    t = q.shape[0]
    n_ctx = kgc.shape[0]
    tq = min(256, t)
    tk = min(512, t)
    gw = KV_GROUP * HEAD_DIM
    head = lambda h, i: (0, h)
    return pl.pallas_call(
        functools.partial(_flash_body, tq=tq, tk=tk, nk=t // tk), grid=(GA_KV, t // tq),
        in_specs=[pl.BlockSpec((tq, gw), lambda h, i: (i, h)),
                  pl.BlockSpec((n_ctx, HEAD_DIM), head), pl.BlockSpec((n_ctx, 2 * HEAD_DIM), head),
                  pl.BlockSpec((t, HEAD_DIM), head), pl.BlockSpec((t, 2 * HEAD_DIM), head)],
        out_specs=pl.BlockSpec((tq, gw), lambda h, i: (i, h)),
        out_shape=jax.ShapeDtypeStruct((t, GA_HEADS * HEAD_DIM), BF16),
        scratch_shapes=[pltpu.VMEM((KV_GROUP * tq, HEAD_DIM), BF16), pltpu.VMEM((KV_GROUP * tq, HEAD_DIM), F32),
                        pltpu.VMEM((KV_GROUP * tq, 2 * HEAD_DIM), F32)],
        compiler_params=_cp(2, 48), name="global_attn")(q, kgc, vgce, kg, vge)


def _win_body(q_ref, km_ref, k0_ref, kp_ref, vm_ref, v0_ref, vp_ref, kc_ref, vc_ref, sink_ref, o_ref, *, nb):
    i = pl.program_id(1)
    q = jnp.concatenate([q_ref[:, g * HEAD_DIM:(g + 1) * HEAD_DIM] for g in range(KV_GROUP)], axis=0)
    kw = jnp.concatenate([km_ref[...], k0_ref[...], kp_ref[...]], axis=0)
    vw = jnp.concatenate([vm_ref[...], v0_ref[...], vp_ref[...]], axis=0)
    s_w = lax.dot_general(q, kw, _NT, preferred_element_type=F32)
    r = lax.broadcasted_iota(jnp.int32, s_w.shape, 0) % BLOCK
    c = lax.broadcasted_iota(jnp.int32, s_w.shape, 1)
    cc = c % BLOCK
    iv = jnp.full(s_w.shape, i, jnp.int32)
    valid = (((c < BLOCK) & (cc >= r) & (iv >= 1)) | ((c >= BLOCK) & (c < 2 * BLOCK))
             | ((c >= 2 * BLOCK) & (cc <= r) & (iv + 1 < nb)))
    s_w = jnp.where(valid, s_w, NEG_INF)
    s_c = lax.dot_general(q, kc_ref[...], _NT, preferred_element_type=F32)
    sink = sink_ref[0]
    m = jnp.maximum(jnp.maximum(jnp.max(s_w, -1, keepdims=True), jnp.max(s_c, -1, keepdims=True)), sink)
    pw, pc = jnp.exp(s_w - m), jnp.exp(s_c - m)
    l = jnp.sum(pw, -1, keepdims=True) + jnp.sum(pc, -1, keepdims=True) + jnp.exp(sink - m)
    inv = 1.0 / l
    out = (jnp.dot((pc * inv).astype(BF16), vc_ref[...], preferred_element_type=F32)
           + jnp.dot((pw * inv).astype(BF16), vw, preferred_element_type=F32))
    for g in range(KV_GROUP):
        o_ref[:, g * HEAD_DIM:(g + 1) * HEAD_DIM] = out[g * BLOCK:(g + 1) * BLOCK, :].astype(o_ref.dtype)


def _sink_rows(sink, rows_per_head):
    return jnp.repeat(sink.astype(F32).reshape(WA_KV, KV_GROUP), rows_per_head, axis=1)[:, :, None]


def _window(qw, kw, proj, kwc, proj_c, sink):
    t = qw.shape[0]
    n_ctx = kwc.shape[0]
    nb = t // BLOCK
    gw = KV_GROUP * HEAD_DIM
    vcol = COL_VW // HEAD_DIM
    kspec = lambda f: pl.BlockSpec((BLOCK, HEAD_DIM), f)
    return pl.pallas_call(
        functools.partial(_win_body, nb=nb), grid=(WA_KV, nb),
        in_specs=[pl.BlockSpec((BLOCK, gw), lambda h, i: (i, h)),
                  kspec(lambda h, i: (jnp.maximum(i - 1, 0), h)), kspec(lambda h, i: (i, h)),
                  kspec(lambda h, i: (jnp.minimum(i + 1, nb - 1), h)),
                  kspec(lambda h, i: (jnp.maximum(i - 1, 0), vcol + h)), kspec(lambda h, i: (i, vcol + h)),
                  kspec(lambda h, i: (jnp.minimum(i + 1, nb - 1), vcol + h)),
                  pl.BlockSpec((n_ctx, HEAD_DIM), lambda h, i: (0, h)),
                  pl.BlockSpec((n_ctx, HEAD_DIM), lambda h, i: (0, vcol + h)),
                  pl.BlockSpec((1, KV_GROUP * BLOCK, 1), lambda h, i: (h, 0, 0))],
        out_specs=pl.BlockSpec((BLOCK, gw), lambda h, i: (i, h)),
        out_shape=jax.ShapeDtypeStruct((t, WA_HEADS * HEAD_DIM), BF16),
        compiler_params=_cp(2, 24), name="window_attn")(
            qw, kw, kw, kw, proj, proj, proj, kwc, proj_c, _sink_rows(sink, BLOCK))


def _ctx_attn_body(q_ref, k_ref, v_ref, sink_ref, o_ref, *, n_ctx, use_sink):
    q = jnp.concatenate([q_ref[:, g * HEAD_DIM:(g + 1) * HEAD_DIM] for g in range(KV_GROUP)], axis=0)
    s = lax.dot_general(q, k_ref[...], _NT, preferred_element_type=F32)
    m = jnp.max(s, -1, keepdims=True)
    if use_sink:
        m = jnp.maximum(m, sink_ref[0])
    p = jnp.exp(s - m)
    l = jnp.sum(p, -1, keepdims=True)
    if use_sink:
        l = l + jnp.exp(sink_ref[0] - m)
    out = jnp.dot((p / l).astype(BF16), v_ref[...], preferred_element_type=F32)
    for g in range(KV_GROUP):
        o_ref[:, g * HEAD_DIM:(g + 1) * HEAD_DIM] = out[g * n_ctx:(g + 1) * n_ctx, :].astype(o_ref.dtype)


def _ctx_attn(q, k, proj_c, vcol0, sink, use_sink):
    n_ctx = q.shape[0]
    gw = KV_GROUP * HEAD_DIM
    vcol = vcol0 // HEAD_DIM
    return pl.pallas_call(
        functools.partial(_ctx_attn_body, n_ctx=n_ctx, use_sink=use_sink), grid=(GA_KV,),
        in_specs=[pl.BlockSpec((n_ctx, gw), lambda h: (0, h)),
                  pl.BlockSpec((n_ctx, HEAD_DIM), lambda h: (0, h)),
                  pl.BlockSpec((n_ctx, HEAD_DIM), lambda h: (0, vcol + h)),
                  pl.BlockSpec((1, KV_GROUP * n_ctx, 1), lambda h: (h, 0, 0))],
        out_specs=pl.BlockSpec((n_ctx, gw), lambda h: (0, h)),
        out_shape=jax.ShapeDtypeStruct((n_ctx, GA_HEADS * HEAD_DIM), BF16),
        compiler_params=_cp(1, 24), name="ctx_attn_sink" if use_sink else "ctx_attn")(
            q, k, proj_c, _sink_rows(sink, n_ctx))


def _sg_body(u_ref, v_ref, g_ref, b_ref, w_ref, bs_ref, o_ref, *, nch):
    for n in range(nch):
        rs = slice(n * CHUNK, (n + 1) * CHUNK)
        vn = _ln(v_ref[rs, :].astype(F32), g_ref[...], b_ref[...]).astype(BF16)
        for g in range(SG_GROUPS):
            cs = slice(g * SG_GROUP_CH, (g + 1) * SG_GROUP_CH)
            s = jnp.dot(w_ref[g], vn[:, cs], preferred_element_type=F32) + bs_ref[g]
            o_ref[rs, cs] = (u_ref[rs, cs].astype(F32) * s).astype(o_ref.dtype)


def _spatial_gate(proj, g_v, b_v, w_sp, b_sp):
    t = proj.shape[0]
    tm = min(512, t)
    full = lambda i: (0, 0)
    return pl.pallas_call(
        functools.partial(_sg_body, nch=tm // CHUNK), grid=(t // tm,),
        in_specs=[pl.BlockSpec((tm, SG_WIDTH), lambda i: (i, COL_DU // SG_WIDTH)),
                  pl.BlockSpec((tm, SG_WIDTH), lambda i: (i, COL_DV // SG_WIDTH)),
                  pl.BlockSpec((1, SG_WIDTH), full), pl.BlockSpec((1, SG_WIDTH), full),
                  pl.BlockSpec((SG_GROUPS, CHUNK, CHUNK), lambda i: (0, 0, 0)),
                  pl.BlockSpec((SG_GROUPS, CHUNK, 1), lambda i: (0, 0, 0))],
        out_specs=pl.BlockSpec((tm, SG_WIDTH), lambda i: (i, 0)),
        out_shape=jax.ShapeDtypeStruct((t, SG_WIDTH), BF16),
        compiler_params=_cp(1, 24), name="spatial_gate")(proj, proj, g_v, b_v, w_sp, b_sp)


def _merge_body(ya, yb, yc, yd, wa, wb, wc, wd, ga, gb, gc, gd, o_ref):
    def br(y, w, g):
        return jax.nn.sigmoid(g[...].astype(F32)) * jnp.dot(y[...], w[...], preferred_element_type=F32)
    o_ref[...] = (br(ya, wa, ga) + br(yb, wb, gb) + br(yc, wc, gc) + br(yd, wd, gd)).astype(o_ref.dtype)


def _merge(ys, ws, gates):
    t = ys[0].shape[0]
    tm, tn = min(512, t), 1024
    nn = D_MODEL // tn
    in_specs = [pl.BlockSpec((tm, y.shape[1]), lambda j, i: (i, 0)) for y in ys]
    in_specs += [pl.BlockSpec((w.shape[0], tn), lambda j, i: (0, j)) for w in ws]
    in_specs += [pl.BlockSpec((tm, tn), functools.partial(lambda j, i, b: (i, b * nn + j), b=b)) for b in range(N_BRANCH)]
    return pl.pallas_call(
        _merge_body, grid=(nn, t // tm), in_specs=in_specs,
        out_specs=pl.BlockSpec((tm, tn), lambda j, i: (i, j)),
        out_shape=jax.ShapeDtypeStruct((t, D_MODEL), BF16),
        compiler_params=_cp(2, 48), name="gated_merge")(*ys, *ws, gates, gates, gates, gates)


def _out_ln_body(mg_ref, w_ref, x_ref, g1_ref, lng_ref, lnb_ref, sc_ref, sh_ref, x1_ref, xmT_ref, *, alpha):
    o = jnp.dot(mg_ref[...], w_ref[...], preferred_element_type=F32)
    x1 = _ln(alpha * x_ref[...] + g1_ref[...] * o, lng_ref[...], lnb_ref[...])
    x1_ref[...] = x1
    xmT_ref[...] = (x1 * (1.0 + sc_ref[...]) + sh_ref[...]).T.astype(xmT_ref.dtype)


def _out_ln(merged, w_out, x, g1, lng, lnb, sc2, sh2, alpha):
    t, d = x.shape
    tm = min(256, t)
    row = lambda i: (i, 0)
    vec = pl.BlockSpec((1, d), lambda i: (0, 0))
    return pl.pallas_call(
        functools.partial(_out_ln_body, alpha=alpha), grid=(t // tm,),
        in_specs=[pl.BlockSpec((tm, d), row), pl.BlockSpec((d, d), lambda i: (0, 0)), pl.BlockSpec((tm, d), row),
                  vec, vec, vec, vec, vec],
        out_specs=[pl.BlockSpec((tm, d), row), pl.BlockSpec((d, tm), lambda i: (0, i))],
        out_shape=[jax.ShapeDtypeStruct((t, d), F32), jax.ShapeDtypeStruct((d, t), BF16)],
        compiler_params=_cp(1, 48), name="out_proj_ln")(merged, w_out, x, g1, lng, lnb, sc2, sh2)


def _peer_topk_body(q_ref, k_ref, s1_o, s2_o, s1l_o, s2l_o, tau_o):
    q = q_ref[...].astype(BF16)
    s1 = jnp.dot(k_ref[0], q[:PEER_HALF], preferred_element_type=F32)
    s2 = jnp.dot(k_ref[1], q[PEER_HALF:], preferred_element_type=F32)
    ninf = -jnp.inf

    def top_distinct(s):
        vals, tops, cnts = s, [], []
        for _ in range(PEER_TOPK):
            m = jnp.max(vals, axis=0, keepdims=True)
            eq = vals == m
            tops.append(m)
            cnts.append(jnp.sum(eq.astype(F32), axis=0, keepdims=True))
            vals = jnp.where(eq, ninf, vals)
        return jnp.concatenate(tops, 0), jnp.concatenate(cnts, 0)

    ta, na = top_distinct(s1)
    tb, nb = top_distinct(s2)
    nrow = [PEER_TOPK // (k + 1) for k in range(PEER_TOPK)]
    pad = -sum(nrow) % 8
    cand = jnp.concatenate([ta[k:k + 1] + tb[:nrow[k]] for k in range(PEER_TOPK)]
                           + [jnp.full((pad, ta.shape[1]), ninf, F32)], 0)
    mult = jnp.concatenate([na[k:k + 1] * nb[:nrow[k]] for k in range(PEER_TOPK)]
                           + [jnp.zeros((pad, ta.shape[1]), F32)], 0)
    vals = cand
    cnt = jnp.zeros_like(cand[0:1])
    tau = jnp.full_like(cand[0:1], ninf)
    for _ in range(PEER_TOPK):
        m = jnp.max(vals, axis=0, keepdims=True)
        eq = vals == m
        tau = jnp.where(cnt < PEER_TOPK, m, tau)
        cnt = cnt + jnp.sum(jnp.where(eq, mult, 0.0), axis=0, keepdims=True)
        vals = jnp.where(eq, ninf, vals)
    cmax = cand[0:1]
    z = jnp.sum(jnp.where(cand >= tau, mult * jnp.exp(cand - cmax), 0.0), axis=0, keepdims=True)
    log2e = math.log2(math.e)
    s2l = s2 * log2e - (cmax * log2e + jnp.log2(z))
    for lc in range(s1.shape[1] // PEER_LANES):
        ls = slice(lc * PEER_LANES, (lc + 1) * PEER_LANES)
        s1_o[lc] = s1[:, ls]
        s2_o[lc] = s2[:, ls]
        s1l_o[lc] = s1[:, ls] * log2e
        s2l_o[lc] = s2l[:, ls]
        tau_o[lc] = tau[:, ls]


PEER_LANES = 128


def _peer_topk(qT, keys):
    t = qT.shape[1]
    tt = min(512, t)
    nl = tt // PEER_LANES
    blk = pl.BlockSpec((None, nl, PEER_NKEYS, PEER_LANES), lambda j, h: (h, j, 0, 0))
    shp = jax.ShapeDtypeStruct((PEER_HEADS, t // PEER_LANES, PEER_NKEYS, PEER_LANES), F32)
    return pl.pallas_call(
        _peer_topk_body, grid=(t // tt, PEER_HEADS),
        in_specs=[pl.BlockSpec((PEER_QDIM, tt), lambda j, h: (h, j)),
                  pl.BlockSpec((None, 2, PEER_NKEYS, PEER_HALF), lambda j, h: (h, 0, 0, 0))],
        out_specs=[blk, blk, blk, blk, pl.BlockSpec((None, nl, 1, PEER_LANES), lambda j, h: (h, j, 0, 0))],
        out_shape=[shp, shp, shp, shp, jax.ShapeDtypeStruct((PEER_HEADS, t // PEER_LANES, 1, PEER_LANES), F32)],
        compiler_params=_cp(2, 32), name="peer_topk")(qT, keys)


PEER_EK = 512
PEER_SUBS = PEER_EK // PEER_NKEYS
PEER_PIECES = 8


def _peer_dense_body(xT_ref, u_ref, vT_ref, s1_ref, s2_ref, s1l_ref, s2l_ref, tau_ref, o_ref, hbuf, awbuf, *, nk):
    s = pl.program_id(1)
    d = o_ref.shape[0]
    h0, h1, aw0, aw1 = hbuf.at[0], hbuf.at[1], awbuf.at[0], awbuf.at[1]

    @pl.when(s == 0)
    def _():
        o_ref[...] = jnp.zeros(o_ref.shape, F32)
        hbuf[...] = jnp.zeros(hbuf.shape, F32)
        awbuf[...] = jnp.zeros(awbuf.shape, BF16)

    def stages(h_cur, h_prev, aw_cur, aw_prev):
        base = jnp.clip(s - 1, 0, nk - 1) * PEER_SUBS
        dq = d // PEER_PIECES

        def second_matmul(q):
            ds_ = slice(q * dq, (q + 1) * dq)
            o_ref[ds_, :] += jnp.dot(vT_ref[ds_, :], aw_cur[...], preferred_element_type=F32)

        def first_matmul(q):
            ks = slice(q * dq, (q + 1) * dq)
            part = jnp.dot(u_ref[:, ks], xT_ref[ks, :], preferred_element_type=F32)
            if q == 0:
                h_cur[...] = part
            else:
                h_cur[...] += part

        def gate_tile(ii, lc):
            i1 = pl.ds(base + ii, 1)
            w = None
            for h in range(PEER_HEADS):
                c = s2_ref[h, lc] + s1_ref[h, lc, i1, :]
                wh = jnp.where(c >= tau_ref[h, lc], jnp.exp2(s2l_ref[h, lc] + s1l_ref[h, lc, i1, :]), 0.0)
                w = wh if w is None else w + wh
            rs = slice(ii * PEER_NKEYS, (ii + 1) * PEER_NKEYS)
            ls = slice(lc * PEER_LANES, (lc + 1) * PEER_LANES)
            aw_prev[rs, ls] = (_gelu(h_prev[rs, ls]) * w).astype(BF16)

        tiles = [(ii, lc) for ii in range(PEER_SUBS) for lc in range(o_ref.shape[1] // PEER_LANES)]
        per = -(-len(tiles) // (2 * PEER_PIECES))
        for q in range(PEER_PIECES):
            second_matmul(q)
            for tl in tiles[(2 * q) * per:(2 * q + 1) * per]:
                gate_tile(*tl)
            first_matmul(q)
            for tl in tiles[(2 * q + 1) * per:(2 * q + 2) * per]:
                gate_tile(*tl)

    @pl.when(s % 2 == 0)
    def _():
        stages(h0, h1, aw0, aw1)

    @pl.when(s % 2 == 1)
    def _():
        stages(h1, h0, aw1, aw0)


def _peer_dense(xmT, u_bf, vT_bf, tk):
    d, t = xmT.shape
    tt = min(512, t)
    nl = tt // PEER_LANES
    nk = PEER_EXPERTS // PEER_EK
    s1, s2, s1l, s2l, tau = tk
    sblk = pl.BlockSpec((PEER_HEADS, nl, PEER_NKEYS, PEER_LANES), lambda j, s: (0, j, 0, 0))
    return pl.pallas_call(
        functools.partial(_peer_dense_body, nk=nk), grid=(t // tt, nk + 2),
        in_specs=[pl.BlockSpec((d, tt), lambda j, s: (0, j)),
                  pl.BlockSpec((PEER_EK, d), lambda j, s: (jnp.minimum(s, nk - 1), 0)),
                  pl.BlockSpec((d, PEER_EK), lambda j, s: (0, jnp.maximum(s - 2, 0))),
                  sblk, sblk, sblk, sblk,
                  pl.BlockSpec((PEER_HEADS, nl, 1, PEER_LANES), lambda j, s: (0, j, 0, 0))],
        out_specs=pl.BlockSpec((d, tt), lambda j, s: (0, j)),
        out_shape=jax.ShapeDtypeStruct((d, t), F32),
        scratch_shapes=[pltpu.VMEM((2, PEER_EK, tt), F32), pltpu.VMEM((2, PEER_EK, tt), BF16)],
        compiler_params=_cp(2, 48), name="peer_dense")(xmT, u_bf, vT_bf, s1, s2, s1l, s2l, tau)


def _ffn_ln_body(fT_ref, x1_ref, g2_ref, lng_ref, lnb_ref, scn_ref, shn_ref, x2_ref, *rest, alpha):
    x2 = _ln(alpha * x1_ref[...] + g2_ref[...] * fT_ref[...].T, lng_ref[...], lnb_ref[...])
    x2_ref[...] = x2
    if rest:
        rest[0][...] = (x2 * (1.0 + scn_ref[...]) + shn_ref[...]).astype(rest[0].dtype)


def _ffn_ln(fT, x1, g2, lng, lnb, scn, shn, alpha, emit_h):
    t, d = x1.shape
    tm = min(256, t)
    row = pl.BlockSpec((tm, d), lambda i: (i, 0))
    vec = pl.BlockSpec((1, d), lambda i: (0, 0))
    out_specs = [row, row] if emit_h else [row]
    out_shape = [jax.ShapeDtypeStruct((t, d), F32)] + ([jax.ShapeDtypeStruct((t, d), BF16)] if emit_h else [])
    return pl.pallas_call(
        functools.partial(_ffn_ln_body, alpha=alpha), grid=(t // tm,),
        in_specs=[pl.BlockSpec((d, tm), lambda i: (0, i)), row, vec, vec, vec, vec, vec],
        out_specs=out_specs, out_shape=out_shape,
        compiler_params=_cp(1, 32), name="ffn_ln")(fT, x1, g2, lng, lnb, scn, shn)


def kernel(x, c, ctx, c_ctx, w_ada, b_ada, w_in, b_in, s5_lam_re, s5_lam_im, s5_log_dt, s5_b_re, s5_b_im, s5_c_re, s5_c_im, s5_d, w_glu, b_glu, qn_gain, kn_gain, sink, sg_ln_g, sg_ln_b, w_sp, b_sp, w_br_a, w_br_b, w_br_c, w_br_d, w_out, ln1_g, ln1_b, ln2_g, ln2_b, w_pq, peer_keys, peer_u, peer_v):
    depth = w_in.shape[0]
    bsz, n_lat, d = x.shape
    assert bsz == 1 and d == D_MODEL and n_lat % 512 == 0 and ctx.shape[1] % BLOCK == 0
    n_ctx = ctx.shape[1]
    alpha = (2 * depth) ** 0.25
    x, ctx = x[0], ctx[0]

    row8 = lax.broadcasted_iota(jnp.int32, (8, d), 0)
    cond8 = jnp.where(row8 == 0, c[0][None], jnp.where(row8 == 1, c_ctx[None], 0.0))
    ada = [_ada(cond8, w_ada, b_ada, l) for l in range(depth)]
    cosf, sinf = _rope_tables(n_lat)
    cos1, sin0 = jnp.ones((n_ctx, HEAD_DIM), F32), jnp.zeros((n_ctx, HEAD_DIM), F32)
    vec = lambda a: a.reshape(1, -1).astype(F32)

    ref_off = {'a': 0, 'qg': 768, 'kg': 1792, 'vg': 2048, 'qw': 2304, 'kw': 3328, 'vw': 3584, 'du': 3840, 'dv': 4608}
    ref_w = {'a': 768, 'qg': 1024, 'kg': 256, 'vg': 256, 'qw': 1024, 'kw': 256, 'vw': 256, 'du': 768, 'dv': 768}
    order = ['a', 'kg', 'qg', 'qw', 'kw', 'vg', 'vw', 'du', 'dv']
    perm = lambda a: jnp.concatenate([a[..., ref_off[n]:ref_off[n] + ref_w[n]] for n in order], -1)
    perm_idx = np.concatenate([np.arange(ref_off[n], ref_off[n] + ref_w[n]) for n in order]).astype(np.int32)

    h_lat = None
    for l in range(depth):
        need_ctx = l < depth - 1
        mods = [[ada[l][r:r + 1, i * d:(i + 1) * d] for i in range(6)] for r in range(2)]
        (sh1, sc1, g1, sh2, sc2, g2), (sh1c, sc1c, g1c, sh2c, sc2c, g2c) = mods
        if need_ctx:
            nxt = [ada[l + 1][r:r + 1, 0:2 * d] for r in range(2)]
            (shn, scn), (shnc, scnc) = [(m[:, :d], m[:, d:]) for m in nxt]
        else:
            shn = scn = shnc = scnc = jnp.zeros((1, d), F32)

        w_small, b_small = perm(w_in[l][:, :SMALL_W]).astype(BF16), vec(jnp.take(b_in[l], perm_idx))
        w_gate, b_gate = w_in[l][:, SMALL_W:].astype(BF16), vec(b_in[l][SMALL_W:])
        s5m = [_s5_mats(s5_lam_re[l, dr], s5_lam_im[l, dr], s5_log_dt[l, dr], s5_b_re[l, dr], s5_b_im[l, dr],
                        s5_c_re[l, dr], s5_c_im[l, dr]) for dr in range(2)]
        wg, bg, s5d = w_glu[l].astype(BF16), vec(b_glu[l]), vec(s5_d[l])
        qn, kn = vec(qn_gain[l]), vec(kn_gain[l])
        wsp, bsp = w_sp[l].astype(BF16), b_sp[l].astype(F32)[:, :, None]
        w_brs = [w[l].astype(BF16) for w in (w_br_a, w_br_b, w_br_c, w_br_d)]
        wo = w_out[l].astype(BF16)
        wpqT = w_pq[l].T.astype(BF16)
        keys = peer_keys[l].astype(BF16)
        u_bf, vT_bf = peer_u[l].astype(BF16), peer_v[l].T.astype(BF16)

        if h_lat is None:
            h_lat, h_ctx = _modulate(x, sc1, sh1), _modulate(ctx, sc1c, sh1c)

        proj = _matmul(h_lat, w_small, b_small, tm=512, tn=1792, out_dtype=BF16, name="in_proj")
        gates = _matmul(h_lat, w_gate, b_gate, tm=512, tn=2048, out_dtype=BF16, name="in_proj_gates")
        proj_c = _matmul(h_ctx, w_small, b_small, tm=512, tn=1792, out_dtype=BF16, name="in_proj_ctx")

        y_dir, y_dir_c = [], []
        for dr in range(2):
            yc_, hc_end = _s5_dir(proj_c, s5m[dr], jnp.zeros((2, S5_CH), F32), bool(dr))
            yl_, _ = _s5_dir(proj, s5m[dr], hc_end, bool(dr))
            y_dir.append(yl_)
            y_dir_c.append(yc_)
        y_a = _s5_out(proj, y_dir[0], y_dir[1], s5d, wg, bg)

        qg, kg, vge, qw, kw = _prep(proj, cosf, sinf, qn, kn, ATTN_SCALE * math.log2(math.e))
        qgc, kgc, vgce, qwc, kwc = _prep(proj_c, cos1, sin0, qn, kn, ATTN_SCALE)
        y_b = _flash(qg, kg, vge, kgc, vgce)
        y_c = _window(qw, kw, proj, kwc, proj_c, sink[l])

        y_d = _spatial_gate(proj, vec(sg_ln_g[l]), vec(sg_ln_b[l]), wsp, bsp)

        merged = _merge((y_a, y_b, y_c, y_d), w_brs, gates)
        x1, xmT = _out_ln(merged, wo, x, g1, vec(ln1_g[l]), vec(ln1_b[l]), sc2, sh2, alpha)
        qT = _matmul(wpqT, xmT, None, tm=D_MODEL, tn=1024, out_dtype=F32, name="peer_query")
        ffnT = _peer_dense(xmT, u_bf, vT_bf, _peer_topk(qT, keys))
        res = _ffn_ln(ffnT, x1, g2, vec(ln2_g[l]), vec(ln2_b[l]), scn, shn, alpha, need_ctx)
        x = res[0]

        if need_ctx:
            h_lat = res[1]
            gates_c = _matmul(h_ctx, w_gate, b_gate, tm=512, tn=2048, out_dtype=BF16, name="in_proj_gates_ctx")
            y_a_c = _s5_out(proj_c, y_dir_c[0], y_dir_c[1], s5d, wg, bg)
            y_b_c = _ctx_attn(qgc, kgc, proj_c, COL_VG, sink[l], False)
            y_c_c = _ctx_attn(qwc, kwc, proj_c, COL_VW, sink[l], True)
            y_d_c = _spatial_gate(proj_c, vec(sg_ln_g[l]), vec(sg_ln_b[l]), wsp, bsp)
            merged_c = _merge((y_a_c, y_b_c, y_c_c, y_d_c), w_brs, gates_c)
            c1, cmT = _out_ln(merged_c, wo, ctx, g1c, vec(ln1_g[l]), vec(ln1_b[l]), sc2c, sh2c, alpha)
            qTc = _matmul(wpqT, cmT, None, tm=D_MODEL, tn=1024, out_dtype=F32, name="peer_query_ctx")
            ffnTc = _peer_dense(cmT, u_bf, vT_bf, _peer_topk(qTc, keys))
            ctx, h_ctx = _ffn_ln(ffnTc, c1, g2c, vec(ln2_g[l]), vec(ln2_b[l]), scnc, shnc, alpha, True)
    return x[None]
```

```python
import functools
import math

import jax
import jax.numpy as jnp
import numpy as np
from jax import lax
from jax.experimental import pallas as pl
from jax.experimental.pallas import tpu as pltpu

F32, BF16 = jnp.float32, jnp.bfloat16

D_MODEL = 2048
GRID_W = 64
BLOCK = 128
HEAD_DIM = 128
AXIS_DIM = HEAD_DIM // 2
ROPE_THETA = 10000.0
ATTN_SCALE = HEAD_DIM ** -0.5
NEG_INF = -1e30
EPS = 1e-6
S5_GROUPS, S5_GROUP_CH, S5_STATE = 48, 16, 64
S5_WIDTH = S5_GROUPS * S5_GROUP_CH
S5_CH = S5_GROUPS * S5_STATE
S5_GB = 16
S5_NB = S5_GROUPS // S5_GB
S5_KB = S5_GB * S5_GROUP_CH
S5_PB = S5_GB * S5_STATE
GA_HEADS, GA_KV = 8, 2
WA_HEADS, WA_KV = 8, 2
KV_GROUP = GA_HEADS // GA_KV
SG_GROUPS, SG_GROUP_CH = 6, 128
SG_WIDTH = SG_GROUPS * SG_GROUP_CH
CHUNK = 128
N_BRANCH = 4
PEER_HEADS, PEER_NKEYS, PEER_QDIM, PEER_TOPK = 8, 128, 256, 16
PEER_HALF = PEER_QDIM // 2
PEER_EXPERTS = PEER_NKEYS * PEER_NKEYS

COL_A, COL_KG, COL_QG, COL_QW, COL_KW, COL_VG, COL_VW, COL_DU, COL_DV = 0, 768, 1024, 2048, 3072, 3328, 3584, 3840, 4608
SMALL_W = 5376
GATE_W = N_BRANCH * D_MODEL

VMEM_CAP_MB = 56


def _cp(ndims, vmem_mb=32):
    return pltpu.CompilerParams(dimension_semantics=("arbitrary",) * ndims,
                                vmem_limit_bytes=min(vmem_mb, VMEM_CAP_MB) << 20)


def _gelu(y):
    return 0.5 * y * (1.0 + lax.erf(y * (2.0 ** -0.5)))


def _ln(y, g, b):
    mu = jnp.mean(y, -1, keepdims=True)
    yc = y - mu
    var = jnp.mean(yc * yc, -1, keepdims=True)
    return yc * lax.rsqrt(var + EPS) * g + b


_NT = (((1,), (1,)), ((), ()))


def _ada_body(c_ref, w_ref, b_ref, o_ref):
    cnd = c_ref[...]
    s = (cnd * jax.nn.sigmoid(cnd)).astype(BF16)
    o_ref[...] = jnp.dot(s, w_ref[...].astype(BF16), preferred_element_type=F32) + b_ref[...]


def _ada(cond8, w_ada, b_ada, l):
    depth, d, n = w_ada.shape
    tn = 1024
    return pl.pallas_call(
        _ada_body, grid=(n // tn,),
        in_specs=[pl.BlockSpec((8, d), lambda j: (0, 0)),
                  pl.BlockSpec((None, d, tn), lambda j: (l, 0, j)),
                  pl.BlockSpec((None, 1, tn), lambda j: (l, 0, j))],
        out_specs=pl.BlockSpec((8, tn), lambda j: (0, j)),
        out_shape=jax.ShapeDtypeStruct((8, n), F32),
        compiler_params=_cp(1, 32), name="ada_mod")(cond8, w_ada, b_ada.reshape(depth, 1, n))


def _mod_body(x_ref, sc_ref, sh_ref, o_ref):
    o_ref[...] = (x_ref[...] * (1.0 + sc_ref[...]) + sh_ref[...]).astype(o_ref.dtype)


def _modulate(x, sc, sh):
    t, d = x.shape
    tm = min(512, t)
    return pl.pallas_call(
        _mod_body, grid=(t // tm,),
        in_specs=[pl.BlockSpec((tm, d), lambda i: (i, 0)),
                  pl.BlockSpec((1, d), lambda i: (0, 0)),
                  pl.BlockSpec((1, d), lambda i: (0, 0))],
        out_specs=pl.BlockSpec((tm, d), lambda i: (i, 0)),
        out_shape=jax.ShapeDtypeStruct((t, d), BF16),
        compiler_params=_cp(1, 24), name="modulate")(x, sc, sh)


def _mm_body(a_ref, w_ref, b_ref, o_ref):
    acc = jnp.dot(a_ref[...], w_ref[...], preferred_element_type=F32)
    o_ref[...] = (acc + b_ref[...]).astype(o_ref.dtype)


def _mm_nobias_body(a_ref, w_ref, o_ref):
    o_ref[...] = jnp.dot(a_ref[...], w_ref[...], preferred_element_type=F32).astype(o_ref.dtype)


def _matmul(a, w, b, *, tm, tn, out_dtype, name):
    m, k = a.shape
    n = w.shape[1]
    tm, tn = min(tm, m), min(tn, n)
    in_specs = [pl.BlockSpec((tm, k), lambda j, i: (i, 0)),
                pl.BlockSpec((k, tn), lambda j, i: (0, j))]
    args = [a, w]
    body = _mm_nobias_body
    if b is not None:
        in_specs.append(pl.BlockSpec((1, tn), lambda j, i: (0, j)))
        args.append(b)
        body = _mm_body
    osz = jnp.dtype(out_dtype).itemsize
    vmem = 2 * (tm * k * 2 + k * tn * 2 + tm * tn * osz) + tm * tn * 4
    return pl.pallas_call(
        body, grid=(n // tn, m // tm), in_specs=in_specs,
        out_specs=pl.BlockSpec((tm, tn), lambda j, i: (i, j)),
        out_shape=jax.ShapeDtypeStruct((m, n), out_dtype),
        compiler_params=_cp(2, (vmem >> 20) + 8), name=name)(*args)


def _s5_body(u_ref, bre_ref, bim_ref, cre_ref, cim_ref, a_ref, h0_ref, y_ref, hT_ref, xr, xi, carry, *, tc, rev):
    @pl.when(pl.program_id(0) == 0)
    def _():
        carry[...] = h0_ref[...]

    for j in range(S5_NB):
        uj = u_ref[:, j * S5_KB:(j + 1) * S5_KB]
        xr[:, j * S5_PB:(j + 1) * S5_PB] = jnp.dot(uj, bre_ref[j], preferred_element_type=F32)
        xi[:, j * S5_PB:(j + 1) * S5_PB] = jnp.dot(uj, bim_ref[j], preferred_element_type=F32)

    for j in range(S5_NB):
        cs = slice(j * S5_PB, (j + 1) * S5_PB)
        ar, ai = a_ref[0:1, cs], a_ref[1:2, cs]

        def step(k, c, cs=cs, ar=ar, ai=ai):
            hr, hi = c
            t = (tc - 1 - k) if rev else k
            nr = ar * hr - ai * hi + xr[pl.ds(t, 1), cs]
            ni = ar * hi + ai * hr + xi[pl.ds(t, 1), cs]
            xr[pl.ds(t, 1), cs] = nr
            xi[pl.ds(t, 1), cs] = ni
            return nr, ni

        hr, hi = lax.fori_loop(0, tc, step, (carry[0:1, cs], carry[1:2, cs]), unroll=8)
        carry[0:1, cs] = hr
        carry[1:2, cs] = hi

    for j in range(S5_NB):
        cs = slice(j * S5_PB, (j + 1) * S5_PB)
        y_ref[:, j * S5_KB:(j + 1) * S5_KB] = (
            jnp.dot(xr[:, cs].astype(BF16), cre_ref[j], preferred_element_type=F32)
            + jnp.dot(xi[:, cs].astype(BF16), cim_ref[j], preferred_element_type=F32))
    hT_ref[...] = carry[...]


def _s5_dir(proj, mats, h0, rev):
    t = proj.shape[0]
    tc = min(256, t)
    nt = t // tc
    bre, bim, cre, cim, a = mats
    tmap = (lambda i: (nt - 1 - i, 0)) if rev else (lambda i: (i, 0))
    full3 = lambda i: (0, 0, 0)
    full2 = lambda i: (0, 0)
    return pl.pallas_call(
        functools.partial(_s5_body, tc=tc, rev=rev), grid=(nt,),
        in_specs=[pl.BlockSpec((tc, S5_WIDTH), tmap),
                  pl.BlockSpec((S5_NB, S5_KB, S5_PB), full3), pl.BlockSpec((S5_NB, S5_KB, S5_PB), full3),
                  pl.BlockSpec((S5_NB, S5_PB, S5_KB), full3), pl.BlockSpec((S5_NB, S5_PB, S5_KB), full3),
                  pl.BlockSpec((2, S5_CH), full2), pl.BlockSpec((2, S5_CH), full2)],
        out_specs=[pl.BlockSpec((tc, S5_WIDTH), tmap), pl.BlockSpec((2, S5_CH), full2)],
        out_shape=[jax.ShapeDtypeStruct((t, S5_WIDTH), F32), jax.ShapeDtypeStruct((2, S5_CH), F32)],
        scratch_shapes=[pltpu.VMEM((tc, S5_CH), F32), pltpu.VMEM((tc, S5_CH), F32), pltpu.VMEM((2, S5_CH), F32)],
        compiler_params=_cp(1, 40), name="s5_scan_bwd" if rev else "s5_scan_fwd")(proj, bre, bim, cre, cim, a, h0)


def _s5_out_body(u_ref, yf_ref, yb_ref, d_ref, wg_ref, bg_ref, o_ref):
    y = u_ref[...].astype(F32) * d_ref[...] + yf_ref[...] + yb_ref[...]
    g = _gelu(y)
    z = jnp.dot(g.astype(BF16), wg_ref[...], preferred_element_type=F32) + bg_ref[...]
    o_ref[...] = (g * jax.nn.sigmoid(z)).astype(o_ref.dtype)


def _s5_out(proj, yf, yb, d, wg, bg):
    t = proj.shape[0]
    tm = min(512, t)
    row = lambda i: (i, 0)
    full = lambda i: (0, 0)
    return pl.pallas_call(
        _s5_out_body, grid=(t // tm,),
        in_specs=[pl.BlockSpec((tm, S5_WIDTH), row), pl.BlockSpec((tm, S5_WIDTH), row), pl.BlockSpec((tm, S5_WIDTH), row),
                  pl.BlockSpec((1, S5_WIDTH), full), pl.BlockSpec((S5_WIDTH, S5_WIDTH), full), pl.BlockSpec((1, S5_WIDTH), full)],
        out_specs=pl.BlockSpec((tm, S5_WIDTH), row),
        out_shape=jax.ShapeDtypeStruct((t, S5_WIDTH), BF16),
        compiler_params=_cp(1, 24), name="s5_glu")(proj, yf, yb, d, wg, bg)


def _s5_mats(lam_re, lam_im, log_dt, b_re, b_im, c_re, c_im):
    dt = jnp.exp(log_dt)[:, None]
    mag = jnp.exp(lam_re * dt)
    ar, ai = mag * jnp.cos(lam_im * dt), mag * jnp.sin(lam_im * dt)
    nr, ni = ar - 1.0, ai
    den = lam_re * lam_re + lam_im * lam_im
    cr = (nr * lam_re + ni * lam_im) / den
    ci = (ni * lam_re - nr * lam_im) / den
    fre = cr[:, :, None] * b_re - ci[:, :, None] * b_im
    fim = cr[:, :, None] * b_im + ci[:, :, None] * b_re
    eye = jnp.eye(S5_GB, dtype=F32)

    def blk_in(f):
        f = f.reshape(S5_NB, S5_GB, S5_STATE, S5_GROUP_CH)
        m = jnp.einsum('jgph,gk->jghkp', f, eye)
        return m.reshape(S5_NB, S5_KB, S5_PB).astype(BF16)

    def blk_out(c):
        c = c.reshape(S5_NB, S5_GB, S5_GROUP_CH, S5_STATE)
        m = jnp.einsum('jghp,gk->jgpkh', c, eye)
        return m.reshape(S5_NB, S5_PB, S5_KB).astype(BF16)

    is_re = lax.broadcasted_iota(jnp.int32, (2, S5_CH), 0) == 0
    a = jnp.where(is_re, ar.reshape(1, -1), ai.reshape(1, -1))
    return blk_in(fre), blk_in(fim), blk_out(c_re), blk_out(-c_im), a


def _prep_body(qg_ref, kg_ref, vg_ref, qw_ref, kw_ref, cos_ref, sin_ref, qn_ref, kn_ref, oqg, okg, ovg, oqw, okw, *,
               qg_scale):
    cosf, sinf = cos_ref[...], sin_ref[...]
    lane = lax.broadcasted_iota(jnp.int32, cosf.shape, 1)
    low = (lane % AXIS_DIM) < (AXIS_DIM // 2)

    def rope(t):
        partner = jnp.where(low, pltpu.roll(t, HEAD_DIM - AXIS_DIM // 2, 1), pltpu.roll(t, AXIS_DIM // 2, 1))
        return t * cosf + partner * sinf

    def rms(t, g):
        return t * lax.rsqrt(jnp.mean(t * t, -1, keepdims=True) + EPS) * g

    for h in range(GA_HEADS):
        cs = slice(h * HEAD_DIM, (h + 1) * HEAD_DIM)
        oqg[:, cs] = (rope(rms(qg_ref[:, cs].astype(F32), qn_ref[...])) * qg_scale).astype(BF16)
        oqw[:, cs] = (rope(qw_ref[:, cs].astype(F32)) * ATTN_SCALE).astype(BF16)
    for h in range(GA_KV):
        cs = slice(h * HEAD_DIM, (h + 1) * HEAD_DIM)
        okg[:, cs] = rope(rms(kg_ref[:, cs].astype(F32), kn_ref[...])).astype(BF16)
        okw[:, cs] = rope(kw_ref[:, cs].astype(F32)).astype(BF16)
        ovg[:, 2 * h * HEAD_DIM:(2 * h + 1) * HEAD_DIM] = vg_ref[:, cs]
        ovg[:, (2 * h + 1) * HEAD_DIM:(2 * h + 2) * HEAD_DIM] = jnp.ones((vg_ref.shape[0], HEAD_DIM), BF16)


def _prep(proj, cosf, sinf, qn, kn, qg_scale):
    t = proj.shape[0]
    tr = min(256, t)
    qw_, kw_ = GA_HEADS * HEAD_DIM, GA_KV * HEAD_DIM
    row = lambda i: (i, 0)
    full = lambda i: (0, 0)
    return pl.pallas_call(
        functools.partial(_prep_body, qg_scale=qg_scale), grid=(t // tr,),
        in_specs=[pl.BlockSpec((tr, qw_), lambda i: (i, COL_QG // qw_)),
                  pl.BlockSpec((tr, kw_), lambda i: (i, COL_KG // kw_)),
                  pl.BlockSpec((tr, kw_), lambda i: (i, COL_VG // kw_)),
                  pl.BlockSpec((tr, qw_), lambda i: (i, COL_QW // qw_)),
                  pl.BlockSpec((tr, kw_), lambda i: (i, COL_KW // kw_)),
                  pl.BlockSpec((tr, HEAD_DIM), row), pl.BlockSpec((tr, HEAD_DIM), row),
                  pl.BlockSpec((1, HEAD_DIM), full), pl.BlockSpec((1, HEAD_DIM), full)],
        out_specs=[pl.BlockSpec((tr, qw_), row), pl.BlockSpec((tr, kw_), row), pl.BlockSpec((tr, 2 * kw_), row),
                   pl.BlockSpec((tr, qw_), row), pl.BlockSpec((tr, kw_), row)],
        out_shape=[jax.ShapeDtypeStruct((t, qw_), BF16), jax.ShapeDtypeStruct((t, kw_), BF16),
                   jax.ShapeDtypeStruct((t, 2 * kw_), BF16),
                   jax.ShapeDtypeStruct((t, qw_), BF16), jax.ShapeDtypeStruct((t, kw_), BF16)],
        compiler_params=_cp(1, 24), name="qk_prep")(proj, proj, proj, proj, proj, cosf, sinf, qn, kn)


def _rope_tables(n_lat):
    rows = n_lat // GRID_W
    row = jnp.repeat(jnp.arange(rows, dtype=F32), GRID_W)
    col = jnp.tile(jnp.arange(GRID_W, dtype=F32), rows)
    inv = jnp.power(ROPE_THETA, -jnp.arange(0, AXIS_DIM, 2, dtype=F32) / AXIS_DIM)
    ang_r, ang_c = row[:, None] * inv, col[:, None] * inv
    cr, sr, cc, sc = jnp.cos(ang_r), jnp.sin(ang_r), jnp.cos(ang_c), jnp.sin(ang_c)
    return jnp.concatenate([cr, cr, cc, cc], -1), jnp.concatenate([-sr, sr, -sc, sc], -1)


def _flash_body(q_ref, kc_ref, vc_ref, k_ref, v_ref, o_ref, qs, m_s, acc, *, tq, tk, nk):
    for g in range(KV_GROUP):
        qs[g * tq:(g + 1) * tq, :] = q_ref[:, g * HEAD_DIM:(g + 1) * HEAD_DIM]
    m_s[...] = jnp.full(m_s.shape, -jnp.inf, F32)
    acc[...] = jnp.zeros(acc.shape, F32)

    def update(k, v):
        for g in range(KV_GROUP):
            rs = slice(g * tq, (g + 1) * tq)
            s = lax.dot_general(qs[rs, :], k, _NT, preferred_element_type=F32)
            m_prev = m_s[rs, :]
            m_next = jnp.maximum(m_prev, jnp.max(s, axis=1, keepdims=True))
            alpha = jnp.exp2(m_prev - m_next)
            p = jnp.exp2(s - jnp.concatenate([m_next] * (s.shape[1] // HEAD_DIM), axis=1))
            acc[rs, :] = (jnp.concatenate([alpha, alpha], axis=1) * acc[rs, :]
                          + jnp.dot(p.astype(BF16), v, preferred_element_type=F32))
            m_s[rs, :] = m_next

    update(kc_ref[...], vc_ref[...])

    def body(j, carry):
        rows = pl.ds(pl.multiple_of(j * tk, tk), tk)
        update(k_ref[rows, :], v_ref[rows, :])
        return carry

    lax.fori_loop(0, nk, body, 0)
    out = acc[:, :HEAD_DIM] / acc[:, HEAD_DIM:]
    for g in range(KV_GROUP):
        o_ref[:, g * HEAD_DIM:(g + 1) * HEAD_DIM] = out[g * tq:(g + 1) * tq, :].astype(o_ref.dtype)


def _flash(q, kg, vge, kgc, vgce):
    t = q.shape[0]
    n_ctx = kgc.shape[0]
    tq = min(256, t)
    tk = min(512, t)
    gw = KV_GROUP * HEAD_DIM
    head = lambda h, i: (0, h)
    return pl.pallas_call(
        functools.partial(_flash_body, tq=tq, tk=tk, nk=t // tk), grid=(GA_KV, t // tq),
        in_specs=[pl.BlockSpec((tq, gw), lambda h, i: (i, h)),
                  pl.BlockSpec((n_ctx, HEAD_DIM), head), pl.BlockSpec((n_ctx, 2 * HEAD_DIM), head),
                  pl.BlockSpec((t, HEAD_DIM), head), pl.BlockSpec((t, 2 * HEAD_DIM), head)],
        out_specs=pl.BlockSpec((tq, gw), lambda h, i: (i, h)),
        out_shape=jax.ShapeDtypeStruct((t, GA_HEADS * HEAD_DIM), BF16),
        scratch_shapes=[pltpu.VMEM((KV_GROUP * tq, HEAD_DIM), BF16), pltpu.VMEM((KV_GROUP * tq, HEAD_DIM), F32),
                        pltpu.VMEM((KV_GROUP * tq, 2 * HEAD_DIM), F32)],
        compiler_params=_cp(2, 48), name="global_attn")(q, kgc, vgce, kg, vge)


def _win_body(q_ref, km_ref, k0_ref, kp_ref, vm_ref, v0_ref, vp_ref, kc_ref, vc_ref, sink_ref, o_ref, *, nb):
    i = pl.program_id(1)
    q = jnp.concatenate([q_ref[:, g * HEAD_DIM:(g + 1) * HEAD_DIM] for g in range(KV_GROUP)], axis=0)
    kw = jnp.concatenate([km_ref[...], k0_ref[...], kp_ref[...]], axis=0)
    vw = jnp.concatenate([vm_ref[...], v0_ref[...], vp_ref[...]], axis=0)
    s_w = lax.dot_general(q, kw, _NT, preferred_element_type=F32)
    r = lax.broadcasted_iota(jnp.int32, s_w.shape, 0) % BLOCK
    c = lax.broadcasted_iota(jnp.int32, s_w.shape, 1)
    cc = c % BLOCK
    iv = jnp.full(s_w.shape, i, jnp.int32)
    valid = (((c < BLOCK) & (cc >= r) & (iv >= 1)) | ((c >= BLOCK) & (c < 2 * BLOCK))
             | ((c >= 2 * BLOCK) & (cc <= r) & (iv + 1 < nb)))
    s_w = jnp.where(valid, s_w, NEG_INF)
    s_c = lax.dot_general(q, kc_ref[...], _NT, preferred_element_type=F32)
    sink = sink_ref[0]
    m = jnp.maximum(jnp.maximum(jnp.max(s_w, -1, keepdims=True), jnp.max(s_c, -1, keepdims=True)), sink)
    pw, pc = jnp.exp(s_w - m), jnp.exp(s_c - m)
    l = jnp.sum(pw, -1, keepdims=True) + jnp.sum(pc, -1, keepdims=True) + jnp.exp(sink - m)
    inv = 1.0 / l
    out = (jnp.dot((pc * inv).astype(BF16), vc_ref[...], preferred_element_type=F32)
           + jnp.dot((pw * inv).astype(BF16), vw, preferred_element_type=F32))
    for g in range(KV_GROUP):
        o_ref[:, g * HEAD_DIM:(g + 1) * HEAD_DIM] = out[g * BLOCK:(g + 1) * BLOCK, :].astype(o_ref.dtype)


def _sink_rows(sink, rows_per_head):
    return jnp.repeat(sink.astype(F32).reshape(WA_KV, KV_GROUP), rows_per_head, axis=1)[:, :, None]


def _window(qw, kw, proj, kwc, proj_c, sink):
    t = qw.shape[0]
    n_ctx = kwc.shape[0]
    nb = t // BLOCK
    gw = KV_GROUP * HEAD_DIM
    vcol = COL_VW // HEAD_DIM
    kspec = lambda f: pl.BlockSpec((BLOCK, HEAD_DIM), f)
    return pl.pallas_call(
        functools.partial(_win_body, nb=nb), grid=(WA_KV, nb),
        in_specs=[pl.BlockSpec((BLOCK, gw), lambda h, i: (i, h)),
                  kspec(lambda h, i: (jnp.maximum(i - 1, 0), h)), kspec(lambda h, i: (i, h)),
                  kspec(lambda h, i: (jnp.minimum(i + 1, nb - 1), h)),
                  kspec(lambda h, i: (jnp.maximum(i - 1, 0), vcol + h)), kspec(lambda h, i: (i, vcol + h)),
                  kspec(lambda h, i: (jnp.minimum(i + 1, nb - 1), vcol + h)),
                  pl.BlockSpec((n_ctx, HEAD_DIM), lambda h, i: (0, h)),
                  pl.BlockSpec((n_ctx, HEAD_DIM), lambda h, i: (0, vcol + h)),
                  pl.BlockSpec((1, KV_GROUP * BLOCK, 1), lambda h, i: (h, 0, 0))],
        out_specs=pl.BlockSpec((BLOCK, gw), lambda h, i: (i, h)),
        out_shape=jax.ShapeDtypeStruct((t, WA_HEADS * HEAD_DIM), BF16),
        compiler_params=_cp(2, 24), name="window_attn")(
            qw, kw, kw, kw, proj, proj, proj, kwc, proj_c, _sink_rows(sink, BLOCK))


def _ctx_attn_body(q_ref, k_ref, v_ref, sink_ref, o_ref, *, n_ctx, use_sink):
    q = jnp.concatenate([q_ref[:, g * HEAD_DIM:(g + 1) * HEAD_DIM] for g in range(KV_GROUP)], axis=0)
    s = lax.dot_general(q, k_ref[...], _NT, preferred_element_type=F32)
    m = jnp.max(s, -1, keepdims=True)
    if use_sink:
        m = jnp.maximum(m, sink_ref[0])
    p = jnp.exp(s - m)
    l = jnp.sum(p, -1, keepdims=True)
    if use_sink:
        l = l + jnp.exp(sink_ref[0] - m)
    out = jnp.dot((p / l).astype(BF16), v_ref[...], preferred_element_type=F32)
    for g in range(KV_GROUP):
        o_ref[:, g * HEAD_DIM:(g + 1) * HEAD_DIM] = out[g * n_ctx:(g + 1) * n_ctx, :].astype(o_ref.dtype)


def _ctx_attn(q, k, proj_c, vcol0, sink, use_sink):
    n_ctx = q.shape[0]
    gw = KV_GROUP * HEAD_DIM
    vcol = vcol0 // HEAD_DIM
    return pl.pallas_call(
        functools.partial(_ctx_attn_body, n_ctx=n_ctx, use_sink=use_sink), grid=(GA_KV,),
        in_specs=[pl.BlockSpec((n_ctx, gw), lambda h: (0, h)),
                  pl.BlockSpec((n_ctx, HEAD_DIM), lambda h: (0, h)),
                  pl.BlockSpec((n_ctx, HEAD_DIM), lambda h: (0, vcol + h)),
                  pl.BlockSpec((1, KV_GROUP * n_ctx, 1), lambda h: (h, 0, 0))],
        out_specs=pl.BlockSpec((n_ctx, gw), lambda h: (0, h)),
        out_shape=jax.ShapeDtypeStruct((n_ctx, GA_HEADS * HEAD_DIM), BF16),
        compiler_params=_cp(1, 24), name="ctx_attn_sink" if use_sink else "ctx_attn")(
            q, k, proj_c, _sink_rows(sink, n_ctx))


def _sg_body(u_ref, v_ref, g_ref, b_ref, w_ref, bs_ref, o_ref, *, nch):
    for n in range(nch):
        rs = slice(n * CHUNK, (n + 1) * CHUNK)
        vn = _ln(v_ref[rs, :].astype(F32), g_ref[...], b_ref[...]).astype(BF16)
        for g in range(SG_GROUPS):
            cs = slice(g * SG_GROUP_CH, (g + 1) * SG_GROUP_CH)
            s = jnp.dot(w_ref[g], vn[:, cs], preferred_element_type=F32) + bs_ref[g]
            o_ref[rs, cs] = (u_ref[rs, cs].astype(F32) * s).astype(o_ref.dtype)


def _spatial_gate(proj, g_v, b_v, w_sp, b_sp):
    t = proj.shape[0]
    tm = min(512, t)
    full = lambda i: (0, 0)
    return pl.pallas_call(
        functools.partial(_sg_body, nch=tm // CHUNK), grid=(t // tm,),
        in_specs=[pl.BlockSpec((tm, SG_WIDTH), lambda i: (i, COL_DU // SG_WIDTH)),
                  pl.BlockSpec((tm, SG_WIDTH), lambda i: (i, COL_DV // SG_WIDTH)),
                  pl.BlockSpec((1, SG_WIDTH), full), pl.BlockSpec((1, SG_WIDTH), full),
                  pl.BlockSpec((SG_GROUPS, CHUNK, CHUNK), lambda i: (0, 0, 0)),
                  pl.BlockSpec((SG_GROUPS, CHUNK, 1), lambda i: (0, 0, 0))],
        out_specs=pl.BlockSpec((tm, SG_WIDTH), lambda i: (i, 0)),
        out_shape=jax.ShapeDtypeStruct((t, SG_WIDTH), BF16),
        compiler_params=_cp(1, 24), name="spatial_gate")(proj, proj, g_v, b_v, w_sp, b_sp)


def _merge_body(ya, yb, yc, yd, wa, wb, wc, wd, ga, gb, gc, gd, o_ref):
    def br(y, w, g):
        return jax.nn.sigmoid(g[...].astype(F32)) * jnp.dot(y[...], w[...], preferred_element_type=F32)
    o_ref[...] = (br(ya, wa, ga) + br(yb, wb, gb) + br(yc, wc, gc) + br(yd, wd, gd)).astype(o_ref.dtype)


def _merge(ys, ws, gates):
    t = ys[0].shape[0]
    tm, tn = min(512, t), 1024
    nn = D_MODEL // tn
    in_specs = [pl.BlockSpec((tm, y.shape[1]), lambda j, i: (i, 0)) for y in ys]
    in_specs += [pl.BlockSpec((w.shape[0], tn), lambda j, i: (0, j)) for w in ws]
    in_specs += [pl.BlockSpec((tm, tn), functools.partial(lambda j, i, b: (i, b * nn + j), b=b)) for b in range(N_BRANCH)]
    return pl.pallas_call(
        _merge_body, grid=(nn, t // tm), in_specs=in_specs,
        out_specs=pl.BlockSpec((tm, tn), lambda j, i: (i, j)),
        out_shape=jax.ShapeDtypeStruct((t, D_MODEL), BF16),
        compiler_params=_cp(2, 48), name="gated_merge")(*ys, *ws, gates, gates, gates, gates)


def _out_ln_body(mg_ref, w_ref, x_ref, g1_ref, lng_ref, lnb_ref, sc_ref, sh_ref, x1_ref, xmT_ref, *, alpha):
    o = jnp.dot(mg_ref[...], w_ref[...], preferred_element_type=F32)
    x1 = _ln(alpha * x_ref[...] + g1_ref[...] * o, lng_ref[...], lnb_ref[...])
    x1_ref[...] = x1
    xmT_ref[...] = (x1 * (1.0 + sc_ref[...]) + sh_ref[...]).T.astype(xmT_ref.dtype)


def _out_ln(merged, w_out, x, g1, lng, lnb, sc2, sh2, alpha):
    t, d = x.shape
    tm = min(256, t)
    row = lambda i: (i, 0)
    vec = pl.BlockSpec((1, d), lambda i: (0, 0))
    return pl.pallas_call(
        functools.partial(_out_ln_body, alpha=alpha), grid=(t // tm,),
        in_specs=[pl.BlockSpec((tm, d), row), pl.BlockSpec((d, d), lambda i: (0, 0)), pl.BlockSpec((tm, d), row),
                  vec, vec, vec, vec, vec],
        out_specs=[pl.BlockSpec((tm, d), row), pl.BlockSpec((d, tm), lambda i: (0, i))],
        out_shape=[jax.ShapeDtypeStruct((t, d), F32), jax.ShapeDtypeStruct((d, t), BF16)],
        compiler_params=_cp(1, 48), name="out_proj_ln")(merged, w_out, x, g1, lng, lnb, sc2, sh2)


def _peer_topk_body(q_ref, k_ref, s1_o, s2_o, s1l_o, s2l_o, tau_o):
    q = q_ref[...].astype(BF16)
    s1 = jnp.dot(k_ref[0], q[:PEER_HALF], preferred_element_type=F32)
    s2 = jnp.dot(k_ref[1], q[PEER_HALF:], preferred_element_type=F32)
    ninf = -jnp.inf

    def top_distinct(s):
        vals, tops, cnts = s, [], []
        for _ in range(PEER_TOPK):
            m = jnp.max(vals, axis=0, keepdims=True)
            eq = vals == m
            tops.append(m)
            cnts.append(jnp.sum(eq.astype(F32), axis=0, keepdims=True))
            vals = jnp.where(eq, ninf, vals)
        return jnp.concatenate(tops, 0), jnp.concatenate(cnts, 0)

    ta, na = top_distinct(s1)
    tb, nb = top_distinct(s2)
    nrow = [PEER_TOPK // (k + 1) for k in range(PEER_TOPK)]
    pad = -sum(nrow) % 8
    cand = jnp.concatenate([ta[k:k + 1] + tb[:nrow[k]] for k in range(PEER_TOPK)]
                           + [jnp.full((pad, ta.shape[1]), ninf, F32)], 0)
    mult = jnp.concatenate([na[k:k + 1] * nb[:nrow[k]] for k in range(PEER_TOPK)]
                           + [jnp.zeros((pad, ta.shape[1]), F32)], 0)
    vals = cand
    cnt = jnp.zeros_like(cand[0:1])
    tau = jnp.full_like(cand[0:1], ninf)
    for _ in range(PEER_TOPK):
        m = jnp.max(vals, axis=0, keepdims=True)
        eq = vals == m
        tau = jnp.where(cnt < PEER_TOPK, m, tau)
        cnt = cnt + jnp.sum(jnp.where(eq, mult, 0.0), axis=0, keepdims=True)
        vals = jnp.where(eq, ninf, vals)
    cmax = cand[0:1]
    z = jnp.sum(jnp.where(cand >= tau, mult * jnp.exp(cand - cmax), 0.0), axis=0, keepdims=True)
    g1 = jnp.exp(s1 - ta[0:1]) * (1.0 / z)
    g2 = jnp.exp(s2 - tb[0:1])
    for lc in range(s1.shape[1] // PEER_LANES):
        ls = slice(lc * PEER_LANES, (lc + 1) * PEER_LANES)
        s1_o[lc] = s1[:, ls]
        s2_o[lc] = s2[:, ls]
        s1l_o[lc] = g1[:, ls]
        s2l_o[lc] = g2[:, ls]
        tau_o[lc] = tau[:, ls]


PEER_LANES = 128


def _peer_topk(qT, keys):
    t = qT.shape[1]
    tt = min(512, t)
    nl = tt // PEER_LANES
    blk = pl.BlockSpec((None, nl, PEER_NKEYS, PEER_LANES), lambda j, h: (h, j, 0, 0))
    shp = jax.ShapeDtypeStruct((PEER_HEADS, t // PEER_LANES, PEER_NKEYS, PEER_LANES), F32)
    return pl.pallas_call(
        _peer_topk_body, grid=(t // tt, PEER_HEADS),
        in_specs=[pl.BlockSpec((PEER_QDIM, tt), lambda j, h: (h, j)),
                  pl.BlockSpec((None, 2, PEER_NKEYS, PEER_HALF), lambda j, h: (h, 0, 0, 0))],
        out_specs=[blk, blk, blk, blk, pl.BlockSpec((None, nl, 1, PEER_LANES), lambda j, h: (h, j, 0, 0))],
        out_shape=[shp, shp, shp, shp, jax.ShapeDtypeStruct((PEER_HEADS, t // PEER_LANES, 1, PEER_LANES), F32)],
        compiler_params=_cp(2, 32), name="peer_topk")(qT, keys)


PEER_EK = 512
PEER_SUBS = PEER_EK // PEER_NKEYS
PEER_PIECES = 8


def _peer_dense_body(xT_ref, u_ref, vT_ref, s1_ref, s2_ref, s1l_ref, s2l_ref, tau_ref, o_ref, hbuf, awbuf, *, nk):
    s = pl.program_id(1)
    d = o_ref.shape[0]
    h0, h1, aw0, aw1 = hbuf.at[0], hbuf.at[1], awbuf.at[0], awbuf.at[1]

    @pl.when(s == 0)
    def _():
        o_ref[...] = jnp.zeros(o_ref.shape, F32)
        hbuf[...] = jnp.zeros(hbuf.shape, F32)
        awbuf[...] = jnp.zeros(awbuf.shape, BF16)

    def stages(h_cur, h_prev, aw_cur, aw_prev):
        base = jnp.clip(s - 1, 0, nk - 1) * PEER_SUBS
        dq = d // PEER_PIECES

        def second_matmul(q):
            ds_ = slice(q * dq, (q + 1) * dq)
            o_ref[ds_, :] += jnp.dot(vT_ref[ds_, :], aw_cur[...], preferred_element_type=F32)

        def first_matmul(q):
            ks = slice(q * dq, (q + 1) * dq)
            part = jnp.dot(u_ref[:, ks], xT_ref[ks, :], preferred_element_type=F32)
            if q == 0:
                h_cur[...] = part
            else:
                h_cur[...] += part

        def gate_tile(ii, lc):
            i1 = pl.ds(base + ii, 1)
            w = None
            for h in range(PEER_HEADS):
                c = s2_ref[h, lc] + s1_ref[h, lc, i1, :]
                wh = jnp.where(c >= tau_ref[h, lc], s2l_ref[h, lc] * s1l_ref[h, lc, i1, :], 0.0)
                w = wh if w is None else w + wh
            rs = slice(ii * PEER_NKEYS, (ii + 1) * PEER_NKEYS)
            ls = slice(lc * PEER_LANES, (lc + 1) * PEER_LANES)
            aw_prev[rs, ls] = (_gelu(h_prev[rs, ls]) * w).astype(BF16)

        tiles = [(ii, lc) for ii in range(PEER_SUBS) for lc in range(o_ref.shape[1] // PEER_LANES)]
        per = -(-len(tiles) // (2 * PEER_PIECES))
        for q in range(PEER_PIECES):
            second_matmul(q)
            for tl in tiles[(2 * q) * per:(2 * q + 1) * per]:
                gate_tile(*tl)
            first_matmul(q)
            for tl in tiles[(2 * q + 1) * per:(2 * q + 2) * per]:
                gate_tile(*tl)

    @pl.when(s % 2 == 0)
    def _():
        stages(h0, h1, aw0, aw1)

    @pl.when(s % 2 == 1)
    def _():
        stages(h1, h0, aw1, aw0)


def _peer_dense(xmT, u_bf, vT_bf, tk):
    d, t = xmT.shape
    tt = min(512, t)
    nl = tt // PEER_LANES
    nk = PEER_EXPERTS // PEER_EK
    s1, s2, s1l, s2l, tau = tk
    sblk = pl.BlockSpec((PEER_HEADS, nl, PEER_NKEYS, PEER_LANES), lambda j, s: (0, j, 0, 0))
    return pl.pallas_call(
        functools.partial(_peer_dense_body, nk=nk), grid=(t // tt, nk + 2),
        in_specs=[pl.BlockSpec((d, tt), lambda j, s: (0, j)),
                  pl.BlockSpec((PEER_EK, d), lambda j, s: (jnp.minimum(s, nk - 1), 0)),
                  pl.BlockSpec((d, PEER_EK), lambda j, s: (0, jnp.maximum(s - 2, 0))),
                  sblk, sblk, sblk, sblk,
                  pl.BlockSpec((PEER_HEADS, nl, 1, PEER_LANES), lambda j, s: (0, j, 0, 0))],
        out_specs=pl.BlockSpec((d, tt), lambda j, s: (0, j)),
        out_shape=jax.ShapeDtypeStruct((d, t), F32),
        scratch_shapes=[pltpu.VMEM((2, PEER_EK, tt), F32), pltpu.VMEM((2, PEER_EK, tt), BF16)],
        compiler_params=_cp(2, 48), name="peer_dense")(xmT, u_bf, vT_bf, s1, s2, s1l, s2l, tau)


def _ffn_ln_body(fT_ref, x1_ref, g2_ref, lng_ref, lnb_ref, scn_ref, shn_ref, x2_ref, *rest, alpha):
    x2 = _ln(alpha * x1_ref[...] + g2_ref[...] * fT_ref[...].T, lng_ref[...], lnb_ref[...])
    x2_ref[...] = x2
    if rest:
        rest[0][...] = (x2 * (1.0 + scn_ref[...]) + shn_ref[...]).astype(rest[0].dtype)


def _ffn_ln(fT, x1, g2, lng, lnb, scn, shn, alpha, emit_h):
    t, d = x1.shape
    tm = min(256, t)
    row = pl.BlockSpec((tm, d), lambda i: (i, 0))
    vec = pl.BlockSpec((1, d), lambda i: (0, 0))
    out_specs = [row, row] if emit_h else [row]
    out_shape = [jax.ShapeDtypeStruct((t, d), F32)] + ([jax.ShapeDtypeStruct((t, d), BF16)] if emit_h else [])
    return pl.pallas_call(
        functools.partial(_ffn_ln_body, alpha=alpha), grid=(t // tm,),
        in_specs=[pl.BlockSpec((d, tm), lambda i: (0, i)), row, vec, vec, vec, vec, vec],
        out_specs=out_specs, out_shape=out_shape,
        compiler_params=_cp(1, 32), name="ffn_ln")(fT, x1, g2, lng, lnb, scn, shn)


def kernel(x, c, ctx, c_ctx, w_ada, b_ada, w_in, b_in, s5_lam_re, s5_lam_im, s5_log_dt, s5_b_re, s5_b_im, s5_c_re, s5_c_im, s5_d, w_glu, b_glu, qn_gain, kn_gain, sink, sg_ln_g, sg_ln_b, w_sp, b_sp, w_br_a, w_br_b, w_br_c, w_br_d, w_out, ln1_g, ln1_b, ln2_g, ln2_b, w_pq, peer_keys, peer_u, peer_v):
    depth = w_in.shape[0]
    bsz, n_lat, d = x.shape
    assert bsz == 1 and d == D_MODEL and n_lat % 512 == 0 and ctx.shape[1] % BLOCK == 0
    n_ctx = ctx.shape[1]
    alpha = (2 * depth) ** 0.25
    x, ctx = x[0], ctx[0]

    row8 = lax.broadcasted_iota(jnp.int32, (8, d), 0)
    cond8 = jnp.where(row8 == 0, c[0][None], jnp.where(row8 == 1, c_ctx[None], 0.0))
    ada = [_ada(cond8, w_ada, b_ada, l) for l in range(depth)]
    cosf, sinf = _rope_tables(n_lat)
    cos1, sin0 = jnp.ones((n_ctx, HEAD_DIM), F32), jnp.zeros((n_ctx, HEAD_DIM), F32)
    vec = lambda a: a.reshape(1, -1).astype(F32)

    ref_off = {'a': 0, 'qg': 768, 'kg': 1792, 'vg': 2048, 'qw': 2304, 'kw': 3328, 'vw': 3584, 'du': 3840, 'dv': 4608}
    ref_w = {'a': 768, 'qg': 1024, 'kg': 256, 'vg': 256, 'qw': 1024, 'kw': 256, 'vw': 256, 'du': 768, 'dv': 768}
    order = ['a', 'kg', 'qg', 'qw', 'kw', 'vg', 'vw', 'du', 'dv']
    perm = lambda a: jnp.concatenate([a[..., ref_off[n]:ref_off[n] + ref_w[n]] for n in order], -1)
    perm_idx = np.concatenate([np.arange(ref_off[n], ref_off[n] + ref_w[n]) for n in order]).astype(np.int32)

    h_lat = None
    for l in range(depth):
        need_ctx = l < depth - 1
        mods = [[ada[l][r:r + 1, i * d:(i + 1) * d] for i in range(6)] for r in range(2)]
        (sh1, sc1, g1, sh2, sc2, g2), (sh1c, sc1c, g1c, sh2c, sc2c, g2c) = mods
        if need_ctx:
            nxt = [ada[l + 1][r:r + 1, 0:2 * d] for r in range(2)]
            (shn, scn), (shnc, scnc) = [(m[:, :d], m[:, d:]) for m in nxt]
        else:
            shn = scn = shnc = scnc = jnp.zeros((1, d), F32)

        w_small, b_small = perm(w_in[l][:, :SMALL_W]).astype(BF16), vec(jnp.take(b_in[l], perm_idx))
        w_gate, b_gate = w_in[l][:, SMALL_W:].astype(BF16), vec(b_in[l][SMALL_W:])
        s5m = [_s5_mats(s5_lam_re[l, dr], s5_lam_im[l, dr], s5_log_dt[l, dr], s5_b_re[l, dr], s5_b_im[l, dr],
                        s5_c_re[l, dr], s5_c_im[l, dr]) for dr in range(2)]
        wg, bg, s5d = w_glu[l].astype(BF16), vec(b_glu[l]), vec(s5_d[l])
        qn, kn = vec(qn_gain[l]), vec(kn_gain[l])
        wsp, bsp = w_sp[l].astype(BF16), b_sp[l].astype(F32)[:, :, None]
        w_brs = [w[l].astype(BF16) for w in (w_br_a, w_br_b, w_br_c, w_br_d)]
        wo = w_out[l].astype(BF16)
        wpqT = w_pq[l].T.astype(BF16)
        keys = peer_keys[l].astype(BF16)
        u_bf, vT_bf = peer_u[l].astype(BF16), peer_v[l].T.astype(BF16)

        if h_lat is None:
            h_lat, h_ctx = _modulate(x, sc1, sh1), _modulate(ctx, sc1c, sh1c)

        proj = _matmul(h_lat, w_small, b_small, tm=512, tn=1792, out_dtype=BF16, name="in_proj")
        gates = _matmul(h_lat, w_gate, b_gate, tm=512, tn=2048, out_dtype=BF16, name="in_proj_gates")
        proj_c = _matmul(h_ctx, w_small, b_small, tm=512, tn=1792, out_dtype=BF16, name="in_proj_ctx")

        y_dir, y_dir_c = [], []
        for dr in range(2):
            yc_, hc_end = _s5_dir(proj_c, s5m[dr], jnp.zeros((2, S5_CH), F32), bool(dr))
            yl_, _ = _s5_dir(proj, s5m[dr], hc_end, bool(dr))
            y_dir.append(yl_)
            y_dir_c.append(yc_)
        y_a = _s5_out(proj, y_dir[0], y_dir[1], s5d, wg, bg)

        qg, kg, vge, qw, kw = _prep(proj, cosf, sinf, qn, kn, ATTN_SCALE * math.log2(math.e))
        qgc, kgc, vgce, qwc, kwc = _prep(proj_c, cos1, sin0, qn, kn, ATTN_SCALE)
        y_b = _flash(qg, kg, vge, kgc, vgce)
        y_c = _window(qw, kw, proj, kwc, proj_c, sink[l])

        y_d = _spatial_gate(proj, vec(sg_ln_g[l]), vec(sg_ln_b[l]), wsp, bsp)

        merged = _merge((y_a, y_b, y_c, y_d), w_brs, gates)
        x1, xmT = _out_ln(merged, wo, x, g1, vec(ln1_g[l]), vec(ln1_b[l]), sc2, sh2, alpha)
        qT = _matmul(wpqT, xmT, None, tm=D_MODEL, tn=1024, out_dtype=F32, name="peer_query")
        ffnT = _peer_dense(xmT, u_bf, vT_bf, _peer_topk(qT, keys))
        res = _ffn_ln(ffnT, x1, g2, vec(ln2_g[l]), vec(ln2_b[l]), scn, shn, alpha, need_ctx)
        x = res[0]

        if need_ctx:
            h_lat = res[1]
            gates_c = _matmul(h_ctx, w_gate, b_gate, tm=512, tn=2048, out_dtype=BF16, name="in_proj_gates_ctx")
            y_a_c = _s5_out(proj_c, y_dir_c[0], y_dir_c[1], s5d, wg, bg)
            y_b_c = _ctx_attn(qgc, kgc, proj_c, COL_VG, sink[l], False)
            y_c_c = _ctx_attn(qwc, kwc, proj_c, COL_VW, sink[l], True)
            y_d_c = _spatial_gate(proj_c, vec(sg_ln_g[l]), vec(sg_ln_b[l]), wsp, bsp)
            merged_c = _merge((y_a_c, y_b_c, y_c_c, y_d_c), w_brs, gates_c)
            c1, cmT = _out_ln(merged_c, wo, ctx, g1c, vec(ln1_g[l]), vec(ln1_b[l]), sc2c, sh2c, alpha)
            qTc = _matmul(wpqT, cmT, None, tm=D_MODEL, tn=1024, out_dtype=F32, name="peer_query_ctx")
            ffnTc = _peer_dense(cmT, u_bf, vT_bf, _peer_topk(qTc, keys))
            ctx, h_ctx = _ffn_ln(ffnTc, c1, g2c, vec(ln2_g[l]), vec(ln2_b[l]), scnc, shnc, alpha, True)
    return x[None]
```

```python
import functools
import math

import jax
import jax.numpy as jnp
import numpy as np
from jax import lax
from jax.experimental import pallas as pl
from jax.experimental.pallas import tpu as pltpu

F32, BF16 = jnp.float32, jnp.bfloat16

D_MODEL = 2048
GRID_W = 64
BLOCK = 128
HEAD_DIM = 128
AXIS_DIM = HEAD_DIM // 2
ROPE_THETA = 10000.0
ATTN_SCALE = HEAD_DIM ** -0.5
NEG_INF = -1e30
EPS = 1e-6
S5_GROUPS, S5_GROUP_CH, S5_STATE = 48, 16, 64
S5_WIDTH = S5_GROUPS * S5_GROUP_CH
S5_CH = S5_GROUPS * S5_STATE
S5_GB = 16
S5_NB = S5_GROUPS // S5_GB
S5_KB = S5_GB * S5_GROUP_CH
S5_PB = S5_GB * S5_STATE
GA_HEADS, GA_KV = 8, 2
WA_HEADS, WA_KV = 8, 2
KV_GROUP = GA_HEADS // GA_KV
SG_GROUPS, SG_GROUP_CH = 6, 128
SG_WIDTH = SG_GROUPS * SG_GROUP_CH
CHUNK = 128
N_BRANCH = 4
PEER_HEADS, PEER_NKEYS, PEER_QDIM, PEER_TOPK = 8, 128, 256, 16
PEER_HALF = PEER_QDIM // 2
PEER_EXPERTS = PEER_NKEYS * PEER_NKEYS

COL_A, COL_KG, COL_QG, COL_QW, COL_KW, COL_VG, COL_VW, COL_DU, COL_DV = 0, 768, 1024, 2048, 3072, 3328, 3584, 3840, 4608
SMALL_W = 5376
GATE_W = N_BRANCH * D_MODEL

VMEM_CAP_MB = 56


def _cp(ndims, vmem_mb=32):
    return pltpu.CompilerParams(dimension_semantics=("arbitrary",) * ndims,
                                vmem_limit_bytes=min(vmem_mb, VMEM_CAP_MB) << 20)


def _gelu(y):
    return 0.5 * y * (1.0 + lax.erf(y * (2.0 ** -0.5)))


def _ln(y, g, b):
    mu = jnp.mean(y, -1, keepdims=True)
    yc = y - mu
    var = jnp.mean(yc * yc, -1, keepdims=True)
    return yc * lax.rsqrt(var + EPS) * g + b


_NT = (((1,), (1,)), ((), ()))


def _ada_body(c_ref, w_ref, b_ref, o_ref):
    cnd = c_ref[...]
    s = (cnd * jax.nn.sigmoid(cnd)).astype(BF16)
    o_ref[...] = jnp.dot(s, w_ref[...].astype(BF16), preferred_element_type=F32) + b_ref[...]


def _ada(cond8, w_ada, b_ada, l):
    depth, d, n = w_ada.shape
    tn = 1024
    return pl.pallas_call(
        _ada_body, grid=(n // tn,),
        in_specs=[pl.BlockSpec((8, d), lambda j: (0, 0)),
                  pl.BlockSpec((None, d, tn), lambda j: (l, 0, j)),
                  pl.BlockSpec((None, 1, tn), lambda j: (l, 0, j))],
        out_specs=pl.BlockSpec((8, tn), lambda j: (0, j)),
        out_shape=jax.ShapeDtypeStruct((8, n), F32),
        compiler_params=_cp(1, 32), name="ada_mod")(cond8, w_ada, b_ada.reshape(depth, 1, n))


def _mod_body(x_ref, sc_ref, sh_ref, o_ref):
    o_ref[...] = (x_ref[...] * (1.0 + sc_ref[...]) + sh_ref[...]).astype(o_ref.dtype)


def _modulate(x, sc, sh):
    t, d = x.shape
    tm = min(512, t)
    return pl.pallas_call(
        _mod_body, grid=(t // tm,),
        in_specs=[pl.BlockSpec((tm, d), lambda i: (i, 0)),
                  pl.BlockSpec((1, d), lambda i: (0, 0)),
                  pl.BlockSpec((1, d), lambda i: (0, 0))],
        out_specs=pl.BlockSpec((tm, d), lambda i: (i, 0)),
        out_shape=jax.ShapeDtypeStruct((t, d), BF16),
        compiler_params=_cp(1, 24), name="modulate")(x, sc, sh)


def _mm_body(a_ref, w_ref, b_ref, o_ref):
    acc = jnp.dot(a_ref[...], w_ref[...], preferred_element_type=F32)
    o_ref[...] = (acc + b_ref[...]).astype(o_ref.dtype)


def _mm_nobias_body(a_ref, w_ref, o_ref):
    o_ref[...] = jnp.dot(a_ref[...], w_ref[...], preferred_element_type=F32).astype(o_ref.dtype)


def _matmul(a, w, b, *, tm, tn, out_dtype, name):
    m, k = a.shape
    n = w.shape[1]
    tm, tn = min(tm, m), min(tn, n)
    in_specs = [pl.BlockSpec((tm, k), lambda j, i: (i, 0)),
                pl.BlockSpec((k, tn), lambda j, i: (0, j))]
    args = [a, w]
    body = _mm_nobias_body
    if b is not None:
        in_specs.append(pl.BlockSpec((1, tn), lambda j, i: (0, j)))
        args.append(b)
        body = _mm_body
    osz = jnp.dtype(out_dtype).itemsize
    vmem = 2 * (tm * k * 2 + k * tn * 2 + tm * tn * osz) + tm * tn * 4
    return pl.pallas_call(
        body, grid=(n // tn, m // tm), in_specs=in_specs,
        out_specs=pl.BlockSpec((tm, tn), lambda j, i: (i, j)),
        out_shape=jax.ShapeDtypeStruct((m, n), out_dtype),
        compiler_params=_cp(2, (vmem >> 20) + 8), name=name)(*args)


def _s5_body(u_ref, bre_ref, bim_ref, cre_ref, cim_ref, a_ref, h0_ref, y_ref, hT_ref, xr, xi, carry, *, tc, rev):
    @pl.when(pl.program_id(0) == 0)
    def _():
        carry[...] = h0_ref[...]

    for j in range(S5_NB):
        uj = u_ref[:, j * S5_KB:(j + 1) * S5_KB]
        xr[:, j * S5_PB:(j + 1) * S5_PB] = jnp.dot(uj, bre_ref[j], preferred_element_type=F32)
        xi[:, j * S5_PB:(j + 1) * S5_PB] = jnp.dot(uj, bim_ref[j], preferred_element_type=F32)

    for j in range(S5_NB):
        cs = slice(j * S5_PB, (j + 1) * S5_PB)
        ar, ai = a_ref[0:1, cs], a_ref[1:2, cs]

        def step(k, c, cs=cs, ar=ar, ai=ai):
            hr, hi = c
            t = (tc - 1 - k) if rev else k
            nr = ar * hr - ai * hi + xr[pl.ds(t, 1), cs]
            ni = ar * hi + ai * hr + xi[pl.ds(t, 1), cs]
            xr[pl.ds(t, 1), cs] = nr
            xi[pl.ds(t, 1), cs] = ni
            return nr, ni

        hr, hi = lax.fori_loop(0, tc, step, (carry[0:1, cs], carry[1:2, cs]), unroll=8)
        carry[0:1, cs] = hr
        carry[1:2, cs] = hi

    for j in range(S5_NB):
        cs = slice(j * S5_PB, (j + 1) * S5_PB)
        y_ref[:, j * S5_KB:(j + 1) * S5_KB] = (
            jnp.dot(xr[:, cs].astype(BF16), cre_ref[j], preferred_element_type=F32)
            + jnp.dot(xi[:, cs].astype(BF16), cim_ref[j], preferred_element_type=F32))
    hT_ref[...] = carry[...]


def _s5_dir(proj, mats, h0, rev):
    t = proj.shape[0]
    tc = min(256, t)
    nt = t // tc
    bre, bim, cre, cim, a = mats
    tmap = (lambda i: (nt - 1 - i, 0)) if rev else (lambda i: (i, 0))
    full3 = lambda i: (0, 0, 0)
    full2 = lambda i: (0, 0)
    return pl.pallas_call(
        functools.partial(_s5_body, tc=tc, rev=rev), grid=(nt,),
        in_specs=[pl.BlockSpec((tc, S5_WIDTH), tmap),
                  pl.BlockSpec((S5_NB, S5_KB, S5_PB), full3), pl.BlockSpec((S5_NB, S5_KB, S5_PB), full3),
                  pl.BlockSpec((S5_NB, S5_PB, S5_KB), full3), pl.BlockSpec((S5_NB, S5_PB, S5_KB), full3),
                  pl.BlockSpec((2, S5_CH), full2), pl.BlockSpec((2, S5_CH), full2)],
        out_specs=[pl.BlockSpec((tc, S5_WIDTH), tmap), pl.BlockSpec((2, S5_CH), full2)],
        out_shape=[jax.ShapeDtypeStruct((t, S5_WIDTH), F32), jax.ShapeDtypeStruct((2, S5_CH), F32)],
        scratch_shapes=[pltpu.VMEM((tc, S5_CH), F32), pltpu.VMEM((tc, S5_CH), F32), pltpu.VMEM((2, S5_CH), F32)],
        compiler_params=_cp(1, 40), name="s5_scan_bwd" if rev else "s5_scan_fwd")(proj, bre, bim, cre, cim, a, h0)


def _s5_out_body(u_ref, yf_ref, yb_ref, d_ref, wg_ref, bg_ref, o_ref):
    y = u_ref[...].astype(F32) * d_ref[...] + yf_ref[...] + yb_ref[...]
    g = _gelu(y)
    z = jnp.dot(g.astype(BF16), wg_ref[...], preferred_element_type=F32) + bg_ref[...]
    o_ref[...] = (g * jax.nn.sigmoid(z)).astype(o_ref.dtype)


def _s5_out(proj, yf, yb, d, wg, bg):
    t = proj.shape[0]
    tm = min(512, t)
    row = lambda i: (i, 0)
    full = lambda i: (0, 0)
    return pl.pallas_call(
        _s5_out_body, grid=(t // tm,),
        in_specs=[pl.BlockSpec((tm, S5_WIDTH), row), pl.BlockSpec((tm, S5_WIDTH), row), pl.BlockSpec((tm, S5_WIDTH), row),
                  pl.BlockSpec((1, S5_WIDTH), full), pl.BlockSpec((S5_WIDTH, S5_WIDTH), full), pl.BlockSpec((1, S5_WIDTH), full)],
        out_specs=pl.BlockSpec((tm, S5_WIDTH), row),
        out_shape=jax.ShapeDtypeStruct((t, S5_WIDTH), BF16),
        compiler_params=_cp(1, 24), name="s5_glu")(proj, yf, yb, d, wg, bg)


def _s5_mats(lam_re, lam_im, log_dt, b_re, b_im, c_re, c_im):
    dt = jnp.exp(log_dt)[:, None]
    mag = jnp.exp(lam_re * dt)
    ar, ai = mag * jnp.cos(lam_im * dt), mag * jnp.sin(lam_im * dt)
    nr, ni = ar - 1.0, ai
    den = lam_re * lam_re + lam_im * lam_im
    cr = (nr * lam_re + ni * lam_im) / den
    ci = (ni * lam_re - nr * lam_im) / den
    fre = cr[:, :, None] * b_re - ci[:, :, None] * b_im
    fim = cr[:, :, None] * b_im + ci[:, :, None] * b_re
    eye = jnp.eye(S5_GB, dtype=F32)

    def blk_in(f):
        f = f.reshape(S5_NB, S5_GB, S5_STATE, S5_GROUP_CH)
        m = jnp.einsum('jgph,gk->jghkp', f, eye)
        return m.reshape(S5_NB, S5_KB, S5_PB).astype(BF16)

    def blk_out(c):
        c = c.reshape(S5_NB, S5_GB, S5_GROUP_CH, S5_STATE)
        m = jnp.einsum('jghp,gk->jgpkh', c, eye)
        return m.reshape(S5_NB, S5_PB, S5_KB).astype(BF16)

    is_re = lax.broadcasted_iota(jnp.int32, (2, S5_CH), 0) == 0
    a = jnp.where(is_re, ar.reshape(1, -1), ai.reshape(1, -1))
    return blk_in(fre), blk_in(fim), blk_out(c_re), blk_out(-c_im), a


def _prep_body(qg_ref, kg_ref, vg_ref, qw_ref, kw_ref, cos_ref, sin_ref, qn_ref, kn_ref, oqg, okg, ovg, oqw, okw, *,
               qg_scale):
    cosf, sinf = cos_ref[...], sin_ref[...]
    lane = lax.broadcasted_iota(jnp.int32, cosf.shape, 1)
    low = (lane % AXIS_DIM) < (AXIS_DIM // 2)

    def rope(t):
        partner = jnp.where(low, pltpu.roll(t, HEAD_DIM - AXIS_DIM // 2, 1), pltpu.roll(t, AXIS_DIM // 2, 1))
        return t * cosf + partner * sinf

    def rms(t, g):
        return t * lax.rsqrt(jnp.mean(t * t, -1, keepdims=True) + EPS) * g

    for h in range(GA_HEADS):
        cs = slice(h * HEAD_DIM, (h + 1) * HEAD_DIM)
        oqg[:, cs] = (rope(rms(qg_ref[:, cs].astype(F32), qn_ref[...])) * qg_scale).astype(BF16)
        oqw[:, cs] = (rope(qw_ref[:, cs].astype(F32)) * ATTN_SCALE).astype(BF16)
    for h in range(GA_KV):
        cs = slice(h * HEAD_DIM, (h + 1) * HEAD_DIM)
        okg[:, cs] = rope(rms(kg_ref[:, cs].astype(F32), kn_ref[...])).astype(BF16)
        okw[:, cs] = rope(kw_ref[:, cs].astype(F32)).astype(BF16)
        ovg[:, 2 * h * HEAD_DIM:(2 * h + 1) * HEAD_DIM] = vg_ref[:, cs]
        ovg[:, (2 * h + 1) * HEAD_DIM:(2 * h + 2) * HEAD_DIM] = jnp.ones((vg_ref.shape[0], HEAD_DIM), BF16)


def _prep(proj, cosf, sinf, qn, kn, qg_scale):
    t = proj.shape[0]
    tr = min(256, t)
    qw_, kw_ = GA_HEADS * HEAD_DIM, GA_KV * HEAD_DIM
    row = lambda i: (i, 0)
    full = lambda i: (0, 0)
    return pl.pallas_call(
        functools.partial(_prep_body, qg_scale=qg_scale), grid=(t // tr,),
        in_specs=[pl.BlockSpec((tr, qw_), lambda i: (i, COL_QG // qw_)),
                  pl.BlockSpec((tr, kw_), lambda i: (i, COL_KG // kw_)),
                  pl.BlockSpec((tr, kw_), lambda i: (i, COL_VG // kw_)),
                  pl.BlockSpec((tr, qw_), lambda i: (i, COL_QW // qw_)),
                  pl.BlockSpec((tr, kw_), lambda i: (i, COL_KW // kw_)),
                  pl.BlockSpec((tr, HEAD_DIM), row), pl.BlockSpec((tr, HEAD_DIM), row),
                  pl.BlockSpec((1, HEAD_DIM), full), pl.BlockSpec((1, HEAD_DIM), full)],
        out_specs=[pl.BlockSpec((tr, qw_), row), pl.BlockSpec((tr, kw_), row), pl.BlockSpec((tr, 2 * kw_), row),
                   pl.BlockSpec((tr, qw_), row), pl.BlockSpec((tr, kw_), row)],
        out_shape=[jax.ShapeDtypeStruct((t, qw_), BF16), jax.ShapeDtypeStruct((t, kw_), BF16),
                   jax.ShapeDtypeStruct((t, 2 * kw_), BF16),
                   jax.ShapeDtypeStruct((t, qw_), BF16), jax.ShapeDtypeStruct((t, kw_), BF16)],
        compiler_params=_cp(1, 24), name="qk_prep")(proj, proj, proj, proj, proj, cosf, sinf, qn, kn)


def _rope_tables(n_lat):
    rows = n_lat // GRID_W
    row = jnp.repeat(jnp.arange(rows, dtype=F32), GRID_W)
    col = jnp.tile(jnp.arange(GRID_W, dtype=F32), rows)
    inv = jnp.power(ROPE_THETA, -jnp.arange(0, AXIS_DIM, 2, dtype=F32) / AXIS_DIM)
    ang_r, ang_c = row[:, None] * inv, col[:, None] * inv
    cr, sr, cc, sc = jnp.cos(ang_r), jnp.sin(ang_r), jnp.cos(ang_c), jnp.sin(ang_c)
    return jnp.concatenate([cr, cr, cc, cc], -1), jnp.concatenate([-sr, sr, -sc, sc], -1)


def _flash_body(q_ref, kc_ref, vc_ref, k_ref, v_ref, o_ref, qs, m_s, acc, *, tq, tk, nk):
    for g in range(KV_GROUP):
        qs[g * tq:(g + 1) * tq, :] = q_ref[:, g * HEAD_DIM:(g + 1) * HEAD_DIM]
    m_s[...] = jnp.full(m_s.shape, -jnp.inf, F32)
    acc[...] = jnp.zeros(acc.shape, F32)

    def update(k, v):
        for g in range(KV_GROUP):
            rs = slice(g * tq, (g + 1) * tq)
            s = lax.dot_general(qs[rs, :], k, _NT, preferred_element_type=F32)
            m_prev = m_s[rs, :]
            m_next = jnp.maximum(m_prev, jnp.max(s, axis=1, keepdims=True))
            alpha = jnp.exp2(m_prev - m_next)
            p = jnp.exp2(s - jnp.concatenate([m_next] * (s.shape[1] // HEAD_DIM), axis=1))
            acc[rs, :] = (jnp.concatenate([alpha, alpha], axis=1) * acc[rs, :]
                          + jnp.dot(p.astype(BF16), v, preferred_element_type=F32))
            m_s[rs, :] = m_next

    update(kc_ref[...], vc_ref[...])

    def body(j, carry):
        rows = pl.ds(pl.multiple_of(j * tk, tk), tk)
        update(k_ref[rows, :], v_ref[rows, :])
        return carry

    lax.fori_loop(0, nk, body, 0, unroll=FLASH_UNROLL)
    out = acc[:, :HEAD_DIM] / acc[:, HEAD_DIM:]
    for g in range(KV_GROUP):
        o_ref[:, g * HEAD_DIM:(g + 1) * HEAD_DIM] = out[g * tq:(g + 1) * tq, :].astype(o_ref.dtype)


FLASH_TK = 512
FLASH_UNROLL = 8


def _flash(q, kg, vge, kgc, vgce):
    t = q.shape[0]
    n_ctx = kgc.shape[0]
    tq = min(256, t)
    tk = min(FLASH_TK, t)
    gw = KV_GROUP * HEAD_DIM
    head = lambda h, i: (0, h)
    return pl.pallas_call(
        functools.partial(_flash_body, tq=tq, tk=tk, nk=t // tk), grid=(GA_KV, t // tq),
        in_specs=[pl.BlockSpec((tq, gw), lambda h, i: (i, h)),
                  pl.BlockSpec((n_ctx, HEAD_DIM), head), pl.BlockSpec((n_ctx, 2 * HEAD_DIM), head),
                  pl.BlockSpec((t, HEAD_DIM), head), pl.BlockSpec((t, 2 * HEAD_DIM), head)],
        out_specs=pl.BlockSpec((tq, gw), lambda h, i: (i, h)),
        out_shape=jax.ShapeDtypeStruct((t, GA_HEADS * HEAD_DIM), BF16),
        scratch_shapes=[pltpu.VMEM((KV_GROUP * tq, HEAD_DIM), BF16), pltpu.VMEM((KV_GROUP * tq, HEAD_DIM), F32),
                        pltpu.VMEM((KV_GROUP * tq, 2 * HEAD_DIM), F32)],
        compiler_params=_cp(2, 48), name="global_attn")(q, kgc, vgce, kg, vge)


def _win_body(q_ref, km_ref, k0_ref, kp_ref, vm_ref, v0_ref, vp_ref, kc_ref, vc_ref, sink_ref, o_ref, *, nb):
    i = pl.program_id(1)
    q = jnp.concatenate([q_ref[:, g * HEAD_DIM:(g + 1) * HEAD_DIM] for g in range(KV_GROUP)], axis=0)
    kw = jnp.concatenate([km_ref[...], k0_ref[...], kp_ref[...]], axis=0)
    vw = jnp.concatenate([vm_ref[...], v0_ref[...], vp_ref[...]], axis=0)
    s_w = lax.dot_general(q, kw, _NT, preferred_element_type=F32)
    r = lax.broadcasted_iota(jnp.int32, s_w.shape, 0) % BLOCK
    c = lax.broadcasted_iota(jnp.int32, s_w.shape, 1)
    cc = c % BLOCK
    iv = jnp.full(s_w.shape, i, jnp.int32)
    valid = (((c < BLOCK) & (cc >= r) & (iv >= 1)) | ((c >= BLOCK) & (c < 2 * BLOCK))
             | ((c >= 2 * BLOCK) & (cc <= r) & (iv + 1 < nb)))
    s_w = jnp.where(valid, s_w, NEG_INF)
    s_c = lax.dot_general(q, kc_ref[...], _NT, preferred_element_type=F32)
    sink = sink_ref[0]
    m = jnp.maximum(jnp.maximum(jnp.max(s_w, -1, keepdims=True), jnp.max(s_c, -1, keepdims=True)), sink)
    pw, pc = jnp.exp(s_w - m), jnp.exp(s_c - m)
    l = jnp.sum(pw, -1, keepdims=True) + jnp.sum(pc, -1, keepdims=True) + jnp.exp(sink - m)
    inv = 1.0 / l
    out = (jnp.dot((pc * inv).astype(BF16), vc_ref[...], preferred_element_type=F32)
           + jnp.dot((pw * inv).astype(BF16), vw, preferred_element_type=F32))
    for g in range(KV_GROUP):
        o_ref[:, g * HEAD_DIM:(g + 1) * HEAD_DIM] = out[g * BLOCK:(g + 1) * BLOCK, :].astype(o_ref.dtype)


def _sink_rows(sink, rows_per_head):
    return jnp.repeat(sink.astype(F32).reshape(WA_KV, KV_GROUP), rows_per_head, axis=1)[:, :, None]


def _window(qw, kw, proj, kwc, proj_c, sink):
    t = qw.shape[0]
    n_ctx = kwc.shape[0]
    nb = t // BLOCK
    gw = KV_GROUP * HEAD_DIM
    vcol = COL_VW // HEAD_DIM
    kspec = lambda f: pl.BlockSpec((BLOCK, HEAD_DIM), f)
    return pl.pallas_call(
        functools.partial(_win_body, nb=nb), grid=(WA_KV, nb),
        in_specs=[pl.BlockSpec((BLOCK, gw), lambda h, i: (i, h)),
                  kspec(lambda h, i: (jnp.maximum(i - 1, 0), h)), kspec(lambda h, i: (i, h)),
                  kspec(lambda h, i: (jnp.minimum(i + 1, nb - 1), h)),
                  kspec(lambda h, i: (jnp.maximum(i - 1, 0), vcol + h)), kspec(lambda h, i: (i, vcol + h)),
                  kspec(lambda h, i: (jnp.minimum(i + 1, nb - 1), vcol + h)),
                  pl.BlockSpec((n_ctx, HEAD_DIM), lambda h, i: (0, h)),
                  pl.BlockSpec((n_ctx, HEAD_DIM), lambda h, i: (0, vcol + h)),
                  pl.BlockSpec((1, KV_GROUP * BLOCK, 1), lambda h, i: (h, 0, 0))],
        out_specs=pl.BlockSpec((BLOCK, gw), lambda h, i: (i, h)),
        out_shape=jax.ShapeDtypeStruct((t, WA_HEADS * HEAD_DIM), BF16),
        compiler_params=_cp(2, 24), name="window_attn")(
            qw, kw, kw, kw, proj, proj, proj, kwc, proj_c, _sink_rows(sink, BLOCK))


def _ctx_attn_body(q_ref, k_ref, v_ref, sink_ref, o_ref, *, n_ctx, use_sink):
    q = jnp.concatenate([q_ref[:, g * HEAD_DIM:(g + 1) * HEAD_DIM] for g in range(KV_GROUP)], axis=0)
    s = lax.dot_general(q, k_ref[...], _NT, preferred_element_type=F32)
    m = jnp.max(s, -1, keepdims=True)
    if use_sink:
        m = jnp.maximum(m, sink_ref[0])
    p = jnp.exp(s - m)
    l = jnp.sum(p, -1, keepdims=True)
    if use_sink:
        l = l + jnp.exp(sink_ref[0] - m)
    out = jnp.dot((p / l).astype(BF16), v_ref[...], preferred_element_type=F32)
    for g in range(KV_GROUP):
        o_ref[:, g * HEAD_DIM:(g + 1) * HEAD_DIM] = out[g * n_ctx:(g + 1) * n_ctx, :].astype(o_ref.dtype)


def _ctx_attn(q, k, proj_c, vcol0, sink, use_sink):
    n_ctx = q.shape[0]
    gw = KV_GROUP * HEAD_DIM
    vcol = vcol0 // HEAD_DIM
    return pl.pallas_call(
        functools.partial(_ctx_attn_body, n_ctx=n_ctx, use_sink=use_sink), grid=(GA_KV,),
        in_specs=[pl.BlockSpec((n_ctx, gw), lambda h: (0, h)),
                  pl.BlockSpec((n_ctx, HEAD_DIM), lambda h: (0, h)),
                  pl.BlockSpec((n_ctx, HEAD_DIM), lambda h: (0, vcol + h)),
                  pl.BlockSpec((1, KV_GROUP * n_ctx, 1), lambda h: (h, 0, 0))],
        out_specs=pl.BlockSpec((n_ctx, gw), lambda h: (0, h)),
        out_shape=jax.ShapeDtypeStruct((n_ctx, GA_HEADS * HEAD_DIM), BF16),
        compiler_params=_cp(1, 24), name="ctx_attn_sink" if use_sink else "ctx_attn")(
            q, k, proj_c, _sink_rows(sink, n_ctx))


def _sg_body(u_ref, v_ref, g_ref, b_ref, w_ref, bs_ref, o_ref, *, nch):
    for n in range(nch):
        rs = slice(n * CHUNK, (n + 1) * CHUNK)
        vn = _ln(v_ref[rs, :].astype(F32), g_ref[...], b_ref[...]).astype(BF16)
        for g in range(SG_GROUPS):
            cs = slice(g * SG_GROUP_CH, (g + 1) * SG_GROUP_CH)
            s = jnp.dot(w_ref[g], vn[:, cs], preferred_element_type=F32) + bs_ref[g]
            o_ref[rs, cs] = (u_ref[rs, cs].astype(F32) * s).astype(o_ref.dtype)


def _spatial_gate(proj, g_v, b_v, w_sp, b_sp):
    t = proj.shape[0]
    tm = min(512, t)
    full = lambda i: (0, 0)
    return pl.pallas_call(
        functools.partial(_sg_body, nch=tm // CHUNK), grid=(t // tm,),
        in_specs=[pl.BlockSpec((tm, SG_WIDTH), lambda i: (i, COL_DU // SG_WIDTH)),
                  pl.BlockSpec((tm, SG_WIDTH), lambda i: (i, COL_DV // SG_WIDTH)),
                  pl.BlockSpec((1, SG_WIDTH), full), pl.BlockSpec((1, SG_WIDTH), full),
                  pl.BlockSpec((SG_GROUPS, CHUNK, CHUNK), lambda i: (0, 0, 0)),
                  pl.BlockSpec((SG_GROUPS, CHUNK, 1), lambda i: (0, 0, 0))],
        out_specs=pl.BlockSpec((tm, SG_WIDTH), lambda i: (i, 0)),
        out_shape=jax.ShapeDtypeStruct((t, SG_WIDTH), BF16),
        compiler_params=_cp(1, 24), name="spatial_gate")(proj, proj, g_v, b_v, w_sp, b_sp)


def _merge_body(ya, yb, yc, yd, wa, wb, wc, wd, ga, gb, gc, gd, o_ref):
    def br(y, w, g):
        return jax.nn.sigmoid(g[...].astype(F32)) * jnp.dot(y[...], w[...], preferred_element_type=F32)
    o_ref[...] = (br(ya, wa, ga) + br(yb, wb, gb) + br(yc, wc, gc) + br(yd, wd, gd)).astype(o_ref.dtype)


def _merge(ys, ws, gates):
    t = ys[0].shape[0]
    tm, tn = min(512, t), 1024
    nn = D_MODEL // tn
    in_specs = [pl.BlockSpec((tm, y.shape[1]), lambda j, i: (i, 0)) for y in ys]
    in_specs += [pl.BlockSpec((w.shape[0], tn), lambda j, i: (0, j)) for w in ws]
    in_specs += [pl.BlockSpec((tm, tn), functools.partial(lambda j, i, b: (i, b * nn + j), b=b)) for b in range(N_BRANCH)]
    return pl.pallas_call(
        _merge_body, grid=(nn, t // tm), in_specs=in_specs,
        out_specs=pl.BlockSpec((tm, tn), lambda j, i: (i, j)),
        out_shape=jax.ShapeDtypeStruct((t, D_MODEL), BF16),
        compiler_params=_cp(2, 48), name="gated_merge")(*ys, *ws, gates, gates, gates, gates)


def _out_ln_body(mg_ref, w_ref, x_ref, g1_ref, lng_ref, lnb_ref, sc_ref, sh_ref, x1_ref, xmT_ref, *, alpha):
    o = jnp.dot(mg_ref[...], w_ref[...], preferred_element_type=F32)
    x1 = _ln(alpha * x_ref[...] + g1_ref[...] * o, lng_ref[...], lnb_ref[...])
    x1_ref[...] = x1
    xmT_ref[...] = (x1 * (1.0 + sc_ref[...]) + sh_ref[...]).T.astype(xmT_ref.dtype)


def _out_ln(merged, w_out, x, g1, lng, lnb, sc2, sh2, alpha):
    t, d = x.shape
    tm = min(256, t)
    row = lambda i: (i, 0)
    vec = pl.BlockSpec((1, d), lambda i: (0, 0))
    return pl.pallas_call(
        functools.partial(_out_ln_body, alpha=alpha), grid=(t // tm,),
        in_specs=[pl.BlockSpec((tm, d), row), pl.BlockSpec((d, d), lambda i: (0, 0)), pl.BlockSpec((tm, d), row),
                  vec, vec, vec, vec, vec],
        out_specs=[pl.BlockSpec((tm, d), row), pl.BlockSpec((d, tm), lambda i: (0, i))],
        out_shape=[jax.ShapeDtypeStruct((t, d), F32), jax.ShapeDtypeStruct((d, t), BF16)],
        compiler_params=_cp(1, 48), name="out_proj_ln")(merged, w_out, x, g1, lng, lnb, sc2, sh2)


def _peer_topk_body(q_ref, k_ref, s1_o, s2_o, s1l_o, s2l_o, tau_o):
    q = q_ref[...].astype(BF16)
    s1 = jnp.dot(k_ref[0], q[:PEER_HALF], preferred_element_type=F32)
    s2 = jnp.dot(k_ref[1], q[PEER_HALF:], preferred_element_type=F32)
    ninf = -jnp.inf

    def top_distinct(s):
        vals, tops, cnts = s, [], []
        for _ in range(PEER_TOPK):
            m = jnp.max(vals, axis=0, keepdims=True)
            eq = vals == m
            tops.append(m)
            cnts.append(jnp.sum(eq.astype(F32), axis=0, keepdims=True))
            vals = jnp.where(eq, ninf, vals)
        return jnp.concatenate(tops, 0), jnp.concatenate(cnts, 0)

    ta, na = top_distinct(s1)
    tb, nb = top_distinct(s2)
    nrow = [PEER_TOPK // (k + 1) for k in range(PEER_TOPK)]
    pad = -sum(nrow) % 8
    cand = jnp.concatenate([ta[k:k + 1] + tb[:nrow[k]] for k in range(PEER_TOPK)]
                           + [jnp.full((pad, ta.shape[1]), ninf, F32)], 0)
    mult = jnp.concatenate([na[k:k + 1] * nb[:nrow[k]] for k in range(PEER_TOPK)]
                           + [jnp.zeros((pad, ta.shape[1]), F32)], 0)
    vals = cand
    cnt = jnp.zeros_like(cand[0:1])
    tau = jnp.full_like(cand[0:1], ninf)
    for _ in range(PEER_TOPK):
        m = jnp.max(vals, axis=0, keepdims=True)
        eq = vals == m
        tau = jnp.where(cnt < PEER_TOPK, m, tau)
        cnt = cnt + jnp.sum(jnp.where(eq, mult, 0.0), axis=0, keepdims=True)
        vals = jnp.where(eq, ninf, vals)
    cmax = cand[0:1]
    z = jnp.sum(jnp.where(cand >= tau, mult * jnp.exp(cand - cmax), 0.0), axis=0, keepdims=True)
    g1 = jnp.exp(s1 - ta[0:1]) * (1.0 / z)
    g2 = jnp.exp(s2 - tb[0:1])
    for lc in range(s1.shape[1] // PEER_LANES):
        ls = slice(lc * PEER_LANES, (lc + 1) * PEER_LANES)
        s1_o[lc] = s1[:, ls]
        s2_o[lc] = s2[:, ls]
        s1l_o[lc] = g1[:, ls]
        s2l_o[lc] = g2[:, ls]
        tau_o[lc] = tau[:, ls]


PEER_LANES = 128


def _peer_topk(qT, keys):
    t = qT.shape[1]
    tt = min(512, t)
    nl = tt // PEER_LANES
    blk = pl.BlockSpec((None, nl, PEER_NKEYS, PEER_LANES), lambda j, h: (h, j, 0, 0))
    shp = jax.ShapeDtypeStruct((PEER_HEADS, t // PEER_LANES, PEER_NKEYS, PEER_LANES), F32)
    return pl.pallas_call(
        _peer_topk_body, grid=(t // tt, PEER_HEADS),
        in_specs=[pl.BlockSpec((PEER_QDIM, tt), lambda j, h: (h, j)),
                  pl.BlockSpec((None, 2, PEER_NKEYS, PEER_HALF), lambda j, h: (h, 0, 0, 0))],
        out_specs=[blk, blk, blk, blk, pl.BlockSpec((None, nl, 1, PEER_LANES), lambda j, h: (h, j, 0, 0))],
        out_shape=[shp, shp, shp, shp, jax.ShapeDtypeStruct((PEER_HEADS, t // PEER_LANES, 1, PEER_LANES), F32)],
        compiler_params=_cp(2, 32), name="peer_topk")(qT, keys)


PEER_EK = 512
PEER_SUBS = PEER_EK // PEER_NKEYS
PEER_PIECES = 8


def _peer_dense_body(xT_ref, u_ref, vT_ref, s1_ref, s2_ref, s1l_ref, s2l_ref, tau_ref, o_ref, hbuf, awbuf, *, nk):
    s = pl.program_id(1)
    d = o_ref.shape[0]
    h0, h1, aw0, aw1 = hbuf.at[0], hbuf.at[1], awbuf.at[0], awbuf.at[1]

    @pl.when(s == 0)
    def _():
        o_ref[...] = jnp.zeros(o_ref.shape, F32)
        hbuf[...] = jnp.zeros(hbuf.shape, F32)
        awbuf[...] = jnp.zeros(awbuf.shape, BF16)

    def stages(h_cur, h_prev, aw_cur, aw_prev):
        base = jnp.clip(s - 1, 0, nk - 1) * PEER_SUBS
        dq = d // PEER_PIECES

        def second_matmul(q):
            ds_ = slice(q * dq, (q + 1) * dq)
            o_ref[ds_, :] += jnp.dot(vT_ref[ds_, :], aw_cur[...], preferred_element_type=F32)

        def first_matmul(q):
            ks = slice(q * dq, (q + 1) * dq)
            part = jnp.dot(u_ref[:, ks], xT_ref[ks, :], preferred_element_type=F32)
            if q == 0:
                h_cur[...] = part
            else:
                h_cur[...] += part

        def gate_tile(ii, lc):
            i1 = pl.ds(base + ii, 1)
            w = None
            for h in range(PEER_HEADS):
                c = s2_ref[h, lc] + s1_ref[h, lc, i1, :]
                wh = jnp.where(c >= tau_ref[h, lc], s2l_ref[h, lc] * s1l_ref[h, lc, i1, :], 0.0)
                w = wh if w is None else w + wh
            rs = slice(ii * PEER_NKEYS, (ii + 1) * PEER_NKEYS)
            ls = slice(lc * PEER_LANES, (lc + 1) * PEER_LANES)
            aw_prev[rs, ls] = (_gelu(h_prev[rs, ls]) * w).astype(BF16)

        tiles = [(ii, lc) for ii in range(PEER_SUBS) for lc in range(o_ref.shape[1] // PEER_LANES)]
        per = -(-len(tiles) // (2 * PEER_PIECES))
        for q in range(PEER_PIECES):
            second_matmul(q)
            for tl in tiles[(2 * q) * per:(2 * q + 1) * per]:
                gate_tile(*tl)
            first_matmul(q)
            for tl in tiles[(2 * q + 1) * per:(2 * q + 2) * per]:
                gate_tile(*tl)

    @pl.when(s % 2 == 0)
    def _():
        stages(h0, h1, aw0, aw1)

    @pl.when(s % 2 == 1)
    def _():
        stages(h1, h0, aw1, aw0)


def _peer_dense(xmT, u_bf, vT_bf, tk):
    d, t = xmT.shape
    tt = min(512, t)
    nl = tt // PEER_LANES
    nk = PEER_EXPERTS // PEER_EK
    s1, s2, s1l, s2l, tau = tk
    sblk = pl.BlockSpec((PEER_HEADS, nl, PEER_NKEYS, PEER_LANES), lambda j, s: (0, j, 0, 0))
    return pl.pallas_call(
        functools.partial(_peer_dense_body, nk=nk), grid=(t // tt, nk + 2),
        in_specs=[pl.BlockSpec((d, tt), lambda j, s: (0, j)),
                  pl.BlockSpec((PEER_EK, d), lambda j, s: (jnp.minimum(s, nk - 1), 0)),
                  pl.BlockSpec((d, PEER_EK), lambda j, s: (0, jnp.maximum(s - 2, 0))),
                  sblk, sblk, sblk, sblk,
                  pl.BlockSpec((PEER_HEADS, nl, 1, PEER_LANES), lambda j, s: (0, j, 0, 0))],
        out_specs=pl.BlockSpec((d, tt), lambda j, s: (0, j)),
        out_shape=jax.ShapeDtypeStruct((d, t), F32),
        scratch_shapes=[pltpu.VMEM((2, PEER_EK, tt), F32), pltpu.VMEM((2, PEER_EK, tt), BF16)],
        compiler_params=_cp(2, 48), name="peer_dense")(xmT, u_bf, vT_bf, s1, s2, s1l, s2l, tau)


def _ffn_ln_body(fT_ref, x1_ref, g2_ref, lng_ref, lnb_ref, scn_ref, shn_ref, x2_ref, *rest, alpha):
    x2 = _ln(alpha * x1_ref[...] + g2_ref[...] * fT_ref[...].T, lng_ref[...], lnb_ref[...])
    x2_ref[...] = x2
    if rest:
        rest[0][...] = (x2 * (1.0 + scn_ref[...]) + shn_ref[...]).astype(rest[0].dtype)


def _ffn_ln(fT, x1, g2, lng, lnb, scn, shn, alpha, emit_h):
    t, d = x1.shape
    tm = min(256, t)
    row = pl.BlockSpec((tm, d), lambda i: (i, 0))
    vec = pl.BlockSpec((1, d), lambda i: (0, 0))
    out_specs = [row, row] if emit_h else [row]
    out_shape = [jax.ShapeDtypeStruct((t, d), F32)] + ([jax.ShapeDtypeStruct((t, d), BF16)] if emit_h else [])
    return pl.pallas_call(
        functools.partial(_ffn_ln_body, alpha=alpha), grid=(t // tm,),
        in_specs=[pl.BlockSpec((d, tm), lambda i: (0, i)), row, vec, vec, vec, vec, vec],
        out_specs=out_specs, out_shape=out_shape,
        compiler_params=_cp(1, 32), name="ffn_ln")(fT, x1, g2, lng, lnb, scn, shn)


def kernel(x, c, ctx, c_ctx, w_ada, b_ada, w_in, b_in, s5_lam_re, s5_lam_im, s5_log_dt, s5_b_re, s5_b_im, s5_c_re, s5_c_im, s5_d, w_glu, b_glu, qn_gain, kn_gain, sink, sg_ln_g, sg_ln_b, w_sp, b_sp, w_br_a, w_br_b, w_br_c, w_br_d, w_out, ln1_g, ln1_b, ln2_g, ln2_b, w_pq, peer_keys, peer_u, peer_v):
    depth = w_in.shape[0]
    bsz, n_lat, d = x.shape
    assert bsz == 1 and d == D_MODEL and n_lat % 512 == 0 and ctx.shape[1] % BLOCK == 0
    n_ctx = ctx.shape[1]
    alpha = (2 * depth) ** 0.25
    x, ctx = x[0], ctx[0]

    row8 = lax.broadcasted_iota(jnp.int32, (8, d), 0)
    cond8 = jnp.where(row8 == 0, c[0][None], jnp.where(row8 == 1, c_ctx[None], 0.0))
    ada = [_ada(cond8, w_ada, b_ada, l) for l in range(depth)]
    cosf, sinf = _rope_tables(n_lat)
    cos1, sin0 = jnp.ones((n_ctx, HEAD_DIM), F32), jnp.zeros((n_ctx, HEAD_DIM), F32)
    vec = lambda a: a.reshape(1, -1).astype(F32)

    ref_off = {'a': 0, 'qg': 768, 'kg': 1792, 'vg': 2048, 'qw': 2304, 'kw': 3328, 'vw': 3584, 'du': 3840, 'dv': 4608}
    ref_w = {'a': 768, 'qg': 1024, 'kg': 256, 'vg': 256, 'qw': 1024, 'kw': 256, 'vw': 256, 'du': 768, 'dv': 768}
    order = ['a', 'kg', 'qg', 'qw', 'kw', 'vg', 'vw', 'du', 'dv']
    perm = lambda a: jnp.concatenate([a[..., ref_off[n]:ref_off[n] + ref_w[n]] for n in order], -1)
    perm_idx = np.concatenate([np.arange(ref_off[n], ref_off[n] + ref_w[n]) for n in order]).astype(np.int32)

    h_lat = None
    for l in range(depth):
        need_ctx = l < depth - 1
        mods = [[ada[l][r:r + 1, i * d:(i + 1) * d] for i in range(6)] for r in range(2)]
        (sh1, sc1, g1, sh2, sc2, g2), (sh1c, sc1c, g1c, sh2c, sc2c, g2c) = mods
        if need_ctx:
            nxt = [ada[l + 1][r:r + 1, 0:2 * d] for r in range(2)]
            (shn, scn), (shnc, scnc) = [(m[:, :d], m[:, d:]) for m in nxt]
        else:
            shn = scn = shnc = scnc = jnp.zeros((1, d), F32)

        w_small, b_small = perm(w_in[l][:, :SMALL_W]).astype(BF16), vec(jnp.take(b_in[l], perm_idx))
        w_gate, b_gate = w_in[l][:, SMALL_W:].astype(BF16), vec(b_in[l][SMALL_W:])
        s5m = [_s5_mats(s5_lam_re[l, dr], s5_lam_im[l, dr], s5_log_dt[l, dr], s5_b_re[l, dr], s5_b_im[l, dr],
                        s5_c_re[l, dr], s5_c_im[l, dr]) for dr in range(2)]
        wg, bg, s5d = w_glu[l].astype(BF16), vec(b_glu[l]), vec(s5_d[l])
        qn, kn = vec(qn_gain[l]), vec(kn_gain[l])
        wsp, bsp = w_sp[l].astype(BF16), b_sp[l].astype(F32)[:, :, None]
        w_brs = [w[l].astype(BF16) for w in (w_br_a, w_br_b, w_br_c, w_br_d)]
        wo = w_out[l].astype(BF16)
        wpqT = w_pq[l].T.astype(BF16)
        keys = peer_keys[l].astype(BF16)
        u_bf, vT_bf = peer_u[l].astype(BF16), peer_v[l].T.astype(BF16)

        if h_lat is None:
            h_lat, h_ctx = _modulate(x, sc1, sh1), _modulate(ctx, sc1c, sh1c)

        proj = _matmul(h_lat, w_small, b_small, tm=512, tn=1792, out_dtype=BF16, name="in_proj")
        gates = _matmul(h_lat, w_gate, b_gate, tm=512, tn=2048, out_dtype=BF16, name="in_proj_gates")
        proj_c = _matmul(h_ctx, w_small, b_small, tm=512, tn=1792, out_dtype=BF16, name="in_proj_ctx")

        y_dir, y_dir_c = [], []
        for dr in range(2):
            yc_, hc_end = _s5_dir(proj_c, s5m[dr], jnp.zeros((2, S5_CH), F32), bool(dr))
            yl_, _ = _s5_dir(proj, s5m[dr], hc_end, bool(dr))
            y_dir.append(yl_)
            y_dir_c.append(yc_)
        y_a = _s5_out(proj, y_dir[0], y_dir[1], s5d, wg, bg)

        qg, kg, vge, qw, kw = _prep(proj, cosf, sinf, qn, kn, ATTN_SCALE * math.log2(math.e))
        qgc, kgc, vgce, qwc, kwc = _prep(proj_c, cos1, sin0, qn, kn, ATTN_SCALE)
        y_b = _flash(qg, kg, vge, kgc, vgce)
        y_c = _window(qw, kw, proj, kwc, proj_c, sink[l])

        y_d = _spatial_gate(proj, vec(sg_ln_g[l]), vec(sg_ln_b[l]), wsp, bsp)

        merged = _merge((y_a, y_b, y_c, y_d), w_brs, gates)
        x1, xmT = _out_ln(merged, wo, x, g1, vec(ln1_g[l]), vec(ln1_b[l]), sc2, sh2, alpha)
        qT = _matmul(wpqT, xmT, None, tm=D_MODEL, tn=1024, out_dtype=F32, name="peer_query")
        ffnT = _peer_dense(xmT, u_bf, vT_bf, _peer_topk(qT, keys))
        res = _ffn_ln(ffnT, x1, g2, vec(ln2_g[l]), vec(ln2_b[l]), scn, shn, alpha, need_ctx)
        x = res[0]

        if need_ctx:
            h_lat = res[1]
            gates_c = _matmul(h_ctx, w_gate, b_gate, tm=512, tn=2048, out_dtype=BF16, name="in_proj_gates_ctx")
            y_a_c = _s5_out(proj_c, y_dir_c[0], y_dir_c[1], s5d, wg, bg)
            y_b_c = _ctx_attn(qgc, kgc, proj_c, COL_VG, sink[l], False)
            y_c_c = _ctx_attn(qwc, kwc, proj_c, COL_VW, sink[l], True)
            y_d_c = _spatial_gate(proj_c, vec(sg_ln_g[l]), vec(sg_ln_b[l]), wsp, bsp)
            merged_c = _merge((y_a_c, y_b_c, y_c_c, y_d_c), w_brs, gates_c)
            c1, cmT = _out_ln(merged_c, wo, ctx, g1c, vec(ln1_g[l]), vec(ln1_b[l]), sc2c, sh2c, alpha)
            qTc = _matmul(wpqT, cmT, None, tm=D_MODEL, tn=1024, out_dtype=F32, name="peer_query_ctx")
            ffnTc = _peer_dense(cmT, u_bf, vT_bf, _peer_topk(qTc, keys))
            ctx, h_ctx = _ffn_ln(ffnTc, c1, g2c, vec(ln2_g[l]), vec(ln2_b[l]), scnc, shnc, alpha, True)
    return x[None]
```

```python
import functools
import math

import jax
import jax.numpy as jnp
import numpy as np
from jax import lax
from jax.experimental import pallas as pl
from jax.experimental.pallas import tpu as pltpu

F32, BF16 = jnp.float32, jnp.bfloat16

D_MODEL = 2048
GRID_W = 64
BLOCK = 128
HEAD_DIM = 128
AXIS_DIM = HEAD_DIM // 2
ROPE_THETA = 10000.0
ATTN_SCALE = HEAD_DIM ** -0.5
NEG_INF = -1e30
EPS = 1e-6
S5_GROUPS, S5_GROUP_CH, S5_STATE = 48, 16, 64
S5_WIDTH = S5_GROUPS * S5_GROUP_CH
S5_CH = S5_GROUPS * S5_STATE
S5_GB = 16
S5_NB = S5_GROUPS // S5_GB
S5_KB = S5_GB * S5_GROUP_CH
S5_PB = S5_GB * S5_STATE
GA_HEADS, GA_KV = 8, 2
WA_HEADS, WA_KV = 8, 2
KV_GROUP = GA_HEADS // GA_KV
SG_GROUPS, SG_GROUP_CH = 6, 128
SG_WIDTH = SG_GROUPS * SG_GROUP_CH
CHUNK = 128
N_BRANCH = 4
PEER_HEADS, PEER_NKEYS, PEER_QDIM, PEER_TOPK = 8, 128, 256, 16
PEER_HALF = PEER_QDIM // 2
PEER_EXPERTS = PEER_NKEYS * PEER_NKEYS

COL_A, COL_KG, COL_QG, COL_QW, COL_KW, COL_VG, COL_VW, COL_DU, COL_DV = 0, 768, 1024, 2048, 3072, 3328, 3584, 3840, 4608
SMALL_W = 5376
GATE_W = N_BRANCH * D_MODEL

VMEM_CAP_MB = 56


def _cp(ndims, vmem_mb=32):
    return pltpu.CompilerParams(dimension_semantics=("arbitrary",) * ndims,
                                vmem_limit_bytes=min(vmem_mb, VMEM_CAP_MB) << 20)


def _gelu(y):
    return 0.5 * y * (1.0 + lax.erf(y * (2.0 ** -0.5)))


def _ln(y, g, b):
    mu = jnp.mean(y, -1, keepdims=True)
    yc = y - mu
    var = jnp.mean(yc * yc, -1, keepdims=True)
    return yc * lax.rsqrt(var + EPS) * g + b


_NT = (((1,), (1,)), ((), ()))


def _ada_body(c_ref, w_ref, b_ref, o_ref):
    cnd = c_ref[...]
    s = (cnd * jax.nn.sigmoid(cnd)).astype(BF16)
    o_ref[...] = jnp.dot(s, w_ref[...].astype(BF16), preferred_element_type=F32) + b_ref[...]


def _ada(cond8, w_ada, b_ada, l):
    depth, d, n = w_ada.shape
    tn = 1024
    return pl.pallas_call(
        _ada_body, grid=(n // tn,),
        in_specs=[pl.BlockSpec((8, d), lambda j: (0, 0)),
                  pl.BlockSpec((None, d, tn), lambda j: (l, 0, j)),
                  pl.BlockSpec((None, 1, tn), lambda j: (l, 0, j))],
        out_specs=pl.BlockSpec((8, tn), lambda j: (0, j)),
        out_shape=jax.ShapeDtypeStruct((8, n), F32),
        compiler_params=_cp(1, 32), name="ada_mod")(cond8, w_ada, b_ada.reshape(depth, 1, n))


def _mod_body(x_ref, sc_ref, sh_ref, o_ref):
    o_ref[...] = (x_ref[...] * (1.0 + sc_ref[...]) + sh_ref[...]).astype(o_ref.dtype)


def _modulate(x, sc, sh):
    t, d = x.shape
    tm = min(512, t)
    return pl.pallas_call(
        _mod_body, grid=(t // tm,),
        in_specs=[pl.BlockSpec((tm, d), lambda i: (i, 0)),
                  pl.BlockSpec((1, d), lambda i: (0, 0)),
                  pl.BlockSpec((1, d), lambda i: (0, 0))],
        out_specs=pl.BlockSpec((tm, d), lambda i: (i, 0)),
        out_shape=jax.ShapeDtypeStruct((t, d), BF16),
        compiler_params=_cp(1, 24), name="modulate")(x, sc, sh)


def _mm_body(a_ref, w_ref, b_ref, o_ref):
    acc = jnp.dot(a_ref[...], w_ref[...], preferred_element_type=F32)
    o_ref[...] = (acc + b_ref[...]).astype(o_ref.dtype)


def _mm_nobias_body(a_ref, w_ref, o_ref):
    o_ref[...] = jnp.dot(a_ref[...], w_ref[...], preferred_element_type=F32).astype(o_ref.dtype)


def _matmul(a, w, b, *, tm, tn, out_dtype, name):
    m, k = a.shape
    n = w.shape[1]
    tm, tn = min(tm, m), min(tn, n)
    in_specs = [pl.BlockSpec((tm, k), lambda j, i: (i, 0)),
                pl.BlockSpec((k, tn), lambda j, i: (0, j))]
    args = [a, w]
    body = _mm_nobias_body
    if b is not None:
        in_specs.append(pl.BlockSpec((1, tn), lambda j, i: (0, j)))
        args.append(b)
        body = _mm_body
    osz = jnp.dtype(out_dtype).itemsize
    vmem = 2 * (tm * k * 2 + k * tn * 2 + tm * tn * osz) + tm * tn * 4
    return pl.pallas_call(
        body, grid=(n // tn, m // tm), in_specs=in_specs,
        out_specs=pl.BlockSpec((tm, tn), lambda j, i: (i, j)),
        out_shape=jax.ShapeDtypeStruct((m, n), out_dtype),
        compiler_params=_cp(2, (vmem >> 20) + 8), name=name)(*args)


def _s5_body(u_ref, bre_ref, bim_ref, cre_ref, cim_ref, a_ref, h0_ref, y_ref, hT_ref, xr, xi, carry, *, tc, rev):
    @pl.when(pl.program_id(0) == 0)
    def _():
        carry[...] = h0_ref[...]

    for j in range(S5_NB):
        uj = u_ref[:, j * S5_KB:(j + 1) * S5_KB]
        xr[:, j * S5_PB:(j + 1) * S5_PB] = jnp.dot(uj, bre_ref[j], preferred_element_type=F32)
        xi[:, j * S5_PB:(j + 1) * S5_PB] = jnp.dot(uj, bim_ref[j], preferred_element_type=F32)

    for j in range(S5_NB):
        cs = slice(j * S5_PB, (j + 1) * S5_PB)
        ar, ai = a_ref[0:1, cs], a_ref[1:2, cs]

        def step(k, c, cs=cs, ar=ar, ai=ai):
            hr, hi = c
            t = (tc - 1 - k) if rev else k
            nr = ar * hr - ai * hi + xr[pl.ds(t, 1), cs]
            ni = ar * hi + ai * hr + xi[pl.ds(t, 1), cs]
            xr[pl.ds(t, 1), cs] = nr
            xi[pl.ds(t, 1), cs] = ni
            return nr, ni

        hr, hi = lax.fori_loop(0, tc, step, (carry[0:1, cs], carry[1:2, cs]), unroll=8)
        carry[0:1, cs] = hr
        carry[1:2, cs] = hi

    for j in range(S5_NB):
        cs = slice(j * S5_PB, (j + 1) * S5_PB)
        y_ref[:, j * S5_KB:(j + 1) * S5_KB] = (
            jnp.dot(xr[:, cs].astype(BF16), cre_ref[j], preferred_element_type=F32)
            + jnp.dot(xi[:, cs].astype(BF16), cim_ref[j], preferred_element_type=F32))
    hT_ref[...] = carry[...]


def _s5_dir(proj, mats, h0, rev):
    t = proj.shape[0]
    tc = min(256, t)
    nt = t // tc
    bre, bim, cre, cim, a = mats
    tmap = (lambda i: (nt - 1 - i, 0)) if rev else (lambda i: (i, 0))
    full3 = lambda i: (0, 0, 0)
    full2 = lambda i: (0, 0)
    return pl.pallas_call(
        functools.partial(_s5_body, tc=tc, rev=rev), grid=(nt,),
        in_specs=[pl.BlockSpec((tc, S5_WIDTH), tmap),
                  pl.BlockSpec((S5_NB, S5_KB, S5_PB), full3), pl.BlockSpec((S5_NB, S5_KB, S5_PB), full3),
                  pl.BlockSpec((S5_NB, S5_PB, S5_KB), full3), pl.BlockSpec((S5_NB, S5_PB, S5_KB), full3),
                  pl.BlockSpec((2, S5_CH), full2), pl.BlockSpec((2, S5_CH), full2)],
        out_specs=[pl.BlockSpec((tc, S5_WIDTH), tmap), pl.BlockSpec((2, S5_CH), full2)],
        out_shape=[jax.ShapeDtypeStruct((t, S5_WIDTH), F32), jax.ShapeDtypeStruct((2, S5_CH), F32)],
        scratch_shapes=[pltpu.VMEM((tc, S5_CH), F32), pltpu.VMEM((tc, S5_CH), F32), pltpu.VMEM((2, S5_CH), F32)],
        compiler_params=_cp(1, 40), name="s5_scan_bwd" if rev else "s5_scan_fwd")(proj, bre, bim, cre, cim, a, h0)


def _s5_out_body(u_ref, yf_ref, yb_ref, d_ref, wg_ref, bg_ref, o_ref):
    y = u_ref[...].astype(F32) * d_ref[...] + yf_ref[...] + yb_ref[...]
    g = _gelu(y)
    z = jnp.dot(g.astype(BF16), wg_ref[...], preferred_element_type=F32) + bg_ref[...]
    o_ref[...] = (g * jax.nn.sigmoid(z)).astype(o_ref.dtype)


def _s5_out(proj, yf, yb, d, wg, bg):
    t = proj.shape[0]
    tm = min(512, t)
    row = lambda i: (i, 0)
    full = lambda i: (0, 0)
    return pl.pallas_call(
        _s5_out_body, grid=(t // tm,),
        in_specs=[pl.BlockSpec((tm, S5_WIDTH), row), pl.BlockSpec((tm, S5_WIDTH), row), pl.BlockSpec((tm, S5_WIDTH), row),
                  pl.BlockSpec((1, S5_WIDTH), full), pl.BlockSpec((S5_WIDTH, S5_WIDTH), full), pl.BlockSpec((1, S5_WIDTH), full)],
        out_specs=pl.BlockSpec((tm, S5_WIDTH), row),
        out_shape=jax.ShapeDtypeStruct((t, S5_WIDTH), BF16),
        compiler_params=_cp(1, 24), name="s5_glu")(proj, yf, yb, d, wg, bg)


def _s5_mats(lam_re, lam_im, log_dt, b_re, b_im, c_re, c_im):
    dt = jnp.exp(log_dt)[:, None]
    mag = jnp.exp(lam_re * dt)
    ar, ai = mag * jnp.cos(lam_im * dt), mag * jnp.sin(lam_im * dt)
    nr, ni = ar - 1.0, ai
    den = lam_re * lam_re + lam_im * lam_im
    cr = (nr * lam_re + ni * lam_im) / den
    ci = (ni * lam_re - nr * lam_im) / den
    fre = cr[:, :, None] * b_re - ci[:, :, None] * b_im
    fim = cr[:, :, None] * b_im + ci[:, :, None] * b_re
    eye = jnp.eye(S5_GB, dtype=F32)

    def blk_in(f):
        f = f.reshape(S5_NB, S5_GB, S5_STATE, S5_GROUP_CH)
        m = jnp.einsum('jgph,gk->jghkp', f, eye)
        return m.reshape(S5_NB, S5_KB, S5_PB).astype(BF16)

    def blk_out(c):
        c = c.reshape(S5_NB, S5_GB, S5_GROUP_CH, S5_STATE)
        m = jnp.einsum('jghp,gk->jgpkh', c, eye)
        return m.reshape(S5_NB, S5_PB, S5_KB).astype(BF16)

    is_re = lax.broadcasted_iota(jnp.int32, (2, S5_CH), 0) == 0
    a = jnp.where(is_re, ar.reshape(1, -1), ai.reshape(1, -1))
    return blk_in(fre), blk_in(fim), blk_out(c_re), blk_out(-c_im), a


def _prep_body(qg_ref, kg_ref, vg_ref, qw_ref, kw_ref, cos_ref, sin_ref, qn_ref, kn_ref, oqg, okg, ovg, oqw, okw, *,
               qg_scale):
    cosf, sinf = cos_ref[...], sin_ref[...]
    lane = lax.broadcasted_iota(jnp.int32, cosf.shape, 1)
    low = (lane % AXIS_DIM) < (AXIS_DIM // 2)

    def rope(t):
        partner = jnp.where(low, pltpu.roll(t, HEAD_DIM - AXIS_DIM // 2, 1), pltpu.roll(t, AXIS_DIM // 2, 1))
        return t * cosf + partner * sinf

    def rms(t, g):
        return t * lax.rsqrt(jnp.mean(t * t, -1, keepdims=True) + EPS) * g

    for h in range(GA_HEADS):
        cs = slice(h * HEAD_DIM, (h + 1) * HEAD_DIM)
        oqg[:, cs] = (rope(rms(qg_ref[:, cs].astype(F32), qn_ref[...])) * qg_scale).astype(BF16)
        oqw[:, cs] = (rope(qw_ref[:, cs].astype(F32)) * ATTN_SCALE).astype(BF16)
    for h in range(GA_KV):
        cs = slice(h * HEAD_DIM, (h + 1) * HEAD_DIM)
        okg[:, cs] = rope(rms(kg_ref[:, cs].astype(F32), kn_ref[...])).astype(BF16)
        okw[:, cs] = rope(kw_ref[:, cs].astype(F32)).astype(BF16)
        ovg[:, 2 * h * HEAD_DIM:(2 * h + 1) * HEAD_DIM] = vg_ref[:, cs]
        ovg[:, (2 * h + 1) * HEAD_DIM:(2 * h + 2) * HEAD_DIM] = jnp.ones((vg_ref.shape[0], HEAD_DIM), BF16)


def _prep(proj, cosf, sinf, qn, kn, qg_scale):
    t = proj.shape[0]
    tr = min(256, t)
    qw_, kw_ = GA_HEADS * HEAD_DIM, GA_KV * HEAD_DIM
    row = lambda i: (i, 0)
    full = lambda i: (0, 0)
    return pl.pallas_call(
        functools.partial(_prep_body, qg_scale=qg_scale), grid=(t // tr,),
        in_specs=[pl.BlockSpec((tr, qw_), lambda i: (i, COL_QG // qw_)),
                  pl.BlockSpec((tr, kw_), lambda i: (i, COL_KG // kw_)),
                  pl.BlockSpec((tr, kw_), lambda i: (i, COL_VG // kw_)),
                  pl.BlockSpec((tr, qw_), lambda i: (i, COL_QW // qw_)),
                  pl.BlockSpec((tr, kw_), lambda i: (i, COL_KW // kw_)),
                  pl.BlockSpec((tr, HEAD_DIM), row), pl.BlockSpec((tr, HEAD_DIM), row),
                  pl.BlockSpec((1, HEAD_DIM), full), pl.BlockSpec((1, HEAD_DIM), full)],
        out_specs=[pl.BlockSpec((tr, qw_), row), pl.BlockSpec((tr, kw_), row), pl.BlockSpec((tr, 2 * kw_), row),
                   pl.BlockSpec((tr, qw_), row), pl.BlockSpec((tr, kw_), row)],
        out_shape=[jax.ShapeDtypeStruct((t, qw_), BF16), jax.ShapeDtypeStruct((t, kw_), BF16),
                   jax.ShapeDtypeStruct((t, 2 * kw_), BF16),
                   jax.ShapeDtypeStruct((t, qw_), BF16), jax.ShapeDtypeStruct((t, kw_), BF16)],
        compiler_params=_cp(1, 24), name="qk_prep")(proj, proj, proj, proj, proj, cosf, sinf, qn, kn)


def _rope_tables(n_lat):
    rows = n_lat // GRID_W
    row = jnp.repeat(jnp.arange(rows, dtype=F32), GRID_W)
    col = jnp.tile(jnp.arange(GRID_W, dtype=F32), rows)
    inv = jnp.power(ROPE_THETA, -jnp.arange(0, AXIS_DIM, 2, dtype=F32) / AXIS_DIM)
    ang_r, ang_c = row[:, None] * inv, col[:, None] * inv
    cr, sr, cc, sc = jnp.cos(ang_r), jnp.sin(ang_r), jnp.cos(ang_c), jnp.sin(ang_c)
    return jnp.concatenate([cr, cr, cc, cc], -1), jnp.concatenate([-sr, sr, -sc, sc], -1)


def _flash_body(q_ref, kc_ref, vc_ref, k_ref, v_ref, o_ref, qs, m_s, acc, *, tq, tk, nk):
    for g in range(KV_GROUP):
        qs[g * tq:(g + 1) * tq, :] = q_ref[:, g * HEAD_DIM:(g + 1) * HEAD_DIM]
    m_s[...] = jnp.full(m_s.shape, -jnp.inf, F32)
    acc[...] = jnp.zeros(acc.shape, F32)

    def update(k, v):
        for g in range(KV_GROUP):
            rs = slice(g * tq, (g + 1) * tq)
            s = lax.dot_general(qs[rs, :], k, _NT, preferred_element_type=F32)
            m_prev = m_s[rs, :]
            m_next = jnp.maximum(m_prev, jnp.max(s, axis=1, keepdims=True))
            alpha = jnp.exp2(m_prev - m_next)
            p = jnp.exp2(s - jnp.concatenate([m_next] * (s.shape[1] // HEAD_DIM), axis=1))
            acc[rs, :] = (jnp.concatenate([alpha, alpha], axis=1) * acc[rs, :]
                          + jnp.dot(p.astype(BF16), v, preferred_element_type=F32))
            m_s[rs, :] = m_next

    update(kc_ref[...], vc_ref[...])

    def body(j, carry):
        rows = pl.ds(pl.multiple_of(j * tk, tk), tk)
        update(k_ref[rows, :], v_ref[rows, :])
        return carry

    lax.fori_loop(0, nk, body, 0, unroll=FLASH_UNROLL)
    out = acc[:, :HEAD_DIM] / acc[:, HEAD_DIM:]
    for g in range(KV_GROUP):
        o_ref[:, g * HEAD_DIM:(g + 1) * HEAD_DIM] = out[g * tq:(g + 1) * tq, :].astype(o_ref.dtype)


FLASH_TK = 512
FLASH_UNROLL = 16


def _flash(q, kg, vge, kgc, vgce):
    t = q.shape[0]
    n_ctx = kgc.shape[0]
    tq = min(256, t)
    tk = min(FLASH_TK, t)
    gw = KV_GROUP * HEAD_DIM
    head = lambda h, i: (0, h)
    return pl.pallas_call(
        functools.partial(_flash_body, tq=tq, tk=tk, nk=t // tk), grid=(GA_KV, t // tq),
        in_specs=[pl.BlockSpec((tq, gw), lambda h, i: (i, h)),
                  pl.BlockSpec((n_ctx, HEAD_DIM), head), pl.BlockSpec((n_ctx, 2 * HEAD_DIM), head),
                  pl.BlockSpec((t, HEAD_DIM), head), pl.BlockSpec((t, 2 * HEAD_DIM), head)],
        out_specs=pl.BlockSpec((tq, gw), lambda h, i: (i, h)),
        out_shape=jax.ShapeDtypeStruct((t, GA_HEADS * HEAD_DIM), BF16),
        scratch_shapes=[pltpu.VMEM((KV_GROUP * tq, HEAD_DIM), BF16), pltpu.VMEM((KV_GROUP * tq, HEAD_DIM), F32),
                        pltpu.VMEM((KV_GROUP * tq, 2 * HEAD_DIM), F32)],
        compiler_params=_cp(2, 48), name="global_attn")(q, kgc, vgce, kg, vge)


def _win_body(q_ref, km_ref, k0_ref, kp_ref, vm_ref, v0_ref, vp_ref, kc_ref, vc_ref, sink_ref, o_ref, *, nb):
    i = pl.program_id(1)
    q = jnp.concatenate([q_ref[:, g * HEAD_DIM:(g + 1) * HEAD_DIM] for g in range(KV_GROUP)], axis=0)
    kw = jnp.concatenate([km_ref[...], k0_ref[...], kp_ref[...]], axis=0)
    vw = jnp.concatenate([vm_ref[...], v0_ref[...], vp_ref[...]], axis=0)
    s_w = lax.dot_general(q, kw, _NT, preferred_element_type=F32)
    r = lax.broadcasted_iota(jnp.int32, s_w.shape, 0) % BLOCK
    c = lax.broadcasted_iota(jnp.int32, s_w.shape, 1)
    cc = c % BLOCK
    iv = jnp.full(s_w.shape, i, jnp.int32)
    valid = (((c < BLOCK) & (cc >= r) & (iv >= 1)) | ((c >= BLOCK) & (c < 2 * BLOCK))
             | ((c >= 2 * BLOCK) & (cc <= r) & (iv + 1 < nb)))
    s_w = jnp.where(valid, s_w, NEG_INF)
    s_c = lax.dot_general(q, kc_ref[...], _NT, preferred_element_type=F32)
    sink = sink_ref[0]
    m = jnp.maximum(jnp.maximum(jnp.max(s_w, -1, keepdims=True), jnp.max(s_c, -1, keepdims=True)), sink)
    pw, pc = jnp.exp(s_w - m), jnp.exp(s_c - m)
    l = jnp.sum(pw, -1, keepdims=True) + jnp.sum(pc, -1, keepdims=True) + jnp.exp(sink - m)
    inv = 1.0 / l
    out = (jnp.dot((pc * inv).astype(BF16), vc_ref[...], preferred_element_type=F32)
           + jnp.dot((pw * inv).astype(BF16), vw, preferred_element_type=F32))
    for g in range(KV_GROUP):
        o_ref[:, g * HEAD_DIM:(g + 1) * HEAD_DIM] = out[g * BLOCK:(g + 1) * BLOCK, :].astype(o_ref.dtype)


def _sink_rows(sink, rows_per_head):
    return jnp.repeat(sink.astype(F32).reshape(WA_KV, KV_GROUP), rows_per_head, axis=1)[:, :, None]


def _window(qw, kw, proj, kwc, proj_c, sink):
    t = qw.shape[0]
    n_ctx = kwc.shape[0]
    nb = t // BLOCK
    gw = KV_GROUP * HEAD_DIM
    vcol = COL_VW // HEAD_DIM
    kspec = lambda f: pl.BlockSpec((BLOCK, HEAD_DIM), f)
    return pl.pallas_call(
        functools.partial(_win_body, nb=nb), grid=(WA_KV, nb),
        in_specs=[pl.BlockSpec((BLOCK, gw), lambda h, i: (i, h)),
                  kspec(lambda h, i: (jnp.maximum(i - 1, 0), h)), kspec(lambda h, i: (i, h)),
                  kspec(lambda h, i: (jnp.minimum(i + 1, nb - 1), h)),
                  kspec(lambda h, i: (jnp.maximum(i - 1, 0), vcol + h)), kspec(lambda h, i: (i, vcol + h)),
                  kspec(lambda h, i: (jnp.minimum(i + 1, nb - 1), vcol + h)),
                  pl.BlockSpec((n_ctx, HEAD_DIM), lambda h, i: (0, h)),
                  pl.BlockSpec((n_ctx, HEAD_DIM), lambda h, i: (0, vcol + h)),
                  pl.BlockSpec((1, KV_GROUP * BLOCK, 1), lambda h, i: (h, 0, 0))],
        out_specs=pl.BlockSpec((BLOCK, gw), lambda h, i: (i, h)),
        out_shape=jax.ShapeDtypeStruct((t, WA_HEADS * HEAD_DIM), BF16),
        compiler_params=_cp(2, 24), name="window_attn")(
            qw, kw, kw, kw, proj, proj, proj, kwc, proj_c, _sink_rows(sink, BLOCK))


def _ctx_attn_body(q_ref, k_ref, v_ref, sink_ref, o_ref, *, n_ctx, use_sink):
    q = jnp.concatenate([q_ref[:, g * HEAD_DIM:(g + 1) * HEAD_DIM] for g in range(KV_GROUP)], axis=0)
    s = lax.dot_general(q, k_ref[...], _NT, preferred_element_type=F32)
    m = jnp.max(s, -1, keepdims=True)
    if use_sink:
        m = jnp.maximum(m, sink_ref[0])
    p = jnp.exp(s - m)
    l = jnp.sum(p, -1, keepdims=True)
    if use_sink:
        l = l + jnp.exp(sink_ref[0] - m)
    out = jnp.dot((p / l).astype(BF16), v_ref[...], preferred_element_type=F32)
    for g in range(KV_GROUP):
        o_ref[:, g * HEAD_DIM:(g + 1) * HEAD_DIM] = out[g * n_ctx:(g + 1) * n_ctx, :].astype(o_ref.dtype)


def _ctx_attn(q, k, proj_c, vcol0, sink, use_sink):
    n_ctx = q.shape[0]
    gw = KV_GROUP * HEAD_DIM
    vcol = vcol0 // HEAD_DIM
    return pl.pallas_call(
        functools.partial(_ctx_attn_body, n_ctx=n_ctx, use_sink=use_sink), grid=(GA_KV,),
        in_specs=[pl.BlockSpec((n_ctx, gw), lambda h: (0, h)),
                  pl.BlockSpec((n_ctx, HEAD_DIM), lambda h: (0, h)),
                  pl.BlockSpec((n_ctx, HEAD_DIM), lambda h: (0, vcol + h)),
                  pl.BlockSpec((1, KV_GROUP * n_ctx, 1), lambda h: (h, 0, 0))],
        out_specs=pl.BlockSpec((n_ctx, gw), lambda h: (0, h)),
        out_shape=jax.ShapeDtypeStruct((n_ctx, GA_HEADS * HEAD_DIM), BF16),
        compiler_params=_cp(1, 24), name="ctx_attn_sink" if use_sink else "ctx_attn")(
            q, k, proj_c, _sink_rows(sink, n_ctx))


def _sg_body(u_ref, v_ref, g_ref, b_ref, w_ref, bs_ref, o_ref, *, nch):
    for n in range(nch):
        rs = slice(n * CHUNK, (n + 1) * CHUNK)
        vn = _ln(v_ref[rs, :].astype(F32), g_ref[...], b_ref[...]).astype(BF16)
        for g in range(SG_GROUPS):
            cs = slice(g * SG_GROUP_CH, (g + 1) * SG_GROUP_CH)
            s = jnp.dot(w_ref[g], vn[:, cs], preferred_element_type=F32) + bs_ref[g]
            o_ref[rs, cs] = (u_ref[rs, cs].astype(F32) * s).astype(o_ref.dtype)


def _spatial_gate(proj, g_v, b_v, w_sp, b_sp):
    t = proj.shape[0]
    tm = min(512, t)
    full = lambda i: (0, 0)
    return pl.pallas_call(
        functools.partial(_sg_body, nch=tm // CHUNK), grid=(t // tm,),
        in_specs=[pl.BlockSpec((tm, SG_WIDTH), lambda i: (i, COL_DU // SG_WIDTH)),
                  pl.BlockSpec((tm, SG_WIDTH), lambda i: (i, COL_DV // SG_WIDTH)),
                  pl.BlockSpec((1, SG_WIDTH), full), pl.BlockSpec((1, SG_WIDTH), full),
                  pl.BlockSpec((SG_GROUPS, CHUNK, CHUNK), lambda i: (0, 0, 0)),
                  pl.BlockSpec((SG_GROUPS, CHUNK, 1), lambda i: (0, 0, 0))],
        out_specs=pl.BlockSpec((tm, SG_WIDTH), lambda i: (i, 0)),
        out_shape=jax.ShapeDtypeStruct((t, SG_WIDTH), BF16),
        compiler_params=_cp(1, 24), name="spatial_gate")(proj, proj, g_v, b_v, w_sp, b_sp)


def _merge_body(ya, yb, yc, yd, wa, wb, wc, wd, ga, gb, gc, gd, o_ref):
    def br(y, w, g):
        return jax.nn.sigmoid(g[...].astype(F32)) * jnp.dot(y[...], w[...], preferred_element_type=F32)
    o_ref[...] = (br(ya, wa, ga) + br(yb, wb, gb) + br(yc, wc, gc) + br(yd, wd, gd)).astype(o_ref.dtype)


def _merge(ys, ws, gates):
    t = ys[0].shape[0]
    tm, tn = min(512, t), 1024
    nn = D_MODEL // tn
    in_specs = [pl.BlockSpec((tm, y.shape[1]), lambda j, i: (i, 0)) for y in ys]
    in_specs += [pl.BlockSpec((w.shape[0], tn), lambda j, i: (0, j)) for w in ws]
    in_specs += [pl.BlockSpec((tm, tn), functools.partial(lambda j, i, b: (i, b * nn + j), b=b)) for b in range(N_BRANCH)]
    return pl.pallas_call(
        _merge_body, grid=(nn, t // tm), in_specs=in_specs,
        out_specs=pl.BlockSpec((tm, tn), lambda j, i: (i, j)),
        out_shape=jax.ShapeDtypeStruct((t, D_MODEL), BF16),
        compiler_params=_cp(2, 48), name="gated_merge")(*ys, *ws, gates, gates, gates, gates)


def _out_ln_body(mg_ref, w_ref, x_ref, g1_ref, lng_ref, lnb_ref, sc_ref, sh_ref, x1_ref, xmT_ref, *, alpha):
    o = jnp.dot(mg_ref[...], w_ref[...], preferred_element_type=F32)
    x1 = _ln(alpha * x_ref[...] + g1_ref[...] * o, lng_ref[...], lnb_ref[...])
    x1_ref[...] = x1
    xmT_ref[...] = (x1 * (1.0 + sc_ref[...]) + sh_ref[...]).T.astype(xmT_ref.dtype)


def _out_ln(merged, w_out, x, g1, lng, lnb, sc2, sh2, alpha):
    t, d = x.shape
    tm = min(256, t)
    row = lambda i: (i, 0)
    vec = pl.BlockSpec((1, d), lambda i: (0, 0))
    return pl.pallas_call(
        functools.partial(_out_ln_body, alpha=alpha), grid=(t // tm,),
        in_specs=[pl.BlockSpec((tm, d), row), pl.BlockSpec((d, d), lambda i: (0, 0)), pl.BlockSpec((tm, d), row),
                  vec, vec, vec, vec, vec],
        out_specs=[pl.BlockSpec((tm, d), row), pl.BlockSpec((d, tm), lambda i: (0, i))],
        out_shape=[jax.ShapeDtypeStruct((t, d), F32), jax.ShapeDtypeStruct((d, t), BF16)],
        compiler_params=_cp(1, 48), name="out_proj_ln")(merged, w_out, x, g1, lng, lnb, sc2, sh2)


def _peer_topk_body(q_ref, k_ref, s1_o, s2_o, s1l_o, s2l_o, tau_o):
    q = q_ref[...].astype(BF16)
    s1 = jnp.dot(k_ref[0], q[:PEER_HALF], preferred_element_type=F32)
    s2 = jnp.dot(k_ref[1], q[PEER_HALF:], preferred_element_type=F32)
    ninf = -jnp.inf

    def top_distinct(s):
        vals, tops, cnts = s, [], []
        for _ in range(PEER_TOPK):
            m = jnp.max(vals, axis=0, keepdims=True)
            eq = vals == m
            tops.append(m)
            cnts.append(jnp.sum(eq.astype(F32), axis=0, keepdims=True))
            vals = jnp.where(eq, ninf, vals)
        return jnp.concatenate(tops, 0), jnp.concatenate(cnts, 0)

    ta, na = top_distinct(s1)
    tb, nb = top_distinct(s2)
    nrow = [PEER_TOPK // (k + 1) for k in range(PEER_TOPK)]
    pad = -sum(nrow) % 8
    cand = jnp.concatenate([ta[k:k + 1] + tb[:nrow[k]] for k in range(PEER_TOPK)]
                           + [jnp.full((pad, ta.shape[1]), ninf, F32)], 0)
    mult = jnp.concatenate([na[k:k + 1] * nb[:nrow[k]] for k in range(PEER_TOPK)]
                           + [jnp.zeros((pad, ta.shape[1]), F32)], 0)
    vals = cand
    cnt = jnp.zeros_like(cand[0:1])
    tau = jnp.full_like(cand[0:1], ninf)
    for _ in range(PEER_TOPK):
        m = jnp.max(vals, axis=0, keepdims=True)
        eq = vals == m
        tau = jnp.where(cnt < PEER_TOPK, m, tau)
        cnt = cnt + jnp.sum(jnp.where(eq, mult, 0.0), axis=0, keepdims=True)
        vals = jnp.where(eq, ninf, vals)
    cmax = cand[0:1]
    z = jnp.sum(jnp.where(cand >= tau, mult * jnp.exp(cand - cmax), 0.0), axis=0, keepdims=True)
    g1 = jnp.exp(s1 - ta[0:1]) * (1.0 / z)
    g2 = jnp.exp(s2 - tb[0:1])
    for lc in range(s1.shape[1] // PEER_LANES):
        ls = slice(lc * PEER_LANES, (lc + 1) * PEER_LANES)
        s1_o[lc] = s1[:, ls]
        s2_o[lc] = s2[:, ls]
        s1l_o[lc] = g1[:, ls]
        s2l_o[lc] = g2[:, ls]
        tau_o[lc] = tau[:, ls]


PEER_LANES = 128


def _peer_topk(qT, keys):
    t = qT.shape[1]
    tt = min(512, t)
    nl = tt // PEER_LANES
    blk = pl.BlockSpec((None, nl, PEER_NKEYS, PEER_LANES), lambda j, h: (h, j, 0, 0))
    shp = jax.ShapeDtypeStruct((PEER_HEADS, t // PEER_LANES, PEER_NKEYS, PEER_LANES), F32)
    return pl.pallas_call(
        _peer_topk_body, grid=(t // tt, PEER_HEADS),
        in_specs=[pl.BlockSpec((PEER_QDIM, tt), lambda j, h: (h, j)),
                  pl.BlockSpec((None, 2, PEER_NKEYS, PEER_HALF), lambda j, h: (h, 0, 0, 0))],
        out_specs=[blk, blk, blk, blk, pl.BlockSpec((None, nl, 1, PEER_LANES), lambda j, h: (h, j, 0, 0))],
        out_shape=[shp, shp, shp, shp, jax.ShapeDtypeStruct((PEER_HEADS, t // PEER_LANES, 1, PEER_LANES), F32)],
        compiler_params=_cp(2, 32), name="peer_topk")(qT, keys)


PEER_EK = 512
PEER_SUBS = PEER_EK // PEER_NKEYS
PEER_PIECES = 8


def _peer_dense_body(xT_ref, u_ref, vT_ref, s1_ref, s2_ref, s1l_ref, s2l_ref, tau_ref, o_ref, hbuf, awbuf, *, nk):
    g = pl.program_id(0)
    d = o_ref.shape[0]
    h0, h1, aw0, aw1 = hbuf.at[0], hbuf.at[1], awbuf.at[0], awbuf.at[1]

    @pl.when(g == 0)
    def _():
        hbuf[...] = jnp.zeros(hbuf.shape, F32)
        awbuf[...] = jnp.zeros(awbuf.shape, BF16)

    @pl.when((g <= 2) | ((g - 2) % nk == 0))
    def _():
        o_ref[...] = jnp.zeros(o_ref.shape, F32)

    def stages(h_cur, h_prev, aw_cur, aw_prev):
        base = (jnp.clip(g - 1, 0, pl.num_programs(0) - 3) % nk) * PEER_SUBS
        dq = d // PEER_PIECES

        def second_matmul(q):
            ds_ = slice(q * dq, (q + 1) * dq)
            o_ref[ds_, :] += jnp.dot(vT_ref[ds_, :], aw_cur[...], preferred_element_type=F32)

        def first_matmul(q):
            ks = slice(q * dq, (q + 1) * dq)
            part = jnp.dot(u_ref[:, ks], xT_ref[ks, :], preferred_element_type=F32)
            if q == 0:
                h_cur[...] = part
            else:
                h_cur[...] += part

        def gate_tile(ii, lc):
            i1 = pl.ds(base + ii, 1)
            w = None
            for h in range(PEER_HEADS):
                c = s2_ref[h, lc] + s1_ref[h, lc, i1, :]
                wh = jnp.where(c >= tau_ref[h, lc], s2l_ref[h, lc] * s1l_ref[h, lc, i1, :], 0.0)
                w = wh if w is None else w + wh
            rs = slice(ii * PEER_NKEYS, (ii + 1) * PEER_NKEYS)
            ls = slice(lc * PEER_LANES, (lc + 1) * PEER_LANES)
            aw_prev[rs, ls] = (_gelu(h_prev[rs, ls]) * w).astype(BF16)

        tiles = [(ii, lc) for ii in range(PEER_SUBS) for lc in range(o_ref.shape[1] // PEER_LANES)]
        per = -(-len(tiles) // (2 * PEER_PIECES))
        for q in range(PEER_PIECES):
            second_matmul(q)
            for tl in tiles[(2 * q) * per:(2 * q + 1) * per]:
                gate_tile(*tl)
            first_matmul(q)
            for tl in tiles[(2 * q + 1) * per:(2 * q + 2) * per]:
                gate_tile(*tl)

    @pl.when(g % 2 == 0)
    def _():
        stages(h0, h1, aw0, aw1)

    @pl.when(g % 2 == 1)
    def _():
        stages(h1, h0, aw1, aw0)


def _peer_dense(xmT, u_bf, vT_bf, tk):
    d, t = xmT.shape
    tt = min(512, t)
    nl = tt // PEER_LANES
    nk = PEER_EXPERTS // PEER_EK
    s1, s2, s1l, s2l, tau = tk
    last = (t // tt) * nk - 1
    pair = lambda g, lag: jnp.clip(g - lag, 0, last)
    sblk = pl.BlockSpec((PEER_HEADS, nl, PEER_NKEYS, PEER_LANES), lambda g: (0, pair(g, 1) // nk, 0, 0))
    return pl.pallas_call(
        functools.partial(_peer_dense_body, nk=nk), grid=(last + 3,),
        in_specs=[pl.BlockSpec((d, tt), lambda g: (0, pair(g, 0) // nk)),
                  pl.BlockSpec((PEER_EK, d), lambda g: (pair(g, 0) % nk, 0)),
                  pl.BlockSpec((d, PEER_EK), lambda g: (0, pair(g, 2) % nk)),
                  sblk, sblk, sblk, sblk,
                  pl.BlockSpec((PEER_HEADS, nl, 1, PEER_LANES), lambda g: (0, pair(g, 1) // nk, 0, 0))],
        out_specs=pl.BlockSpec((d, tt), lambda g: (0, pair(g, 2) // nk)),
        out_shape=jax.ShapeDtypeStruct((d, t), F32),
        scratch_shapes=[pltpu.VMEM((2, PEER_EK, tt), F32), pltpu.VMEM((2, PEER_EK, tt), BF16)],
        compiler_params=_cp(1, 48), name="peer_dense")(xmT, u_bf, vT_bf, s1, s2, s1l, s2l, tau)


def _ffn_ln_body(fT_ref, x1_ref, g2_ref, lng_ref, lnb_ref, scn_ref, shn_ref, x2_ref, *rest, alpha):
    x2 = _ln(alpha * x1_ref[...] + g2_ref[...] * fT_ref[...].T, lng_ref[...], lnb_ref[...])
    x2_ref[...] = x2
    if rest:
        rest[0][...] = (x2 * (1.0 + scn_ref[...]) + shn_ref[...]).astype(rest[0].dtype)


def _ffn_ln(fT, x1, g2, lng, lnb, scn, shn, alpha, emit_h):
    t, d = x1.shape
    tm = min(256, t)
    row = pl.BlockSpec((tm, d), lambda i: (i, 0))
    vec = pl.BlockSpec((1, d), lambda i: (0, 0))
    out_specs = [row, row] if emit_h else [row]
    out_shape = [jax.ShapeDtypeStruct((t, d), F32)] + ([jax.ShapeDtypeStruct((t, d), BF16)] if emit_h else [])
    return pl.pallas_call(
        functools.partial(_ffn_ln_body, alpha=alpha), grid=(t // tm,),
        in_specs=[pl.BlockSpec((d, tm), lambda i: (0, i)), row, vec, vec, vec, vec, vec],
        out_specs=out_specs, out_shape=out_shape,
        compiler_params=_cp(1, 32), name="ffn_ln")(fT, x1, g2, lng, lnb, scn, shn)


def kernel(x, c, ctx, c_ctx, w_ada, b_ada, w_in, b_in, s5_lam_re, s5_lam_im, s5_log_dt, s5_b_re, s5_b_im, s5_c_re, s5_c_im, s5_d, w_glu, b_glu, qn_gain, kn_gain, sink, sg_ln_g, sg_ln_b, w_sp, b_sp, w_br_a, w_br_b, w_br_c, w_br_d, w_out, ln1_g, ln1_b, ln2_g, ln2_b, w_pq, peer_keys, peer_u, peer_v):
    depth = w_in.shape[0]
    bsz, n_lat, d = x.shape
    assert bsz == 1 and d == D_MODEL and n_lat % 512 == 0 and ctx.shape[1] % BLOCK == 0
    n_ctx = ctx.shape[1]
    alpha = (2 * depth) ** 0.25
    x, ctx = x[0], ctx[0]

    row8 = lax.broadcasted_iota(jnp.int32, (8, d), 0)
    cond8 = jnp.where(row8 == 0, c[0][None], jnp.where(row8 == 1, c_ctx[None], 0.0))
    ada = [_ada(cond8, w_ada, b_ada, l) for l in range(depth)]
    cosf, sinf = _rope_tables(n_lat)
    cos1, sin0 = jnp.ones((n_ctx, HEAD_DIM), F32), jnp.zeros((n_ctx, HEAD_DIM), F32)
    vec = lambda a: a.reshape(1, -1).astype(F32)

    ref_off = {'a': 0, 'qg': 768, 'kg': 1792, 'vg': 2048, 'qw': 2304, 'kw': 3328, 'vw': 3584, 'du': 3840, 'dv': 4608}
    ref_w = {'a': 768, 'qg': 1024, 'kg': 256, 'vg': 256, 'qw': 1024, 'kw': 256, 'vw': 256, 'du': 768, 'dv': 768}
    order = ['a', 'kg', 'qg', 'qw', 'kw', 'vg', 'vw', 'du', 'dv']
    perm = lambda a: jnp.concatenate([a[..., ref_off[n]:ref_off[n] + ref_w[n]] for n in order], -1)
    perm_idx = np.concatenate([np.arange(ref_off[n], ref_off[n] + ref_w[n]) for n in order]).astype(np.int32)

    h_lat = None
    for l in range(depth):
        need_ctx = l < depth - 1
        mods = [[ada[l][r:r + 1, i * d:(i + 1) * d] for i in range(6)] for r in range(2)]
        (sh1, sc1, g1, sh2, sc2, g2), (sh1c, sc1c, g1c, sh2c, sc2c, g2c) = mods
        if need_ctx:
            nxt = [ada[l + 1][r:r + 1, 0:2 * d] for r in range(2)]
            (shn, scn), (shnc, scnc) = [(m[:, :d], m[:, d:]) for m in nxt]
        else:
            shn = scn = shnc = scnc = jnp.zeros((1, d), F32)

        w_small, b_small = perm(w_in[l][:, :SMALL_W]).astype(BF16), vec(jnp.take(b_in[l], perm_idx))
        w_gate, b_gate = w_in[l][:, SMALL_W:].astype(BF16), vec(b_in[l][SMALL_W:])
        s5m = [_s5_mats(s5_lam_re[l, dr], s5_lam_im[l, dr], s5_log_dt[l, dr], s5_b_re[l, dr], s5_b_im[l, dr],
                        s5_c_re[l, dr], s5_c_im[l, dr]) for dr in range(2)]
        wg, bg, s5d = w_glu[l].astype(BF16), vec(b_glu[l]), vec(s5_d[l])
        qn, kn = vec(qn_gain[l]), vec(kn_gain[l])
        wsp, bsp = w_sp[l].astype(BF16), b_sp[l].astype(F32)[:, :, None]
        w_brs = [w[l].astype(BF16) for w in (w_br_a, w_br_b, w_br_c, w_br_d)]
        wo = w_out[l].astype(BF16)
        wpqT = w_pq[l].T.astype(BF16)
        keys = peer_keys[l].astype(BF16)
        u_bf, vT_bf = peer_u[l].astype(BF16), peer_v[l].T.astype(BF16)

        if h_lat is None:
            h_lat, h_ctx = _modulate(x, sc1, sh1), _modulate(ctx, sc1c, sh1c)

        proj = _matmul(h_lat, w_small, b_small, tm=512, tn=1792, out_dtype=BF16, name="in_proj")
        gates = _matmul(h_lat, w_gate, b_gate, tm=512, tn=2048, out_dtype=BF16, name="in_proj_gates")
        proj_c = _matmul(h_ctx, w_small, b_small, tm=512, tn=1792, out_dtype=BF16, name="in_proj_ctx")

        y_dir, y_dir_c = [], []
        for dr in range(2):
            yc_, hc_end = _s5_dir(proj_c, s5m[dr], jnp.zeros((2, S5_CH), F32), bool(dr))
            yl_, _ = _s5_dir(proj, s5m[dr], hc_end, bool(dr))
            y_dir.append(yl_)
            y_dir_c.append(yc_)
        y_a = _s5_out(proj, y_dir[0], y_dir[1], s5d, wg, bg)

        qg, kg, vge, qw, kw = _prep(proj, cosf, sinf, qn, kn, ATTN_SCALE * math.log2(math.e))
        qgc, kgc, vgce, qwc, kwc = _prep(proj_c, cos1, sin0, qn, kn, ATTN_SCALE)
        y_b = _flash(qg, kg, vge, kgc, vgce)
        y_c = _window(qw, kw, proj, kwc, proj_c, sink[l])

        y_d = _spatial_gate(proj, vec(sg_ln_g[l]), vec(sg_ln_b[l]), wsp, bsp)

        merged = _merge((y_a, y_b, y_c, y_d), w_brs, gates)
        x1, xmT = _out_ln(merged, wo, x, g1, vec(ln1_g[l]), vec(ln1_b[l]), sc2, sh2, alpha)
        qT = _matmul(wpqT, xmT, None, tm=D_MODEL, tn=1024, out_dtype=F32, name="peer_query")
        ffnT = _peer_dense(xmT, u_bf, vT_bf, _peer_topk(qT, keys))
        res = _ffn_ln(ffnT, x1, g2, vec(ln2_g[l]), vec(ln2_b[l]), scn, shn, alpha, need_ctx)
        x = res[0]

        if need_ctx:
            h_lat = res[1]
            gates_c = _matmul(h_ctx, w_gate, b_gate, tm=512, tn=2048, out_dtype=BF16, name="in_proj_gates_ctx")
            y_a_c = _s5_out(proj_c, y_dir_c[0], y_dir_c[1], s5d, wg, bg)
            y_b_c = _ctx_attn(qgc, kgc, proj_c, COL_VG, sink[l], False)
            y_c_c = _ctx_attn(qwc, kwc, proj_c, COL_VW, sink[l], True)
            y_d_c = _spatial_gate(proj_c, vec(sg_ln_g[l]), vec(sg_ln_b[l]), wsp, bsp)
            merged_c = _merge((y_a_c, y_b_c, y_c_c, y_d_c), w_brs, gates_c)
            c1, cmT = _out_ln(merged_c, wo, ctx, g1c, vec(ln1_g[l]), vec(ln1_b[l]), sc2c, sh2c, alpha)
            qTc = _matmul(wpqT, cmT, None, tm=D_MODEL, tn=1024, out_dtype=F32, name="peer_query_ctx")
            ffnTc = _peer_dense(cmT, u_bf, vT_bf, _peer_topk(qTc, keys))
            ctx, h_ctx = _ffn_ln(ffnTc, c1, g2c, vec(ln2_g[l]), vec(ln2_b[l]), scnc, shnc, alpha, True)
    return x[None]
```

```python
import functools
import math

import jax
import jax.numpy as jnp
import numpy as np
from jax import lax
from jax.experimental import pallas as pl
from jax.experimental.pallas import tpu as pltpu

F32, BF16 = jnp.float32, jnp.bfloat16

D_MODEL = 2048
GRID_W = 64
BLOCK = 128
HEAD_DIM = 128
AXIS_DIM = HEAD_DIM // 2
ROPE_THETA = 10000.0
ATTN_SCALE = HEAD_DIM ** -0.5
NEG_INF = -1e30
EPS = 1e-6
S5_GROUPS, S5_GROUP_CH, S5_STATE = 48, 16, 64
S5_WIDTH = S5_GROUPS * S5_GROUP_CH
S5_CH = S5_GROUPS * S5_STATE
S5_GB = 16
S5_NB = S5_GROUPS // S5_GB
S5_KB = S5_GB * S5_GROUP_CH
S5_PB = S5_GB * S5_STATE
GA_HEADS, GA_KV = 8, 2
WA_HEADS, WA_KV = 8, 2
KV_GROUP = GA_HEADS // GA_KV
SG_GROUPS, SG_GROUP_CH = 6, 128
SG_WIDTH = SG_GROUPS * SG_GROUP_CH
CHUNK = 128
N_BRANCH = 4
PEER_HEADS, PEER_NKEYS, PEER_QDIM, PEER_TOPK = 8, 128, 256, 16
PEER_HALF = PEER_QDIM // 2
PEER_EXPERTS = PEER_NKEYS * PEER_NKEYS

COL_A, COL_KG, COL_QG, COL_QW, COL_KW, COL_VG, COL_VW, COL_DU, COL_DV = 0, 768, 1024, 2048, 3072, 3328, 3584, 3840, 4608
SMALL_W = 5376
GATE_W = N_BRANCH * D_MODEL

VMEM_CAP_MB = 56


def _cp(ndims, vmem_mb=32):
    return pltpu.CompilerParams(dimension_semantics=("arbitrary",) * ndims,
                                vmem_limit_bytes=min(vmem_mb, VMEM_CAP_MB) << 20)


def _gelu(y):
    return 0.5 * y * (1.0 + lax.erf(y * (2.0 ** -0.5)))


def _ln(y, g, b):
    mu = jnp.mean(y, -1, keepdims=True)
    yc = y - mu
    var = jnp.mean(yc * yc, -1, keepdims=True)
    return yc * lax.rsqrt(var + EPS) * g + b


_NT = (((1,), (1,)), ((), ()))


def _ada_body(c_ref, w_ref, b_ref, o_ref):
    cnd = c_ref[...]
    s = (cnd * jax.nn.sigmoid(cnd)).astype(BF16)
    o_ref[...] = jnp.dot(s, w_ref[...].astype(BF16), preferred_element_type=F32) + b_ref[...]


def _ada(cond8, w_ada, b_ada, l):
    depth, d, n = w_ada.shape
    tn = 1024
    return pl.pallas_call(
        _ada_body, grid=(n // tn,),
        in_specs=[pl.BlockSpec((8, d), lambda j: (0, 0)),
                  pl.BlockSpec((None, d, tn), lambda j: (l, 0, j)),
                  pl.BlockSpec((None, 1, tn), lambda j: (l, 0, j))],
        out_specs=pl.BlockSpec((8, tn), lambda j: (0, j)),
        out_shape=jax.ShapeDtypeStruct((8, n), F32),
        compiler_params=_cp(1, 32), name="ada_mod")(cond8, w_ada, b_ada.reshape(depth, 1, n))


def _mod_body(x_ref, sc_ref, sh_ref, o_ref):
    o_ref[...] = (x_ref[...] * (1.0 + sc_ref[...]) + sh_ref[...]).astype(o_ref.dtype)


def _modulate(x, sc, sh):
    t, d = x.shape
    tm = min(512, t)
    return pl.pallas_call(
        _mod_body, grid=(t // tm,),
        in_specs=[pl.BlockSpec((tm, d), lambda i: (i, 0)),
                  pl.BlockSpec((1, d), lambda i: (0, 0)),
                  pl.BlockSpec((1, d), lambda i: (0, 0))],
        out_specs=pl.BlockSpec((tm, d), lambda i: (i, 0)),
        out_shape=jax.ShapeDtypeStruct((t, d), BF16),
        compiler_params=_cp(1, 24), name="modulate")(x, sc, sh)


def _mm_body(a_ref, w_ref, b_ref, o_ref):
    acc = jnp.dot(a_ref[...], w_ref[...], preferred_element_type=F32)
    o_ref[...] = (acc + b_ref[...]).astype(o_ref.dtype)


def _mm_nobias_body(a_ref, w_ref, o_ref):
    o_ref[...] = jnp.dot(a_ref[...], w_ref[...], preferred_element_type=F32).astype(o_ref.dtype)


def _matmul(a, w, b, *, tm, tn, out_dtype, name):
    m, k = a.shape
    n = w.shape[1]
    tm, tn = min(tm, m), min(tn, n)
    in_specs = [pl.BlockSpec((tm, k), lambda j, i: (i, 0)),
                pl.BlockSpec((k, tn), lambda j, i: (0, j))]
    args = [a, w]
    body = _mm_nobias_body
    if b is not None:
        in_specs.append(pl.BlockSpec((1, tn), lambda j, i: (0, j)))
        args.append(b)
        body = _mm_body
    osz = jnp.dtype(out_dtype).itemsize
    vmem = 2 * (tm * k * 2 + k * tn * 2 + tm * tn * osz) + tm * tn * 4
    return pl.pallas_call(
        body, grid=(n // tn, m // tm), in_specs=in_specs,
        out_specs=pl.BlockSpec((tm, tn), lambda j, i: (i, j)),
        out_shape=jax.ShapeDtypeStruct((m, n), out_dtype),
        compiler_params=_cp(2, (vmem >> 20) + 8), name=name)(*args)


def _s5_body(u_ref, bre_ref, bim_ref, cre_ref, cim_ref, a_ref, h0_ref, y_ref, hT_ref, xr, xi, carry, *, tc, rev):
    @pl.when(pl.program_id(0) == 0)
    def _():
        carry[...] = h0_ref[...]

    for j in range(S5_NB):
        uj = u_ref[:, j * S5_KB:(j + 1) * S5_KB]
        xr[:, j * S5_PB:(j + 1) * S5_PB] = jnp.dot(uj, bre_ref[j], preferred_element_type=F32)
        xi[:, j * S5_PB:(j + 1) * S5_PB] = jnp.dot(uj, bim_ref[j], preferred_element_type=F32)

    for j in range(S5_NB):
        cs = slice(j * S5_PB, (j + 1) * S5_PB)
        ar, ai = a_ref[0:1, cs], a_ref[1:2, cs]

        def step(k, c, cs=cs, ar=ar, ai=ai):
            hr, hi = c
            t = (tc - 1 - k) if rev else k
            nr = ar * hr - ai * hi + xr[pl.ds(t, 1), cs]
            ni = ar * hi + ai * hr + xi[pl.ds(t, 1), cs]
            xr[pl.ds(t, 1), cs] = nr
            xi[pl.ds(t, 1), cs] = ni
            return nr, ni

        hr, hi = lax.fori_loop(0, tc, step, (carry[0:1, cs], carry[1:2, cs]), unroll=8)
        carry[0:1, cs] = hr
        carry[1:2, cs] = hi

    for j in range(S5_NB):
        cs = slice(j * S5_PB, (j + 1) * S5_PB)
        y_ref[:, j * S5_KB:(j + 1) * S5_KB] = (
            jnp.dot(xr[:, cs].astype(BF16), cre_ref[j], preferred_element_type=F32)
            + jnp.dot(xi[:, cs].astype(BF16), cim_ref[j], preferred_element_type=F32))
    hT_ref[...] = carry[...]


def _s5_dir(proj, mats, h0, rev):
    t = proj.shape[0]
    tc = min(256, t)
    nt = t // tc
    bre, bim, cre, cim, a = mats
    tmap = (lambda i: (nt - 1 - i, 0)) if rev else (lambda i: (i, 0))
    full3 = lambda i: (0, 0, 0)
    full2 = lambda i: (0, 0)
    return pl.pallas_call(
        functools.partial(_s5_body, tc=tc, rev=rev), grid=(nt,),
        in_specs=[pl.BlockSpec((tc, S5_WIDTH), tmap),
                  pl.BlockSpec((S5_NB, S5_KB, S5_PB), full3), pl.BlockSpec((S5_NB, S5_KB, S5_PB), full3),
                  pl.BlockSpec((S5_NB, S5_PB, S5_KB), full3), pl.BlockSpec((S5_NB, S5_PB, S5_KB), full3),
                  pl.BlockSpec((2, S5_CH), full2), pl.BlockSpec((2, S5_CH), full2)],
        out_specs=[pl.BlockSpec((tc, S5_WIDTH), tmap), pl.BlockSpec((2, S5_CH), full2)],
        out_shape=[jax.ShapeDtypeStruct((t, S5_WIDTH), F32), jax.ShapeDtypeStruct((2, S5_CH), F32)],
        scratch_shapes=[pltpu.VMEM((tc, S5_CH), F32), pltpu.VMEM((tc, S5_CH), F32), pltpu.VMEM((2, S5_CH), F32)],
        compiler_params=_cp(1, 40), name="s5_scan_bwd" if rev else "s5_scan_fwd")(proj, bre, bim, cre, cim, a, h0)


def _s5_out_body(u_ref, yf_ref, yb_ref, d_ref, wg_ref, bg_ref, o_ref):
    y = u_ref[...].astype(F32) * d_ref[...] + yf_ref[...] + yb_ref[...]
    g = _gelu(y)
    z = jnp.dot(g.astype(BF16), wg_ref[...], preferred_element_type=F32) + bg_ref[...]
    o_ref[...] = (g * jax.nn.sigmoid(z)).astype(o_ref.dtype)


def _s5_out(proj, yf, yb, d, wg, bg):
    t = proj.shape[0]
    tm = min(512, t)
    row = lambda i: (i, 0)
    full = lambda i: (0, 0)
    return pl.pallas_call(
        _s5_out_body, grid=(t // tm,),
        in_specs=[pl.BlockSpec((tm, S5_WIDTH), row), pl.BlockSpec((tm, S5_WIDTH), row), pl.BlockSpec((tm, S5_WIDTH), row),
                  pl.BlockSpec((1, S5_WIDTH), full), pl.BlockSpec((S5_WIDTH, S5_WIDTH), full), pl.BlockSpec((1, S5_WIDTH), full)],
        out_specs=pl.BlockSpec((tm, S5_WIDTH), row),
        out_shape=jax.ShapeDtypeStruct((t, S5_WIDTH), BF16),
        compiler_params=_cp(1, 24), name="s5_glu")(proj, yf, yb, d, wg, bg)


def _s5_mats(lam_re, lam_im, log_dt, b_re, b_im, c_re, c_im):
    dt = jnp.exp(log_dt)[:, None]
    mag = jnp.exp(lam_re * dt)
    ar, ai = mag * jnp.cos(lam_im * dt), mag * jnp.sin(lam_im * dt)
    nr, ni = ar - 1.0, ai
    den = lam_re * lam_re + lam_im * lam_im
    cr = (nr * lam_re + ni * lam_im) / den
    ci = (ni * lam_re - nr * lam_im) / den
    fre = cr[:, :, None] * b_re - ci[:, :, None] * b_im
    fim = cr[:, :, None] * b_im + ci[:, :, None] * b_re
    eye = jnp.eye(S5_GB, dtype=F32)

    def blk_in(f):
        f = f.reshape(S5_NB, S5_GB, S5_STATE, S5_GROUP_CH)
        m = jnp.einsum('jgph,gk->jghkp', f, eye)
        return m.reshape(S5_NB, S5_KB, S5_PB).astype(BF16)

    def blk_out(c):
        c = c.reshape(S5_NB, S5_GB, S5_GROUP_CH, S5_STATE)
        m = jnp.einsum('jghp,gk->jgpkh', c, eye)
        return m.reshape(S5_NB, S5_PB, S5_KB).astype(BF16)

    is_re = lax.broadcasted_iota(jnp.int32, (2, S5_CH), 0) == 0
    a = jnp.where(is_re, ar.reshape(1, -1), ai.reshape(1, -1))
    return blk_in(fre), blk_in(fim), blk_out(c_re), blk_out(-c_im), a


def _prep_body(qg_ref, kg_ref, vg_ref, qw_ref, kw_ref, cos_ref, sin_ref, qn_ref, kn_ref, oqg, okg, ovg, oqw, okw, *,
               qg_scale):
    cosf, sinf = cos_ref[...], sin_ref[...]
    lane = lax.broadcasted_iota(jnp.int32, cosf.shape, 1)
    low = (lane % AXIS_DIM) < (AXIS_DIM // 2)

    def rope(t):
        partner = jnp.where(low, pltpu.roll(t, HEAD_DIM - AXIS_DIM // 2, 1), pltpu.roll(t, AXIS_DIM // 2, 1))
        return t * cosf + partner * sinf

    def rms(t, g):
        return t * lax.rsqrt(jnp.mean(t * t, -1, keepdims=True) + EPS) * g

    for h in range(GA_HEADS):
        cs = slice(h * HEAD_DIM, (h + 1) * HEAD_DIM)
        oqg[:, cs] = (rope(rms(qg_ref[:, cs].astype(F32), qn_ref[...])) * qg_scale).astype(BF16)
        oqw[:, cs] = (rope(qw_ref[:, cs].astype(F32)) * ATTN_SCALE).astype(BF16)
    for h in range(GA_KV):
        cs = slice(h * HEAD_DIM, (h + 1) * HEAD_DIM)
        okg[:, cs] = rope(rms(kg_ref[:, cs].astype(F32), kn_ref[...])).astype(BF16)
        okw[:, cs] = rope(kw_ref[:, cs].astype(F32)).astype(BF16)
        ovg[:, 2 * h * HEAD_DIM:(2 * h + 1) * HEAD_DIM] = vg_ref[:, cs]
        ovg[:, (2 * h + 1) * HEAD_DIM:(2 * h + 2) * HEAD_DIM] = jnp.ones((vg_ref.shape[0], HEAD_DIM), BF16)


def _prep(proj, cosf, sinf, qn, kn, qg_scale):
    t = proj.shape[0]
    tr = min(256, t)
    qw_, kw_ = GA_HEADS * HEAD_DIM, GA_KV * HEAD_DIM
    row = lambda i: (i, 0)
    full = lambda i: (0, 0)
    return pl.pallas_call(
        functools.partial(_prep_body, qg_scale=qg_scale), grid=(t // tr,),
        in_specs=[pl.BlockSpec((tr, qw_), lambda i: (i, COL_QG // qw_)),
                  pl.BlockSpec((tr, kw_), lambda i: (i, COL_KG // kw_)),
                  pl.BlockSpec((tr, kw_), lambda i: (i, COL_VG // kw_)),
                  pl.BlockSpec((tr, qw_), lambda i: (i, COL_QW // qw_)),
                  pl.BlockSpec((tr, kw_), lambda i: (i, COL_KW // kw_)),
                  pl.BlockSpec((tr, HEAD_DIM), row), pl.BlockSpec((tr, HEAD_DIM), row),
                  pl.BlockSpec((1, HEAD_DIM), full), pl.BlockSpec((1, HEAD_DIM), full)],
        out_specs=[pl.BlockSpec((tr, qw_), row), pl.BlockSpec((tr, kw_), row), pl.BlockSpec((tr, 2 * kw_), row),
                   pl.BlockSpec((tr, qw_), row), pl.BlockSpec((tr, kw_), row)],
        out_shape=[jax.ShapeDtypeStruct((t, qw_), BF16), jax.ShapeDtypeStruct((t, kw_), BF16),
                   jax.ShapeDtypeStruct((t, 2 * kw_), BF16),
                   jax.ShapeDtypeStruct((t, qw_), BF16), jax.ShapeDtypeStruct((t, kw_), BF16)],
        compiler_params=_cp(1, 24), name="qk_prep")(proj, proj, proj, proj, proj, cosf, sinf, qn, kn)


def _rope_tables(n_lat):
    rows = n_lat // GRID_W
    row = jnp.repeat(jnp.arange(rows, dtype=F32), GRID_W)
    col = jnp.tile(jnp.arange(GRID_W, dtype=F32), rows)
    inv = jnp.power(ROPE_THETA, -jnp.arange(0, AXIS_DIM, 2, dtype=F32) / AXIS_DIM)
    ang_r, ang_c = row[:, None] * inv, col[:, None] * inv
    cr, sr, cc, sc = jnp.cos(ang_r), jnp.sin(ang_r), jnp.cos(ang_c), jnp.sin(ang_c)
    return jnp.concatenate([cr, cr, cc, cc], -1), jnp.concatenate([-sr, sr, -sc, sc], -1)


def _flash_body(q_ref, kc_ref, vc_ref, k_ref, v_ref, o_ref, qs, m_s, acc, *, tq, tk, nk):
    for g in range(KV_GROUP):
        qs[g * tq:(g + 1) * tq, :] = q_ref[:, g * HEAD_DIM:(g + 1) * HEAD_DIM]
    m_s[...] = jnp.full(m_s.shape, -jnp.inf, F32)
    acc[...] = jnp.zeros(acc.shape, F32)

    def update(k, v):
        for g in range(KV_GROUP):
            rs = slice(g * tq, (g + 1) * tq)
            s = lax.dot_general(qs[rs, :], k, _NT, preferred_element_type=F32)
            m_prev = m_s[rs, :]
            m_next = jnp.maximum(m_prev, jnp.max(s, axis=1, keepdims=True))
            alpha = jnp.exp2(m_prev - m_next)
            p = jnp.exp2(s - jnp.concatenate([m_next] * (s.shape[1] // HEAD_DIM), axis=1))
            acc[rs, :] = (jnp.concatenate([alpha, alpha], axis=1) * acc[rs, :]
                          + jnp.dot(p.astype(BF16), v, preferred_element_type=F32))
            m_s[rs, :] = m_next

    update(kc_ref[...], vc_ref[...])

    def body(j, carry):
        rows = pl.ds(pl.multiple_of(j * tk, tk), tk)
        update(k_ref[rows, :], v_ref[rows, :])
        return carry

    lax.fori_loop(0, nk, body, 0, unroll=FLASH_UNROLL)
    out = acc[:, :HEAD_DIM] / acc[:, HEAD_DIM:]
    for g in range(KV_GROUP):
        o_ref[:, g * HEAD_DIM:(g + 1) * HEAD_DIM] = out[g * tq:(g + 1) * tq, :].astype(o_ref.dtype)


FLASH_TK = 512
FLASH_UNROLL = 16


def _flash(q, kg, vge, kgc, vgce):
    t = q.shape[0]
    n_ctx = kgc.shape[0]
    tq = min(256, t)
    tk = min(FLASH_TK, t)
    gw = KV_GROUP * HEAD_DIM
    head = lambda h, i: (0, h)
    return pl.pallas_call(
        functools.partial(_flash_body, tq=tq, tk=tk, nk=t // tk), grid=(GA_KV, t // tq),
        in_specs=[pl.BlockSpec((tq, gw), lambda h, i: (i, h)),
                  pl.BlockSpec((n_ctx, HEAD_DIM), head), pl.BlockSpec((n_ctx, 2 * HEAD_DIM), head),
                  pl.BlockSpec((t, HEAD_DIM), head), pl.BlockSpec((t, 2 * HEAD_DIM), head)],
        out_specs=pl.BlockSpec((tq, gw), lambda h, i: (i, h)),
        out_shape=jax.ShapeDtypeStruct((t, GA_HEADS * HEAD_DIM), BF16),
        scratch_shapes=[pltpu.VMEM((KV_GROUP * tq, HEAD_DIM), BF16), pltpu.VMEM((KV_GROUP * tq, HEAD_DIM), F32),
                        pltpu.VMEM((KV_GROUP * tq, 2 * HEAD_DIM), F32)],
        compiler_params=_cp(2, 48), name="global_attn")(q, kgc, vgce, kg, vge)


WIN_QB = 4


def _win_body(q_ref, km_ref, k0_ref, kp_ref, vm_ref, v0_ref, vp_ref, kc_ref, vc_ref, sink_ref, o_ref, *, nb):
    kslab = jnp.concatenate([km_ref[...], k0_ref[...], kp_ref[...]], axis=0)
    vslab = jnp.concatenate([vm_ref[...], v0_ref[...], vp_ref[...]], axis=0)
    sink = sink_ref[0]
    shape = (KV_GROUP * BLOCK, 3 * BLOCK)
    r = lax.broadcasted_iota(jnp.int32, shape, 0) % BLOCK
    c = lax.broadcasted_iota(jnp.int32, shape, 1)
    cc = c % BLOCK
    for b in range(WIN_QB):
        i = pl.program_id(1) * WIN_QB + b
        rows = slice(b * BLOCK, (b + 1) * BLOCK)
        q = jnp.concatenate([q_ref[rows, g * HEAD_DIM:(g + 1) * HEAD_DIM] for g in range(KV_GROUP)], axis=0)
        kw, vw = kslab[b * BLOCK:(b + 3) * BLOCK], vslab[b * BLOCK:(b + 3) * BLOCK]
        s_w = lax.dot_general(q, kw, _NT, preferred_element_type=F32)
        iv = jnp.full(shape, i, jnp.int32)
        valid = (((c < BLOCK) & (cc >= r) & (iv >= 1)) | ((c >= BLOCK) & (c < 2 * BLOCK))
                 | ((c >= 2 * BLOCK) & (cc <= r) & (iv + 1 < nb)))
        s_w = jnp.where(valid, s_w, NEG_INF)
        s_c = lax.dot_general(q, kc_ref[...], _NT, preferred_element_type=F32)
        m = jnp.maximum(jnp.maximum(jnp.max(s_w, -1, keepdims=True), jnp.max(s_c, -1, keepdims=True)), sink)
        pw, pc = jnp.exp(s_w - m), jnp.exp(s_c - m)
        l = jnp.sum(pw, -1, keepdims=True) + jnp.sum(pc, -1, keepdims=True) + jnp.exp(sink - m)
        inv = 1.0 / l
        out = (jnp.dot((pc * inv).astype(BF16), vc_ref[...], preferred_element_type=F32)
               + jnp.dot((pw * inv).astype(BF16), vw, preferred_element_type=F32))
        for g in range(KV_GROUP):
            o_ref[rows, g * HEAD_DIM:(g + 1) * HEAD_DIM] = out[g * BLOCK:(g + 1) * BLOCK, :].astype(o_ref.dtype)


def _sink_rows(sink, rows_per_head):
    return jnp.repeat(sink.astype(F32).reshape(WA_KV, KV_GROUP), rows_per_head, axis=1)[:, :, None]


def _window(qw, kw, proj, kwc, proj_c, sink):
    t = qw.shape[0]
    n_ctx = kwc.shape[0]
    nb = t // BLOCK
    gw = KV_GROUP * HEAD_DIM
    vcol = COL_VW // HEAD_DIM
    assert nb % WIN_QB == 0
    edge = lambda f: pl.BlockSpec((BLOCK, HEAD_DIM), f)
    main = lambda f: pl.BlockSpec((WIN_QB * BLOCK, HEAD_DIM), f)
    before = lambda i: jnp.maximum(i * WIN_QB - 1, 0)
    after = lambda i: jnp.minimum((i + 1) * WIN_QB, nb - 1)
    return pl.pallas_call(
        functools.partial(_win_body, nb=nb), grid=(WA_KV, nb // WIN_QB),
        in_specs=[pl.BlockSpec((WIN_QB * BLOCK, gw), lambda h, i: (i, h)),
                  edge(lambda h, i: (before(i), h)), main(lambda h, i: (i, h)), edge(lambda h, i: (after(i), h)),
                  edge(lambda h, i: (before(i), vcol + h)), main(lambda h, i: (i, vcol + h)),
                  edge(lambda h, i: (after(i), vcol + h)),
                  pl.BlockSpec((n_ctx, HEAD_DIM), lambda h, i: (0, h)),
                  pl.BlockSpec((n_ctx, HEAD_DIM), lambda h, i: (0, vcol + h)),
                  pl.BlockSpec((1, KV_GROUP * BLOCK, 1), lambda h, i: (h, 0, 0))],
        out_specs=pl.BlockSpec((WIN_QB * BLOCK, gw), lambda h, i: (i, h)),
        out_shape=jax.ShapeDtypeStruct((t, WA_HEADS * HEAD_DIM), BF16),
        compiler_params=_cp(2, 24), name="window_attn")(
            qw, kw, kw, kw, proj, proj, proj, kwc, proj_c, _sink_rows(sink, BLOCK))


def _ctx_attn_body(q_ref, k_ref, v_ref, sink_ref, o_ref, *, n_ctx, use_sink):
    q = jnp.concatenate([q_ref[:, g * HEAD_DIM:(g + 1) * HEAD_DIM] for g in range(KV_GROUP)], axis=0)
    s = lax.dot_general(q, k_ref[...], _NT, preferred_element_type=F32)
    m = jnp.max(s, -1, keepdims=True)
    if use_sink:
        m = jnp.maximum(m, sink_ref[0])
    p = jnp.exp(s - m)
    l = jnp.sum(p, -1, keepdims=True)
    if use_sink:
        l = l + jnp.exp(sink_ref[0] - m)
    out = jnp.dot((p / l).astype(BF16), v_ref[...], preferred_element_type=F32)
    for g in range(KV_GROUP):
        o_ref[:, g * HEAD_DIM:(g + 1) * HEAD_DIM] = out[g * n_ctx:(g + 1) * n_ctx, :].astype(o_ref.dtype)


def _ctx_attn(q, k, proj_c, vcol0, sink, use_sink):
    n_ctx = q.shape[0]
    gw = KV_GROUP * HEAD_DIM
    vcol = vcol0 // HEAD_DIM
    return pl.pallas_call(
        functools.partial(_ctx_attn_body, n_ctx=n_ctx, use_sink=use_sink), grid=(GA_KV,),
        in_specs=[pl.BlockSpec((n_ctx, gw), lambda h: (0, h)),
                  pl.BlockSpec((n_ctx, HEAD_DIM), lambda h: (0, h)),
                  pl.BlockSpec((n_ctx, HEAD_DIM), lambda h: (0, vcol + h)),
                  pl.BlockSpec((1, KV_GROUP * n_ctx, 1), lambda h: (h, 0, 0))],
        out_specs=pl.BlockSpec((n_ctx, gw), lambda h: (0, h)),
        out_shape=jax.ShapeDtypeStruct((n_ctx, GA_HEADS * HEAD_DIM), BF16),
        compiler_params=_cp(1, 24), name="ctx_attn_sink" if use_sink else "ctx_attn")(
            q, k, proj_c, _sink_rows(sink, n_ctx))


def _sg_body(u_ref, v_ref, g_ref, b_ref, w_ref, bs_ref, o_ref, *, nch):
    for n in range(nch):
        rs = slice(n * CHUNK, (n + 1) * CHUNK)
        vn = _ln(v_ref[rs, :].astype(F32), g_ref[...], b_ref[...]).astype(BF16)
        for g in range(SG_GROUPS):
            cs = slice(g * SG_GROUP_CH, (g + 1) * SG_GROUP_CH)
            s = jnp.dot(w_ref[g], vn[:, cs], preferred_element_type=F32) + bs_ref[g]
            o_ref[rs, cs] = (u_ref[rs, cs].astype(F32) * s).astype(o_ref.dtype)


def _spatial_gate(proj, g_v, b_v, w_sp, b_sp):
    t = proj.shape[0]
    tm = min(512, t)
    full = lambda i: (0, 0)
    return pl.pallas_call(
        functools.partial(_sg_body, nch=tm // CHUNK), grid=(t // tm,),
        in_specs=[pl.BlockSpec((tm, SG_WIDTH), lambda i: (i, COL_DU // SG_WIDTH)),
                  pl.BlockSpec((tm, SG_WIDTH), lambda i: (i, COL_DV // SG_WIDTH)),
                  pl.BlockSpec((1, SG_WIDTH), full), pl.BlockSpec((1, SG_WIDTH), full),
                  pl.BlockSpec((SG_GROUPS, CHUNK, CHUNK), lambda i: (0, 0, 0)),
                  pl.BlockSpec((SG_GROUPS, CHUNK, 1), lambda i: (0, 0, 0))],
        out_specs=pl.BlockSpec((tm, SG_WIDTH), lambda i: (i, 0)),
        out_shape=jax.ShapeDtypeStruct((t, SG_WIDTH), BF16),
        compiler_params=_cp(1, 24), name="spatial_gate")(proj, proj, g_v, b_v, w_sp, b_sp)


def _merge_body(ya, yb, yc, yd, wa, wb, wc, wd, ga, gb, gc, gd, o_ref):
    def br(y, w, g):
        return jax.nn.sigmoid(g[...].astype(F32)) * jnp.dot(y[...], w[...], preferred_element_type=F32)
    o_ref[...] = (br(ya, wa, ga) + br(yb, wb, gb) + br(yc, wc, gc) + br(yd, wd, gd)).astype(o_ref.dtype)


def _merge(ys, ws, gates):
    t = ys[0].shape[0]
    tm, tn = min(512, t), 1024
    nn = D_MODEL // tn
    in_specs = [pl.BlockSpec((tm, y.shape[1]), lambda j, i: (i, 0)) for y in ys]
    in_specs += [pl.BlockSpec((w.shape[0], tn), lambda j, i: (0, j)) for w in ws]
    in_specs += [pl.BlockSpec((tm, tn), functools.partial(lambda j, i, b: (i, b * nn + j), b=b)) for b in range(N_BRANCH)]
    return pl.pallas_call(
        _merge_body, grid=(nn, t // tm), in_specs=in_specs,
        out_specs=pl.BlockSpec((tm, tn), lambda j, i: (i, j)),
        out_shape=jax.ShapeDtypeStruct((t, D_MODEL), BF16),
        compiler_params=_cp(2, 48), name="gated_merge")(*ys, *ws, gates, gates, gates, gates)


def _out_ln_body(mg_ref, w_ref, x_ref, g1_ref, lng_ref, lnb_ref, sc_ref, sh_ref, x1_ref, xmT_ref, *, alpha):
    o = jnp.dot(mg_ref[...], w_ref[...], preferred_element_type=F32)
    x1 = _ln(alpha * x_ref[...] + g1_ref[...] * o, lng_ref[...], lnb_ref[...])
    x1_ref[...] = x1
    xmT_ref[...] = (x1 * (1.0 + sc_ref[...]) + sh_ref[...]).T.astype(xmT_ref.dtype)


def _out_ln(merged, w_out, x, g1, lng, lnb, sc2, sh2, alpha):
    t, d = x.shape
    tm = min(256, t)
    row = lambda i: (i, 0)
    vec = pl.BlockSpec((1, d), lambda i: (0, 0))
    return pl.pallas_call(
        functools.partial(_out_ln_body, alpha=alpha), grid=(t // tm,),
        in_specs=[pl.BlockSpec((tm, d), row), pl.BlockSpec((d, d), lambda i: (0, 0)), pl.BlockSpec((tm, d), row),
                  vec, vec, vec, vec, vec],
        out_specs=[pl.BlockSpec((tm, d), row), pl.BlockSpec((d, tm), lambda i: (0, i))],
        out_shape=[jax.ShapeDtypeStruct((t, d), F32), jax.ShapeDtypeStruct((d, t), BF16)],
        compiler_params=_cp(1, 48), name="out_proj_ln")(merged, w_out, x, g1, lng, lnb, sc2, sh2)


def _peer_topk_body(q_ref, k_ref, s1_o, s2_o, s1l_o, s2l_o, tau_o):
    q = q_ref[...].astype(BF16)
    s1 = jnp.dot(k_ref[0], q[:PEER_HALF], preferred_element_type=F32)
    s2 = jnp.dot(k_ref[1], q[PEER_HALF:], preferred_element_type=F32)
    ninf = -jnp.inf

    def top_distinct(s):
        vals, tops, cnts = s, [], []
        for _ in range(PEER_TOPK):
            m = jnp.max(vals, axis=0, keepdims=True)
            eq = vals == m
            tops.append(m)
            cnts.append(jnp.sum(eq.astype(F32), axis=0, keepdims=True))
            vals = jnp.where(eq, ninf, vals)
        return jnp.concatenate(tops, 0), jnp.concatenate(cnts, 0)

    ta, na = top_distinct(s1)
    tb, nb = top_distinct(s2)
    nrow = [PEER_TOPK // (k + 1) for k in range(PEER_TOPK)]
    pad = -sum(nrow) % 8
    cand = jnp.concatenate([ta[k:k + 1] + tb[:nrow[k]] for k in range(PEER_TOPK)]
                           + [jnp.full((pad, ta.shape[1]), ninf, F32)], 0)
    mult = jnp.concatenate([na[k:k + 1] * nb[:nrow[k]] for k in range(PEER_TOPK)]
                           + [jnp.zeros((pad, ta.shape[1]), F32)], 0)
    vals = cand
    cnt = jnp.zeros_like(cand[0:1])
    tau = jnp.full_like(cand[0:1], ninf)
    for _ in range(PEER_TOPK):
        m = jnp.max(vals, axis=0, keepdims=True)
        eq = vals == m
        tau = jnp.where(cnt < PEER_TOPK, m, tau)
        cnt = cnt + jnp.sum(jnp.where(eq, mult, 0.0), axis=0, keepdims=True)
        vals = jnp.where(eq, ninf, vals)
    cmax = cand[0:1]
    z = jnp.sum(jnp.where(cand >= tau, mult * jnp.exp(cand - cmax), 0.0), axis=0, keepdims=True)
    g1 = jnp.exp(s1 - ta[0:1]) * (1.0 / z)
    g2 = jnp.exp(s2 - tb[0:1])
    for lc in range(s1.shape[1] // PEER_LANES):
        ls = slice(lc * PEER_LANES, (lc + 1) * PEER_LANES)
        s1_o[lc] = s1[:, ls]
        s2_o[lc] = s2[:, ls]
        s1l_o[lc] = g1[:, ls]
        s2l_o[lc] = g2[:, ls]
        tau_o[lc] = tau[:, ls]


PEER_LANES = 128
TOPK_TT = 1024


def _peer_topk(qT, keys):
    t = qT.shape[1]
    tt = min(TOPK_TT, t)
    nl = tt // PEER_LANES
    blk = pl.BlockSpec((None, nl, PEER_NKEYS, PEER_LANES), lambda j, h: (h, j, 0, 0))
    shp = jax.ShapeDtypeStruct((PEER_HEADS, t // PEER_LANES, PEER_NKEYS, PEER_LANES), F32)
    return pl.pallas_call(
        _peer_topk_body, grid=(t // tt, PEER_HEADS),
        in_specs=[pl.BlockSpec((PEER_QDIM, tt), lambda j, h: (h, j)),
                  pl.BlockSpec((None, 2, PEER_NKEYS, PEER_HALF), lambda j, h: (h, 0, 0, 0))],
        out_specs=[blk, blk, blk, blk, pl.BlockSpec((None, nl, 1, PEER_LANES), lambda j, h: (h, j, 0, 0))],
        out_shape=[shp, shp, shp, shp, jax.ShapeDtypeStruct((PEER_HEADS, t // PEER_LANES, 1, PEER_LANES), F32)],
        compiler_params=_cp(2, 32), name="peer_topk")(qT, keys)


PEER_EK = 512
PEER_SUBS = PEER_EK // PEER_NKEYS
PEER_PIECES = 8
PEER_KPIECES = 8


def _peer_dense_body(xT_ref, u_ref, vT_ref, s1_ref, s2_ref, s1l_ref, s2l_ref, tau_ref, o_ref, hbuf, awbuf, *, nk):
    g = pl.program_id(0)
    d = o_ref.shape[0]
    h0, h1, aw0, aw1 = hbuf.at[0], hbuf.at[1], awbuf.at[0], awbuf.at[1]

    @pl.when(g == 0)
    def _():
        hbuf[...] = jnp.zeros(hbuf.shape, F32)
        awbuf[...] = jnp.zeros(awbuf.shape, BF16)

    @pl.when((g <= 2) | ((g - 2) % nk == 0))
    def _():
        o_ref[...] = jnp.zeros(o_ref.shape, F32)

    def stages(h_cur, h_prev, aw_cur, aw_prev):
        base = (jnp.clip(g - 1, 0, pl.num_programs(0) - 3) % nk) * PEER_SUBS
        dq = d // PEER_PIECES

        def second_matmul(q):
            ds_ = slice(q * dq, (q + 1) * dq)
            o_ref[ds_, :] += jnp.dot(vT_ref[ds_, :], aw_cur[...], preferred_element_type=F32)

        kq = d // PEER_KPIECES

        def first_matmul(q):
            ks = slice(q * kq, (q + 1) * kq)
            part = jnp.dot(u_ref[:, ks], xT_ref[ks, :], preferred_element_type=F32)
            if q == 0:
                h_cur[...] = part
            else:
                h_cur[...] += part

        def gate_tile(ii, lc):
            i1 = pl.ds(base + ii, 1)
            w = None
            for h in range(PEER_HEADS):
                c = s2_ref[h, lc] + s1_ref[h, lc, i1, :]
                wh = jnp.where(c >= tau_ref[h, lc], s2l_ref[h, lc] * s1l_ref[h, lc, i1, :], 0.0)
                w = wh if w is None else w + wh
            rs = slice(ii * PEER_NKEYS, (ii + 1) * PEER_NKEYS)
            ls = slice(lc * PEER_LANES, (lc + 1) * PEER_LANES)
            aw_prev[rs, ls] = (_gelu(h_prev[rs, ls]) * w).astype(BF16)

        tiles = [(ii, lc) for ii in range(PEER_SUBS) for lc in range(o_ref.shape[1] // PEER_LANES)]
        per = -(-len(tiles) // (2 * PEER_PIECES))
        for q in range(PEER_PIECES):
            second_matmul(q)
            for tl in tiles[(2 * q) * per:(2 * q + 1) * per]:
                gate_tile(*tl)
            if q % (PEER_PIECES // PEER_KPIECES) == 0:
                first_matmul(q // (PEER_PIECES // PEER_KPIECES))
            for tl in tiles[(2 * q + 1) * per:(2 * q + 2) * per]:
                gate_tile(*tl)

    @pl.when(g % 2 == 0)
    def _():
        stages(h0, h1, aw0, aw1)

    @pl.when(g % 2 == 1)
    def _():
        stages(h1, h0, aw1, aw0)


def _peer_dense(xmT, u_bf, vT_bf, tk):
    d, t = xmT.shape
    tt = min(512, t)
    nl = tt // PEER_LANES
    nk = PEER_EXPERTS // PEER_EK
    s1, s2, s1l, s2l, tau = tk
    last = (t // tt) * nk - 1
    pair = lambda g, lag: jnp.clip(g - lag, 0, last)
    sblk = pl.BlockSpec((PEER_HEADS, nl, PEER_NKEYS, PEER_LANES), lambda g: (0, pair(g, 1) // nk, 0, 0))
    return pl.pallas_call(
        functools.partial(_peer_dense_body, nk=nk), grid=(last + 3,),
        in_specs=[pl.BlockSpec((d, tt), lambda g: (0, pair(g, 0) // nk)),
                  pl.BlockSpec((PEER_EK, d), lambda g: (pair(g, 0) % nk, 0)),
                  pl.BlockSpec((d, PEER_EK), lambda g: (0, pair(g, 2) % nk)),
                  sblk, sblk, sblk, sblk,
                  pl.BlockSpec((PEER_HEADS, nl, 1, PEER_LANES), lambda g: (0, pair(g, 1) // nk, 0, 0))],
        out_specs=pl.BlockSpec((d, tt), lambda g: (0, pair(g, 2) // nk)),
        out_shape=jax.ShapeDtypeStruct((d, t), F32),
        scratch_shapes=[pltpu.VMEM((2, PEER_EK, tt), F32), pltpu.VMEM((2, PEER_EK, tt), BF16)],
        compiler_params=_cp(1, 48), name="peer_dense")(xmT, u_bf, vT_bf, s1, s2, s1l, s2l, tau)


def _ffn_ln_body(fT_ref, x1_ref, g2_ref, lng_ref, lnb_ref, scn_ref, shn_ref, x2_ref, *rest, alpha):
    x2 = _ln(alpha * x1_ref[...] + g2_ref[...] * fT_ref[...].T, lng_ref[...], lnb_ref[...])
    x2_ref[...] = x2
    if rest:
        rest[0][...] = (x2 * (1.0 + scn_ref[...]) + shn_ref[...]).astype(rest[0].dtype)


def _ffn_ln(fT, x1, g2, lng, lnb, scn, shn, alpha, emit_h):
    t, d = x1.shape
    tm = min(256, t)
    row = pl.BlockSpec((tm, d), lambda i: (i, 0))
    vec = pl.BlockSpec((1, d), lambda i: (0, 0))
    out_specs = [row, row] if emit_h else [row]
    out_shape = [jax.ShapeDtypeStruct((t, d), F32)] + ([jax.ShapeDtypeStruct((t, d), BF16)] if emit_h else [])
    return pl.pallas_call(
        functools.partial(_ffn_ln_body, alpha=alpha), grid=(t // tm,),
        in_specs=[pl.BlockSpec((d, tm), lambda i: (0, i)), row, vec, vec, vec, vec, vec],
        out_specs=out_specs, out_shape=out_shape,
        compiler_params=_cp(1, 32), name="ffn_ln")(fT, x1, g2, lng, lnb, scn, shn)


def kernel(x, c, ctx, c_ctx, w_ada, b_ada, w_in, b_in, s5_lam_re, s5_lam_im, s5_log_dt, s5_b_re, s5_b_im, s5_c_re, s5_c_im, s5_d, w_glu, b_glu, qn_gain, kn_gain, sink, sg_ln_g, sg_ln_b, w_sp, b_sp, w_br_a, w_br_b, w_br_c, w_br_d, w_out, ln1_g, ln1_b, ln2_g, ln2_b, w_pq, peer_keys, peer_u, peer_v):
    depth = w_in.shape[0]
    bsz, n_lat, d = x.shape
    assert bsz == 1 and d == D_MODEL and n_lat % 512 == 0 and ctx.shape[1] % BLOCK == 0
    n_ctx = ctx.shape[1]
    alpha = (2 * depth) ** 0.25
    x, ctx = x[0], ctx[0]

    row8 = lax.broadcasted_iota(jnp.int32, (8, d), 0)
    cond8 = jnp.where(row8 == 0, c[0][None], jnp.where(row8 == 1, c_ctx[None], 0.0))
    ada = [_ada(cond8, w_ada, b_ada, l) for l in range(depth)]
    cosf, sinf = _rope_tables(n_lat)
    cos1, sin0 = jnp.ones((n_ctx, HEAD_DIM), F32), jnp.zeros((n_ctx, HEAD_DIM), F32)
    vec = lambda a: a.reshape(1, -1).astype(F32)

    ref_off = {'a': 0, 'qg': 768, 'kg': 1792, 'vg': 2048, 'qw': 2304, 'kw': 3328, 'vw': 3584, 'du': 3840, 'dv': 4608}
    ref_w = {'a': 768, 'qg': 1024, 'kg': 256, 'vg': 256, 'qw': 1024, 'kw': 256, 'vw': 256, 'du': 768, 'dv': 768}
    order = ['a', 'kg', 'qg', 'qw', 'kw', 'vg', 'vw', 'du', 'dv']
    perm = lambda a: jnp.concatenate([a[..., ref_off[n]:ref_off[n] + ref_w[n]] for n in order], -1)
    perm_idx = np.concatenate([np.arange(ref_off[n], ref_off[n] + ref_w[n]) for n in order]).astype(np.int32)

    h_lat = None
    for l in range(depth):
        need_ctx = l < depth - 1
        mods = [[ada[l][r:r + 1, i * d:(i + 1) * d] for i in range(6)] for r in range(2)]
        (sh1, sc1, g1, sh2, sc2, g2), (sh1c, sc1c, g1c, sh2c, sc2c, g2c) = mods
        if need_ctx:
            nxt = [ada[l + 1][r:r + 1, 0:2 * d] for r in range(2)]
            (shn, scn), (shnc, scnc) = [(m[:, :d], m[:, d:]) for m in nxt]
        else:
            shn = scn = shnc = scnc = jnp.zeros((1, d), F32)

        w_small, b_small = perm(w_in[l][:, :SMALL_W]).astype(BF16), vec(jnp.take(b_in[l], perm_idx))
        w_gate, b_gate = w_in[l][:, SMALL_W:].astype(BF16), vec(b_in[l][SMALL_W:])
        s5m = [_s5_mats(s5_lam_re[l, dr], s5_lam_im[l, dr], s5_log_dt[l, dr], s5_b_re[l, dr], s5_b_im[l, dr],
                        s5_c_re[l, dr], s5_c_im[l, dr]) for dr in range(2)]
        wg, bg, s5d = w_glu[l].astype(BF16), vec(b_glu[l]), vec(s5_d[l])
        qn, kn = vec(qn_gain[l]), vec(kn_gain[l])
        wsp, bsp = w_sp[l].astype(BF16), b_sp[l].astype(F32)[:, :, None]
        w_brs = [w[l].astype(BF16) for w in (w_br_a, w_br_b, w_br_c, w_br_d)]
        wo = w_out[l].astype(BF16)
        wpqT = w_pq[l].T.astype(BF16)
        keys = peer_keys[l].astype(BF16)
        u_bf, vT_bf = peer_u[l].astype(BF16), peer_v[l].T.astype(BF16)

        if h_lat is None:
            h_lat, h_ctx = _modulate(x, sc1, sh1), _modulate(ctx, sc1c, sh1c)

        proj = _matmul(h_lat, w_small, b_small, tm=512, tn=1792, out_dtype=BF16, name="in_proj")
        gates = _matmul(h_lat, w_gate, b_gate, tm=512, tn=2048, out_dtype=BF16, name="in_proj_gates")
        proj_c = _matmul(h_ctx, w_small, b_small, tm=512, tn=1792, out_dtype=BF16, name="in_proj_ctx")

        y_dir, y_dir_c = [], []
        for dr in range(2):
            yc_, hc_end = _s5_dir(proj_c, s5m[dr], jnp.zeros((2, S5_CH), F32), bool(dr))
            yl_, _ = _s5_dir(proj, s5m[dr], hc_end, bool(dr))
            y_dir.append(yl_)
            y_dir_c.append(yc_)
        y_a = _s5_out(proj, y_dir[0], y_dir[1], s5d, wg, bg)

        qg, kg, vge, qw, kw = _prep(proj, cosf, sinf, qn, kn, ATTN_SCALE * math.log2(math.e))
        qgc, kgc, vgce, qwc, kwc = _prep(proj_c, cos1, sin0, qn, kn, ATTN_SCALE)
        y_b = _flash(qg, kg, vge, kgc, vgce)
        y_c = _window(qw, kw, proj, kwc, proj_c, sink[l])

        y_d = _spatial_gate(proj, vec(sg_ln_g[l]), vec(sg_ln_b[l]), wsp, bsp)

        merged = _merge((y_a, y_b, y_c, y_d), w_brs, gates)
        x1, xmT = _out_ln(merged, wo, x, g1, vec(ln1_g[l]), vec(ln1_b[l]), sc2, sh2, alpha)
        qT = _matmul(wpqT, xmT, None, tm=D_MODEL, tn=1024, out_dtype=F32, name="peer_query")
        ffnT = _peer_dense(xmT, u_bf, vT_bf, _peer_topk(qT, keys))
        res = _ffn_ln(ffnT, x1, g2, vec(ln2_g[l]), vec(ln2_b[l]), scn, shn, alpha, need_ctx)
        x = res[0]

        if need_ctx:
            h_lat = res[1]
            gates_c = _matmul(h_ctx, w_gate, b_gate, tm=512, tn=2048, out_dtype=BF16, name="in_proj_gates_ctx")
            y_a_c = _s5_out(proj_c, y_dir_c[0], y_dir_c[1], s5d, wg, bg)
            y_b_c = _ctx_attn(qgc, kgc, proj_c, COL_VG, sink[l], False)
            y_c_c = _ctx_attn(qwc, kwc, proj_c, COL_VW, sink[l], True)
            y_d_c = _spatial_gate(proj_c, vec(sg_ln_g[l]), vec(sg_ln_b[l]), wsp, bsp)
            merged_c = _merge((y_a_c, y_b_c, y_c_c, y_d_c), w_brs, gates_c)
            c1, cmT = _out_ln(merged_c, wo, ctx, g1c, vec(ln1_g[l]), vec(ln1_b[l]), sc2c, sh2c, alpha)
            qTc = _matmul(wpqT, cmT, None, tm=D_MODEL, tn=1024, out_dtype=F32, name="peer_query_ctx")
            ffnTc = _peer_dense(cmT, u_bf, vT_bf, _peer_topk(qTc, keys))
            ctx, h_ctx = _ffn_ln(ffnTc, c1, g2c, vec(ln2_g[l]), vec(ln2_b[l]), scnc, shnc, alpha, True)
    return x[None]
```

```python
import functools
import math

import jax
import jax.numpy as jnp
import numpy as np
from jax import lax
from jax.experimental import pallas as pl
from jax.experimental.pallas import tpu as pltpu

F32, BF16 = jnp.float32, jnp.bfloat16

D_MODEL = 2048
GRID_W = 64
BLOCK = 128
HEAD_DIM = 128
AXIS_DIM = HEAD_DIM // 2
ROPE_THETA = 10000.0
ATTN_SCALE = HEAD_DIM ** -0.5
NEG_INF = -1e30
EPS = 1e-6
S5_GROUPS, S5_GROUP_CH, S5_STATE = 48, 16, 64
S5_WIDTH = S5_GROUPS * S5_GROUP_CH
S5_CH = S5_GROUPS * S5_STATE
S5_GB = 16
S5_NB = S5_GROUPS // S5_GB
S5_KB = S5_GB * S5_GROUP_CH
S5_PB = S5_GB * S5_STATE
GA_HEADS, GA_KV = 8, 2
WA_HEADS, WA_KV = 8, 2
KV_GROUP = GA_HEADS // GA_KV
SG_GROUPS, SG_GROUP_CH = 6, 128
SG_WIDTH = SG_GROUPS * SG_GROUP_CH
CHUNK = 128
N_BRANCH = 4
PEER_HEADS, PEER_NKEYS, PEER_QDIM, PEER_TOPK = 8, 128, 256, 16
PEER_HALF = PEER_QDIM // 2
PEER_EXPERTS = PEER_NKEYS * PEER_NKEYS

COL_A, COL_KG, COL_QG, COL_QW, COL_KW, COL_VG, COL_VW, COL_DU, COL_DV = 0, 768, 1024, 2048, 3072, 3328, 3584, 3840, 4608
SMALL_W = 5376
GATE_W = N_BRANCH * D_MODEL

VMEM_CAP_MB = 56


def _cp(ndims, vmem_mb=32):
    return pltpu.CompilerParams(dimension_semantics=("arbitrary",) * ndims,
                                vmem_limit_bytes=min(vmem_mb, VMEM_CAP_MB) << 20)


def _gelu(y):
    return 0.5 * y * (1.0 + lax.erf(y * (2.0 ** -0.5)))


def _ln(y, g, b):
    mu = jnp.mean(y, -1, keepdims=True)
    yc = y - mu
    var = jnp.mean(yc * yc, -1, keepdims=True)
    return yc * lax.rsqrt(var + EPS) * g + b


_NT = (((1,), (1,)), ((), ()))


def _ada_body(c_ref, w_ref, b_ref, o_ref):
    cnd = c_ref[...]
    s = (cnd * jax.nn.sigmoid(cnd)).astype(BF16)
    o_ref[...] = jnp.dot(s, w_ref[...].astype(BF16), preferred_element_type=F32) + b_ref[...]


def _ada(cond8, w_ada, b_ada, l):
    depth, d, n = w_ada.shape
    tn = 1024
    return pl.pallas_call(
        _ada_body, grid=(n // tn,),
        in_specs=[pl.BlockSpec((8, d), lambda j: (0, 0)),
                  pl.BlockSpec((None, d, tn), lambda j: (l, 0, j)),
                  pl.BlockSpec((None, 1, tn), lambda j: (l, 0, j))],
        out_specs=pl.BlockSpec((8, tn), lambda j: (0, j)),
        out_shape=jax.ShapeDtypeStruct((8, n), F32),
        compiler_params=_cp(1, 32), name="ada_mod")(cond8, w_ada, b_ada.reshape(depth, 1, n))


def _mod_body(x_ref, sc_ref, sh_ref, o_ref):
    o_ref[...] = (x_ref[...] * (1.0 + sc_ref[...]) + sh_ref[...]).astype(o_ref.dtype)


def _modulate(x, sc, sh):
    t, d = x.shape
    tm = min(512, t)
    return pl.pallas_call(
        _mod_body, grid=(t // tm,),
        in_specs=[pl.BlockSpec((tm, d), lambda i: (i, 0)),
                  pl.BlockSpec((1, d), lambda i: (0, 0)),
                  pl.BlockSpec((1, d), lambda i: (0, 0))],
        out_specs=pl.BlockSpec((tm, d), lambda i: (i, 0)),
        out_shape=jax.ShapeDtypeStruct((t, d), BF16),
        compiler_params=_cp(1, 24), name="modulate")(x, sc, sh)


def _mm_body(a_ref, w_ref, b_ref, o_ref):
    acc = jnp.dot(a_ref[...], w_ref[...], preferred_element_type=F32)
    o_ref[...] = (acc + b_ref[...]).astype(o_ref.dtype)


def _mm_nobias_body(a_ref, w_ref, o_ref):
    o_ref[...] = jnp.dot(a_ref[...], w_ref[...], preferred_element_type=F32).astype(o_ref.dtype)


def _matmul(a, w, b, *, tm, tn, out_dtype, name):
    m, k = a.shape
    n = w.shape[1]
    tm, tn = min(tm, m), min(tn, n)
    in_specs = [pl.BlockSpec((tm, k), lambda j, i: (i, 0)),
                pl.BlockSpec((k, tn), lambda j, i: (0, j))]
    args = [a, w]
    body = _mm_nobias_body
    if b is not None:
        in_specs.append(pl.BlockSpec((1, tn), lambda j, i: (0, j)))
        args.append(b)
        body = _mm_body
    osz = jnp.dtype(out_dtype).itemsize
    vmem = 2 * (tm * k * 2 + k * tn * 2 + tm * tn * osz) + tm * tn * 4
    return pl.pallas_call(
        body, grid=(n // tn, m // tm), in_specs=in_specs,
        out_specs=pl.BlockSpec((tm, tn), lambda j, i: (i, j)),
        out_shape=jax.ShapeDtypeStruct((m, n), out_dtype),
        compiler_params=_cp(2, (vmem >> 20) + 8), name=name)(*args)


S5_TC = 256
S5_SEG = 8
S5_L = S5_TC // S5_SEG


def _s5_body(u_ref, perm_ref, permT_ref, bre_ref, bim_ref, cre_ref, cim_ref, apow_ref, h0_ref, y_ref, hT_ref,
             xr, xi, carry, hin_r, hin_i, *, rev):
    @pl.when(pl.program_id(0) == 0)
    def _():
        carry[...] = h0_ref[...]

    up = jnp.dot(perm_ref[...], u_ref[...], preferred_element_type=F32).astype(BF16)
    for j in range(S5_NB):
        uj = up[:, j * S5_KB:(j + 1) * S5_KB]
        xr[:, j * S5_PB:(j + 1) * S5_PB] = jnp.dot(uj, bre_ref[j], preferred_element_type=F32)
        xi[:, j * S5_PB:(j + 1) * S5_PB] = jnp.dot(uj, bim_ref[j], preferred_element_type=F32)

    seg_order = range(S5_SEG - 1, -1, -1) if rev else range(S5_SEG)
    for j in range(S5_NB):
        cs = slice(j * S5_PB, (j + 1) * S5_PB)
        ar = jnp.broadcast_to(apow_ref[0, 0:1, cs], (S5_SEG, S5_PB))
        ai = jnp.broadcast_to(apow_ref[0, 1:2, cs], (S5_SEG, S5_PB))

        def local_step(k, c, cs=cs, ar=ar, ai=ai):
            hr, hi = c
            rows = pl.ds(pl.multiple_of(k * S5_SEG, S5_SEG), S5_SEG)
            nr = ar * hr - ai * hi + xr[rows, cs]
            ni = ar * hi + ai * hr + xi[rows, cs]
            xr[rows, cs] = nr
            xi[rows, cs] = ni
            return nr, ni

        zero = jnp.zeros((S5_SEG, S5_PB), F32)
        er, ei = lax.fori_loop(0, S5_L, local_step, (zero, zero), unroll=4)

        alr, ali = apow_ref[S5_L - 1, 0:1, cs], apow_ref[S5_L - 1, 1:2, cs]
        hr, hi = carry[0:1, cs], carry[1:2, cs]
        for s in seg_order:
            hin_r[s:s + 1, cs] = hr
            hin_i[s:s + 1, cs] = hi
            hr, hi = (er[s:s + 1] + alr * hr - ali * hi, ei[s:s + 1] + alr * hi + ali * hr)
        carry[0:1, cs] = hr
        carry[1:2, cs] = hi

        def fix_step(k, c, cs=cs):
            rows = pl.ds(pl.multiple_of(k * S5_SEG, S5_SEG), S5_SEG)
            pr, pi = apow_ref[k, 0:1, cs], apow_ref[k, 1:2, cs]
            h_r, h_i = hin_r[:, cs], hin_i[:, cs]
            xr[rows, cs] += pr * h_r - pi * h_i
            xi[rows, cs] += pr * h_i + pi * h_r
            return c

        lax.fori_loop(0, S5_L, fix_step, 0, unroll=4)

    yp = jnp.concatenate(
        [jnp.dot(xr[:, j * S5_PB:(j + 1) * S5_PB].astype(BF16), cre_ref[j], preferred_element_type=F32)
         + jnp.dot(xi[:, j * S5_PB:(j + 1) * S5_PB].astype(BF16), cim_ref[j], preferred_element_type=F32)
         for j in range(S5_NB)], axis=1)
    hi = yp.astype(BF16)
    r1 = yp - hi.astype(F32)
    mid = r1.astype(BF16)
    lo = (r1 - mid.astype(F32)).astype(BF16)
    parts = jnp.dot(permT_ref[...], jnp.concatenate([hi, mid, lo], axis=1), preferred_element_type=F32)
    y_ref[...] = parts[:, :S5_WIDTH] + parts[:, S5_WIDTH:2 * S5_WIDTH] + parts[:, 2 * S5_WIDTH:]
    hT_ref[...] = carry[...]


def _s5_perm(rev):
    p = np.zeros((S5_TC, S5_TC), np.float32)
    for k in range(S5_L):
        for s in range(S5_SEG):
            p[k * S5_SEG + s, s * S5_L + ((S5_L - 1 - k) if rev else k)] = 1.0
    return jnp.asarray(p, BF16), jnp.asarray(p.T, BF16)


def _s5_dir(proj, mats, h0, rev):
    t = proj.shape[0]
    assert t % S5_TC == 0
    nt = t // S5_TC
    bre, bim, cre, cim, apow = mats
    tmap = (lambda i: (nt - 1 - i, 0)) if rev else (lambda i: (i, 0))
    full3 = lambda i: (0, 0, 0)
    full2 = lambda i: (0, 0)
    return pl.pallas_call(
        functools.partial(_s5_body, rev=rev), grid=(nt,),
        in_specs=[pl.BlockSpec((S5_TC, S5_WIDTH), tmap),
                  pl.BlockSpec((S5_TC, S5_TC), full2), pl.BlockSpec((S5_TC, S5_TC), full2),
                  pl.BlockSpec((S5_NB, S5_KB, S5_PB), full3), pl.BlockSpec((S5_NB, S5_KB, S5_PB), full3),
                  pl.BlockSpec((S5_NB, S5_PB, S5_KB), full3), pl.BlockSpec((S5_NB, S5_PB, S5_KB), full3),
                  pl.BlockSpec((S5_L, 2, S5_CH), full3), pl.BlockSpec((2, S5_CH), full2)],
        out_specs=[pl.BlockSpec((S5_TC, S5_WIDTH), tmap), pl.BlockSpec((2, S5_CH), full2)],
        out_shape=[jax.ShapeDtypeStruct((t, S5_WIDTH), F32), jax.ShapeDtypeStruct((2, S5_CH), F32)],
        scratch_shapes=[pltpu.VMEM((S5_TC, S5_CH), F32), pltpu.VMEM((S5_TC, S5_CH), F32), pltpu.VMEM((2, S5_CH), F32),
                        pltpu.VMEM((S5_SEG, S5_CH), F32), pltpu.VMEM((S5_SEG, S5_CH), F32)],
        compiler_params=_cp(1, 40), name="s5_scan_bwd" if rev else "s5_scan_fwd")(
            proj, *_s5_perm(rev), bre, bim, cre, cim, apow, h0)


def _s5_out_body(u_ref, yf_ref, yb_ref, d_ref, wg_ref, bg_ref, o_ref):
    y = u_ref[...].astype(F32) * d_ref[...] + yf_ref[...] + yb_ref[...]
    g = _gelu(y)
    z = jnp.dot(g.astype(BF16), wg_ref[...], preferred_element_type=F32) + bg_ref[...]
    o_ref[...] = (g * jax.nn.sigmoid(z)).astype(o_ref.dtype)


def _s5_out(proj, yf, yb, d, wg, bg):
    t = proj.shape[0]
    tm = min(512, t)
    row = lambda i: (i, 0)
    full = lambda i: (0, 0)
    return pl.pallas_call(
        _s5_out_body, grid=(t // tm,),
        in_specs=[pl.BlockSpec((tm, S5_WIDTH), row), pl.BlockSpec((tm, S5_WIDTH), row), pl.BlockSpec((tm, S5_WIDTH), row),
                  pl.BlockSpec((1, S5_WIDTH), full), pl.BlockSpec((S5_WIDTH, S5_WIDTH), full), pl.BlockSpec((1, S5_WIDTH), full)],
        out_specs=pl.BlockSpec((tm, S5_WIDTH), row),
        out_shape=jax.ShapeDtypeStruct((t, S5_WIDTH), BF16),
        compiler_params=_cp(1, 24), name="s5_glu")(proj, yf, yb, d, wg, bg)


def _s5_mats(lam_re, lam_im, log_dt, b_re, b_im, c_re, c_im):
    dt = jnp.exp(log_dt)[:, None]
    mag = jnp.exp(lam_re * dt)
    ar, ai = mag * jnp.cos(lam_im * dt), mag * jnp.sin(lam_im * dt)
    nr, ni = ar - 1.0, ai
    den = lam_re * lam_re + lam_im * lam_im
    cr = (nr * lam_re + ni * lam_im) / den
    ci = (ni * lam_re - nr * lam_im) / den
    fre = cr[:, :, None] * b_re - ci[:, :, None] * b_im
    fim = cr[:, :, None] * b_im + ci[:, :, None] * b_re
    eye = jnp.eye(S5_GB, dtype=F32)

    def blk_in(f):
        f = f.reshape(S5_NB, S5_GB, S5_STATE, S5_GROUP_CH)
        m = jnp.einsum('jgph,gk->jghkp', f, eye)
        return m.reshape(S5_NB, S5_KB, S5_PB).astype(BF16)

    def blk_out(c):
        c = c.reshape(S5_NB, S5_GB, S5_GROUP_CH, S5_STATE)
        m = jnp.einsum('jghp,gk->jgpkh', c, eye)
        return m.reshape(S5_NB, S5_PB, S5_KB).astype(BF16)

    kk = jnp.arange(1, S5_L + 1, dtype=F32)[:, None, None]
    magk, angk = jnp.exp(lam_re * dt * kk), lam_im * dt * kk
    is_re = lax.broadcasted_iota(jnp.int32, (S5_L, 2, S5_CH), 1) == 0
    apow = jnp.where(is_re, (magk * jnp.cos(angk)).reshape(S5_L, 1, S5_CH), (magk * jnp.sin(angk)).reshape(S5_L, 1, S5_CH))
    return blk_in(fre), blk_in(fim), blk_out(c_re), blk_out(-c_im), apow


def _prep_body(qg_ref, kg_ref, vg_ref, qw_ref, kw_ref, cos_ref, sin_ref, qn_ref, kn_ref, oqg, okg, ovg, oqw, okw, *,
               qg_scale):
    cosf, sinf = cos_ref[...], sin_ref[...]
    lane = lax.broadcasted_iota(jnp.int32, cosf.shape, 1)
    low = (lane % AXIS_DIM) < (AXIS_DIM // 2)

    def rope(t):
        partner = jnp.where(low, pltpu.roll(t, HEAD_DIM - AXIS_DIM // 2, 1), pltpu.roll(t, AXIS_DIM // 2, 1))
        return t * cosf + partner * sinf

    def rms(t, g):
        return t * lax.rsqrt(jnp.mean(t * t, -1, keepdims=True) + EPS) * g

    for h in range(GA_HEADS):
        cs = slice(h * HEAD_DIM, (h + 1) * HEAD_DIM)
        oqg[:, cs] = (rope(rms(qg_ref[:, cs].astype(F32), qn_ref[...])) * qg_scale).astype(BF16)
        oqw[:, cs] = (rope(qw_ref[:, cs].astype(F32)) * ATTN_SCALE).astype(BF16)
    for h in range(GA_KV):
        cs = slice(h * HEAD_DIM, (h + 1) * HEAD_DIM)
        okg[:, cs] = rope(rms(kg_ref[:, cs].astype(F32), kn_ref[...])).astype(BF16)
        okw[:, cs] = rope(kw_ref[:, cs].astype(F32)).astype(BF16)
        ovg[:, 2 * h * HEAD_DIM:(2 * h + 1) * HEAD_DIM] = vg_ref[:, cs]
        ovg[:, (2 * h + 1) * HEAD_DIM:(2 * h + 2) * HEAD_DIM] = jnp.ones((vg_ref.shape[0], HEAD_DIM), BF16)


def _prep(proj, cosf, sinf, qn, kn, qg_scale):
    t = proj.shape[0]
    tr = min(256, t)
    qw_, kw_ = GA_HEADS * HEAD_DIM, GA_KV * HEAD_DIM
    row = lambda i: (i, 0)
    full = lambda i: (0, 0)
    return pl.pallas_call(
        functools.partial(_prep_body, qg_scale=qg_scale), grid=(t // tr,),
        in_specs=[pl.BlockSpec((tr, qw_), lambda i: (i, COL_QG // qw_)),
                  pl.BlockSpec((tr, kw_), lambda i: (i, COL_KG // kw_)),
                  pl.BlockSpec((tr, kw_), lambda i: (i, COL_VG // kw_)),
                  pl.BlockSpec((tr, qw_), lambda i: (i, COL_QW // qw_)),
                  pl.BlockSpec((tr, kw_), lambda i: (i, COL_KW // kw_)),
                  pl.BlockSpec((tr, HEAD_DIM), row), pl.BlockSpec((tr, HEAD_DIM), row),
                  pl.BlockSpec((1, HEAD_DIM), full), pl.BlockSpec((1, HEAD_DIM), full)],
        out_specs=[pl.BlockSpec((tr, qw_), row), pl.BlockSpec((tr, kw_), row), pl.BlockSpec((tr, 2 * kw_), row),
                   pl.BlockSpec((tr, qw_), row), pl.BlockSpec((tr, kw_), row)],
        out_shape=[jax.ShapeDtypeStruct((t, qw_), BF16), jax.ShapeDtypeStruct((t, kw_), BF16),
                   jax.ShapeDtypeStruct((t, 2 * kw_), BF16),
                   jax.ShapeDtypeStruct((t, qw_), BF16), jax.ShapeDtypeStruct((t, kw_), BF16)],
        compiler_params=_cp(1, 24), name="qk_prep")(proj, proj, proj, proj, proj, cosf, sinf, qn, kn)


def _rope_tables(n_lat):
    rows = n_lat // GRID_W
    row = jnp.repeat(jnp.arange(rows, dtype=F32), GRID_W)
    col = jnp.tile(jnp.arange(GRID_W, dtype=F32), rows)
    inv = jnp.power(ROPE_THETA, -jnp.arange(0, AXIS_DIM, 2, dtype=F32) / AXIS_DIM)
    ang_r, ang_c = row[:, None] * inv, col[:, None] * inv
    cr, sr, cc, sc = jnp.cos(ang_r), jnp.sin(ang_r), jnp.cos(ang_c), jnp.sin(ang_c)
    return jnp.concatenate([cr, cr, cc, cc], -1), jnp.concatenate([-sr, sr, -sc, sc], -1)


def _flash_body(q_ref, kc_ref, vc_ref, k_ref, v_ref, o_ref, qs, m_s, acc, *, tq, tk, nk):
    for g in range(KV_GROUP):
        qs[g * tq:(g + 1) * tq, :] = q_ref[:, g * HEAD_DIM:(g + 1) * HEAD_DIM]
    m_s[...] = jnp.full(m_s.shape, -jnp.inf, F32)
    acc[...] = jnp.zeros(acc.shape, F32)

    def update(k, v):
        for g in range(KV_GROUP):
            rs = slice(g * tq, (g + 1) * tq)
            s = lax.dot_general(qs[rs, :], k, _NT, preferred_element_type=F32)
            m_prev = m_s[rs, :]
            m_next = jnp.maximum(m_prev, jnp.max(s, axis=1, keepdims=True))
            alpha = jnp.exp2(m_prev - m_next)
            p = jnp.exp2(s - jnp.concatenate([m_next] * (s.shape[1] // HEAD_DIM), axis=1))
            acc[rs, :] = (jnp.concatenate([alpha, alpha], axis=1) * acc[rs, :]
                          + jnp.dot(p.astype(BF16), v, preferred_element_type=F32))
            m_s[rs, :] = m_next

    update(kc_ref[...], vc_ref[...])

    def body(j, carry):
        rows = pl.ds(pl.multiple_of(j * tk, tk), tk)
        update(k_ref[rows, :], v_ref[rows, :])
        return carry

    lax.fori_loop(0, nk, body, 0, unroll=FLASH_UNROLL)
    out = acc[:, :HEAD_DIM] / acc[:, HEAD_DIM:]
    for g in range(KV_GROUP):
        o_ref[:, g * HEAD_DIM:(g + 1) * HEAD_DIM] = out[g * tq:(g + 1) * tq, :].astype(o_ref.dtype)


FLASH_TK = 512
FLASH_UNROLL = 16


def _flash(q, kg, vge, kgc, vgce):
    t = q.shape[0]
    n_ctx = kgc.shape[0]
    tq = min(256, t)
    tk = min(FLASH_TK, t)
    gw = KV_GROUP * HEAD_DIM
    head = lambda h, i: (0, h)
    return pl.pallas_call(
        functools.partial(_flash_body, tq=tq, tk=tk, nk=t // tk), grid=(GA_KV, t // tq),
        in_specs=[pl.BlockSpec((tq, gw), lambda h, i: (i, h)),
                  pl.BlockSpec((n_ctx, HEAD_DIM), head), pl.BlockSpec((n_ctx, 2 * HEAD_DIM), head),
                  pl.BlockSpec((t, HEAD_DIM), head), pl.BlockSpec((t, 2 * HEAD_DIM), head)],
        out_specs=pl.BlockSpec((tq, gw), lambda h, i: (i, h)),
        out_shape=jax.ShapeDtypeStruct((t, GA_HEADS * HEAD_DIM), BF16),
        scratch_shapes=[pltpu.VMEM((KV_GROUP * tq, HEAD_DIM), BF16), pltpu.VMEM((KV_GROUP * tq, HEAD_DIM), F32),
                        pltpu.VMEM((KV_GROUP * tq, 2 * HEAD_DIM), F32)],
        compiler_params=_cp(2, 48), name="global_attn")(q, kgc, vgce, kg, vge)


WIN_QB = 4


def _win_body(q_ref, km_ref, k0_ref, kp_ref, vm_ref, v0_ref, vp_ref, kc_ref, vc_ref, sink_ref, o_ref, *, nb):
    kslab = jnp.concatenate([km_ref[...], k0_ref[...], kp_ref[...]], axis=0)
    vslab = jnp.concatenate([vm_ref[...], v0_ref[...], vp_ref[...]], axis=0)
    sink = sink_ref[0]
    shape = (KV_GROUP * BLOCK, 3 * BLOCK)
    r = lax.broadcasted_iota(jnp.int32, shape, 0) % BLOCK
    c = lax.broadcasted_iota(jnp.int32, shape, 1)
    cc = c % BLOCK
    for b in range(WIN_QB):
        i = pl.program_id(1) * WIN_QB + b
        rows = slice(b * BLOCK, (b + 1) * BLOCK)
        q = jnp.concatenate([q_ref[rows, g * HEAD_DIM:(g + 1) * HEAD_DIM] for g in range(KV_GROUP)], axis=0)
        kw, vw = kslab[b * BLOCK:(b + 3) * BLOCK], vslab[b * BLOCK:(b + 3) * BLOCK]
        s_w = lax.dot_general(q, kw, _NT, preferred_element_type=F32)
        iv = jnp.full(shape, i, jnp.int32)
        valid = (((c < BLOCK) & (cc >= r) & (iv >= 1)) | ((c >= BLOCK) & (c < 2 * BLOCK))
                 | ((c >= 2 * BLOCK) & (cc <= r) & (iv + 1 < nb)))
        s_w = jnp.where(valid, s_w, NEG_INF)
        s_c = lax.dot_general(q, kc_ref[...], _NT, preferred_element_type=F32)
        m = jnp.maximum(jnp.maximum(jnp.max(s_w, -1, keepdims=True), jnp.max(s_c, -1, keepdims=True)), sink)
        pw, pc = jnp.exp(s_w - m), jnp.exp(s_c - m)
        l = jnp.sum(pw, -1, keepdims=True) + jnp.sum(pc, -1, keepdims=True) + jnp.exp(sink - m)
        inv = 1.0 / l
        out = (jnp.dot((pc * inv).astype(BF16), vc_ref[...], preferred_element_type=F32)
               + jnp.dot((pw * inv).astype(BF16), vw, preferred_element_type=F32))
        for g in range(KV_GROUP):
            o_ref[rows, g * HEAD_DIM:(g + 1) * HEAD_DIM] = out[g * BLOCK:(g + 1) * BLOCK, :].astype(o_ref.dtype)


def _sink_rows(sink, rows_per_head):
    return jnp.repeat(sink.astype(F32).reshape(WA_KV, KV_GROUP), rows_per_head, axis=1)[:, :, None]


def _window(qw, kw, proj, kwc, proj_c, sink):
    t = qw.shape[0]
    n_ctx = kwc.shape[0]
    nb = t // BLOCK
    gw = KV_GROUP * HEAD_DIM
    vcol = COL_VW // HEAD_DIM
    assert nb % WIN_QB == 0
    edge = lambda f: pl.BlockSpec((BLOCK, HEAD_DIM), f)
    main = lambda f: pl.BlockSpec((WIN_QB * BLOCK, HEAD_DIM), f)
    before = lambda i: jnp.maximum(i * WIN_QB - 1, 0)
    after = lambda i: jnp.minimum((i + 1) * WIN_QB, nb - 1)
    return pl.pallas_call(
        functools.partial(_win_body, nb=nb), grid=(WA_KV, nb // WIN_QB),
        in_specs=[pl.BlockSpec((WIN_QB * BLOCK, gw), lambda h, i: (i, h)),
                  edge(lambda h, i: (before(i), h)), main(lambda h, i: (i, h)), edge(lambda h, i: (after(i), h)),
                  edge(lambda h, i: (before(i), vcol + h)), main(lambda h, i: (i, vcol + h)),
                  edge(lambda h, i: (after(i), vcol + h)),
                  pl.BlockSpec((n_ctx, HEAD_DIM), lambda h, i: (0, h)),
                  pl.BlockSpec((n_ctx, HEAD_DIM), lambda h, i: (0, vcol + h)),
                  pl.BlockSpec((1, KV_GROUP * BLOCK, 1), lambda h, i: (h, 0, 0))],
        out_specs=pl.BlockSpec((WIN_QB * BLOCK, gw), lambda h, i: (i, h)),
        out_shape=jax.ShapeDtypeStruct((t, WA_HEADS * HEAD_DIM), BF16),
        compiler_params=_cp(2, 24), name="window_attn")(
            qw, kw, kw, kw, proj, proj, proj, kwc, proj_c, _sink_rows(sink, BLOCK))


def _ctx_attn_body(q_ref, k_ref, v_ref, sink_ref, o_ref, *, n_ctx, use_sink):
    q = jnp.concatenate([q_ref[:, g * HEAD_DIM:(g + 1) * HEAD_DIM] for g in range(KV_GROUP)], axis=0)
    s = lax.dot_general(q, k_ref[...], _NT, preferred_element_type=F32)
    m = jnp.max(s, -1, keepdims=True)
    if use_sink:
        m = jnp.maximum(m, sink_ref[0])
    p = jnp.exp(s - m)
    l = jnp.sum(p, -1, keepdims=True)
    if use_sink:
        l = l + jnp.exp(sink_ref[0] - m)
    out = jnp.dot((p / l).astype(BF16), v_ref[...], preferred_element_type=F32)
    for g in range(KV_GROUP):
        o_ref[:, g * HEAD_DIM:(g + 1) * HEAD_DIM] = out[g * n_ctx:(g + 1) * n_ctx, :].astype(o_ref.dtype)


def _ctx_attn(q, k, proj_c, vcol0, sink, use_sink):
    n_ctx = q.shape[0]
    gw = KV_GROUP * HEAD_DIM
    vcol = vcol0 // HEAD_DIM
    return pl.pallas_call(
        functools.partial(_ctx_attn_body, n_ctx=n_ctx, use_sink=use_sink), grid=(GA_KV,),
        in_specs=[pl.BlockSpec((n_ctx, gw), lambda h: (0, h)),
                  pl.BlockSpec((n_ctx, HEAD_DIM), lambda h: (0, h)),
                  pl.BlockSpec((n_ctx, HEAD_DIM), lambda h: (0, vcol + h)),
                  pl.BlockSpec((1, KV_GROUP * n_ctx, 1), lambda h: (h, 0, 0))],
        out_specs=pl.BlockSpec((n_ctx, gw), lambda h: (0, h)),
        out_shape=jax.ShapeDtypeStruct((n_ctx, GA_HEADS * HEAD_DIM), BF16),
        compiler_params=_cp(1, 24), name="ctx_attn_sink" if use_sink else "ctx_attn")(
            q, k, proj_c, _sink_rows(sink, n_ctx))


def _sg_body(u_ref, v_ref, g_ref, b_ref, w_ref, bs_ref, o_ref, *, nch):
    for n in range(nch):
        rs = slice(n * CHUNK, (n + 1) * CHUNK)
        vn = _ln(v_ref[rs, :].astype(F32), g_ref[...], b_ref[...]).astype(BF16)
        for g in range(SG_GROUPS):
            cs = slice(g * SG_GROUP_CH, (g + 1) * SG_GROUP_CH)
            s = jnp.dot(w_ref[g], vn[:, cs], preferred_element_type=F32) + bs_ref[g]
            o_ref[rs, cs] = (u_ref[rs, cs].astype(F32) * s).astype(o_ref.dtype)


def _spatial_gate(proj, g_v, b_v, w_sp, b_sp):
    t = proj.shape[0]
    tm = min(512, t)
    full = lambda i: (0, 0)
    return pl.pallas_call(
        functools.partial(_sg_body, nch=tm // CHUNK), grid=(t // tm,),
        in_specs=[pl.BlockSpec((tm, SG_WIDTH), lambda i: (i, COL_DU // SG_WIDTH)),
                  pl.BlockSpec((tm, SG_WIDTH), lambda i: (i, COL_DV // SG_WIDTH)),
                  pl.BlockSpec((1, SG_WIDTH), full), pl.BlockSpec((1, SG_WIDTH), full),
                  pl.BlockSpec((SG_GROUPS, CHUNK, CHUNK), lambda i: (0, 0, 0)),
                  pl.BlockSpec((SG_GROUPS, CHUNK, 1), lambda i: (0, 0, 0))],
        out_specs=pl.BlockSpec((tm, SG_WIDTH), lambda i: (i, 0)),
        out_shape=jax.ShapeDtypeStruct((t, SG_WIDTH), BF16),
        compiler_params=_cp(1, 24), name="spatial_gate")(proj, proj, g_v, b_v, w_sp, b_sp)


def _merge_body(ya, yb, yc, yd, wa, wb, wc, wd, ga, gb, gc, gd, o_ref):
    def br(y, w, g):
        return jax.nn.sigmoid(g[...].astype(F32)) * jnp.dot(y[...], w[...], preferred_element_type=F32)
    o_ref[...] = (br(ya, wa, ga) + br(yb, wb, gb) + br(yc, wc, gc) + br(yd, wd, gd)).astype(o_ref.dtype)


def _merge(ys, ws, gates):
    t = ys[0].shape[0]
    tm, tn = min(512, t), 1024
    nn = D_MODEL // tn
    in_specs = [pl.BlockSpec((tm, y.shape[1]), lambda j, i: (i, 0)) for y in ys]
    in_specs += [pl.BlockSpec((w.shape[0], tn), lambda j, i: (0, j)) for w in ws]
    in_specs += [pl.BlockSpec((tm, tn), functools.partial(lambda j, i, b: (i, b * nn + j), b=b)) for b in range(N_BRANCH)]
    return pl.pallas_call(
        _merge_body, grid=(nn, t // tm), in_specs=in_specs,
        out_specs=pl.BlockSpec((tm, tn), lambda j, i: (i, j)),
        out_shape=jax.ShapeDtypeStruct((t, D_MODEL), BF16),
        compiler_params=_cp(2, 48), name="gated_merge")(*ys, *ws, gates, gates, gates, gates)


def _out_ln_body(mg_ref, w_ref, x_ref, g1_ref, lng_ref, lnb_ref, sc_ref, sh_ref, x1_ref, xmT_ref, *, alpha):
    o = jnp.dot(mg_ref[...], w_ref[...], preferred_element_type=F32)
    x1 = _ln(alpha * x_ref[...] + g1_ref[...] * o, lng_ref[...], lnb_ref[...])
    x1_ref[...] = x1
    xmT_ref[...] = (x1 * (1.0 + sc_ref[...]) + sh_ref[...]).T.astype(xmT_ref.dtype)


def _out_ln(merged, w_out, x, g1, lng, lnb, sc2, sh2, alpha):
    t, d = x.shape
    tm = min(256, t)
    row = lambda i: (i, 0)
    vec = pl.BlockSpec((1, d), lambda i: (0, 0))
    return pl.pallas_call(
        functools.partial(_out_ln_body, alpha=alpha), grid=(t // tm,),
        in_specs=[pl.BlockSpec((tm, d), row), pl.BlockSpec((d, d), lambda i: (0, 0)), pl.BlockSpec((tm, d), row),
                  vec, vec, vec, vec, vec],
        out_specs=[pl.BlockSpec((tm, d), row), pl.BlockSpec((d, tm), lambda i: (0, i))],
        out_shape=[jax.ShapeDtypeStruct((t, d), F32), jax.ShapeDtypeStruct((d, t), BF16)],
        compiler_params=_cp(1, 48), name="out_proj_ln")(merged, w_out, x, g1, lng, lnb, sc2, sh2)


def _peer_topk_body(q_ref, k_ref, s1_o, s2_o, s1l_o, s2l_o, tau_o):
    q = q_ref[...].astype(BF16)
    s1 = jnp.dot(k_ref[0], q[:PEER_HALF], preferred_element_type=F32)
    s2 = jnp.dot(k_ref[1], q[PEER_HALF:], preferred_element_type=F32)
    ninf = -jnp.inf

    def top_distinct(s):
        vals, tops, cnts = s, [], []
        for _ in range(PEER_TOPK):
            m = jnp.max(vals, axis=0, keepdims=True)
            eq = vals == m
            tops.append(m)
            cnts.append(jnp.sum(eq.astype(F32), axis=0, keepdims=True))
            vals = jnp.where(eq, ninf, vals)
        return jnp.concatenate(tops, 0), jnp.concatenate(cnts, 0)

    ta, na = top_distinct(s1)
    tb, nb = top_distinct(s2)
    nrow = [PEER_TOPK // (k + 1) for k in range(PEER_TOPK)]
    pad = -sum(nrow) % 8
    cand = jnp.concatenate([ta[k:k + 1] + tb[:nrow[k]] for k in range(PEER_TOPK)]
                           + [jnp.full((pad, ta.shape[1]), ninf, F32)], 0)
    mult = jnp.concatenate([na[k:k + 1] * nb[:nrow[k]] for k in range(PEER_TOPK)]
                           + [jnp.zeros((pad, ta.shape[1]), F32)], 0)
    vals = cand
    cnt = jnp.zeros_like(cand[0:1])
    tau = jnp.full_like(cand[0:1], ninf)
    for _ in range(PEER_TOPK):
        m = jnp.max(vals, axis=0, keepdims=True)
        eq = vals == m
        tau = jnp.where(cnt < PEER_TOPK, m, tau)
        cnt = cnt + jnp.sum(jnp.where(eq, mult, 0.0), axis=0, keepdims=True)
        vals = jnp.where(eq, ninf, vals)
    cmax = cand[0:1]
    z = jnp.sum(jnp.where(cand >= tau, mult * jnp.exp(cand - cmax), 0.0), axis=0, keepdims=True)
    g1 = jnp.exp(s1 - ta[0:1]) * (1.0 / z)
    g2 = jnp.exp(s2 - tb[0:1])
    for lc in range(s1.shape[1] // PEER_LANES):
        ls = slice(lc * PEER_LANES, (lc + 1) * PEER_LANES)
        s1_o[lc] = s1[:, ls]
        s2_o[lc] = s2[:, ls]
        s1l_o[lc] = g1[:, ls]
        s2l_o[lc] = g2[:, ls]
        tau_o[lc] = tau[:, ls]


PEER_LANES = 128
TOPK_TT = 1024


def _peer_topk(qT, keys):
    t = qT.shape[1]
    tt = min(TOPK_TT, t)
    nl = tt // PEER_LANES
    blk = pl.BlockSpec((None, nl, PEER_NKEYS, PEER_LANES), lambda j, h: (h, j, 0, 0))
    shp = jax.ShapeDtypeStruct((PEER_HEADS, t // PEER_LANES, PEER_NKEYS, PEER_LANES), F32)
    return pl.pallas_call(
        _peer_topk_body, grid=(t // tt, PEER_HEADS),
        in_specs=[pl.BlockSpec((PEER_QDIM, tt), lambda j, h: (h, j)),
                  pl.BlockSpec((None, 2, PEER_NKEYS, PEER_HALF), lambda j, h: (h, 0, 0, 0))],
        out_specs=[blk, blk, blk, blk, pl.BlockSpec((None, nl, 1, PEER_LANES), lambda j, h: (h, j, 0, 0))],
        out_shape=[shp, shp, shp, shp, jax.ShapeDtypeStruct((PEER_HEADS, t // PEER_LANES, 1, PEER_LANES), F32)],
        compiler_params=_cp(2, 32), name="peer_topk")(qT, keys)


PEER_EK = 512
PEER_SUBS = PEER_EK // PEER_NKEYS
PEER_PIECES = 8
PEER_KPIECES = 8


def _peer_dense_body(xT_ref, u_ref, vT_ref, s1_ref, s2_ref, s1l_ref, s2l_ref, tau_ref, o_ref, hbuf, awbuf, *, nk):
    g = pl.program_id(0)
    d = o_ref.shape[0]
    h0, h1, aw0, aw1 = hbuf.at[0], hbuf.at[1], awbuf.at[0], awbuf.at[1]

    @pl.when(g == 0)
    def _():
        hbuf[...] = jnp.zeros(hbuf.shape, F32)
        awbuf[...] = jnp.zeros(awbuf.shape, BF16)

    @pl.when((g <= 2) | ((g - 2) % nk == 0))
    def _():
        o_ref[...] = jnp.zeros(o_ref.shape, F32)

    def stages(h_cur, h_prev, aw_cur, aw_prev):
        base = (jnp.clip(g - 1, 0, pl.num_programs(0) - 3) % nk) * PEER_SUBS
        dq = d // PEER_PIECES

        def second_matmul(q):
            ds_ = slice(q * dq, (q + 1) * dq)
            o_ref[ds_, :] += jnp.dot(vT_ref[ds_, :], aw_cur[...], preferred_element_type=F32)

        kq = d // PEER_KPIECES

        def first_matmul(q):
            ks = slice(q * kq, (q + 1) * kq)
            part = jnp.dot(u_ref[:, ks], xT_ref[ks, :], preferred_element_type=F32)
            if q == 0:
                h_cur[...] = part
            else:
                h_cur[...] += part

        def gate_tile(ii, lc):
            i1 = pl.ds(base + ii, 1)
            w = None
            for h in range(PEER_HEADS):
                c = s2_ref[h, lc] + s1_ref[h, lc, i1, :]
                wh = jnp.where(c >= tau_ref[h, lc], s2l_ref[h, lc] * s1l_ref[h, lc, i1, :], 0.0)
                w = wh if w is None else w + wh
            rs = slice(ii * PEER_NKEYS, (ii + 1) * PEER_NKEYS)
            ls = slice(lc * PEER_LANES, (lc + 1) * PEER_LANES)
            aw_prev[rs, ls] = (_gelu(h_prev[rs, ls]) * w).astype(BF16)

        tiles = [(ii, lc) for ii in range(PEER_SUBS) for lc in range(o_ref.shape[1] // PEER_LANES)]
        per = -(-len(tiles) // (2 * PEER_PIECES))
        for q in range(PEER_PIECES):
            second_matmul(q)
            for tl in tiles[(2 * q) * per:(2 * q + 1) * per]:
                gate_tile(*tl)
            if q % (PEER_PIECES // PEER_KPIECES) == 0:
                first_matmul(q // (PEER_PIECES // PEER_KPIECES))
            for tl in tiles[(2 * q + 1) * per:(2 * q + 2) * per]:
                gate_tile(*tl)

    @pl.when(g % 2 == 0)
    def _():
        stages(h0, h1, aw0, aw1)

    @pl.when(g % 2 == 1)
    def _():
        stages(h1, h0, aw1, aw0)


def _peer_dense(xmT, u_bf, vT_bf, tk):
    d, t = xmT.shape
    tt = min(512, t)
    nl = tt // PEER_LANES
    nk = PEER_EXPERTS // PEER_EK
    s1, s2, s1l, s2l, tau = tk
    last = (t // tt) * nk - 1
    pair = lambda g, lag: jnp.clip(g - lag, 0, last)
    sblk = pl.BlockSpec((PEER_HEADS, nl, PEER_NKEYS, PEER_LANES), lambda g: (0, pair(g, 1) // nk, 0, 0))
    return pl.pallas_call(
        functools.partial(_peer_dense_body, nk=nk), grid=(last + 3,),
        in_specs=[pl.BlockSpec((d, tt), lambda g: (0, pair(g, 0) // nk)),
                  pl.BlockSpec((PEER_EK, d), lambda g: (pair(g, 0) % nk, 0)),
                  pl.BlockSpec((d, PEER_EK), lambda g: (0, pair(g, 2) % nk)),
                  sblk, sblk, sblk, sblk,
                  pl.BlockSpec((PEER_HEADS, nl, 1, PEER_LANES), lambda g: (0, pair(g, 1) // nk, 0, 0))],
        out_specs=pl.BlockSpec((d, tt), lambda g: (0, pair(g, 2) // nk)),
        out_shape=jax.ShapeDtypeStruct((d, t), F32),
        scratch_shapes=[pltpu.VMEM((2, PEER_EK, tt), F32), pltpu.VMEM((2, PEER_EK, tt), BF16)],
        compiler_params=_cp(1, 48), name="peer_dense")(xmT, u_bf, vT_bf, s1, s2, s1l, s2l, tau)


def _ffn_ln_body(fT_ref, x1_ref, g2_ref, lng_ref, lnb_ref, scn_ref, shn_ref, x2_ref, *rest, alpha):
    x2 = _ln(alpha * x1_ref[...] + g2_ref[...] * fT_ref[...].T, lng_ref[...], lnb_ref[...])
    x2_ref[...] = x2
    if rest:
        rest[0][...] = (x2 * (1.0 + scn_ref[...]) + shn_ref[...]).astype(rest[0].dtype)


def _ffn_ln(fT, x1, g2, lng, lnb, scn, shn, alpha, emit_h):
    t, d = x1.shape
    tm = min(256, t)
    row = pl.BlockSpec((tm, d), lambda i: (i, 0))
    vec = pl.BlockSpec((1, d), lambda i: (0, 0))
    out_specs = [row, row] if emit_h else [row]
    out_shape = [jax.ShapeDtypeStruct((t, d), F32)] + ([jax.ShapeDtypeStruct((t, d), BF16)] if emit_h else [])
    return pl.pallas_call(
        functools.partial(_ffn_ln_body, alpha=alpha), grid=(t // tm,),
        in_specs=[pl.BlockSpec((d, tm), lambda i: (0, i)), row, vec, vec, vec, vec, vec],
        out_specs=out_specs, out_shape=out_shape,
        compiler_params=_cp(1, 32), name="ffn_ln")(fT, x1, g2, lng, lnb, scn, shn)


def kernel(x, c, ctx, c_ctx, w_ada, b_ada, w_in, b_in, s5_lam_re, s5_lam_im, s5_log_dt, s5_b_re, s5_b_im, s5_c_re, s5_c_im, s5_d, w_glu, b_glu, qn_gain, kn_gain, sink, sg_ln_g, sg_ln_b, w_sp, b_sp, w_br_a, w_br_b, w_br_c, w_br_d, w_out, ln1_g, ln1_b, ln2_g, ln2_b, w_pq, peer_keys, peer_u, peer_v):
    depth = w_in.shape[0]
    bsz, n_lat, d = x.shape
    assert bsz == 1 and d == D_MODEL and n_lat % 512 == 0 and ctx.shape[1] % BLOCK == 0
    n_ctx = ctx.shape[1]
    alpha = (2 * depth) ** 0.25
    x, ctx = x[0], ctx[0]

    row8 = lax.broadcasted_iota(jnp.int32, (8, d), 0)
    cond8 = jnp.where(row8 == 0, c[0][None], jnp.where(row8 == 1, c_ctx[None], 0.0))
    ada = [_ada(cond8, w_ada, b_ada, l) for l in range(depth)]
    cosf, sinf = _rope_tables(n_lat)
    cos1, sin0 = jnp.ones((n_ctx, HEAD_DIM), F32), jnp.zeros((n_ctx, HEAD_DIM), F32)
    vec = lambda a: a.reshape(1, -1).astype(F32)

    ref_off = {'a': 0, 'qg': 768, 'kg': 1792, 'vg': 2048, 'qw': 2304, 'kw': 3328, 'vw': 3584, 'du': 3840, 'dv': 4608}
    ref_w = {'a': 768, 'qg': 1024, 'kg': 256, 'vg': 256, 'qw': 1024, 'kw': 256, 'vw': 256, 'du': 768, 'dv': 768}
    order = ['a', 'kg', 'qg', 'qw', 'kw', 'vg', 'vw', 'du', 'dv']
    perm = lambda a: jnp.concatenate([a[..., ref_off[n]:ref_off[n] + ref_w[n]] for n in order], -1)
    perm_idx = np.concatenate([np.arange(ref_off[n], ref_off[n] + ref_w[n]) for n in order]).astype(np.int32)

    h_lat = None
    for l in range(depth):
        need_ctx = l < depth - 1
        mods = [[ada[l][r:r + 1, i * d:(i + 1) * d] for i in range(6)] for r in range(2)]
        (sh1, sc1, g1, sh2, sc2, g2), (sh1c, sc1c, g1c, sh2c, sc2c, g2c) = mods
        if need_ctx:
            nxt = [ada[l + 1][r:r + 1, 0:2 * d] for r in range(2)]
            (shn, scn), (shnc, scnc) = [(m[:, :d], m[:, d:]) for m in nxt]
        else:
            shn = scn = shnc = scnc = jnp.zeros((1, d), F32)

        w_small, b_small = perm(w_in[l][:, :SMALL_W]).astype(BF16), vec(jnp.take(b_in[l], perm_idx))
        w_gate, b_gate = w_in[l][:, SMALL_W:].astype(BF16), vec(b_in[l][SMALL_W:])
        s5m = [_s5_mats(s5_lam_re[l, dr], s5_lam_im[l, dr], s5_log_dt[l, dr], s5_b_re[l, dr], s5_b_im[l, dr],
                        s5_c_re[l, dr], s5_c_im[l, dr]) for dr in range(2)]
        wg, bg, s5d = w_glu[l].astype(BF16), vec(b_glu[l]), vec(s5_d[l])
        qn, kn = vec(qn_gain[l]), vec(kn_gain[l])
        wsp, bsp = w_sp[l].astype(BF16), b_sp[l].astype(F32)[:, :, None]
        w_brs = [w[l].astype(BF16) for w in (w_br_a, w_br_b, w_br_c, w_br_d)]
        wo = w_out[l].astype(BF16)
        wpqT = w_pq[l].T.astype(BF16)
        keys = peer_keys[l].astype(BF16)
        u_bf, vT_bf = peer_u[l].astype(BF16), peer_v[l].T.astype(BF16)

        if h_lat is None:
            h_lat, h_ctx = _modulate(x, sc1, sh1), _modulate(ctx, sc1c, sh1c)

        proj = _matmul(h_lat, w_small, b_small, tm=512, tn=1792, out_dtype=BF16, name="in_proj")
        gates = _matmul(h_lat, w_gate, b_gate, tm=512, tn=2048, out_dtype=BF16, name="in_proj_gates")
        proj_c = _matmul(h_ctx, w_small, b_small, tm=512, tn=1792, out_dtype=BF16, name="in_proj_ctx")

        y_dir, y_dir_c = [], []
        for dr in range(2):
            yc_, hc_end = _s5_dir(proj_c, s5m[dr], jnp.zeros((2, S5_CH), F32), bool(dr))
            yl_, _ = _s5_dir(proj, s5m[dr], hc_end, bool(dr))
            y_dir.append(yl_)
            y_dir_c.append(yc_)
        y_a = _s5_out(proj, y_dir[0], y_dir[1], s5d, wg, bg)

        qg, kg, vge, qw, kw = _prep(proj, cosf, sinf, qn, kn, ATTN_SCALE * math.log2(math.e))
        qgc, kgc, vgce, qwc, kwc = _prep(proj_c, cos1, sin0, qn, kn, ATTN_SCALE)
        y_b = _flash(qg, kg, vge, kgc, vgce)
        y_c = _window(qw, kw, proj, kwc, proj_c, sink[l])

        y_d = _spatial_gate(proj, vec(sg_ln_g[l]), vec(sg_ln_b[l]), wsp, bsp)

        merged = _merge((y_a, y_b, y_c, y_d), w_brs, gates)
        x1, xmT = _out_ln(merged, wo, x, g1, vec(ln1_g[l]), vec(ln1_b[l]), sc2, sh2, alpha)
        qT = _matmul(wpqT, xmT, None, tm=D_MODEL, tn=1024, out_dtype=F32, name="peer_query")
        ffnT = _peer_dense(xmT, u_bf, vT_bf, _peer_topk(qT, keys))
        res = _ffn_ln(ffnT, x1, g2, vec(ln2_g[l]), vec(ln2_b[l]), scn, shn, alpha, need_ctx)
        x = res[0]

        if need_ctx:
            h_lat = res[1]
            gates_c = _matmul(h_ctx, w_gate, b_gate, tm=512, tn=2048, out_dtype=BF16, name="in_proj_gates_ctx")
            y_a_c = _s5_out(proj_c, y_dir_c[0], y_dir_c[1], s5d, wg, bg)
            y_b_c = _ctx_attn(qgc, kgc, proj_c, COL_VG, sink[l], False)
            y_c_c = _ctx_attn(qwc, kwc, proj_c, COL_VW, sink[l], True)
            y_d_c = _spatial_gate(proj_c, vec(sg_ln_g[l]), vec(sg_ln_b[l]), wsp, bsp)
            merged_c = _merge((y_a_c, y_b_c, y_c_c, y_d_c), w_brs, gates_c)
            c1, cmT = _out_ln(merged_c, wo, ctx, g1c, vec(ln1_g[l]), vec(ln1_b[l]), sc2c, sh2c, alpha)
            qTc = _matmul(wpqT, cmT, None, tm=D_MODEL, tn=1024, out_dtype=F32, name="peer_query_ctx")
            ffnTc = _peer_dense(cmT, u_bf, vT_bf, _peer_topk(qTc, keys))
            ctx, h_ctx = _ffn_ln(ffnTc, c1, g2c, vec(ln2_g[l]), vec(ln2_b[l]), scnc, shnc, alpha, True)
    return x[None]
```

```python
import functools
import math

import jax
import jax.numpy as jnp
import numpy as np
from jax import lax
from jax.experimental import pallas as pl
from jax.experimental.pallas import tpu as pltpu

F32, BF16 = jnp.float32, jnp.bfloat16

D_MODEL = 2048
GRID_W = 64
BLOCK = 128
HEAD_DIM = 128
AXIS_DIM = HEAD_DIM // 2
ROPE_THETA = 10000.0
ATTN_SCALE = HEAD_DIM ** -0.5
NEG_INF = -1e30
EPS = 1e-6
S5_GROUPS, S5_GROUP_CH, S5_STATE = 48, 16, 64
S5_WIDTH = S5_GROUPS * S5_GROUP_CH
S5_CH = S5_GROUPS * S5_STATE
S5_GB = 16
S5_NB = S5_GROUPS // S5_GB
S5_KB = S5_GB * S5_GROUP_CH
S5_PB = S5_GB * S5_STATE
GA_HEADS, GA_KV = 8, 2
WA_HEADS, WA_KV = 8, 2
KV_GROUP = GA_HEADS // GA_KV
SG_GROUPS, SG_GROUP_CH = 6, 128
SG_WIDTH = SG_GROUPS * SG_GROUP_CH
CHUNK = 128
N_BRANCH = 4
PEER_HEADS, PEER_NKEYS, PEER_QDIM, PEER_TOPK = 8, 128, 256, 16
PEER_HALF = PEER_QDIM // 2
PEER_EXPERTS = PEER_NKEYS * PEER_NKEYS

COL_A, COL_KG, COL_QG, COL_QW, COL_KW, COL_VG, COL_VW, COL_DU, COL_DV = 0, 768, 1024, 2048, 3072, 3328, 3584, 3840, 4608
SMALL_W = 5376
GATE_W = N_BRANCH * D_MODEL

VMEM_CAP_MB = 56


def _cp(ndims, vmem_mb=32):
    return pltpu.CompilerParams(dimension_semantics=("arbitrary",) * ndims,
                                vmem_limit_bytes=min(vmem_mb, VMEM_CAP_MB) << 20)


def _gelu(y):
    return 0.5 * y * (1.0 + lax.erf(y * (2.0 ** -0.5)))


def _ln(y, g, b):
    mu = jnp.mean(y, -1, keepdims=True)
    yc = y - mu
    var = jnp.mean(yc * yc, -1, keepdims=True)
    return yc * lax.rsqrt(var + EPS) * g + b


_NT = (((1,), (1,)), ((), ()))


def _ada_body(c_ref, w_ref, b_ref, o_ref):
    cnd = c_ref[...]
    s = (cnd * jax.nn.sigmoid(cnd)).astype(BF16)
    o_ref[...] = jnp.dot(s, w_ref[...].astype(BF16), preferred_element_type=F32) + b_ref[...]


def _ada(cond8, w_ada, b_ada, l):
    depth, d, n = w_ada.shape
    tn = 1024
    return pl.pallas_call(
        _ada_body, grid=(n // tn,),
        in_specs=[pl.BlockSpec((8, d), lambda j: (0, 0)),
                  pl.BlockSpec((None, d, tn), lambda j: (l, 0, j)),
                  pl.BlockSpec((None, 1, tn), lambda j: (l, 0, j))],
        out_specs=pl.BlockSpec((8, tn), lambda j: (0, j)),
        out_shape=jax.ShapeDtypeStruct((8, n), F32),
        compiler_params=_cp(1, 32), name="ada_mod")(cond8, w_ada, b_ada.reshape(depth, 1, n))


def _mod_body(x_ref, sc_ref, sh_ref, o_ref):
    o_ref[...] = (x_ref[...] * (1.0 + sc_ref[...]) + sh_ref[...]).astype(o_ref.dtype)


def _modulate(x, sc, sh):
    t, d = x.shape
    tm = min(512, t)
    return pl.pallas_call(
        _mod_body, grid=(t // tm,),
        in_specs=[pl.BlockSpec((tm, d), lambda i: (i, 0)),
                  pl.BlockSpec((1, d), lambda i: (0, 0)),
                  pl.BlockSpec((1, d), lambda i: (0, 0))],
        out_specs=pl.BlockSpec((tm, d), lambda i: (i, 0)),
        out_shape=jax.ShapeDtypeStruct((t, d), BF16),
        compiler_params=_cp(1, 24), name="modulate")(x, sc, sh)


def _mm_body(a_ref, w_ref, b_ref, o_ref):
    acc = jnp.dot(a_ref[...], w_ref[...], preferred_element_type=F32)
    o_ref[...] = (acc + b_ref[...]).astype(o_ref.dtype)


def _mm_nobias_body(a_ref, w_ref, o_ref):
    o_ref[...] = jnp.dot(a_ref[...], w_ref[...], preferred_element_type=F32).astype(o_ref.dtype)


def _matmul(a, w, b, *, tm, tn, out_dtype, name):
    m, k = a.shape
    n = w.shape[1]
    tm, tn = min(tm, m), min(tn, n)
    in_specs = [pl.BlockSpec((tm, k), lambda j, i: (i, 0)),
                pl.BlockSpec((k, tn), lambda j, i: (0, j))]
    args = [a, w]
    body = _mm_nobias_body
    if b is not None:
        in_specs.append(pl.BlockSpec((1, tn), lambda j, i: (0, j)))
        args.append(b)
        body = _mm_body
    osz = jnp.dtype(out_dtype).itemsize
    vmem = 2 * (tm * k * 2 + k * tn * 2 + tm * tn * osz) + tm * tn * 4
    return pl.pallas_call(
        body, grid=(n // tn, m // tm), in_specs=in_specs,
        out_specs=pl.BlockSpec((tm, tn), lambda j, i: (i, j)),
        out_shape=jax.ShapeDtypeStruct((m, n), out_dtype),
        compiler_params=_cp(2, (vmem >> 20) + 8), name=name)(*args)


S5_TC = 256
S5_SEG = 8
S5_L = S5_TC // S5_SEG


def _s5_body(u_ref, perm_ref, permT_ref, bre_ref, bim_ref, cre_ref, cim_ref, apow_ref, h0_ref, y_ref, hT_ref,
             xr, xi, carry, hin_r, hin_i, *, rev, seg_len):
    @pl.when(pl.program_id(0) == 0)
    def _():
        carry[...] = h0_ref[...]

    up = jnp.dot(perm_ref[...], u_ref[...], preferred_element_type=F32).astype(BF16)
    for j in range(S5_NB):
        uj = up[:, j * S5_KB:(j + 1) * S5_KB]
        xr[:, j * S5_PB:(j + 1) * S5_PB] = jnp.dot(uj, bre_ref[j], preferred_element_type=F32)
        xi[:, j * S5_PB:(j + 1) * S5_PB] = jnp.dot(uj, bim_ref[j], preferred_element_type=F32)

    seg_order = range(S5_SEG - 1, -1, -1) if rev else range(S5_SEG)
    for j in range(S5_NB):
        cs = slice(j * S5_PB, (j + 1) * S5_PB)
        ar = jnp.broadcast_to(apow_ref[0, 0:1, cs], (S5_SEG, S5_PB))
        ai = jnp.broadcast_to(apow_ref[1, 0:1, cs], (S5_SEG, S5_PB))

        def local_step(k, c, cs=cs, ar=ar, ai=ai):
            hr, hi = c
            rows = pl.ds(pl.multiple_of(k * S5_SEG, S5_SEG), S5_SEG)
            nr = ar * hr - ai * hi + xr[rows, cs]
            ni = ar * hi + ai * hr + xi[rows, cs]
            xr[rows, cs] = nr
            xi[rows, cs] = ni
            return nr, ni

        zero = jnp.zeros((S5_SEG, S5_PB), F32)
        er, ei = lax.fori_loop(0, seg_len, local_step, (zero, zero), unroll=4)

        alr, ali = apow_ref[0, seg_len - 1:seg_len, cs], apow_ref[1, seg_len - 1:seg_len, cs]
        hr, hi = carry[0:1, cs], carry[1:2, cs]
        for s in seg_order:
            hin_r[s:s + 1, cs] = hr
            hin_i[s:s + 1, cs] = hi
            hr, hi = (er[s:s + 1] + alr * hr - ali * hi, ei[s:s + 1] + alr * hi + ali * hr)
        carry[0:1, cs] = hr
        carry[1:2, cs] = hi

        def fix_step(k, c, cs=cs):
            rows = pl.ds(pl.multiple_of(k * S5_SEG, S5_SEG), S5_SEG)
            pr, pi = apow_ref[0, pl.ds(k, 1), cs], apow_ref[1, pl.ds(k, 1), cs]
            h_r, h_i = hin_r[:, cs], hin_i[:, cs]
            xr[rows, cs] += pr * h_r - pi * h_i
            xi[rows, cs] += pr * h_i + pi * h_r
            return c

        lax.fori_loop(0, seg_len, fix_step, 0, unroll=4)

    yp = jnp.concatenate(
        [jnp.dot(xr[:, j * S5_PB:(j + 1) * S5_PB].astype(BF16), cre_ref[j], preferred_element_type=F32)
         + jnp.dot(xi[:, j * S5_PB:(j + 1) * S5_PB].astype(BF16), cim_ref[j], preferred_element_type=F32)
         for j in range(S5_NB)], axis=1)
    hi = yp.astype(BF16)
    r1 = yp - hi.astype(F32)
    mid = r1.astype(BF16)
    lo = (r1 - mid.astype(F32)).astype(BF16)
    parts = jnp.dot(permT_ref[...], jnp.concatenate([hi, mid, lo], axis=1), preferred_element_type=F32)
    y_ref[...] = parts[:, :S5_WIDTH] + parts[:, S5_WIDTH:2 * S5_WIDTH] + parts[:, 2 * S5_WIDTH:]
    hT_ref[...] = carry[...]


def _s5_perm(rev, tc):
    seg_len = tc // S5_SEG
    p = np.zeros((tc, tc), np.float32)
    for k in range(seg_len):
        for s in range(S5_SEG):
            p[k * S5_SEG + s, s * seg_len + ((seg_len - 1 - k) if rev else k)] = 1.0
    return jnp.asarray(p, BF16), jnp.asarray(p.T, BF16)


def _s5_dir(proj, mats, h0, rev):
    t = proj.shape[0]
    tc = min(S5_TC, t)
    assert t % tc == 0 and tc % (8 * S5_SEG) == 0
    nt = t // tc
    bre, bim, cre, cim, apow = mats
    tmap = (lambda i: (nt - 1 - i, 0)) if rev else (lambda i: (i, 0))
    full3 = lambda i: (0, 0, 0)
    full2 = lambda i: (0, 0)
    return pl.pallas_call(
        functools.partial(_s5_body, rev=rev, seg_len=tc // S5_SEG), grid=(nt,),
        in_specs=[pl.BlockSpec((tc, S5_WIDTH), tmap),
                  pl.BlockSpec((tc, tc), full2), pl.BlockSpec((tc, tc), full2),
                  pl.BlockSpec((S5_NB, S5_KB, S5_PB), full3), pl.BlockSpec((S5_NB, S5_KB, S5_PB), full3),
                  pl.BlockSpec((S5_NB, S5_PB, S5_KB), full3), pl.BlockSpec((S5_NB, S5_PB, S5_KB), full3),
                  pl.BlockSpec((2, S5_L, S5_CH), full3), pl.BlockSpec((2, S5_CH), full2)],
        out_specs=[pl.BlockSpec((tc, S5_WIDTH), tmap), pl.BlockSpec((2, S5_CH), full2)],
        out_shape=[jax.ShapeDtypeStruct((t, S5_WIDTH), F32), jax.ShapeDtypeStruct((2, S5_CH), F32)],
        scratch_shapes=[pltpu.VMEM((tc, S5_CH), F32), pltpu.VMEM((tc, S5_CH), F32), pltpu.VMEM((2, S5_CH), F32),
                        pltpu.VMEM((S5_SEG, S5_CH), F32), pltpu.VMEM((S5_SEG, S5_CH), F32)],
        compiler_params=_cp(1, 48), name="s5_scan_bwd" if rev else "s5_scan_fwd")(
            proj, *_s5_perm(rev, tc), bre, bim, cre, cim, apow, h0)


def _s5_out_body(u_ref, yf_ref, yb_ref, d_ref, wg_ref, bg_ref, o_ref):
    y = u_ref[...].astype(F32) * d_ref[...] + yf_ref[...] + yb_ref[...]
    g = _gelu(y)
    z = jnp.dot(g.astype(BF16), wg_ref[...], preferred_element_type=F32) + bg_ref[...]
    o_ref[...] = (g * jax.nn.sigmoid(z)).astype(o_ref.dtype)


def _s5_out(proj, yf, yb, d, wg, bg):
    t = proj.shape[0]
    tm = min(512, t)
    row = lambda i: (i, 0)
    full = lambda i: (0, 0)
    return pl.pallas_call(
        _s5_out_body, grid=(t // tm,),
        in_specs=[pl.BlockSpec((tm, S5_WIDTH), row), pl.BlockSpec((tm, S5_WIDTH), row), pl.BlockSpec((tm, S5_WIDTH), row),
                  pl.BlockSpec((1, S5_WIDTH), full), pl.BlockSpec((S5_WIDTH, S5_WIDTH), full), pl.BlockSpec((1, S5_WIDTH), full)],
        out_specs=pl.BlockSpec((tm, S5_WIDTH), row),
        out_shape=jax.ShapeDtypeStruct((t, S5_WIDTH), BF16),
        compiler_params=_cp(1, 24), name="s5_glu")(proj, yf, yb, d, wg, bg)


def _s5_mats(lam_re, lam_im, log_dt, b_re, b_im, c_re, c_im):
    dt = jnp.exp(log_dt)[:, None]
    mag = jnp.exp(lam_re * dt)
    ar, ai = mag * jnp.cos(lam_im * dt), mag * jnp.sin(lam_im * dt)
    nr, ni = ar - 1.0, ai
    den = lam_re * lam_re + lam_im * lam_im
    cr = (nr * lam_re + ni * lam_im) / den
    ci = (ni * lam_re - nr * lam_im) / den
    fre = cr[:, :, None] * b_re - ci[:, :, None] * b_im
    fim = cr[:, :, None] * b_im + ci[:, :, None] * b_re
    eye = jnp.eye(S5_GB, dtype=F32)

    def blk_in(f):
        f = f.reshape(S5_NB, S5_GB, S5_STATE, S5_GROUP_CH)
        m = jnp.einsum('jgph,gk->jghkp', f, eye)
        return m.reshape(S5_NB, S5_KB, S5_PB).astype(BF16)

    def blk_out(c):
        c = c.reshape(S5_NB, S5_GB, S5_GROUP_CH, S5_STATE)
        m = jnp.einsum('jghp,gk->jgpkh', c, eye)
        return m.reshape(S5_NB, S5_PB, S5_KB).astype(BF16)

    kk = jnp.arange(1, S5_L + 1, dtype=F32)[:, None, None]
    magk, angk = jnp.exp(lam_re * dt * kk), lam_im * dt * kk
    is_re = lax.broadcasted_iota(jnp.int32, (2, S5_L, S5_CH), 0) == 0
    apow = jnp.where(is_re, (magk * jnp.cos(angk)).reshape(1, S5_L, S5_CH), (magk * jnp.sin(angk)).reshape(1, S5_L, S5_CH))
    return blk_in(fre), blk_in(fim), blk_out(c_re), blk_out(-c_im), apow


def _prep_body(qg_ref, kg_ref, vg_ref, qw_ref, kw_ref, cos_ref, sin_ref, qn_ref, kn_ref, oqg, okg, ovg, oqw, okw, *,
               qg_scale):
    cosf, sinf = cos_ref[...], sin_ref[...]
    lane = lax.broadcasted_iota(jnp.int32, cosf.shape, 1)
    low = (lane % AXIS_DIM) < (AXIS_DIM // 2)

    def rope(t):
        partner = jnp.where(low, pltpu.roll(t, HEAD_DIM - AXIS_DIM // 2, 1), pltpu.roll(t, AXIS_DIM // 2, 1))
        return t * cosf + partner * sinf

    def rms(t, g):
        return t * lax.rsqrt(jnp.mean(t * t, -1, keepdims=True) + EPS) * g

    for h in range(GA_HEADS):
        cs = slice(h * HEAD_DIM, (h + 1) * HEAD_DIM)
        oqg[:, cs] = (rope(rms(qg_ref[:, cs].astype(F32), qn_ref[...])) * qg_scale).astype(BF16)
        oqw[:, cs] = (rope(qw_ref[:, cs].astype(F32)) * ATTN_SCALE).astype(BF16)
    for h in range(GA_KV):
        cs = slice(h * HEAD_DIM, (h + 1) * HEAD_DIM)
        okg[:, cs] = rope(rms(kg_ref[:, cs].astype(F32), kn_ref[...])).astype(BF16)
        okw[:, cs] = rope(kw_ref[:, cs].astype(F32)).astype(BF16)
        ovg[:, 2 * h * HEAD_DIM:(2 * h + 1) * HEAD_DIM] = vg_ref[:, cs]
        ovg[:, (2 * h + 1) * HEAD_DIM:(2 * h + 2) * HEAD_DIM] = jnp.ones((vg_ref.shape[0], HEAD_DIM), BF16)


def _prep(proj, cosf, sinf, qn, kn, qg_scale):
    t = proj.shape[0]
    tr = min(256, t)
    qw_, kw_ = GA_HEADS * HEAD_DIM, GA_KV * HEAD_DIM
    row = lambda i: (i, 0)
    full = lambda i: (0, 0)
    return pl.pallas_call(
        functools.partial(_prep_body, qg_scale=qg_scale), grid=(t // tr,),
        in_specs=[pl.BlockSpec((tr, qw_), lambda i: (i, COL_QG // qw_)),
                  pl.BlockSpec((tr, kw_), lambda i: (i, COL_KG // kw_)),
                  pl.BlockSpec((tr, kw_), lambda i: (i, COL_VG // kw_)),
                  pl.BlockSpec((tr, qw_), lambda i: (i, COL_QW // qw_)),
                  pl.BlockSpec((tr, kw_), lambda i: (i, COL_KW // kw_)),
                  pl.BlockSpec((tr, HEAD_DIM), row), pl.BlockSpec((tr, HEAD_DIM), row),
                  pl.BlockSpec((1, HEAD_DIM), full), pl.BlockSpec((1, HEAD_DIM), full)],
        out_specs=[pl.BlockSpec((tr, qw_), row), pl.BlockSpec((tr, kw_), row), pl.BlockSpec((tr, 2 * kw_), row),
                   pl.BlockSpec((tr, qw_), row), pl.BlockSpec((tr, kw_), row)],
        out_shape=[jax.ShapeDtypeStruct((t, qw_), BF16), jax.ShapeDtypeStruct((t, kw_), BF16),
                   jax.ShapeDtypeStruct((t, 2 * kw_), BF16),
                   jax.ShapeDtypeStruct((t, qw_), BF16), jax.ShapeDtypeStruct((t, kw_), BF16)],
        compiler_params=_cp(1, 24), name="qk_prep")(proj, proj, proj, proj, proj, cosf, sinf, qn, kn)


def _rope_tables(n_lat):
    rows = n_lat // GRID_W
    row = jnp.repeat(jnp.arange(rows, dtype=F32), GRID_W)
    col = jnp.tile(jnp.arange(GRID_W, dtype=F32), rows)
    inv = jnp.power(ROPE_THETA, -jnp.arange(0, AXIS_DIM, 2, dtype=F32) / AXIS_DIM)
    ang_r, ang_c = row[:, None] * inv, col[:, None] * inv
    cr, sr, cc, sc = jnp.cos(ang_r), jnp.sin(ang_r), jnp.cos(ang_c), jnp.sin(ang_c)
    return jnp.concatenate([cr, cr, cc, cc], -1), jnp.concatenate([-sr, sr, -sc, sc], -1)


def _flash_body(q_ref, kc_ref, vc_ref, k_ref, v_ref, o_ref, qs, m_s, acc, *, tq, tk, nk):
    for g in range(KV_GROUP):
        qs[g * tq:(g + 1) * tq, :] = q_ref[:, g * HEAD_DIM:(g + 1) * HEAD_DIM]
    m_s[...] = jnp.full(m_s.shape, -jnp.inf, F32)
    acc[...] = jnp.zeros(acc.shape, F32)

    def update(k, v):
        for g in range(KV_GROUP):
            rs = slice(g * tq, (g + 1) * tq)
            s = lax.dot_general(qs[rs, :], k, _NT, preferred_element_type=F32)
            m_prev = m_s[rs, :]
            m_next = jnp.maximum(m_prev, jnp.max(s, axis=1, keepdims=True))
            alpha = jnp.exp2(m_prev - m_next)
            p = jnp.exp2(s - jnp.concatenate([m_next] * (s.shape[1] // HEAD_DIM), axis=1))
            acc[rs, :] = (jnp.concatenate([alpha, alpha], axis=1) * acc[rs, :]
                          + jnp.dot(p.astype(BF16), v, preferred_element_type=F32))
            m_s[rs, :] = m_next

    update(kc_ref[...], vc_ref[...])

    def body(j, carry):
        rows = pl.ds(pl.multiple_of(j * tk, tk), tk)
        update(k_ref[rows, :], v_ref[rows, :])
        return carry

    lax.fori_loop(0, nk, body, 0, unroll=FLASH_UNROLL)
    out = acc[:, :HEAD_DIM] / acc[:, HEAD_DIM:]
    for g in range(KV_GROUP):
        o_ref[:, g * HEAD_DIM:(g + 1) * HEAD_DIM] = out[g * tq:(g + 1) * tq, :].astype(o_ref.dtype)


FLASH_TK = 512
FLASH_UNROLL = 16


def _flash(q, kg, vge, kgc, vgce):
    t = q.shape[0]
    n_ctx = kgc.shape[0]
    tq = min(256, t)
    tk = min(FLASH_TK, t)
    gw = KV_GROUP * HEAD_DIM
    head = lambda h, i: (0, h)
    return pl.pallas_call(
        functools.partial(_flash_body, tq=tq, tk=tk, nk=t // tk), grid=(GA_KV, t // tq),
        in_specs=[pl.BlockSpec((tq, gw), lambda h, i: (i, h)),
                  pl.BlockSpec((n_ctx, HEAD_DIM), head), pl.BlockSpec((n_ctx, 2 * HEAD_DIM), head),
                  pl.BlockSpec((t, HEAD_DIM), head), pl.BlockSpec((t, 2 * HEAD_DIM), head)],
        out_specs=pl.BlockSpec((tq, gw), lambda h, i: (i, h)),
        out_shape=jax.ShapeDtypeStruct((t, GA_HEADS * HEAD_DIM), BF16),
        scratch_shapes=[pltpu.VMEM((KV_GROUP * tq, HEAD_DIM), BF16), pltpu.VMEM((KV_GROUP * tq, HEAD_DIM), F32),
                        pltpu.VMEM((KV_GROUP * tq, 2 * HEAD_DIM), F32)],
        compiler_params=_cp(2, 48), name="global_attn")(q, kgc, vgce, kg, vge)


WIN_QB = 4


def _win_body(q_ref, km_ref, k0_ref, kp_ref, vm_ref, v0_ref, vp_ref, kc_ref, vc_ref, sink_ref, o_ref, *, nb):
    kslab = jnp.concatenate([km_ref[...], k0_ref[...], kp_ref[...]], axis=0)
    vslab = jnp.concatenate([vm_ref[...], v0_ref[...], vp_ref[...]], axis=0)
    sink = sink_ref[0]
    shape = (KV_GROUP * BLOCK, 3 * BLOCK)
    r = lax.broadcasted_iota(jnp.int32, shape, 0) % BLOCK
    c = lax.broadcasted_iota(jnp.int32, shape, 1)
    cc = c % BLOCK
    for b in range(WIN_QB):
        i = pl.program_id(1) * WIN_QB + b
        rows = slice(b * BLOCK, (b + 1) * BLOCK)
        q = jnp.concatenate([q_ref[rows, g * HEAD_DIM:(g + 1) * HEAD_DIM] for g in range(KV_GROUP)], axis=0)
        kw, vw = kslab[b * BLOCK:(b + 3) * BLOCK], vslab[b * BLOCK:(b + 3) * BLOCK]
        s_w = lax.dot_general(q, kw, _NT, preferred_element_type=F32)
        iv = jnp.full(shape, i, jnp.int32)
        valid = (((c < BLOCK) & (cc >= r) & (iv >= 1)) | ((c >= BLOCK) & (c < 2 * BLOCK))
                 | ((c >= 2 * BLOCK) & (cc <= r) & (iv + 1 < nb)))
        s_w = jnp.where(valid, s_w, NEG_INF)
        s_c = lax.dot_general(q, kc_ref[...], _NT, preferred_element_type=F32)
        m = jnp.maximum(jnp.maximum(jnp.max(s_w, -1, keepdims=True), jnp.max(s_c, -1, keepdims=True)), sink)
        pw, pc = jnp.exp(s_w - m), jnp.exp(s_c - m)
        l = jnp.sum(pw, -1, keepdims=True) + jnp.sum(pc, -1, keepdims=True) + jnp.exp(sink - m)
        inv = 1.0 / l
        out = (jnp.dot((pc * inv).astype(BF16), vc_ref[...], preferred_element_type=F32)
               + jnp.dot((pw * inv).astype(BF16), vw, preferred_element_type=F32))
        for g in range(KV_GROUP):
            o_ref[rows, g * HEAD_DIM:(g + 1) * HEAD_DIM] = out[g * BLOCK:(g + 1) * BLOCK, :].astype(o_ref.dtype)


def _sink_rows(sink, rows_per_head):
    return jnp.repeat(sink.astype(F32).reshape(WA_KV, KV_GROUP), rows_per_head, axis=1)[:, :, None]


def _window(qw, kw, proj, kwc, proj_c, sink):
    t = qw.shape[0]
    n_ctx = kwc.shape[0]
    nb = t // BLOCK
    gw = KV_GROUP * HEAD_DIM
    vcol = COL_VW // HEAD_DIM
    assert nb % WIN_QB == 0
    edge = lambda f: pl.BlockSpec((BLOCK, HEAD_DIM), f)
    main = lambda f: pl.BlockSpec((WIN_QB * BLOCK, HEAD_DIM), f)
    before = lambda i: jnp.maximum(i * WIN_QB - 1, 0)
    after = lambda i: jnp.minimum((i + 1) * WIN_QB, nb - 1)
    return pl.pallas_call(
        functools.partial(_win_body, nb=nb), grid=(WA_KV, nb // WIN_QB),
        in_specs=[pl.BlockSpec((WIN_QB * BLOCK, gw), lambda h, i: (i, h)),
                  edge(lambda h, i: (before(i), h)), main(lambda h, i: (i, h)), edge(lambda h, i: (after(i), h)),
                  edge(lambda h, i: (before(i), vcol + h)), main(lambda h, i: (i, vcol + h)),
                  edge(lambda h, i: (after(i), vcol + h)),
                  pl.BlockSpec((n_ctx, HEAD_DIM), lambda h, i: (0, h)),
                  pl.BlockSpec((n_ctx, HEAD_DIM), lambda h, i: (0, vcol + h)),
                  pl.BlockSpec((1, KV_GROUP * BLOCK, 1), lambda h, i: (h, 0, 0))],
        out_specs=pl.BlockSpec((WIN_QB * BLOCK, gw), lambda h, i: (i, h)),
        out_shape=jax.ShapeDtypeStruct((t, WA_HEADS * HEAD_DIM), BF16),
        compiler_params=_cp(2, 24), name="window_attn")(
            qw, kw, kw, kw, proj, proj, proj, kwc, proj_c, _sink_rows(sink, BLOCK))


def _ctx_attn_body(q_ref, k_ref, v_ref, sink_ref, o_ref, *, n_ctx, use_sink):
    q = jnp.concatenate([q_ref[:, g * HEAD_DIM:(g + 1) * HEAD_DIM] for g in range(KV_GROUP)], axis=0)
    s = lax.dot_general(q, k_ref[...], _NT, preferred_element_type=F32)
    m = jnp.max(s, -1, keepdims=True)
    if use_sink:
        m = jnp.maximum(m, sink_ref[0])
    p = jnp.exp(s - m)
    l = jnp.sum(p, -1, keepdims=True)
    if use_sink:
        l = l + jnp.exp(sink_ref[0] - m)
    out = jnp.dot((p / l).astype(BF16), v_ref[...], preferred_element_type=F32)
    for g in range(KV_GROUP):
        o_ref[:, g * HEAD_DIM:(g + 1) * HEAD_DIM] = out[g * n_ctx:(g + 1) * n_ctx, :].astype(o_ref.dtype)


def _ctx_attn(q, k, proj_c, vcol0, sink, use_sink):
    n_ctx = q.shape[0]
    gw = KV_GROUP * HEAD_DIM
    vcol = vcol0 // HEAD_DIM
    return pl.pallas_call(
        functools.partial(_ctx_attn_body, n_ctx=n_ctx, use_sink=use_sink), grid=(GA_KV,),
        in_specs=[pl.BlockSpec((n_ctx, gw), lambda h: (0, h)),
                  pl.BlockSpec((n_ctx, HEAD_DIM), lambda h: (0, h)),
                  pl.BlockSpec((n_ctx, HEAD_DIM), lambda h: (0, vcol + h)),
                  pl.BlockSpec((1, KV_GROUP * n_ctx, 1), lambda h: (h, 0, 0))],
        out_specs=pl.BlockSpec((n_ctx, gw), lambda h: (0, h)),
        out_shape=jax.ShapeDtypeStruct((n_ctx, GA_HEADS * HEAD_DIM), BF16),
        compiler_params=_cp(1, 24), name="ctx_attn_sink" if use_sink else "ctx_attn")(
            q, k, proj_c, _sink_rows(sink, n_ctx))


def _sg_body(u_ref, v_ref, g_ref, b_ref, w_ref, bs_ref, o_ref, *, nch):
    for n in range(nch):
        rs = slice(n * CHUNK, (n + 1) * CHUNK)
        vn = _ln(v_ref[rs, :].astype(F32), g_ref[...], b_ref[...]).astype(BF16)
        for g in range(SG_GROUPS):
            cs = slice(g * SG_GROUP_CH, (g + 1) * SG_GROUP_CH)
            s = jnp.dot(w_ref[g], vn[:, cs], preferred_element_type=F32) + bs_ref[g]
            o_ref[rs, cs] = (u_ref[rs, cs].astype(F32) * s).astype(o_ref.dtype)


def _spatial_gate(proj, g_v, b_v, w_sp, b_sp):
    t = proj.shape[0]
    tm = min(512, t)
    full = lambda i: (0, 0)
    return pl.pallas_call(
        functools.partial(_sg_body, nch=tm // CHUNK), grid=(t // tm,),
        in_specs=[pl.BlockSpec((tm, SG_WIDTH), lambda i: (i, COL_DU // SG_WIDTH)),
                  pl.BlockSpec((tm, SG_WIDTH), lambda i: (i, COL_DV // SG_WIDTH)),
                  pl.BlockSpec((1, SG_WIDTH), full), pl.BlockSpec((1, SG_WIDTH), full),
                  pl.BlockSpec((SG_GROUPS, CHUNK, CHUNK), lambda i: (0, 0, 0)),
                  pl.BlockSpec((SG_GROUPS, CHUNK, 1), lambda i: (0, 0, 0))],
        out_specs=pl.BlockSpec((tm, SG_WIDTH), lambda i: (i, 0)),
        out_shape=jax.ShapeDtypeStruct((t, SG_WIDTH), BF16),
        compiler_params=_cp(1, 24), name="spatial_gate")(proj, proj, g_v, b_v, w_sp, b_sp)


def _merge_body(ya, yb, yc, yd, wa, wb, wc, wd, ga, gb, gc, gd, o_ref):
    def br(y, w, g):
        return jax.nn.sigmoid(g[...].astype(F32)) * jnp.dot(y[...], w[...], preferred_element_type=F32)
    o_ref[...] = (br(ya, wa, ga) + br(yb, wb, gb) + br(yc, wc, gc) + br(yd, wd, gd)).astype(o_ref.dtype)


def _merge(ys, ws, gates):
    t = ys[0].shape[0]
    tm, tn = min(512, t), 1024
    nn = D_MODEL // tn
    in_specs = [pl.BlockSpec((tm, y.shape[1]), lambda j, i: (i, 0)) for y in ys]
    in_specs += [pl.BlockSpec((w.shape[0], tn), lambda j, i: (0, j)) for w in ws]
    in_specs += [pl.BlockSpec((tm, tn), functools.partial(lambda j, i, b: (i, b * nn + j), b=b)) for b in range(N_BRANCH)]
    return pl.pallas_call(
        _merge_body, grid=(nn, t // tm), in_specs=in_specs,
        out_specs=pl.BlockSpec((tm, tn), lambda j, i: (i, j)),
        out_shape=jax.ShapeDtypeStruct((t, D_MODEL), BF16),
        compiler_params=_cp(2, 48), name="gated_merge")(*ys, *ws, gates, gates, gates, gates)


def _out_ln_body(mg_ref, w_ref, x_ref, g1_ref, lng_ref, lnb_ref, sc_ref, sh_ref, x1_ref, xmT_ref, *, alpha):
    o = jnp.dot(mg_ref[...], w_ref[...], preferred_element_type=F32)
    x1 = _ln(alpha * x_ref[...] + g1_ref[...] * o, lng_ref[...], lnb_ref[...])
    x1_ref[...] = x1
    xmT_ref[...] = (x1 * (1.0 + sc_ref[...]) + sh_ref[...]).T.astype(xmT_ref.dtype)


def _out_ln(merged, w_out, x, g1, lng, lnb, sc2, sh2, alpha):
    t, d = x.shape
    tm = min(256, t)
    row = lambda i: (i, 0)
    vec = pl.BlockSpec((1, d), lambda i: (0, 0))
    return pl.pallas_call(
        functools.partial(_out_ln_body, alpha=alpha), grid=(t // tm,),
        in_specs=[pl.BlockSpec((tm, d), row), pl.BlockSpec((d, d), lambda i: (0, 0)), pl.BlockSpec((tm, d), row),
                  vec, vec, vec, vec, vec],
        out_specs=[pl.BlockSpec((tm, d), row), pl.BlockSpec((d, tm), lambda i: (0, i))],
        out_shape=[jax.ShapeDtypeStruct((t, d), F32), jax.ShapeDtypeStruct((d, t), BF16)],
        compiler_params=_cp(1, 48), name="out_proj_ln")(merged, w_out, x, g1, lng, lnb, sc2, sh2)


def _peer_topk_body(q_ref, k_ref, s1_o, s2_o, s1l_o, s2l_o, tau_o):
    q = q_ref[...].astype(BF16)
    s1 = jnp.dot(k_ref[0], q[:PEER_HALF], preferred_element_type=F32)
    s2 = jnp.dot(k_ref[1], q[PEER_HALF:], preferred_element_type=F32)
    ninf = -jnp.inf

    def top_distinct(s):
        vals, tops, cnts = s, [], []
        for _ in range(PEER_TOPK):
            m = jnp.max(vals, axis=0, keepdims=True)
            eq = vals == m
            tops.append(m)
            cnts.append(jnp.sum(eq.astype(F32), axis=0, keepdims=True))
            vals = jnp.where(eq, ninf, vals)
        return jnp.concatenate(tops, 0), jnp.concatenate(cnts, 0)

    ta, na = top_distinct(s1)
    tb, nb = top_distinct(s2)
    nrow = [PEER_TOPK // (k + 1) for k in range(PEER_TOPK)]
    pad = -sum(nrow) % 8
    cand = jnp.concatenate([ta[k:k + 1] + tb[:nrow[k]] for k in range(PEER_TOPK)]
                           + [jnp.full((pad, ta.shape[1]), ninf, F32)], 0)
    mult = jnp.concatenate([na[k:k + 1] * nb[:nrow[k]] for k in range(PEER_TOPK)]
                           + [jnp.zeros((pad, ta.shape[1]), F32)], 0)
    vals = cand
    cnt = jnp.zeros_like(cand[0:1])
    tau = jnp.full_like(cand[0:1], ninf)
    for _ in range(PEER_TOPK):
        m = jnp.max(vals, axis=0, keepdims=True)
        eq = vals == m
        tau = jnp.where(cnt < PEER_TOPK, m, tau)
        cnt = cnt + jnp.sum(jnp.where(eq, mult, 0.0), axis=0, keepdims=True)
        vals = jnp.where(eq, ninf, vals)
    cmax = cand[0:1]
    z = jnp.sum(jnp.where(cand >= tau, mult * jnp.exp(cand - cmax), 0.0), axis=0, keepdims=True)
    g1 = jnp.exp(s1 - ta[0:1]) * (1.0 / z)
    g2 = jnp.exp(s2 - tb[0:1])
    for lc in range(s1.shape[1] // PEER_LANES):
        ls = slice(lc * PEER_LANES, (lc + 1) * PEER_LANES)
        s1_o[lc] = s1[:, ls]
        s2_o[lc] = s2[:, ls]
        s1l_o[lc] = g1[:, ls]
        s2l_o[lc] = g2[:, ls]
        tau_o[lc] = tau[:, ls]


PEER_LANES = 128
TOPK_TT = 1024


def _peer_topk(qT, keys):
    t = qT.shape[1]
    tt = min(TOPK_TT, t)
    nl = tt // PEER_LANES
    blk = pl.BlockSpec((None, nl, PEER_NKEYS, PEER_LANES), lambda j, h: (h, j, 0, 0))
    shp = jax.ShapeDtypeStruct((PEER_HEADS, t // PEER_LANES, PEER_NKEYS, PEER_LANES), F32)
    return pl.pallas_call(
        _peer_topk_body, grid=(t // tt, PEER_HEADS),
        in_specs=[pl.BlockSpec((PEER_QDIM, tt), lambda j, h: (h, j)),
                  pl.BlockSpec((None, 2, PEER_NKEYS, PEER_HALF), lambda j, h: (h, 0, 0, 0))],
        out_specs=[blk, blk, blk, blk, pl.BlockSpec((None, nl, 1, PEER_LANES), lambda j, h: (h, j, 0, 0))],
        out_shape=[shp, shp, shp, shp, jax.ShapeDtypeStruct((PEER_HEADS, t // PEER_LANES, 1, PEER_LANES), F32)],
        compiler_params=_cp(2, 32), name="peer_topk")(qT, keys)


PEER_EK = 512
PEER_SUBS = PEER_EK // PEER_NKEYS
PEER_PIECES = 8
PEER_KPIECES = 4


def _peer_dense_body(xT_ref, u_ref, vT_ref, s1_ref, s2_ref, s1l_ref, s2l_ref, tau_ref, o_ref, hbuf, awbuf, *, nk):
    g = pl.program_id(0)
    d = o_ref.shape[0]
    h0, h1, aw0, aw1 = hbuf.at[0], hbuf.at[1], awbuf.at[0], awbuf.at[1]

    @pl.when(g == 0)
    def _():
        hbuf[...] = jnp.zeros(hbuf.shape, F32)
        awbuf[...] = jnp.zeros(awbuf.shape, BF16)

    @pl.when((g <= 2) | ((g - 2) % nk == 0))
    def _():
        o_ref[...] = jnp.zeros(o_ref.shape, F32)

    def stages(h_cur, h_prev, aw_cur, aw_prev):
        base = (jnp.clip(g - 1, 0, pl.num_programs(0) - 3) % nk) * PEER_SUBS
        dq = d // PEER_PIECES

        def second_matmul(q):
            ds_ = slice(q * dq, (q + 1) * dq)
            o_ref[ds_, :] += jnp.dot(vT_ref[ds_, :], aw_cur[...], preferred_element_type=F32)

        kq = d // PEER_KPIECES

        def first_matmul(q):
            ks = slice(q * kq, (q + 1) * kq)
            part = jnp.dot(u_ref[:, ks], xT_ref[ks, :], preferred_element_type=F32)
            if q == 0:
                h_cur[...] = part
            else:
                h_cur[...] += part

        def gate_tile(ii, lc):
            i1 = pl.ds(base + ii, 1)
            w = None
            for h in range(PEER_HEADS):
                c = s2_ref[h, lc] + s1_ref[h, lc, i1, :]
                wh = jnp.where(c >= tau_ref[h, lc], s2l_ref[h, lc] * s1l_ref[h, lc, i1, :], 0.0)
                w = wh if w is None else w + wh
            rs = slice(ii * PEER_NKEYS, (ii + 1) * PEER_NKEYS)
            ls = slice(lc * PEER_LANES, (lc + 1) * PEER_LANES)
            aw_prev[rs, ls] = (_gelu(h_prev[rs, ls]) * w).astype(BF16)

        tiles = [(ii, lc) for ii in range(PEER_SUBS) for lc in range(o_ref.shape[1] // PEER_LANES)]
        per = -(-len(tiles) // (2 * PEER_PIECES))
        for q in range(PEER_PIECES):
            second_matmul(q)
            for tl in tiles[(2 * q) * per:(2 * q + 1) * per]:
                gate_tile(*tl)
            if q % (PEER_PIECES // PEER_KPIECES) == 0:
                first_matmul(q // (PEER_PIECES // PEER_KPIECES))
            for tl in tiles[(2 * q + 1) * per:(2 * q + 2) * per]:
                gate_tile(*tl)

    @pl.when(g % 2 == 0)
    def _():
        stages(h0, h1, aw0, aw1)

    @pl.when(g % 2 == 1)
    def _():
        stages(h1, h0, aw1, aw0)


def _peer_dense(xmT, u_bf, vT_bf, tk):
    d, t = xmT.shape
    tt = min(512, t)
    nl = tt // PEER_LANES
    nk = PEER_EXPERTS // PEER_EK
    s1, s2, s1l, s2l, tau = tk
    last = (t // tt) * nk - 1
    pair = lambda g, lag: jnp.clip(g - lag, 0, last)
    sblk = pl.BlockSpec((PEER_HEADS, nl, PEER_NKEYS, PEER_LANES), lambda g: (0, pair(g, 1) // nk, 0, 0))
    return pl.pallas_call(
        functools.partial(_peer_dense_body, nk=nk), grid=(last + 3,),
        in_specs=[pl.BlockSpec((d, tt), lambda g: (0, pair(g, 0) // nk)),
                  pl.BlockSpec((PEER_EK, d), lambda g: (pair(g, 0) % nk, 0)),
                  pl.BlockSpec((d, PEER_EK), lambda g: (0, pair(g, 2) % nk)),
                  sblk, sblk, sblk, sblk,
                  pl.BlockSpec((PEER_HEADS, nl, 1, PEER_LANES), lambda g: (0, pair(g, 1) // nk, 0, 0))],
        out_specs=pl.BlockSpec((d, tt), lambda g: (0, pair(g, 2) // nk)),
        out_shape=jax.ShapeDtypeStruct((d, t), F32),
        scratch_shapes=[pltpu.VMEM((2, PEER_EK, tt), F32), pltpu.VMEM((2, PEER_EK, tt), BF16)],
        compiler_params=_cp(1, 48), name="peer_dense")(xmT, u_bf, vT_bf, s1, s2, s1l, s2l, tau)


def _ffn_ln_body(fT_ref, x1_ref, g2_ref, lng_ref, lnb_ref, scn_ref, shn_ref, x2_ref, *rest, alpha):
    x2 = _ln(alpha * x1_ref[...] + g2_ref[...] * fT_ref[...].T, lng_ref[...], lnb_ref[...])
    x2_ref[...] = x2
    if rest:
        rest[0][...] = (x2 * (1.0 + scn_ref[...]) + shn_ref[...]).astype(rest[0].dtype)


def _ffn_ln(fT, x1, g2, lng, lnb, scn, shn, alpha, emit_h):
    t, d = x1.shape
    tm = min(256, t)
    row = pl.BlockSpec((tm, d), lambda i: (i, 0))
    vec = pl.BlockSpec((1, d), lambda i: (0, 0))
    out_specs = [row, row] if emit_h else [row]
    out_shape = [jax.ShapeDtypeStruct((t, d), F32)] + ([jax.ShapeDtypeStruct((t, d), BF16)] if emit_h else [])
    return pl.pallas_call(
        functools.partial(_ffn_ln_body, alpha=alpha), grid=(t // tm,),
        in_specs=[pl.BlockSpec((d, tm), lambda i: (0, i)), row, vec, vec, vec, vec, vec],
        out_specs=out_specs, out_shape=out_shape,
        compiler_params=_cp(1, 32), name="ffn_ln")(fT, x1, g2, lng, lnb, scn, shn)


def kernel(x, c, ctx, c_ctx, w_ada, b_ada, w_in, b_in, s5_lam_re, s5_lam_im, s5_log_dt, s5_b_re, s5_b_im, s5_c_re, s5_c_im, s5_d, w_glu, b_glu, qn_gain, kn_gain, sink, sg_ln_g, sg_ln_b, w_sp, b_sp, w_br_a, w_br_b, w_br_c, w_br_d, w_out, ln1_g, ln1_b, ln2_g, ln2_b, w_pq, peer_keys, peer_u, peer_v):
    depth = w_in.shape[0]
    bsz, n_lat, d = x.shape
    assert bsz == 1 and d == D_MODEL and n_lat % 512 == 0 and ctx.shape[1] % BLOCK == 0
    n_ctx = ctx.shape[1]
    alpha = (2 * depth) ** 0.25
    x, ctx = x[0], ctx[0]

    row8 = lax.broadcasted_iota(jnp.int32, (8, d), 0)
    cond8 = jnp.where(row8 == 0, c[0][None], jnp.where(row8 == 1, c_ctx[None], 0.0))
    ada = [_ada(cond8, w_ada, b_ada, l) for l in range(depth)]
    cosf, sinf = _rope_tables(n_lat)
    cos1, sin0 = jnp.ones((n_ctx, HEAD_DIM), F32), jnp.zeros((n_ctx, HEAD_DIM), F32)
    vec = lambda a: a.reshape(1, -1).astype(F32)

    ref_off = {'a': 0, 'qg': 768, 'kg': 1792, 'vg': 2048, 'qw': 2304, 'kw': 3328, 'vw': 3584, 'du': 3840, 'dv': 4608}
    ref_w = {'a': 768, 'qg': 1024, 'kg': 256, 'vg': 256, 'qw': 1024, 'kw': 256, 'vw': 256, 'du': 768, 'dv': 768}
    order = ['a', 'kg', 'qg', 'qw', 'kw', 'vg', 'vw', 'du', 'dv']
    perm = lambda a: jnp.concatenate([a[..., ref_off[n]:ref_off[n] + ref_w[n]] for n in order], -1)
    perm_idx = np.concatenate([np.arange(ref_off[n], ref_off[n] + ref_w[n]) for n in order]).astype(np.int32)

    h_lat = None
    for l in range(depth):
        need_ctx = l < depth - 1
        mods = [[ada[l][r:r + 1, i * d:(i + 1) * d] for i in range(6)] for r in range(2)]
        (sh1, sc1, g1, sh2, sc2, g2), (sh1c, sc1c, g1c, sh2c, sc2c, g2c) = mods
        if need_ctx:
            nxt = [ada[l + 1][r:r + 1, 0:2 * d] for r in range(2)]
            (shn, scn), (shnc, scnc) = [(m[:, :d], m[:, d:]) for m in nxt]
        else:
            shn = scn = shnc = scnc = jnp.zeros((1, d), F32)

        w_small, b_small = perm(w_in[l][:, :SMALL_W]).astype(BF16), vec(jnp.take(b_in[l], perm_idx))
        w_gate, b_gate = w_in[l][:, SMALL_W:].astype(BF16), vec(b_in[l][SMALL_W:])
        s5m = [_s5_mats(s5_lam_re[l, dr], s5_lam_im[l, dr], s5_log_dt[l, dr], s5_b_re[l, dr], s5_b_im[l, dr],
                        s5_c_re[l, dr], s5_c_im[l, dr]) for dr in range(2)]
        wg, bg, s5d = w_glu[l].astype(BF16), vec(b_glu[l]), vec(s5_d[l])
        qn, kn = vec(qn_gain[l]), vec(kn_gain[l])
        wsp, bsp = w_sp[l].astype(BF16), b_sp[l].astype(F32)[:, :, None]
        w_brs = [w[l].astype(BF16) for w in (w_br_a, w_br_b, w_br_c, w_br_d)]
        wo = w_out[l].astype(BF16)
        wpqT = w_pq[l].T.astype(BF16)
        keys = peer_keys[l].astype(BF16)
        u_bf, vT_bf = peer_u[l].astype(BF16), peer_v[l].T.astype(BF16)

        if h_lat is None:
            h_lat, h_ctx = _modulate(x, sc1, sh1), _modulate(ctx, sc1c, sh1c)

        proj = _matmul(h_lat, w_small, b_small, tm=512, tn=1792, out_dtype=BF16, name="in_proj")
        gates = _matmul(h_lat, w_gate, b_gate, tm=512, tn=2048, out_dtype=BF16, name="in_proj_gates")
        proj_c = _matmul(h_ctx, w_small, b_small, tm=512, tn=1792, out_dtype=BF16, name="in_proj_ctx")

        y_dir, y_dir_c = [], []
        for dr in range(2):
            yc_, hc_end = _s5_dir(proj_c, s5m[dr], jnp.zeros((2, S5_CH), F32), bool(dr))
            yl_, _ = _s5_dir(proj, s5m[dr], hc_end, bool(dr))
            y_dir.append(yl_)
            y_dir_c.append(yc_)
        y_a = _s5_out(proj, y_dir[0], y_dir[1], s5d, wg, bg)

        qg, kg, vge, qw, kw = _prep(proj, cosf, sinf, qn, kn, ATTN_SCALE * math.log2(math.e))
        qgc, kgc, vgce, qwc, kwc = _prep(proj_c, cos1, sin0, qn, kn, ATTN_SCALE)
        y_b = _flash(qg, kg, vge, kgc, vgce)
        y_c = _window(qw, kw, proj, kwc, proj_c, sink[l])

        y_d = _spatial_gate(proj, vec(sg_ln_g[l]), vec(sg_ln_b[l]), wsp, bsp)

        merged = _merge((y_a, y_b, y_c, y_d), w_brs, gates)
        x1, xmT = _out_ln(merged, wo, x, g1, vec(ln1_g[l]), vec(ln1_b[l]), sc2, sh2, alpha)
        qT = _matmul(wpqT, xmT, None, tm=D_MODEL, tn=1024, out_dtype=F32, name="peer_query")
        ffnT = _peer_dense(xmT, u_bf, vT_bf, _peer_topk(qT, keys))
        res = _ffn_ln(ffnT, x1, g2, vec(ln2_g[l]), vec(ln2_b[l]), scn, shn, alpha, need_ctx)
        x = res[0]

        if need_ctx:
            h_lat = res[1]
            gates_c = _matmul(h_ctx, w_gate, b_gate, tm=512, tn=2048, out_dtype=BF16, name="in_proj_gates_ctx")
            y_a_c = _s5_out(proj_c, y_dir_c[0], y_dir_c[1], s5d, wg, bg)
            y_b_c = _ctx_attn(qgc, kgc, proj_c, COL_VG, sink[l], False)
            y_c_c = _ctx_attn(qwc, kwc, proj_c, COL_VW, sink[l], True)
            y_d_c = _spatial_gate(proj_c, vec(sg_ln_g[l]), vec(sg_ln_b[l]), wsp, bsp)
            merged_c = _merge((y_a_c, y_b_c, y_c_c, y_d_c), w_brs, gates_c)
            c1, cmT = _out_ln(merged_c, wo, ctx, g1c, vec(ln1_g[l]), vec(ln1_b[l]), sc2c, sh2c, alpha)
            qTc = _matmul(wpqT, cmT, None, tm=D_MODEL, tn=1024, out_dtype=F32, name="peer_query_ctx")
            ffnTc = _peer_dense(cmT, u_bf, vT_bf, _peer_topk(qTc, keys))
            ctx, h_ctx = _ffn_ln(ffnTc, c1, g2c, vec(ln2_g[l]), vec(ln2_b[l]), scnc, shnc, alpha, True)
    return x[None]
```

```python
import functools
import math

import jax
import jax.numpy as jnp
import numpy as np
from jax import lax
from jax.experimental import pallas as pl
from jax.experimental.pallas import tpu as pltpu

F32, BF16 = jnp.float32, jnp.bfloat16

D_MODEL = 2048
GRID_W = 64
BLOCK = 128
HEAD_DIM = 128
AXIS_DIM = HEAD_DIM // 2
ROPE_THETA = 10000.0
ATTN_SCALE = HEAD_DIM ** -0.5
NEG_INF = -1e30
EPS = 1e-6
S5_GROUPS, S5_GROUP_CH, S5_STATE = 48, 16, 64
S5_WIDTH = S5_GROUPS * S5_GROUP_CH
S5_CH = S5_GROUPS * S5_STATE
S5_GB = 16
S5_NB = S5_GROUPS // S5_GB
S5_KB = S5_GB * S5_GROUP_CH
S5_PB = S5_GB * S5_STATE
GA_HEADS, GA_KV = 8, 2
WA_HEADS, WA_KV = 8, 2
KV_GROUP = GA_HEADS // GA_KV
SG_GROUPS, SG_GROUP_CH = 6, 128
SG_WIDTH = SG_GROUPS * SG_GROUP_CH
CHUNK = 128
N_BRANCH = 4
PEER_HEADS, PEER_NKEYS, PEER_QDIM, PEER_TOPK = 8, 128, 256, 16
PEER_HALF = PEER_QDIM // 2
PEER_EXPERTS = PEER_NKEYS * PEER_NKEYS

COL_A, COL_KG, COL_QG, COL_QW, COL_KW, COL_VG, COL_VW, COL_DU, COL_DV = 0, 768, 1024, 2048, 3072, 3328, 3584, 3840, 4608
SMALL_W = 5376
GATE_W = N_BRANCH * D_MODEL

VMEM_CAP_MB = 56


def _cp(ndims, vmem_mb=32):
    return pltpu.CompilerParams(dimension_semantics=("arbitrary",) * ndims,
                                vmem_limit_bytes=min(vmem_mb, VMEM_CAP_MB) << 20)


def _gelu(y):
    return 0.5 * y * (1.0 + lax.erf(y * (2.0 ** -0.5)))


def _ln(y, g, b):
    mu = jnp.mean(y, -1, keepdims=True)
    yc = y - mu
    var = jnp.mean(yc * yc, -1, keepdims=True)
    return yc * lax.rsqrt(var + EPS) * g + b


_NT = (((1,), (1,)), ((), ()))


def _ada_body(c_ref, w_ref, b_ref, o_ref):
    cnd = c_ref[...]
    s = (cnd * jax.nn.sigmoid(cnd)).astype(BF16)
    o_ref[...] = jnp.dot(s, w_ref[...].astype(BF16), preferred_element_type=F32) + b_ref[...]


def _ada(cond8, w_ada, b_ada, l):
    depth, d, n = w_ada.shape
    tn = 1024
    return pl.pallas_call(
        _ada_body, grid=(n // tn,),
        in_specs=[pl.BlockSpec((8, d), lambda j: (0, 0)),
                  pl.BlockSpec((None, d, tn), lambda j: (l, 0, j)),
                  pl.BlockSpec((None, 1, tn), lambda j: (l, 0, j))],
        out_specs=pl.BlockSpec((8, tn), lambda j: (0, j)),
        out_shape=jax.ShapeDtypeStruct((8, n), F32),
        compiler_params=_cp(1, 32), name="ada_mod")(cond8, w_ada, b_ada.reshape(depth, 1, n))


def _mod_body(x_ref, sc_ref, sh_ref, o_ref):
    o_ref[...] = (x_ref[...] * (1.0 + sc_ref[...]) + sh_ref[...]).astype(o_ref.dtype)


def _modulate(x, sc, sh):
    t, d = x.shape
    tm = min(512, t)
    return pl.pallas_call(
        _mod_body, grid=(t // tm,),
        in_specs=[pl.BlockSpec((tm, d), lambda i: (i, 0)),
                  pl.BlockSpec((1, d), lambda i: (0, 0)),
                  pl.BlockSpec((1, d), lambda i: (0, 0))],
        out_specs=pl.BlockSpec((tm, d), lambda i: (i, 0)),
        out_shape=jax.ShapeDtypeStruct((t, d), BF16),
        compiler_params=_cp(1, 24), name="modulate")(x, sc, sh)


def _mm_body(a_ref, w_ref, b_ref, o_ref):
    acc = jnp.dot(a_ref[...], w_ref[...], preferred_element_type=F32)
    o_ref[...] = (acc + b_ref[...]).astype(o_ref.dtype)


def _mm_nobias_body(a_ref, w_ref, o_ref):
    o_ref[...] = jnp.dot(a_ref[...], w_ref[...], preferred_element_type=F32).astype(o_ref.dtype)


def _matmul(a, w, b, *, tm, tn, out_dtype, name):
    m, k = a.shape
    n = w.shape[1]
    tm, tn = min(tm, m), min(tn, n)
    in_specs = [pl.BlockSpec((tm, k), lambda j, i: (i, 0)),
                pl.BlockSpec((k, tn), lambda j, i: (0, j))]
    args = [a, w]
    body = _mm_nobias_body
    if b is not None:
        in_specs.append(pl.BlockSpec((1, tn), lambda j, i: (0, j)))
        args.append(b)
        body = _mm_body
    osz = jnp.dtype(out_dtype).itemsize
    vmem = 2 * (tm * k * 2 + k * tn * 2 + tm * tn * osz) + tm * tn * 4
    return pl.pallas_call(
        body, grid=(n // tn, m // tm), in_specs=in_specs,
        out_specs=pl.BlockSpec((tm, tn), lambda j, i: (i, j)),
        out_shape=jax.ShapeDtypeStruct((m, n), out_dtype),
        compiler_params=_cp(2, (vmem >> 20) + 8), name=name)(*args)


S5_TC = 256
S5_SEG = 8
S5_L = S5_TC // S5_SEG


def _s5_body(u_ref, perm_ref, permT_ref, bre_ref, bim_ref, cre_ref, cim_ref, apow_ref, h0_ref, y_ref, hT_ref,
             xr, xi, carry, hin_r, hin_i, *, rev, seg_len):
    @pl.when(pl.program_id(0) == 0)
    def _():
        carry[...] = h0_ref[...]

    up = jnp.dot(perm_ref[...], u_ref[...], preferred_element_type=F32).astype(BF16)
    for j in range(S5_NB):
        uj = up[:, j * S5_KB:(j + 1) * S5_KB]
        xr[:, j * S5_PB:(j + 1) * S5_PB] = jnp.dot(uj, bre_ref[j], preferred_element_type=F32)
        xi[:, j * S5_PB:(j + 1) * S5_PB] = jnp.dot(uj, bim_ref[j], preferred_element_type=F32)

    seg_order = range(S5_SEG - 1, -1, -1) if rev else range(S5_SEG)
    for j in range(S5_NB):
        cs = slice(j * S5_PB, (j + 1) * S5_PB)
        ar = jnp.broadcast_to(apow_ref[0, 0:1, cs], (S5_SEG, S5_PB))
        ai = jnp.broadcast_to(apow_ref[1, 0:1, cs], (S5_SEG, S5_PB))

        def local_step(k, c, cs=cs, ar=ar, ai=ai):
            hr, hi = c
            rows = pl.ds(pl.multiple_of(k * S5_SEG, S5_SEG), S5_SEG)
            nr = ar * hr - ai * hi + xr[rows, cs]
            ni = ar * hi + ai * hr + xi[rows, cs]
            xr[rows, cs] = nr
            xi[rows, cs] = ni
            return nr, ni

        zero = jnp.zeros((S5_SEG, S5_PB), F32)
        er, ei = lax.fori_loop(0, seg_len, local_step, (zero, zero), unroll=4)

        alr, ali = apow_ref[0, seg_len - 1:seg_len, cs], apow_ref[1, seg_len - 1:seg_len, cs]
        hr, hi = carry[0:1, cs], carry[1:2, cs]
        for s in seg_order:
            hin_r[s:s + 1, cs] = hr
            hin_i[s:s + 1, cs] = hi
            hr, hi = (er[s:s + 1] + alr * hr - ali * hi, ei[s:s + 1] + alr * hi + ali * hr)
        carry[0:1, cs] = hr
        carry[1:2, cs] = hi

        def fix_step(k, c, cs=cs):
            rows = pl.ds(pl.multiple_of(k * S5_SEG, S5_SEG), S5_SEG)
            pr, pi = apow_ref[0, pl.ds(k, 1), cs], apow_ref[1, pl.ds(k, 1), cs]
            h_r, h_i = hin_r[:, cs], hin_i[:, cs]
            xr[rows, cs] += pr * h_r - pi * h_i
            xi[rows, cs] += pr * h_i + pi * h_r
            return c

        lax.fori_loop(0, seg_len, fix_step, 0, unroll=4)

    yp = jnp.concatenate(
        [jnp.dot(xr[:, j * S5_PB:(j + 1) * S5_PB].astype(BF16), cre_ref[j], preferred_element_type=F32)
         + jnp.dot(xi[:, j * S5_PB:(j + 1) * S5_PB].astype(BF16), cim_ref[j], preferred_element_type=F32)
         for j in range(S5_NB)], axis=1)
    hi = yp.astype(BF16)
    r1 = yp - hi.astype(F32)
    mid = r1.astype(BF16)
    lo = (r1 - mid.astype(F32)).astype(BF16)
    parts = jnp.dot(permT_ref[...], jnp.concatenate([hi, mid, lo], axis=1), preferred_element_type=F32)
    y_ref[...] = parts[:, :S5_WIDTH] + parts[:, S5_WIDTH:2 * S5_WIDTH] + parts[:, 2 * S5_WIDTH:]
    hT_ref[...] = carry[...]


def _s5_perm(rev, tc):
    seg_len = tc // S5_SEG
    p = np.zeros((tc, tc), np.float32)
    for k in range(seg_len):
        for s in range(S5_SEG):
            p[k * S5_SEG + s, s * seg_len + ((seg_len - 1 - k) if rev else k)] = 1.0
    return jnp.asarray(p, BF16), jnp.asarray(p.T, BF16)


def _s5_dir(proj, mats, h0, rev):
    t = proj.shape[0]
    tc = min(S5_TC, t)
    assert t % tc == 0 and tc % (8 * S5_SEG) == 0
    nt = t // tc
    bre, bim, cre, cim, apow = mats
    tmap = (lambda i: (nt - 1 - i, 0)) if rev else (lambda i: (i, 0))
    full3 = lambda i: (0, 0, 0)
    full2 = lambda i: (0, 0)
    return pl.pallas_call(
        functools.partial(_s5_body, rev=rev, seg_len=tc // S5_SEG), grid=(nt,),
        in_specs=[pl.BlockSpec((tc, S5_WIDTH), tmap),
                  pl.BlockSpec((tc, tc), full2), pl.BlockSpec((tc, tc), full2),
                  pl.BlockSpec((S5_NB, S5_KB, S5_PB), full3), pl.BlockSpec((S5_NB, S5_KB, S5_PB), full3),
                  pl.BlockSpec((S5_NB, S5_PB, S5_KB), full3), pl.BlockSpec((S5_NB, S5_PB, S5_KB), full3),
                  pl.BlockSpec((2, S5_L, S5_CH), full3), pl.BlockSpec((2, S5_CH), full2)],
        out_specs=[pl.BlockSpec((tc, S5_WIDTH), tmap), pl.BlockSpec((2, S5_CH), full2)],
        out_shape=[jax.ShapeDtypeStruct((t, S5_WIDTH), F32), jax.ShapeDtypeStruct((2, S5_CH), F32)],
        scratch_shapes=[pltpu.VMEM((tc, S5_CH), F32), pltpu.VMEM((tc, S5_CH), F32), pltpu.VMEM((2, S5_CH), F32),
                        pltpu.VMEM((S5_SEG, S5_CH), F32), pltpu.VMEM((S5_SEG, S5_CH), F32)],
        compiler_params=_cp(1, 48), name="s5_scan_bwd" if rev else "s5_scan_fwd")(
            proj, *_s5_perm(rev, tc), bre, bim, cre, cim, apow, h0)


def _s5_out_body(u_ref, yf_ref, yb_ref, d_ref, wg_ref, bg_ref, o_ref):
    y = u_ref[...].astype(F32) * d_ref[...] + yf_ref[...] + yb_ref[...]
    g = _gelu(y)
    z = jnp.dot(g.astype(BF16), wg_ref[...], preferred_element_type=F32) + bg_ref[...]
    o_ref[...] = (g * jax.nn.sigmoid(z)).astype(o_ref.dtype)


def _s5_out(proj, yf, yb, d, wg, bg):
    t = proj.shape[0]
    tm = min(512, t)
    row = lambda i: (i, 0)
    full = lambda i: (0, 0)
    return pl.pallas_call(
        _s5_out_body, grid=(t // tm,),
        in_specs=[pl.BlockSpec((tm, S5_WIDTH), row), pl.BlockSpec((tm, S5_WIDTH), row), pl.BlockSpec((tm, S5_WIDTH), row),
                  pl.BlockSpec((1, S5_WIDTH), full), pl.BlockSpec((S5_WIDTH, S5_WIDTH), full), pl.BlockSpec((1, S5_WIDTH), full)],
        out_specs=pl.BlockSpec((tm, S5_WIDTH), row),
        out_shape=jax.ShapeDtypeStruct((t, S5_WIDTH), BF16),
        compiler_params=_cp(1, 24), name="s5_glu")(proj, yf, yb, d, wg, bg)


def _s5_mats(lam_re, lam_im, log_dt, b_re, b_im, c_re, c_im):
    dt = jnp.exp(log_dt)[:, None]
    mag = jnp.exp(lam_re * dt)
    ar, ai = mag * jnp.cos(lam_im * dt), mag * jnp.sin(lam_im * dt)
    nr, ni = ar - 1.0, ai
    den = lam_re * lam_re + lam_im * lam_im
    cr = (nr * lam_re + ni * lam_im) / den
    ci = (ni * lam_re - nr * lam_im) / den
    fre = cr[:, :, None] * b_re - ci[:, :, None] * b_im
    fim = cr[:, :, None] * b_im + ci[:, :, None] * b_re
    eye = jnp.eye(S5_GB, dtype=F32)

    def blk_in(f):
        f = f.reshape(S5_NB, S5_GB, S5_STATE, S5_GROUP_CH)
        m = jnp.einsum('jgph,gk->jghkp', f, eye)
        return m.reshape(S5_NB, S5_KB, S5_PB).astype(BF16)

    def blk_out(c):
        c = c.reshape(S5_NB, S5_GB, S5_GROUP_CH, S5_STATE)
        m = jnp.einsum('jghp,gk->jgpkh', c, eye)
        return m.reshape(S5_NB, S5_PB, S5_KB).astype(BF16)

    kk = jnp.arange(1, S5_L + 1, dtype=F32)[:, None, None]
    magk, angk = jnp.exp(lam_re * dt * kk), lam_im * dt * kk
    is_re = lax.broadcasted_iota(jnp.int32, (2, S5_L, S5_CH), 0) == 0
    apow = jnp.where(is_re, (magk * jnp.cos(angk)).reshape(1, S5_L, S5_CH), (magk * jnp.sin(angk)).reshape(1, S5_L, S5_CH))
    return blk_in(fre), blk_in(fim), blk_out(c_re), blk_out(-c_im), apow


def _prep_body(qg_ref, kg_ref, vg_ref, qw_ref, kw_ref, cos_ref, sin_ref, qn_ref, kn_ref, oqg, okg, ovg, oqw, okw, *,
               qg_scale):
    cosf, sinf = cos_ref[...], sin_ref[...]
    lane = lax.broadcasted_iota(jnp.int32, cosf.shape, 1)
    low = (lane % AXIS_DIM) < (AXIS_DIM // 2)

    def rope(t):
        partner = jnp.where(low, pltpu.roll(t, HEAD_DIM - AXIS_DIM // 2, 1), pltpu.roll(t, AXIS_DIM // 2, 1))
        return t * cosf + partner * sinf

    def rms(t, g):
        return t * lax.rsqrt(jnp.mean(t * t, -1, keepdims=True) + EPS) * g

    for h in range(GA_HEADS):
        cs = slice(h * HEAD_DIM, (h + 1) * HEAD_DIM)
        oqg[:, cs] = (rope(rms(qg_ref[:, cs].astype(F32), qn_ref[...])) * qg_scale).astype(BF16)
        oqw[:, cs] = (rope(qw_ref[:, cs].astype(F32)) * ATTN_SCALE).astype(BF16)
    for h in range(GA_KV):
        cs = slice(h * HEAD_DIM, (h + 1) * HEAD_DIM)
        okg[:, cs] = rope(rms(kg_ref[:, cs].astype(F32), kn_ref[...])).astype(BF16)
        okw[:, cs] = rope(kw_ref[:, cs].astype(F32)).astype(BF16)
        ovg[:, 2 * h * HEAD_DIM:(2 * h + 1) * HEAD_DIM] = vg_ref[:, cs]
        ovg[:, (2 * h + 1) * HEAD_DIM:(2 * h + 2) * HEAD_DIM] = jnp.ones((vg_ref.shape[0], HEAD_DIM), BF16)


def _prep(proj, cosf, sinf, qn, kn, qg_scale):
    t = proj.shape[0]
    tr = min(256, t)
    qw_, kw_ = GA_HEADS * HEAD_DIM, GA_KV * HEAD_DIM
    row = lambda i: (i, 0)
    full = lambda i: (0, 0)
    return pl.pallas_call(
        functools.partial(_prep_body, qg_scale=qg_scale), grid=(t // tr,),
        in_specs=[pl.BlockSpec((tr, qw_), lambda i: (i, COL_QG // qw_)),
                  pl.BlockSpec((tr, kw_), lambda i: (i, COL_KG // kw_)),
                  pl.BlockSpec((tr, kw_), lambda i: (i, COL_VG // kw_)),
                  pl.BlockSpec((tr, qw_), lambda i: (i, COL_QW // qw_)),
                  pl.BlockSpec((tr, kw_), lambda i: (i, COL_KW // kw_)),
                  pl.BlockSpec((tr, HEAD_DIM), row), pl.BlockSpec((tr, HEAD_DIM), row),
                  pl.BlockSpec((1, HEAD_DIM), full), pl.BlockSpec((1, HEAD_DIM), full)],
        out_specs=[pl.BlockSpec((tr, qw_), row), pl.BlockSpec((tr, kw_), row), pl.BlockSpec((tr, 2 * kw_), row),
                   pl.BlockSpec((tr, qw_), row), pl.BlockSpec((tr, kw_), row)],
        out_shape=[jax.ShapeDtypeStruct((t, qw_), BF16), jax.ShapeDtypeStruct((t, kw_), BF16),
                   jax.ShapeDtypeStruct((t, 2 * kw_), BF16),
                   jax.ShapeDtypeStruct((t, qw_), BF16), jax.ShapeDtypeStruct((t, kw_), BF16)],
        compiler_params=_cp(1, 24), name="qk_prep")(proj, proj, proj, proj, proj, cosf, sinf, qn, kn)


def _rope_tables(n_lat):
    rows = n_lat // GRID_W
    row = jnp.repeat(jnp.arange(rows, dtype=F32), GRID_W)
    col = jnp.tile(jnp.arange(GRID_W, dtype=F32), rows)
    inv = jnp.power(ROPE_THETA, -jnp.arange(0, AXIS_DIM, 2, dtype=F32) / AXIS_DIM)
    ang_r, ang_c = row[:, None] * inv, col[:, None] * inv
    cr, sr, cc, sc = jnp.cos(ang_r), jnp.sin(ang_r), jnp.cos(ang_c), jnp.sin(ang_c)
    return jnp.concatenate([cr, cr, cc, cc], -1), jnp.concatenate([-sr, sr, -sc, sc], -1)


def _flash_body(q_ref, kc_ref, vc_ref, k_ref, v_ref, o_ref, qs, m_s, acc, *, tq, tk, nk):
    for g in range(KV_GROUP):
        qs[g * tq:(g + 1) * tq, :] = q_ref[:, g * HEAD_DIM:(g + 1) * HEAD_DIM]
    m_s[...] = jnp.full(m_s.shape, -jnp.inf, F32)
    acc[...] = jnp.zeros(acc.shape, F32)

    def update(k, v):
        for g in range(KV_GROUP):
            rs = slice(g * tq, (g + 1) * tq)
            s = lax.dot_general(qs[rs, :], k, _NT, preferred_element_type=F32)
            m_prev = m_s[rs, :]
            m_next = jnp.maximum(m_prev, jnp.max(s, axis=1, keepdims=True))
            alpha = jnp.exp2(m_prev - m_next)
            p = jnp.exp2(s - jnp.concatenate([m_next] * (s.shape[1] // HEAD_DIM), axis=1))
            acc[rs, :] = (jnp.concatenate([alpha, alpha], axis=1) * acc[rs, :]
                          + jnp.dot(p.astype(BF16), v, preferred_element_type=F32))
            m_s[rs, :] = m_next

    update(kc_ref[...], vc_ref[...])

    def body(j, carry):
        rows = pl.ds(pl.multiple_of(j * tk, tk), tk)
        update(k_ref[rows, :], v_ref[rows, :])
        return carry

    lax.fori_loop(0, nk, body, 0, unroll=FLASH_UNROLL)
    out = acc[:, :HEAD_DIM] / acc[:, HEAD_DIM:]
    for g in range(KV_GROUP):
        o_ref[:, g * HEAD_DIM:(g + 1) * HEAD_DIM] = out[g * tq:(g + 1) * tq, :].astype(o_ref.dtype)


FLASH_TK = 512
FLASH_UNROLL = 32


def _flash(q, kg, vge, kgc, vgce):
    t = q.shape[0]
    n_ctx = kgc.shape[0]
    tq = min(256, t)
    tk = min(FLASH_TK, t)
    gw = KV_GROUP * HEAD_DIM
    head = lambda h, i: (0, h)
    return pl.pallas_call(
        functools.partial(_flash_body, tq=tq, tk=tk, nk=t // tk), grid=(GA_KV, t // tq),
        in_specs=[pl.BlockSpec((tq, gw), lambda h, i: (i, h)),
                  pl.BlockSpec((n_ctx, HEAD_DIM), head), pl.BlockSpec((n_ctx, 2 * HEAD_DIM), head),
                  pl.BlockSpec((t, HEAD_DIM), head), pl.BlockSpec((t, 2 * HEAD_DIM), head)],
        out_specs=pl.BlockSpec((tq, gw), lambda h, i: (i, h)),
        out_shape=jax.ShapeDtypeStruct((t, GA_HEADS * HEAD_DIM), BF16),
        scratch_shapes=[pltpu.VMEM((KV_GROUP * tq, HEAD_DIM), BF16), pltpu.VMEM((KV_GROUP * tq, HEAD_DIM), F32),
                        pltpu.VMEM((KV_GROUP * tq, 2 * HEAD_DIM), F32)],
        compiler_params=_cp(2, 48), name="global_attn")(q, kgc, vgce, kg, vge)


WIN_QB = 4


def _win_body(q_ref, km_ref, k0_ref, kp_ref, vm_ref, v0_ref, vp_ref, kc_ref, vc_ref, sink_ref, o_ref, *, nb):
    kslab = jnp.concatenate([km_ref[...], k0_ref[...], kp_ref[...]], axis=0)
    vslab = jnp.concatenate([vm_ref[...], v0_ref[...], vp_ref[...]], axis=0)
    sink = sink_ref[0]
    shape = (KV_GROUP * BLOCK, 3 * BLOCK)
    r = lax.broadcasted_iota(jnp.int32, shape, 0) % BLOCK
    c = lax.broadcasted_iota(jnp.int32, shape, 1)
    cc = c % BLOCK
    for b in range(WIN_QB):
        i = pl.program_id(1) * WIN_QB + b
        rows = slice(b * BLOCK, (b + 1) * BLOCK)
        q = jnp.concatenate([q_ref[rows, g * HEAD_DIM:(g + 1) * HEAD_DIM] for g in range(KV_GROUP)], axis=0)
        kw, vw = kslab[b * BLOCK:(b + 3) * BLOCK], vslab[b * BLOCK:(b + 3) * BLOCK]
        s_w = lax.dot_general(q, kw, _NT, preferred_element_type=F32)
        iv = jnp.full(shape, i, jnp.int32)
        valid = (((c < BLOCK) & (cc >= r) & (iv >= 1)) | ((c >= BLOCK) & (c < 2 * BLOCK))
                 | ((c >= 2 * BLOCK) & (cc <= r) & (iv + 1 < nb)))
        s_w = jnp.where(valid, s_w, NEG_INF)
        s_c = lax.dot_general(q, kc_ref[...], _NT, preferred_element_type=F32)
        m = jnp.maximum(jnp.maximum(jnp.max(s_w, -1, keepdims=True), jnp.max(s_c, -1, keepdims=True)), sink)
        pw, pc = jnp.exp(s_w - m), jnp.exp(s_c - m)
        l = jnp.sum(pw, -1, keepdims=True) + jnp.sum(pc, -1, keepdims=True) + jnp.exp(sink - m)
        inv = 1.0 / l
        out = (jnp.dot((pc * inv).astype(BF16), vc_ref[...], preferred_element_type=F32)
               + jnp.dot((pw * inv).astype(BF16), vw, preferred_element_type=F32))
        for g in range(KV_GROUP):
            o_ref[rows, g * HEAD_DIM:(g + 1) * HEAD_DIM] = out[g * BLOCK:(g + 1) * BLOCK, :].astype(o_ref.dtype)


def _sink_rows(sink, rows_per_head):
    return jnp.repeat(sink.astype(F32).reshape(WA_KV, KV_GROUP), rows_per_head, axis=1)[:, :, None]


def _window(qw, kw, proj, kwc, proj_c, sink):
    t = qw.shape[0]
    n_ctx = kwc.shape[0]
    nb = t // BLOCK
    gw = KV_GROUP * HEAD_DIM
    vcol = COL_VW // HEAD_DIM
    assert nb % WIN_QB == 0
    edge = lambda f: pl.BlockSpec((BLOCK, HEAD_DIM), f)
    main = lambda f: pl.BlockSpec((WIN_QB * BLOCK, HEAD_DIM), f)
    before = lambda i: jnp.maximum(i * WIN_QB - 1, 0)
    after = lambda i: jnp.minimum((i + 1) * WIN_QB, nb - 1)
    return pl.pallas_call(
        functools.partial(_win_body, nb=nb), grid=(WA_KV, nb // WIN_QB),
        in_specs=[pl.BlockSpec((WIN_QB * BLOCK, gw), lambda h, i: (i, h)),
                  edge(lambda h, i: (before(i), h)), main(lambda h, i: (i, h)), edge(lambda h, i: (after(i), h)),
                  edge(lambda h, i: (before(i), vcol + h)), main(lambda h, i: (i, vcol + h)),
                  edge(lambda h, i: (after(i), vcol + h)),
                  pl.BlockSpec((n_ctx, HEAD_DIM), lambda h, i: (0, h)),
                  pl.BlockSpec((n_ctx, HEAD_DIM), lambda h, i: (0, vcol + h)),
                  pl.BlockSpec((1, KV_GROUP * BLOCK, 1), lambda h, i: (h, 0, 0))],
        out_specs=pl.BlockSpec((WIN_QB * BLOCK, gw), lambda h, i: (i, h)),
        out_shape=jax.ShapeDtypeStruct((t, WA_HEADS * HEAD_DIM), BF16),
        compiler_params=_cp(2, 24), name="window_attn")(
            qw, kw, kw, kw, proj, proj, proj, kwc, proj_c, _sink_rows(sink, BLOCK))


def _ctx_attn_body(q_ref, k_ref, v_ref, sink_ref, o_ref, *, n_ctx, use_sink):
    q = jnp.concatenate([q_ref[:, g * HEAD_DIM:(g + 1) * HEAD_DIM] for g in range(KV_GROUP)], axis=0)
    s = lax.dot_general(q, k_ref[...], _NT, preferred_element_type=F32)
    m = jnp.max(s, -1, keepdims=True)
    if use_sink:
        m = jnp.maximum(m, sink_ref[0])
    p = jnp.exp(s - m)
    l = jnp.sum(p, -1, keepdims=True)
    if use_sink:
        l = l + jnp.exp(sink_ref[0] - m)
    out = jnp.dot((p / l).astype(BF16), v_ref[...], preferred_element_type=F32)
    for g in range(KV_GROUP):
        o_ref[:, g * HEAD_DIM:(g + 1) * HEAD_DIM] = out[g * n_ctx:(g + 1) * n_ctx, :].astype(o_ref.dtype)


def _ctx_attn(q, k, proj_c, vcol0, sink, use_sink):
    n_ctx = q.shape[0]
    gw = KV_GROUP * HEAD_DIM
    vcol = vcol0 // HEAD_DIM
    return pl.pallas_call(
        functools.partial(_ctx_attn_body, n_ctx=n_ctx, use_sink=use_sink), grid=(GA_KV,),
        in_specs=[pl.BlockSpec((n_ctx, gw), lambda h: (0, h)),
                  pl.BlockSpec((n_ctx, HEAD_DIM), lambda h: (0, h)),
                  pl.BlockSpec((n_ctx, HEAD_DIM), lambda h: (0, vcol + h)),
                  pl.BlockSpec((1, KV_GROUP * n_ctx, 1), lambda h: (h, 0, 0))],
        out_specs=pl.BlockSpec((n_ctx, gw), lambda h: (0, h)),
        out_shape=jax.ShapeDtypeStruct((n_ctx, GA_HEADS * HEAD_DIM), BF16),
        compiler_params=_cp(1, 24), name="ctx_attn_sink" if use_sink else "ctx_attn")(
            q, k, proj_c, _sink_rows(sink, n_ctx))


def _sg_body(u_ref, v_ref, g_ref, b_ref, w_ref, bs_ref, o_ref, *, nch):
    for n in range(nch):
        rs = slice(n * CHUNK, (n + 1) * CHUNK)
        vn = _ln(v_ref[rs, :].astype(F32), g_ref[...], b_ref[...]).astype(BF16)
        for g in range(SG_GROUPS):
            cs = slice(g * SG_GROUP_CH, (g + 1) * SG_GROUP_CH)
            s = jnp.dot(w_ref[g], vn[:, cs], preferred_element_type=F32) + bs_ref[g]
            o_ref[rs, cs] = (u_ref[rs, cs].astype(F32) * s).astype(o_ref.dtype)


def _spatial_gate(proj, g_v, b_v, w_sp, b_sp):
    t = proj.shape[0]
    tm = min(512, t)
    full = lambda i: (0, 0)
    return pl.pallas_call(
        functools.partial(_sg_body, nch=tm // CHUNK), grid=(t // tm,),
        in_specs=[pl.BlockSpec((tm, SG_WIDTH), lambda i: (i, COL_DU // SG_WIDTH)),
                  pl.BlockSpec((tm, SG_WIDTH), lambda i: (i, COL_DV // SG_WIDTH)),
                  pl.BlockSpec((1, SG_WIDTH), full), pl.BlockSpec((1, SG_WIDTH), full),
                  pl.BlockSpec((SG_GROUPS, CHUNK, CHUNK), lambda i: (0, 0, 0)),
                  pl.BlockSpec((SG_GROUPS, CHUNK, 1), lambda i: (0, 0, 0))],
        out_specs=pl.BlockSpec((tm, SG_WIDTH), lambda i: (i, 0)),
        out_shape=jax.ShapeDtypeStruct((t, SG_WIDTH), BF16),
        compiler_params=_cp(1, 24), name="spatial_gate")(proj, proj, g_v, b_v, w_sp, b_sp)


def _merge_body(ya, yb, yc, yd, wa, wb, wc, wd, ga, gb, gc, gd, o_ref):
    def br(y, w, g):
        return jax.nn.sigmoid(g[...].astype(F32)) * jnp.dot(y[...], w[...], preferred_element_type=F32)
    o_ref[...] = (br(ya, wa, ga) + br(yb, wb, gb) + br(yc, wc, gc) + br(yd, wd, gd)).astype(o_ref.dtype)


def _merge(ys, ws, gates):
    t = ys[0].shape[0]
    tm, tn = min(512, t), 1024
    nn = D_MODEL // tn
    in_specs = [pl.BlockSpec((tm, y.shape[1]), lambda j, i: (i, 0)) for y in ys]
    in_specs += [pl.BlockSpec((w.shape[0], tn), lambda j, i: (0, j)) for w in ws]
    in_specs += [pl.BlockSpec((tm, tn), functools.partial(lambda j, i, b: (i, b * nn + j), b=b)) for b in range(N_BRANCH)]
    return pl.pallas_call(
        _merge_body, grid=(nn, t // tm), in_specs=in_specs,
        out_specs=pl.BlockSpec((tm, tn), lambda j, i: (i, j)),
        out_shape=jax.ShapeDtypeStruct((t, D_MODEL), BF16),
        compiler_params=_cp(2, 48), name="gated_merge")(*ys, *ws, gates, gates, gates, gates)


def _out_ln_body(mg_ref, w_ref, x_ref, g1_ref, lng_ref, lnb_ref, sc_ref, sh_ref, x1_ref, xmT_ref, *, alpha):
    o = jnp.dot(mg_ref[...], w_ref[...], preferred_element_type=F32)
    x1 = _ln(alpha * x_ref[...] + g1_ref[...] * o, lng_ref[...], lnb_ref[...])
    x1_ref[...] = x1
    xmT_ref[...] = (x1 * (1.0 + sc_ref[...]) + sh_ref[...]).T.astype(xmT_ref.dtype)


def _out_ln(merged, w_out, x, g1, lng, lnb, sc2, sh2, alpha):
    t, d = x.shape
    tm = min(256, t)
    row = lambda i: (i, 0)
    vec = pl.BlockSpec((1, d), lambda i: (0, 0))
    return pl.pallas_call(
        functools.partial(_out_ln_body, alpha=alpha), grid=(t // tm,),
        in_specs=[pl.BlockSpec((tm, d), row), pl.BlockSpec((d, d), lambda i: (0, 0)), pl.BlockSpec((tm, d), row),
                  vec, vec, vec, vec, vec],
        out_specs=[pl.BlockSpec((tm, d), row), pl.BlockSpec((d, tm), lambda i: (0, i))],
        out_shape=[jax.ShapeDtypeStruct((t, d), F32), jax.ShapeDtypeStruct((d, t), BF16)],
        compiler_params=_cp(1, 48), name="out_proj_ln")(merged, w_out, x, g1, lng, lnb, sc2, sh2)


def _peer_topk_body(q_ref, k_ref, s1_o, s2_o, s1l_o, s2l_o, tau_o):
    q = q_ref[...].astype(BF16)
    s1 = jnp.dot(k_ref[0], q[:PEER_HALF], preferred_element_type=F32)
    s2 = jnp.dot(k_ref[1], q[PEER_HALF:], preferred_element_type=F32)
    ninf = -jnp.inf

    def top_distinct(s):
        vals, tops, cnts = s, [], []
        for _ in range(PEER_TOPK):
            m = jnp.max(vals, axis=0, keepdims=True)
            eq = vals == m
            tops.append(m)
            cnts.append(jnp.sum(eq.astype(F32), axis=0, keepdims=True))
            vals = jnp.where(eq, ninf, vals)
        return jnp.concatenate(tops, 0), jnp.concatenate(cnts, 0)

    ta, na = top_distinct(s1)
    tb, nb = top_distinct(s2)
    nrow = [PEER_TOPK // (k + 1) for k in range(PEER_TOPK)]
    pad = -sum(nrow) % 8
    cand = jnp.concatenate([ta[k:k + 1] + tb[:nrow[k]] for k in range(PEER_TOPK)]
                           + [jnp.full((pad, ta.shape[1]), ninf, F32)], 0)
    mult = jnp.concatenate([na[k:k + 1] * nb[:nrow[k]] for k in range(PEER_TOPK)]
                           + [jnp.zeros((pad, ta.shape[1]), F32)], 0)
    vals = cand
    cnt = jnp.zeros_like(cand[0:1])
    tau = jnp.full_like(cand[0:1], ninf)
    for _ in range(PEER_TOPK):
        m = jnp.max(vals, axis=0, keepdims=True)
        eq = vals == m
        tau = jnp.where(cnt < PEER_TOPK, m, tau)
        cnt = cnt + jnp.sum(jnp.where(eq, mult, 0.0), axis=0, keepdims=True)
        vals = jnp.where(eq, ninf, vals)
    cmax = cand[0:1]
    z = jnp.sum(jnp.where(cand >= tau, mult * jnp.exp(cand - cmax), 0.0), axis=0, keepdims=True)
    g1 = jnp.exp(s1 - ta[0:1]) * (1.0 / z)
    g2 = jnp.exp(s2 - tb[0:1])
    for lc in range(s1.shape[1] // PEER_LANES):
        ls = slice(lc * PEER_LANES, (lc + 1) * PEER_LANES)
        s1_o[lc] = s1[:, ls]
        s2_o[lc] = s2[:, ls]
        s1l_o[lc] = g1[:, ls]
        s2l_o[lc] = g2[:, ls]
        tau_o[lc] = tau[:, ls]


PEER_LANES = 128
TOPK_TT = 1024


def _peer_topk(qT, keys):
    t = qT.shape[1]
    tt = min(TOPK_TT, t)
    nl = tt // PEER_LANES
    blk = pl.BlockSpec((None, nl, PEER_NKEYS, PEER_LANES), lambda j, h: (h, j, 0, 0))
    shp = jax.ShapeDtypeStruct((PEER_HEADS, t // PEER_LANES, PEER_NKEYS, PEER_LANES), F32)
    return pl.pallas_call(
        _peer_topk_body, grid=(t // tt, PEER_HEADS),
        in_specs=[pl.BlockSpec((PEER_QDIM, tt), lambda j, h: (h, j)),
                  pl.BlockSpec((None, 2, PEER_NKEYS, PEER_HALF), lambda j, h: (h, 0, 0, 0))],
        out_specs=[blk, blk, blk, blk, pl.BlockSpec((None, nl, 1, PEER_LANES), lambda j, h: (h, j, 0, 0))],
        out_shape=[shp, shp, shp, shp, jax.ShapeDtypeStruct((PEER_HEADS, t // PEER_LANES, 1, PEER_LANES), F32)],
        compiler_params=_cp(2, 32), name="peer_topk")(qT, keys)


PEER_EK = 512
PEER_SUBS = PEER_EK // PEER_NKEYS
PEER_PIECES = 8
PEER_KPIECES = 8


def _peer_dense_body(xT_ref, u_ref, vT_ref, s1_ref, s2_ref, s1l_ref, s2l_ref, tau_ref, o_ref, hbuf, awbuf, *, nk):
    g = pl.program_id(0)
    d = o_ref.shape[0]
    h0, h1, aw0, aw1 = hbuf.at[0], hbuf.at[1], awbuf.at[0], awbuf.at[1]

    @pl.when(g == 0)
    def _():
        hbuf[...] = jnp.zeros(hbuf.shape, F32)
        awbuf[...] = jnp.zeros(awbuf.shape, BF16)

    @pl.when((g <= 2) | ((g - 2) % nk == 0))
    def _():
        o_ref[...] = jnp.zeros(o_ref.shape, F32)

    def stages(h_cur, h_prev, aw_cur, aw_prev):
        base = (jnp.clip(g - 1, 0, pl.num_programs(0) - 3) % nk) * PEER_SUBS
        dq = d // PEER_PIECES

        def second_matmul(q):
            ds_ = slice(q * dq, (q + 1) * dq)
            o_ref[ds_, :] += jnp.dot(vT_ref[ds_, :], aw_cur[...], preferred_element_type=F32)

        kq = d // PEER_KPIECES

        def first_matmul(q):
            ks = slice(q * kq, (q + 1) * kq)
            part = jnp.dot(u_ref[:, ks], xT_ref[ks, :], preferred_element_type=F32)
            if q == 0:
                h_cur[...] = part
            else:
                h_cur[...] += part

        def gate_tile(ii, lc):
            i1 = pl.ds(base + ii, 1)
            w = None
            for h in range(PEER_HEADS):
                c = s2_ref[h, lc] + s1_ref[h, lc, i1, :]
                wh = jnp.where(c >= tau_ref[h, lc], s2l_ref[h, lc] * s1l_ref[h, lc, i1, :], 0.0)
                w = wh if w is None else w + wh
            rs = slice(ii * PEER_NKEYS, (ii + 1) * PEER_NKEYS)
            ls = slice(lc * PEER_LANES, (lc + 1) * PEER_LANES)
            aw_prev[rs, ls] = (_gelu(h_prev[rs, ls]) * w).astype(BF16)

        tiles = [(ii, lc) for ii in range(PEER_SUBS) for lc in range(o_ref.shape[1] // PEER_LANES)]
        per = -(-len(tiles) // (2 * PEER_PIECES))
        for q in range(PEER_PIECES):
            second_matmul(q)
            for tl in tiles[(2 * q) * per:(2 * q + 1) * per]:
                gate_tile(*tl)
            if q % (PEER_PIECES // PEER_KPIECES) == 0:
                first_matmul(q // (PEER_PIECES // PEER_KPIECES))
            for tl in tiles[(2 * q + 1) * per:(2 * q + 2) * per]:
                gate_tile(*tl)

    @pl.when(g % 2 == 0)
    def _():
        stages(h0, h1, aw0, aw1)

    @pl.when(g % 2 == 1)
    def _():
        stages(h1, h0, aw1, aw0)


def _peer_dense(xmT, u_all, vT_all, layer, tk):
    d, t = xmT.shape
    tt = min(512, t)
    nl = tt // PEER_LANES
    nk = PEER_EXPERTS // PEER_EK
    s1, s2, s1l, s2l, tau = tk
    last = (t // tt) * nk - 1
    pair = lambda g, lag: jnp.clip(g - lag, 0, last)
    sblk = pl.BlockSpec((PEER_HEADS, nl, PEER_NKEYS, PEER_LANES), lambda g: (0, pair(g, 1) // nk, 0, 0))
    return pl.pallas_call(
        functools.partial(_peer_dense_body, nk=nk), grid=(last + 3,),
        in_specs=[pl.BlockSpec((d, tt), lambda g: (0, pair(g, 0) // nk)),
                  pl.BlockSpec((None, PEER_EK, d), lambda g: (layer, pair(g, 0) % nk, 0)),
                  pl.BlockSpec((None, d, PEER_EK), lambda g: (layer, 0, pair(g, 2) % nk)),
                  sblk, sblk, sblk, sblk,
                  pl.BlockSpec((PEER_HEADS, nl, 1, PEER_LANES), lambda g: (0, pair(g, 1) // nk, 0, 0))],
        out_specs=pl.BlockSpec((d, tt), lambda g: (0, pair(g, 2) // nk)),
        out_shape=jax.ShapeDtypeStruct((d, t), F32),
        scratch_shapes=[pltpu.VMEM((2, PEER_EK, tt), F32), pltpu.VMEM((2, PEER_EK, tt), BF16)],
        compiler_params=_cp(1, 48), name="peer_dense")(xmT, u_all, vT_all, s1, s2, s1l, s2l, tau)


def _ffn_ln_body(fT_ref, x1_ref, g2_ref, lng_ref, lnb_ref, scn_ref, shn_ref, x2_ref, *rest, alpha):
    x2 = _ln(alpha * x1_ref[...] + g2_ref[...] * fT_ref[...].T, lng_ref[...], lnb_ref[...])
    x2_ref[...] = x2
    if rest:
        rest[0][...] = (x2 * (1.0 + scn_ref[...]) + shn_ref[...]).astype(rest[0].dtype)


def _ffn_ln(fT, x1, g2, lng, lnb, scn, shn, alpha, emit_h):
    t, d = x1.shape
    tm = min(256, t)
    row = pl.BlockSpec((tm, d), lambda i: (i, 0))
    vec = pl.BlockSpec((1, d), lambda i: (0, 0))
    out_specs = [row, row] if emit_h else [row]
    out_shape = [jax.ShapeDtypeStruct((t, d), F32)] + ([jax.ShapeDtypeStruct((t, d), BF16)] if emit_h else [])
    return pl.pallas_call(
        functools.partial(_ffn_ln_body, alpha=alpha), grid=(t // tm,),
        in_specs=[pl.BlockSpec((d, tm), lambda i: (0, i)), row, vec, vec, vec, vec, vec],
        out_specs=out_specs, out_shape=out_shape,
        compiler_params=_cp(1, 32), name="ffn_ln")(fT, x1, g2, lng, lnb, scn, shn)


def kernel(x, c, ctx, c_ctx, w_ada, b_ada, w_in, b_in, s5_lam_re, s5_lam_im, s5_log_dt, s5_b_re, s5_b_im, s5_c_re, s5_c_im, s5_d, w_glu, b_glu, qn_gain, kn_gain, sink, sg_ln_g, sg_ln_b, w_sp, b_sp, w_br_a, w_br_b, w_br_c, w_br_d, w_out, ln1_g, ln1_b, ln2_g, ln2_b, w_pq, peer_keys, peer_u, peer_v):
    depth = w_in.shape[0]
    bsz, n_lat, d = x.shape
    assert bsz == 1 and d == D_MODEL and n_lat % 512 == 0 and ctx.shape[1] % BLOCK == 0
    n_ctx = ctx.shape[1]
    alpha = (2 * depth) ** 0.25
    x, ctx = x[0], ctx[0]

    row8 = lax.broadcasted_iota(jnp.int32, (8, d), 0)
    cond8 = jnp.where(row8 == 0, c[0][None], jnp.where(row8 == 1, c_ctx[None], 0.0))
    ada = [_ada(cond8, w_ada, b_ada, l) for l in range(depth)]
    cosf, sinf = _rope_tables(n_lat)
    cos1, sin0 = jnp.ones((n_ctx, HEAD_DIM), F32), jnp.zeros((n_ctx, HEAD_DIM), F32)
    vec = lambda a: a.reshape(1, -1).astype(F32)

    ref_off = {'a': 0, 'qg': 768, 'kg': 1792, 'vg': 2048, 'qw': 2304, 'kw': 3328, 'vw': 3584, 'du': 3840, 'dv': 4608}
    ref_w = {'a': 768, 'qg': 1024, 'kg': 256, 'vg': 256, 'qw': 1024, 'kw': 256, 'vw': 256, 'du': 768, 'dv': 768}
    order = ['a', 'kg', 'qg', 'qw', 'kw', 'vg', 'vw', 'du', 'dv']
    perm = lambda a: jnp.concatenate([a[..., ref_off[n]:ref_off[n] + ref_w[n]] for n in order], -1)
    perm_idx = np.concatenate([np.arange(ref_off[n], ref_off[n] + ref_w[n]) for n in order]).astype(np.int32)

    u_all, vT_all = peer_u.astype(BF16), jnp.swapaxes(peer_v, 1, 2).astype(BF16)

    h_lat = None
    for l in range(depth):
        need_ctx = l < depth - 1
        mods = [[ada[l][r:r + 1, i * d:(i + 1) * d] for i in range(6)] for r in range(2)]
        (sh1, sc1, g1, sh2, sc2, g2), (sh1c, sc1c, g1c, sh2c, sc2c, g2c) = mods
        if need_ctx:
            nxt = [ada[l + 1][r:r + 1, 0:2 * d] for r in range(2)]
            (shn, scn), (shnc, scnc) = [(m[:, :d], m[:, d:]) for m in nxt]
        else:
            shn = scn = shnc = scnc = jnp.zeros((1, d), F32)

        w_small, b_small = perm(w_in[l][:, :SMALL_W]).astype(BF16), vec(jnp.take(b_in[l], perm_idx))
        w_gate, b_gate = w_in[l][:, SMALL_W:].astype(BF16), vec(b_in[l][SMALL_W:])
        s5m = [_s5_mats(s5_lam_re[l, dr], s5_lam_im[l, dr], s5_log_dt[l, dr], s5_b_re[l, dr], s5_b_im[l, dr],
                        s5_c_re[l, dr], s5_c_im[l, dr]) for dr in range(2)]
        wg, bg, s5d = w_glu[l].astype(BF16), vec(b_glu[l]), vec(s5_d[l])
        qn, kn = vec(qn_gain[l]), vec(kn_gain[l])
        wsp, bsp = w_sp[l].astype(BF16), b_sp[l].astype(F32)[:, :, None]
        w_brs = [w[l].astype(BF16) for w in (w_br_a, w_br_b, w_br_c, w_br_d)]
        wo = w_out[l].astype(BF16)
        wpqT = w_pq[l].T.astype(BF16)
        keys = peer_keys[l].astype(BF16)

        if h_lat is None:
            h_lat, h_ctx = _modulate(x, sc1, sh1), _modulate(ctx, sc1c, sh1c)

        proj = _matmul(h_lat, w_small, b_small, tm=512, tn=1792, out_dtype=BF16, name="in_proj")
        gates = _matmul(h_lat, w_gate, b_gate, tm=512, tn=2048, out_dtype=BF16, name="in_proj_gates")
        proj_c = _matmul(h_ctx, w_small, b_small, tm=512, tn=1792, out_dtype=BF16, name="in_proj_ctx")

        y_dir, y_dir_c = [], []
        for dr in range(2):
            yc_, hc_end = _s5_dir(proj_c, s5m[dr], jnp.zeros((2, S5_CH), F32), bool(dr))
            yl_, _ = _s5_dir(proj, s5m[dr], hc_end, bool(dr))
            y_dir.append(yl_)
            y_dir_c.append(yc_)
        y_a = _s5_out(proj, y_dir[0], y_dir[1], s5d, wg, bg)

        qg, kg, vge, qw, kw = _prep(proj, cosf, sinf, qn, kn, ATTN_SCALE * math.log2(math.e))
        qgc, kgc, vgce, qwc, kwc = _prep(proj_c, cos1, sin0, qn, kn, ATTN_SCALE)
        y_b = _flash(qg, kg, vge, kgc, vgce)
        y_c = _window(qw, kw, proj, kwc, proj_c, sink[l])

        y_d = _spatial_gate(proj, vec(sg_ln_g[l]), vec(sg_ln_b[l]), wsp, bsp)

        merged = _merge((y_a, y_b, y_c, y_d), w_brs, gates)
        x1, xmT = _out_ln(merged, wo, x, g1, vec(ln1_g[l]), vec(ln1_b[l]), sc2, sh2, alpha)
        qT = _matmul(wpqT, xmT, None, tm=D_MODEL, tn=1024, out_dtype=F32, name="peer_query")
        ffnT = _peer_dense(xmT, u_all, vT_all, l, _peer_topk(qT, keys))
        res = _ffn_ln(ffnT, x1, g2, vec(ln2_g[l]), vec(ln2_b[l]), scn, shn, alpha, need_ctx)
        x = res[0]

        if need_ctx:
            h_lat = res[1]
            gates_c = _matmul(h_ctx, w_gate, b_gate, tm=512, tn=2048, out_dtype=BF16, name="in_proj_gates_ctx")
            y_a_c = _s5_out(proj_c, y_dir_c[0], y_dir_c[1], s5d, wg, bg)
            y_b_c = _ctx_attn(qgc, kgc, proj_c, COL_VG, sink[l], False)
            y_c_c = _ctx_attn(qwc, kwc, proj_c, COL_VW, sink[l], True)
            y_d_c = _spatial_gate(proj_c, vec(sg_ln_g[l]), vec(sg_ln_b[l]), wsp, bsp)
            merged_c = _merge((y_a_c, y_b_c, y_c_c, y_d_c), w_brs, gates_c)
            c1, cmT = _out_ln(merged_c, wo, ctx, g1c, vec(ln1_g[l]), vec(ln1_b[l]), sc2c, sh2c, alpha)
            qTc = _matmul(wpqT, cmT, None, tm=D_MODEL, tn=1024, out_dtype=F32, name="peer_query_ctx")
            ffnTc = _peer_dense(cmT, u_all, vT_all, l, _peer_topk(qTc, keys))
            ctx, h_ctx = _ffn_ln(ffnTc, c1, g2c, vec(ln2_g[l]), vec(ln2_b[l]), scnc, shnc, alpha, True)
    return x[None]
```

```python
import functools
import math

import jax
import jax.numpy as jnp
import numpy as np
from jax import lax
from jax.experimental import pallas as pl
from jax.experimental.pallas import tpu as pltpu

F32, BF16 = jnp.float32, jnp.bfloat16

D_MODEL = 2048
GRID_W = 64
BLOCK = 128
HEAD_DIM = 128
AXIS_DIM = HEAD_DIM // 2
ROPE_THETA = 10000.0
ATTN_SCALE = HEAD_DIM ** -0.5
NEG_INF = -1e30
EPS = 1e-6
S5_GROUPS, S5_GROUP_CH, S5_STATE = 48, 16, 64
S5_WIDTH = S5_GROUPS * S5_GROUP_CH
S5_CH = S5_GROUPS * S5_STATE
S5_GB = 16
S5_NB = S5_GROUPS // S5_GB
S5_KB = S5_GB * S5_GROUP_CH
S5_PB = S5_GB * S5_STATE
GA_HEADS, GA_KV = 8, 2
WA_HEADS, WA_KV = 8, 2
KV_GROUP = GA_HEADS // GA_KV
SG_GROUPS, SG_GROUP_CH = 6, 128
SG_WIDTH = SG_GROUPS * SG_GROUP_CH
CHUNK = 128
N_BRANCH = 4
PEER_HEADS, PEER_NKEYS, PEER_QDIM, PEER_TOPK = 8, 128, 256, 16
PEER_HALF = PEER_QDIM // 2
PEER_EXPERTS = PEER_NKEYS * PEER_NKEYS

COL_A, COL_KG, COL_QG, COL_QW, COL_KW, COL_VG, COL_VW, COL_DU, COL_DV = 0, 768, 1024, 2048, 3072, 3328, 3584, 3840, 4608
SMALL_W = 5376
GATE_W = N_BRANCH * D_MODEL

VMEM_CAP_MB = 56


def _cp(ndims, vmem_mb=32):
    return pltpu.CompilerParams(dimension_semantics=("arbitrary",) * ndims,
                                vmem_limit_bytes=min(vmem_mb, VMEM_CAP_MB) << 20)


def _gelu(y):
    return 0.5 * y * (1.0 + lax.erf(y * (2.0 ** -0.5)))


def _ln(y, g, b):
    mu = jnp.mean(y, -1, keepdims=True)
    yc = y - mu
    var = jnp.mean(yc * yc, -1, keepdims=True)
    return yc * lax.rsqrt(var + EPS) * g + b


_NT = (((1,), (1,)), ((), ()))


def _ada_body(c_ref, w_ref, b_ref, o_ref):
    cnd = c_ref[...]
    s = (cnd * jax.nn.sigmoid(cnd)).astype(BF16)
    o_ref[...] = jnp.dot(s, w_ref[...].astype(BF16), preferred_element_type=F32) + b_ref[...]


def _ada(cond8, w_ada, b_ada, l):
    depth, d, n = w_ada.shape
    tn = 1024
    return pl.pallas_call(
        _ada_body, grid=(n // tn,),
        in_specs=[pl.BlockSpec((8, d), lambda j: (0, 0)),
                  pl.BlockSpec((None, d, tn), lambda j: (l, 0, j)),
                  pl.BlockSpec((None, 1, tn), lambda j: (l, 0, j))],
        out_specs=pl.BlockSpec((8, tn), lambda j: (0, j)),
        out_shape=jax.ShapeDtypeStruct((8, n), F32),
        compiler_params=_cp(1, 32), name="ada_mod")(cond8, w_ada, b_ada.reshape(depth, 1, n))


def _mod_body(x_ref, sc_ref, sh_ref, o_ref):
    o_ref[...] = (x_ref[...] * (1.0 + sc_ref[...]) + sh_ref[...]).astype(o_ref.dtype)


def _modulate(x, sc, sh):
    t, d = x.shape
    tm = min(512, t)
    return pl.pallas_call(
        _mod_body, grid=(t // tm,),
        in_specs=[pl.BlockSpec((tm, d), lambda i: (i, 0)),
                  pl.BlockSpec((1, d), lambda i: (0, 0)),
                  pl.BlockSpec((1, d), lambda i: (0, 0))],
        out_specs=pl.BlockSpec((tm, d), lambda i: (i, 0)),
        out_shape=jax.ShapeDtypeStruct((t, d), BF16),
        compiler_params=_cp(1, 24), name="modulate")(x, sc, sh)


def _mm_body(a_ref, w_ref, b_ref, o_ref):
    acc = jnp.dot(a_ref[...], w_ref[...], preferred_element_type=F32)
    o_ref[...] = (acc + b_ref[...]).astype(o_ref.dtype)


def _mm_nobias_body(a_ref, w_ref, o_ref):
    o_ref[...] = jnp.dot(a_ref[...], w_ref[...], preferred_element_type=F32).astype(o_ref.dtype)


def _matmul(a, w, b, *, tm, tn, out_dtype, name):
    m, k = a.shape
    n = w.shape[1]
    tm, tn = min(tm, m), min(tn, n)
    in_specs = [pl.BlockSpec((tm, k), lambda j, i: (i, 0)),
                pl.BlockSpec((k, tn), lambda j, i: (0, j))]
    args = [a, w]
    body = _mm_nobias_body
    if b is not None:
        in_specs.append(pl.BlockSpec((1, tn), lambda j, i: (0, j)))
        args.append(b)
        body = _mm_body
    osz = jnp.dtype(out_dtype).itemsize
    vmem = 2 * (tm * k * 2 + k * tn * 2 + tm * tn * osz) + tm * tn * 4
    return pl.pallas_call(
        body, grid=(n // tn, m // tm), in_specs=in_specs,
        out_specs=pl.BlockSpec((tm, tn), lambda j, i: (i, j)),
        out_shape=jax.ShapeDtypeStruct((m, n), out_dtype),
        compiler_params=_cp(2, (vmem >> 20) + 8), name=name)(*args)


S5_TC = 256
S5_SEG = 8
S5_L = S5_TC // S5_SEG


def _s5_body(u_ref, perm_ref, permT_ref, bre_ref, bim_ref, cre_ref, cim_ref, apow_ref, h0_ref, y_ref, hT_ref,
             xr, xi, carry, hin_r, hin_i, *, rev, seg_len):
    @pl.when(pl.program_id(0) == 0)
    def _():
        carry[...] = h0_ref[...]

    up = jnp.dot(perm_ref[...], u_ref[...], preferred_element_type=F32).astype(BF16)
    for j in range(S5_NB):
        uj = up[:, j * S5_KB:(j + 1) * S5_KB]
        xr[:, j * S5_PB:(j + 1) * S5_PB] = jnp.dot(uj, bre_ref[j], preferred_element_type=F32)
        xi[:, j * S5_PB:(j + 1) * S5_PB] = jnp.dot(uj, bim_ref[j], preferred_element_type=F32)

    seg_order = range(S5_SEG - 1, -1, -1) if rev else range(S5_SEG)
    for j in range(S5_NB):
        cs = slice(j * S5_PB, (j + 1) * S5_PB)
        ar = jnp.broadcast_to(apow_ref[0, 0:1, cs], (S5_SEG, S5_PB))
        ai = jnp.broadcast_to(apow_ref[1, 0:1, cs], (S5_SEG, S5_PB))

        def local_step(k, c, cs=cs, ar=ar, ai=ai):
            hr, hi = c
            rows = pl.ds(pl.multiple_of(k * S5_SEG, S5_SEG), S5_SEG)
            nr = ar * hr - ai * hi + xr[rows, cs]
            ni = ar * hi + ai * hr + xi[rows, cs]
            xr[rows, cs] = nr
            xi[rows, cs] = ni
            return nr, ni

        zero = jnp.zeros((S5_SEG, S5_PB), F32)
        er, ei = lax.fori_loop(0, seg_len, local_step, (zero, zero), unroll=4)

        alr, ali = apow_ref[0, seg_len - 1:seg_len, cs], apow_ref[1, seg_len - 1:seg_len, cs]
        hr, hi = carry[0:1, cs], carry[1:2, cs]
        for s in seg_order:
            hin_r[s:s + 1, cs] = hr
            hin_i[s:s + 1, cs] = hi
            hr, hi = (er[s:s + 1] + alr * hr - ali * hi, ei[s:s + 1] + alr * hi + ali * hr)
        carry[0:1, cs] = hr
        carry[1:2, cs] = hi

        def fix_step(k, c, cs=cs):
            rows = pl.ds(pl.multiple_of(k * S5_SEG, S5_SEG), S5_SEG)
            pr, pi = apow_ref[0, pl.ds(k, 1), cs], apow_ref[1, pl.ds(k, 1), cs]
            h_r, h_i = hin_r[:, cs], hin_i[:, cs]
            xr[rows, cs] += pr * h_r - pi * h_i
            xi[rows, cs] += pr * h_i + pi * h_r
            return c

        lax.fori_loop(0, seg_len, fix_step, 0, unroll=4)

    yp = jnp.concatenate(
        [jnp.dot(xr[:, j * S5_PB:(j + 1) * S5_PB].astype(BF16), cre_ref[j], preferred_element_type=F32)
         + jnp.dot(xi[:, j * S5_PB:(j + 1) * S5_PB].astype(BF16), cim_ref[j], preferred_element_type=F32)
         for j in range(S5_NB)], axis=1)
    hi = yp.astype(BF16)
    r1 = yp - hi.astype(F32)
    mid = r1.astype(BF16)
    lo = (r1 - mid.astype(F32)).astype(BF16)
    parts = jnp.dot(permT_ref[...], jnp.concatenate([hi, mid, lo], axis=1), preferred_element_type=F32)
    y_ref[...] = parts[:, :S5_WIDTH] + parts[:, S5_WIDTH:2 * S5_WIDTH] + parts[:, 2 * S5_WIDTH:]
    hT_ref[...] = carry[...]


def _s5_perm(rev, tc):
    seg_len = tc // S5_SEG
    p = np.zeros((tc, tc), np.float32)
    for k in range(seg_len):
        for s in range(S5_SEG):
            p[k * S5_SEG + s, s * seg_len + ((seg_len - 1 - k) if rev else k)] = 1.0
    return jnp.asarray(p, BF16), jnp.asarray(p.T, BF16)


def _s5_dir(proj, mats, h0, rev):
    t = proj.shape[0]
    tc = min(S5_TC, t)
    assert t % tc == 0 and tc % (8 * S5_SEG) == 0
    nt = t // tc
    bre, bim, cre, cim, apow = mats
    tmap = (lambda i: (nt - 1 - i, 0)) if rev else (lambda i: (i, 0))
    full3 = lambda i: (0, 0, 0)
    full2 = lambda i: (0, 0)
    return pl.pallas_call(
        functools.partial(_s5_body, rev=rev, seg_len=tc // S5_SEG), grid=(nt,),
        in_specs=[pl.BlockSpec((tc, S5_WIDTH), tmap),
                  pl.BlockSpec((tc, tc), full2), pl.BlockSpec((tc, tc), full2),
                  pl.BlockSpec((S5_NB, S5_KB, S5_PB), full3), pl.BlockSpec((S5_NB, S5_KB, S5_PB), full3),
                  pl.BlockSpec((S5_NB, S5_PB, S5_KB), full3), pl.BlockSpec((S5_NB, S5_PB, S5_KB), full3),
                  pl.BlockSpec((2, S5_L, S5_CH), full3), pl.BlockSpec((2, S5_CH), full2)],
        out_specs=[pl.BlockSpec((tc, S5_WIDTH), tmap), pl.BlockSpec((2, S5_CH), full2)],
        out_shape=[jax.ShapeDtypeStruct((t, S5_WIDTH), F32), jax.ShapeDtypeStruct((2, S5_CH), F32)],
        scratch_shapes=[pltpu.VMEM((tc, S5_CH), F32), pltpu.VMEM((tc, S5_CH), F32), pltpu.VMEM((2, S5_CH), F32),
                        pltpu.VMEM((S5_SEG, S5_CH), F32), pltpu.VMEM((S5_SEG, S5_CH), F32)],
        compiler_params=_cp(1, 48), name="s5_scan_bwd" if rev else "s5_scan_fwd")(
            proj, *_s5_perm(rev, tc), bre, bim, cre, cim, apow, h0)


def _s5_out_body(u_ref, yf_ref, yb_ref, d_ref, wg_ref, bg_ref, o_ref):
    y = u_ref[...].astype(F32) * d_ref[...] + yf_ref[...] + yb_ref[...]
    g = _gelu(y)
    z = jnp.dot(g.astype(BF16), wg_ref[...], preferred_element_type=F32) + bg_ref[...]
    o_ref[...] = (g * jax.nn.sigmoid(z)).astype(o_ref.dtype)


def _s5_out(proj, yf, yb, d, wg, bg):
    t = proj.shape[0]
    tm = min(512, t)
    row = lambda i: (i, 0)
    full = lambda i: (0, 0)
    return pl.pallas_call(
        _s5_out_body, grid=(t // tm,),
        in_specs=[pl.BlockSpec((tm, S5_WIDTH), row), pl.BlockSpec((tm, S5_WIDTH), row), pl.BlockSpec((tm, S5_WIDTH), row),
                  pl.BlockSpec((1, S5_WIDTH), full), pl.BlockSpec((S5_WIDTH, S5_WIDTH), full), pl.BlockSpec((1, S5_WIDTH), full)],
        out_specs=pl.BlockSpec((tm, S5_WIDTH), row),
        out_shape=jax.ShapeDtypeStruct((t, S5_WIDTH), BF16),
        compiler_params=_cp(1, 24), name="s5_glu")(proj, yf, yb, d, wg, bg)


def _s5_mats(lam_re, lam_im, log_dt, b_re, b_im, c_re, c_im):
    dt = jnp.exp(log_dt)[:, None]
    mag = jnp.exp(lam_re * dt)
    ar, ai = mag * jnp.cos(lam_im * dt), mag * jnp.sin(lam_im * dt)
    nr, ni = ar - 1.0, ai
    den = lam_re * lam_re + lam_im * lam_im
    cr = (nr * lam_re + ni * lam_im) / den
    ci = (ni * lam_re - nr * lam_im) / den
    fre = cr[:, :, None] * b_re - ci[:, :, None] * b_im
    fim = cr[:, :, None] * b_im + ci[:, :, None] * b_re
    eye = jnp.eye(S5_GB, dtype=F32)

    def blk_in(f):
        f = f.reshape(S5_NB, S5_GB, S5_STATE, S5_GROUP_CH)
        m = jnp.einsum('jgph,gk->jghkp', f, eye)
        return m.reshape(S5_NB, S5_KB, S5_PB).astype(BF16)

    def blk_out(c):
        c = c.reshape(S5_NB, S5_GB, S5_GROUP_CH, S5_STATE)
        m = jnp.einsum('jghp,gk->jgpkh', c, eye)
        return m.reshape(S5_NB, S5_PB, S5_KB).astype(BF16)

    kk = jnp.arange(1, S5_L + 1, dtype=F32)[:, None, None]
    magk, angk = jnp.exp(lam_re * dt * kk), lam_im * dt * kk
    is_re = lax.broadcasted_iota(jnp.int32, (2, S5_L, S5_CH), 0) == 0
    apow = jnp.where(is_re, (magk * jnp.cos(angk)).reshape(1, S5_L, S5_CH), (magk * jnp.sin(angk)).reshape(1, S5_L, S5_CH))
    return blk_in(fre), blk_in(fim), blk_out(c_re), blk_out(-c_im), apow


def _prep_body(qg_ref, kg_ref, vg_ref, qw_ref, kw_ref, cos_ref, sin_ref, qn_ref, kn_ref, oqg, okg, ovg, oqw, okw, *,
               qg_scale):
    cosf, sinf = cos_ref[...], sin_ref[...]
    lane = lax.broadcasted_iota(jnp.int32, cosf.shape, 1)
    low = (lane % AXIS_DIM) < (AXIS_DIM // 2)

    def rope(t):
        partner = jnp.where(low, pltpu.roll(t, HEAD_DIM - AXIS_DIM // 2, 1), pltpu.roll(t, AXIS_DIM // 2, 1))
        return t * cosf + partner * sinf

    def rms(t, g):
        return t * lax.rsqrt(jnp.mean(t * t, -1, keepdims=True) + EPS) * g

    for h in range(GA_HEADS):
        cs = slice(h * HEAD_DIM, (h + 1) * HEAD_DIM)
        oqg[:, cs] = (rope(rms(qg_ref[:, cs].astype(F32), qn_ref[...])) * qg_scale).astype(BF16)
        oqw[:, cs] = (rope(qw_ref[:, cs].astype(F32)) * ATTN_SCALE).astype(BF16)
    for h in range(GA_KV):
        cs = slice(h * HEAD_DIM, (h + 1) * HEAD_DIM)
        okg[:, cs] = rope(rms(kg_ref[:, cs].astype(F32), kn_ref[...])).astype(BF16)
        okw[:, cs] = rope(kw_ref[:, cs].astype(F32)).astype(BF16)
        ovg[:, 2 * h * HEAD_DIM:(2 * h + 1) * HEAD_DIM] = vg_ref[:, cs]
        ovg[:, (2 * h + 1) * HEAD_DIM:(2 * h + 2) * HEAD_DIM] = jnp.ones((vg_ref.shape[0], HEAD_DIM), BF16)


def _prep(proj, cosf, sinf, qn, kn, qg_scale):
    t = proj.shape[0]
    tr = min(256, t)
    qw_, kw_ = GA_HEADS * HEAD_DIM, GA_KV * HEAD_DIM
    row = lambda i: (i, 0)
    full = lambda i: (0, 0)
    return pl.pallas_call(
        functools.partial(_prep_body, qg_scale=qg_scale), grid=(t // tr,),
        in_specs=[pl.BlockSpec((tr, qw_), lambda i: (i, COL_QG // qw_)),
                  pl.BlockSpec((tr, kw_), lambda i: (i, COL_KG // kw_)),
                  pl.BlockSpec((tr, kw_), lambda i: (i, COL_VG // kw_)),
                  pl.BlockSpec((tr, qw_), lambda i: (i, COL_QW // qw_)),
                  pl.BlockSpec((tr, kw_), lambda i: (i, COL_KW // kw_)),
                  pl.BlockSpec((tr, HEAD_DIM), row), pl.BlockSpec((tr, HEAD_DIM), row),
                  pl.BlockSpec((1, HEAD_DIM), full), pl.BlockSpec((1, HEAD_DIM), full)],
        out_specs=[pl.BlockSpec((tr, qw_), row), pl.BlockSpec((tr, kw_), row), pl.BlockSpec((tr, 2 * kw_), row),
                   pl.BlockSpec((tr, qw_), row), pl.BlockSpec((tr, kw_), row)],
        out_shape=[jax.ShapeDtypeStruct((t, qw_), BF16), jax.ShapeDtypeStruct((t, kw_), BF16),
                   jax.ShapeDtypeStruct((t, 2 * kw_), BF16),
                   jax.ShapeDtypeStruct((t, qw_), BF16), jax.ShapeDtypeStruct((t, kw_), BF16)],
        compiler_params=_cp(1, 24), name="qk_prep")(proj, proj, proj, proj, proj, cosf, sinf, qn, kn)


def _rope_tables(n_lat):
    rows = n_lat // GRID_W
    inv = jnp.power(ROPE_THETA, -jnp.arange(0, AXIS_DIM, 2, dtype=F32) / AXIS_DIM)
    ang_r = jnp.arange(rows, dtype=F32)[:, None] * inv
    ang_c = jnp.arange(GRID_W, dtype=F32)[:, None] * inv
    rep = lambda a: jnp.repeat(a, GRID_W, axis=0)
    til = lambda a: jnp.tile(a, (rows, 1))
    cr, sr, cc, sc = rep(jnp.cos(ang_r)), rep(jnp.sin(ang_r)), til(jnp.cos(ang_c)), til(jnp.sin(ang_c))
    return jnp.concatenate([cr, cr, cc, cc], -1), jnp.concatenate([-sr, sr, -sc, sc], -1)


def _flash_body(q_ref, kc_ref, vc_ref, k_ref, v_ref, o_ref, qs, m_s, acc, *, tq, tk, nk):
    for g in range(KV_GROUP):
        qs[g * tq:(g + 1) * tq, :] = q_ref[:, g * HEAD_DIM:(g + 1) * HEAD_DIM]
    m_s[...] = jnp.full(m_s.shape, -jnp.inf, F32)
    acc[...] = jnp.zeros(acc.shape, F32)

    def update(k, v):
        for g in range(KV_GROUP):
            rs = slice(g * tq, (g + 1) * tq)
            s = lax.dot_general(qs[rs, :], k, _NT, preferred_element_type=F32)
            m_prev = m_s[rs, :]
            m_next = jnp.maximum(m_prev, jnp.max(s, axis=1, keepdims=True))
            alpha = jnp.exp2(m_prev - m_next)
            p = jnp.exp2(s - jnp.concatenate([m_next] * (s.shape[1] // HEAD_DIM), axis=1))
            acc[rs, :] = (jnp.concatenate([alpha, alpha], axis=1) * acc[rs, :]
                          + jnp.dot(p.astype(BF16), v, preferred_element_type=F32))
            m_s[rs, :] = m_next

    update(kc_ref[...], vc_ref[...])

    def body(j, carry):
        rows = pl.ds(pl.multiple_of(j * tk, tk), tk)
        update(k_ref[rows, :], v_ref[rows, :])
        return carry

    lax.fori_loop(0, nk, body, 0, unroll=FLASH_UNROLL)
    out = acc[:, :HEAD_DIM] / acc[:, HEAD_DIM:]
    for g in range(KV_GROUP):
        o_ref[:, g * HEAD_DIM:(g + 1) * HEAD_DIM] = out[g * tq:(g + 1) * tq, :].astype(o_ref.dtype)


FLASH_TK = 512
FLASH_UNROLL = 32


def _flash(q, kg, vge, kgc, vgce):
    t = q.shape[0]
    n_ctx = kgc.shape[0]
    tq = min(256, t)
    tk = min(FLASH_TK, t)
    gw = KV_GROUP * HEAD_DIM
    head = lambda h, i: (0, h)
    return pl.pallas_call(
        functools.partial(_flash_body, tq=tq, tk=tk, nk=t // tk), grid=(GA_KV, t // tq),
        in_specs=[pl.BlockSpec((tq, gw), lambda h, i: (i, h)),
                  pl.BlockSpec((n_ctx, HEAD_DIM), head), pl.BlockSpec((n_ctx, 2 * HEAD_DIM), head),
                  pl.BlockSpec((t, HEAD_DIM), head), pl.BlockSpec((t, 2 * HEAD_DIM), head)],
        out_specs=pl.BlockSpec((tq, gw), lambda h, i: (i, h)),
        out_shape=jax.ShapeDtypeStruct((t, GA_HEADS * HEAD_DIM), BF16),
        scratch_shapes=[pltpu.VMEM((KV_GROUP * tq, HEAD_DIM), BF16), pltpu.VMEM((KV_GROUP * tq, HEAD_DIM), F32),
                        pltpu.VMEM((KV_GROUP * tq, 2 * HEAD_DIM), F32)],
        compiler_params=_cp(2, 48), name="global_attn")(q, kgc, vgce, kg, vge)


WIN_QB = 4


def _win_body(q_ref, km_ref, k0_ref, kp_ref, vm_ref, v0_ref, vp_ref, kc_ref, vc_ref, sink_ref, o_ref, *, nb):
    kslab = jnp.concatenate([km_ref[...], k0_ref[...], kp_ref[...]], axis=0)
    vslab = jnp.concatenate([vm_ref[...], v0_ref[...], vp_ref[...]], axis=0)
    sink = sink_ref[0]
    shape = (KV_GROUP * BLOCK, 3 * BLOCK)
    r = lax.broadcasted_iota(jnp.int32, shape, 0) % BLOCK
    c = lax.broadcasted_iota(jnp.int32, shape, 1)
    cc = c % BLOCK
    for b in range(WIN_QB):
        i = pl.program_id(1) * WIN_QB + b
        rows = slice(b * BLOCK, (b + 1) * BLOCK)
        q = jnp.concatenate([q_ref[rows, g * HEAD_DIM:(g + 1) * HEAD_DIM] for g in range(KV_GROUP)], axis=0)
        kw, vw = kslab[b * BLOCK:(b + 3) * BLOCK], vslab[b * BLOCK:(b + 3) * BLOCK]
        s_w = lax.dot_general(q, kw, _NT, preferred_element_type=F32)
        iv = jnp.full(shape, i, jnp.int32)
        valid = (((c < BLOCK) & (cc >= r) & (iv >= 1)) | ((c >= BLOCK) & (c < 2 * BLOCK))
                 | ((c >= 2 * BLOCK) & (cc <= r) & (iv + 1 < nb)))
        s_w = jnp.where(valid, s_w, NEG_INF)
        s_c = lax.dot_general(q, kc_ref[...], _NT, preferred_element_type=F32)
        m = jnp.maximum(jnp.maximum(jnp.max(s_w, -1, keepdims=True), jnp.max(s_c, -1, keepdims=True)), sink)
        pw, pc = jnp.exp(s_w - m), jnp.exp(s_c - m)
        l = jnp.sum(pw, -1, keepdims=True) + jnp.sum(pc, -1, keepdims=True) + jnp.exp(sink - m)
        inv = 1.0 / l
        out = (jnp.dot((pc * inv).astype(BF16), vc_ref[...], preferred_element_type=F32)
               + jnp.dot((pw * inv).astype(BF16), vw, preferred_element_type=F32))
        for g in range(KV_GROUP):
            o_ref[rows, g * HEAD_DIM:(g + 1) * HEAD_DIM] = out[g * BLOCK:(g + 1) * BLOCK, :].astype(o_ref.dtype)


def _sink_rows(sink, rows_per_head):
    return jnp.repeat(sink.astype(F32).reshape(WA_KV, KV_GROUP), rows_per_head, axis=1)[:, :, None]


def _window(qw, kw, proj, kwc, proj_c, sink):
    t = qw.shape[0]
    n_ctx = kwc.shape[0]
    nb = t // BLOCK
    gw = KV_GROUP * HEAD_DIM
    vcol = COL_VW // HEAD_DIM
    assert nb % WIN_QB == 0
    edge = lambda f: pl.BlockSpec((BLOCK, HEAD_DIM), f)
    main = lambda f: pl.BlockSpec((WIN_QB * BLOCK, HEAD_DIM), f)
    before = lambda i: jnp.maximum(i * WIN_QB - 1, 0)
    after = lambda i: jnp.minimum((i + 1) * WIN_QB, nb - 1)
    return pl.pallas_call(
        functools.partial(_win_body, nb=nb), grid=(WA_KV, nb // WIN_QB),
        in_specs=[pl.BlockSpec((WIN_QB * BLOCK, gw), lambda h, i: (i, h)),
                  edge(lambda h, i: (before(i), h)), main(lambda h, i: (i, h)), edge(lambda h, i: (after(i), h)),
                  edge(lambda h, i: (before(i), vcol + h)), main(lambda h, i: (i, vcol + h)),
                  edge(lambda h, i: (after(i), vcol + h)),
                  pl.BlockSpec((n_ctx, HEAD_DIM), lambda h, i: (0, h)),
                  pl.BlockSpec((n_ctx, HEAD_DIM), lambda h, i: (0, vcol + h)),
                  pl.BlockSpec((1, KV_GROUP * BLOCK, 1), lambda h, i: (h, 0, 0))],
        out_specs=pl.BlockSpec((WIN_QB * BLOCK, gw), lambda h, i: (i, h)),
        out_shape=jax.ShapeDtypeStruct((t, WA_HEADS * HEAD_DIM), BF16),
        compiler_params=_cp(2, 24), name="window_attn")(
            qw, kw, kw, kw, proj, proj, proj, kwc, proj_c, _sink_rows(sink, BLOCK))


def _ctx_attn_body(q_ref, k_ref, v_ref, sink_ref, o_ref, *, n_ctx, use_sink):
    q = jnp.concatenate([q_ref[:, g * HEAD_DIM:(g + 1) * HEAD_DIM] for g in range(KV_GROUP)], axis=0)
    s = lax.dot_general(q, k_ref[...], _NT, preferred_element_type=F32)
    m = jnp.max(s, -1, keepdims=True)
    if use_sink:
        m = jnp.maximum(m, sink_ref[0])
    p = jnp.exp(s - m)
    l = jnp.sum(p, -1, keepdims=True)
    if use_sink:
        l = l + jnp.exp(sink_ref[0] - m)
    out = jnp.dot((p / l).astype(BF16), v_ref[...], preferred_element_type=F32)
    for g in range(KV_GROUP):
        o_ref[:, g * HEAD_DIM:(g + 1) * HEAD_DIM] = out[g * n_ctx:(g + 1) * n_ctx, :].astype(o_ref.dtype)


def _ctx_attn(q, k, proj_c, vcol0, sink, use_sink):
    n_ctx = q.shape[0]
    gw = KV_GROUP * HEAD_DIM
    vcol = vcol0 // HEAD_DIM
    return pl.pallas_call(
        functools.partial(_ctx_attn_body, n_ctx=n_ctx, use_sink=use_sink), grid=(GA_KV,),
        in_specs=[pl.BlockSpec((n_ctx, gw), lambda h: (0, h)),
                  pl.BlockSpec((n_ctx, HEAD_DIM), lambda h: (0, h)),
                  pl.BlockSpec((n_ctx, HEAD_DIM), lambda h: (0, vcol + h)),
                  pl.BlockSpec((1, KV_GROUP * n_ctx, 1), lambda h: (h, 0, 0))],
        out_specs=pl.BlockSpec((n_ctx, gw), lambda h: (0, h)),
        out_shape=jax.ShapeDtypeStruct((n_ctx, GA_HEADS * HEAD_DIM), BF16),
        compiler_params=_cp(1, 24), name="ctx_attn_sink" if use_sink else "ctx_attn")(
            q, k, proj_c, _sink_rows(sink, n_ctx))


def _sg_body(u_ref, v_ref, g_ref, b_ref, w_ref, bs_ref, o_ref, *, nch):
    for n in range(nch):
        rs = slice(n * CHUNK, (n + 1) * CHUNK)
        vn = _ln(v_ref[rs, :].astype(F32), g_ref[...], b_ref[...]).astype(BF16)
        for g in range(SG_GROUPS):
            cs = slice(g * SG_GROUP_CH, (g + 1) * SG_GROUP_CH)
            s = jnp.dot(w_ref[g], vn[:, cs], preferred_element_type=F32) + bs_ref[g]
            o_ref[rs, cs] = (u_ref[rs, cs].astype(F32) * s).astype(o_ref.dtype)


def _spatial_gate(proj, g_v, b_v, w_sp, b_sp):
    t = proj.shape[0]
    tm = min(512, t)
    full = lambda i: (0, 0)
    return pl.pallas_call(
        functools.partial(_sg_body, nch=tm // CHUNK), grid=(t // tm,),
        in_specs=[pl.BlockSpec((tm, SG_WIDTH), lambda i: (i, COL_DU // SG_WIDTH)),
                  pl.BlockSpec((tm, SG_WIDTH), lambda i: (i, COL_DV // SG_WIDTH)),
                  pl.BlockSpec((1, SG_WIDTH), full), pl.BlockSpec((1, SG_WIDTH), full),
                  pl.BlockSpec((SG_GROUPS, CHUNK, CHUNK), lambda i: (0, 0, 0)),
                  pl.BlockSpec((SG_GROUPS, CHUNK, 1), lambda i: (0, 0, 0))],
        out_specs=pl.BlockSpec((tm, SG_WIDTH), lambda i: (i, 0)),
        out_shape=jax.ShapeDtypeStruct((t, SG_WIDTH), BF16),
        compiler_params=_cp(1, 24), name="spatial_gate")(proj, proj, g_v, b_v, w_sp, b_sp)


def _merge_body(ya, yb, yc, yd, wa, wb, wc, wd, ga, gb, gc, gd, o_ref):
    def br(y, w, g):
        return jax.nn.sigmoid(g[...].astype(F32)) * jnp.dot(y[...], w[...], preferred_element_type=F32)
    o_ref[...] = (br(ya, wa, ga) + br(yb, wb, gb) + br(yc, wc, gc) + br(yd, wd, gd)).astype(o_ref.dtype)


def _merge(ys, ws, gates):
    t = ys[0].shape[0]
    tm, tn = min(512, t), 1024
    nn = D_MODEL // tn
    in_specs = [pl.BlockSpec((tm, y.shape[1]), lambda j, i: (i, 0)) for y in ys]
    in_specs += [pl.BlockSpec((w.shape[0], tn), lambda j, i: (0, j)) for w in ws]
    in_specs += [pl.BlockSpec((tm, tn), functools.partial(lambda j, i, b: (i, b * nn + j), b=b)) for b in range(N_BRANCH)]
    return pl.pallas_call(
        _merge_body, grid=(nn, t // tm), in_specs=in_specs,
        out_specs=pl.BlockSpec((tm, tn), lambda j, i: (i, j)),
        out_shape=jax.ShapeDtypeStruct((t, D_MODEL), BF16),
        compiler_params=_cp(2, 48), name="gated_merge")(*ys, *ws, gates, gates, gates, gates)


def _out_ln_body(mg_ref, w_ref, x_ref, g1_ref, lng_ref, lnb_ref, sc_ref, sh_ref, x1_ref, xmT_ref, *, alpha):
    o = jnp.dot(mg_ref[...], w_ref[...], preferred_element_type=F32)
    x1 = _ln(alpha * x_ref[...] + g1_ref[...] * o, lng_ref[...], lnb_ref[...])
    x1_ref[...] = x1
    xmT_ref[...] = (x1 * (1.0 + sc_ref[...]) + sh_ref[...]).T.astype(xmT_ref.dtype)


def _out_ln(merged, w_out, x, g1, lng, lnb, sc2, sh2, alpha):
    t, d = x.shape
    tm = min(256, t)
    row = lambda i: (i, 0)
    vec = pl.BlockSpec((1, d), lambda i: (0, 0))
    return pl.pallas_call(
        functools.partial(_out_ln_body, alpha=alpha), grid=(t // tm,),
        in_specs=[pl.BlockSpec((tm, d), row), pl.BlockSpec((d, d), lambda i: (0, 0)), pl.BlockSpec((tm, d), row),
                  vec, vec, vec, vec, vec],
        out_specs=[pl.BlockSpec((tm, d), row), pl.BlockSpec((d, tm), lambda i: (0, i))],
        out_shape=[jax.ShapeDtypeStruct((t, d), F32), jax.ShapeDtypeStruct((d, t), BF16)],
        compiler_params=_cp(1, 48), name="out_proj_ln")(merged, w_out, x, g1, lng, lnb, sc2, sh2)


def _peer_topk_body(q_ref, k_ref, s1_o, s2_o, g1_o, g2_o, tau_o):
    q = q_ref[...].astype(BF16)
    s1 = jnp.dot(k_ref[0], q[:PEER_HALF], preferred_element_type=F32)
    s2 = jnp.dot(k_ref[1], q[PEER_HALF:], preferred_element_type=F32)
    ninf = -jnp.inf

    def top_distinct(s):
        vals, tops, cnts = s, [], []
        for _ in range(PEER_TOPK):
            m = jnp.max(vals, axis=0, keepdims=True)
            eq = vals == m
            tops.append(m)
            cnts.append(jnp.sum(eq.astype(F32), axis=0, keepdims=True))
            vals = jnp.where(eq, ninf, vals)
        return jnp.concatenate(tops, 0), jnp.concatenate(cnts, 0)

    ta, na = top_distinct(s1)
    tb, nb = top_distinct(s2)
    nrow = [PEER_TOPK // (k + 1) for k in range(PEER_TOPK)]
    pad = -sum(nrow) % 8
    cand = jnp.concatenate([ta[k:k + 1] + tb[:nrow[k]] for k in range(PEER_TOPK)]
                           + [jnp.full((pad, ta.shape[1]), ninf, F32)], 0)
    mult = jnp.concatenate([na[k:k + 1] * nb[:nrow[k]] for k in range(PEER_TOPK)]
                           + [jnp.zeros((pad, ta.shape[1]), F32)], 0)
    vals = cand
    cnt = jnp.zeros_like(cand[0:1])
    tau = jnp.full_like(cand[0:1], ninf)
    for _ in range(PEER_TOPK):
        m = jnp.max(vals, axis=0, keepdims=True)
        eq = vals == m
        tau = jnp.where(cnt < PEER_TOPK, m, tau)
        cnt = cnt + jnp.sum(jnp.where(eq, mult, 0.0), axis=0, keepdims=True)
        vals = jnp.where(eq, ninf, vals)
    cmax = cand[0:1]
    z = jnp.sum(jnp.where(cand >= tau, mult * jnp.exp(cand - cmax), 0.0), axis=0, keepdims=True)
    g1 = jnp.exp(s1 - ta[0:1]) * (1.0 / z)
    g2 = jnp.exp(s2 - tb[0:1])
    for lc in range(s1.shape[1] // PEER_LANES):
        ls = slice(lc * PEER_LANES, (lc + 1) * PEER_LANES)
        s1_o[lc] = s1[:, ls]
        s2_o[lc] = s2[:, ls]
        g1_o[lc] = g1[:, ls]
        g2_o[lc] = g2[:, ls]
        tau_o[lc] = tau[:, ls]


PEER_LANES = 128
TOPK_TT = 1024


def _peer_topk(qT, keys):
    t = qT.shape[1]
    tt = min(TOPK_TT, t)
    nl = tt // PEER_LANES
    blk = pl.BlockSpec((None, nl, PEER_NKEYS, PEER_LANES), lambda j, h: (h, j, 0, 0))
    shp = jax.ShapeDtypeStruct((PEER_HEADS, t // PEER_LANES, PEER_NKEYS, PEER_LANES), F32)
    return pl.pallas_call(
        _peer_topk_body, grid=(t // tt, PEER_HEADS),
        in_specs=[pl.BlockSpec((PEER_QDIM, tt), lambda j, h: (h, j)),
                  pl.BlockSpec((None, 2, PEER_NKEYS, PEER_HALF), lambda j, h: (h, 0, 0, 0))],
        out_specs=[blk, blk, blk, blk, pl.BlockSpec((None, nl, 1, PEER_LANES), lambda j, h: (h, j, 0, 0))],
        out_shape=[shp, shp, shp, shp, jax.ShapeDtypeStruct((PEER_HEADS, t // PEER_LANES, 1, PEER_LANES), F32)],
        compiler_params=_cp(2, 32), name="peer_topk")(qT, keys)


PEER_EK = 512
PEER_SUBS = PEER_EK // PEER_NKEYS
PEER_PIECES = 8
PEER_KPIECES = 8


def _peer_dense_body(xT_ref, u_ref, vT_ref, s1_ref, s2_ref, g1_ref, g2_ref, tau_ref, o_ref, hbuf, awbuf, *, nk):
    g = pl.program_id(0)
    d = o_ref.shape[0]
    h0, h1, aw0, aw1 = hbuf.at[0], hbuf.at[1], awbuf.at[0], awbuf.at[1]

    @pl.when(g == 0)
    def _():
        hbuf[...] = jnp.zeros(hbuf.shape, F32)
        awbuf[...] = jnp.zeros(awbuf.shape, BF16)

    @pl.when((g <= 2) | ((g - 2) % nk == 0))
    def _():
        o_ref[...] = jnp.zeros(o_ref.shape, F32)

    def stages(h_cur, h_prev, aw_cur, aw_prev):
        base = (jnp.clip(g - 1, 0, pl.num_programs(0) - 3) % nk) * PEER_SUBS
        dq = d // PEER_PIECES

        def second_matmul(q):
            ds_ = slice(q * dq, (q + 1) * dq)
            o_ref[ds_, :] += jnp.dot(vT_ref[ds_, :], aw_cur[...], preferred_element_type=F32)

        kq = d // PEER_KPIECES

        def first_matmul(q):
            ks = slice(q * kq, (q + 1) * kq)
            part = jnp.dot(u_ref[:, ks], xT_ref[ks, :], preferred_element_type=F32)
            if q == 0:
                h_cur[...] = part
            else:
                h_cur[...] += part

        def gate_tile(ii, lc):
            i1 = pl.ds(base + ii, 1)
            w = None
            for h in range(PEER_HEADS):
                c = s2_ref[h, lc] + s1_ref[h, lc, i1, :]
                wh = jnp.where(c >= tau_ref[h, lc], g2_ref[h, lc] * g1_ref[h, lc, i1, :], 0.0)
                w = wh if w is None else w + wh
            rs = slice(ii * PEER_NKEYS, (ii + 1) * PEER_NKEYS)
            ls = slice(lc * PEER_LANES, (lc + 1) * PEER_LANES)
            aw_prev[rs, ls] = (_gelu(h_prev[rs, ls]) * w).astype(BF16)

        tiles = [(ii, lc) for ii in range(PEER_SUBS) for lc in range(o_ref.shape[1] // PEER_LANES)]
        per = -(-len(tiles) // (2 * PEER_PIECES))
        for q in range(PEER_PIECES):
            second_matmul(q)
            for tl in tiles[(2 * q) * per:(2 * q + 1) * per]:
                gate_tile(*tl)
            if q % (PEER_PIECES // PEER_KPIECES) == 0:
                first_matmul(q // (PEER_PIECES // PEER_KPIECES))
            for tl in tiles[(2 * q + 1) * per:(2 * q + 2) * per]:
                gate_tile(*tl)

    @pl.when(g % 2 == 0)
    def _():
        stages(h0, h1, aw0, aw1)

    @pl.when(g % 2 == 1)
    def _():
        stages(h1, h0, aw1, aw0)


def _peer_dense(xmT, u_all, vT_all, layer, tk):
    d, t = xmT.shape
    tt = min(512, t)
    nl = tt // PEER_LANES
    nk = PEER_EXPERTS // PEER_EK
    s1, s2, g1, g2, tau = tk
    last = (t // tt) * nk - 1
    pair = lambda g, lag: jnp.clip(g - lag, 0, last)
    sblk = pl.BlockSpec((PEER_HEADS, nl, PEER_NKEYS, PEER_LANES), lambda g: (0, pair(g, 1) // nk, 0, 0))
    return pl.pallas_call(
        functools.partial(_peer_dense_body, nk=nk), grid=(last + 3,),
        in_specs=[pl.BlockSpec((d, tt), lambda g: (0, pair(g, 0) // nk)),
                  pl.BlockSpec((None, PEER_EK, d), lambda g: (layer, pair(g, 0) % nk, 0)),
                  pl.BlockSpec((None, d, PEER_EK), lambda g: (layer, 0, pair(g, 2) % nk)),
                  sblk, sblk, sblk, sblk,
                  pl.BlockSpec((PEER_HEADS, nl, 1, PEER_LANES), lambda g: (0, pair(g, 1) // nk, 0, 0))],
        out_specs=pl.BlockSpec((d, tt), lambda g: (0, pair(g, 2) // nk)),
        out_shape=jax.ShapeDtypeStruct((d, t), F32),
        scratch_shapes=[pltpu.VMEM((2, PEER_EK, tt), F32), pltpu.VMEM((2, PEER_EK, tt), BF16)],
        compiler_params=_cp(1, 48), name="peer_dense")(xmT, u_all, vT_all, s1, s2, g1, g2, tau)


def _ffn_ln_body(fT_ref, x1_ref, g2_ref, lng_ref, lnb_ref, scn_ref, shn_ref, x2_ref, *rest, alpha):
    x2 = _ln(alpha * x1_ref[...] + g2_ref[...] * fT_ref[...].T, lng_ref[...], lnb_ref[...])
    x2_ref[...] = x2
    if rest:
        rest[0][...] = (x2 * (1.0 + scn_ref[...]) + shn_ref[...]).astype(rest[0].dtype)


def _ffn_ln(fT, x1, g2, lng, lnb, scn, shn, alpha, emit_h):
    t, d = x1.shape
    tm = min(256, t)
    row = pl.BlockSpec((tm, d), lambda i: (i, 0))
    vec = pl.BlockSpec((1, d), lambda i: (0, 0))
    out_specs = [row, row] if emit_h else [row]
    out_shape = [jax.ShapeDtypeStruct((t, d), F32)] + ([jax.ShapeDtypeStruct((t, d), BF16)] if emit_h else [])
    return pl.pallas_call(
        functools.partial(_ffn_ln_body, alpha=alpha), grid=(t // tm,),
        in_specs=[pl.BlockSpec((d, tm), lambda i: (0, i)), row, vec, vec, vec, vec, vec],
        out_specs=out_specs, out_shape=out_shape,
        compiler_params=_cp(1, 32), name="ffn_ln")(fT, x1, g2, lng, lnb, scn, shn)


def kernel(x, c, ctx, c_ctx, w_ada, b_ada, w_in, b_in, s5_lam_re, s5_lam_im, s5_log_dt, s5_b_re, s5_b_im, s5_c_re, s5_c_im, s5_d, w_glu, b_glu, qn_gain, kn_gain, sink, sg_ln_g, sg_ln_b, w_sp, b_sp, w_br_a, w_br_b, w_br_c, w_br_d, w_out, ln1_g, ln1_b, ln2_g, ln2_b, w_pq, peer_keys, peer_u, peer_v):
    depth = w_in.shape[0]
    bsz, n_lat, d = x.shape
    assert bsz == 1 and d == D_MODEL and n_lat % 512 == 0 and ctx.shape[1] % BLOCK == 0
    n_ctx = ctx.shape[1]
    alpha = (2 * depth) ** 0.25
    x, ctx = x[0], ctx[0]

    row8 = lax.broadcasted_iota(jnp.int32, (8, d), 0)
    cond8 = jnp.where(row8 == 0, c[0][None], jnp.where(row8 == 1, c_ctx[None], 0.0))
    ada = [_ada(cond8, w_ada, b_ada, l) for l in range(depth)]
    cosf, sinf = _rope_tables(n_lat)
    cos1, sin0 = jnp.ones((n_ctx, HEAD_DIM), F32), jnp.zeros((n_ctx, HEAD_DIM), F32)
    vec = lambda a: a.reshape(1, -1).astype(F32)

    ref_off = {'a': 0, 'qg': 768, 'kg': 1792, 'vg': 2048, 'qw': 2304, 'kw': 3328, 'vw': 3584, 'du': 3840, 'dv': 4608}
    ref_w = {'a': 768, 'qg': 1024, 'kg': 256, 'vg': 256, 'qw': 1024, 'kw': 256, 'vw': 256, 'du': 768, 'dv': 768}
    order = ['a', 'kg', 'qg', 'qw', 'kw', 'vg', 'vw', 'du', 'dv']
    perm = lambda a: jnp.concatenate([a[..., ref_off[n]:ref_off[n] + ref_w[n]] for n in order], -1)
    perm_idx = np.concatenate([np.arange(ref_off[n], ref_off[n] + ref_w[n]) for n in order]).astype(np.int32)

    u_all, vT_all = peer_u.astype(BF16), jnp.swapaxes(peer_v, 1, 2).astype(BF16)

    h_lat = None
    for l in range(depth):
        need_ctx = l < depth - 1
        mods = [[ada[l][r:r + 1, i * d:(i + 1) * d] for i in range(6)] for r in range(2)]
        (sh1, sc1, g1, sh2, sc2, g2), (sh1c, sc1c, g1c, sh2c, sc2c, g2c) = mods
        if need_ctx:
            nxt = [ada[l + 1][r:r + 1, 0:2 * d] for r in range(2)]
            (shn, scn), (shnc, scnc) = [(m[:, :d], m[:, d:]) for m in nxt]
        else:
            shn = scn = shnc = scnc = jnp.zeros((1, d), F32)

        w_small, b_small = perm(w_in[l][:, :SMALL_W]).astype(BF16), vec(jnp.take(b_in[l], perm_idx))
        w_gate, b_gate = w_in[l][:, SMALL_W:].astype(BF16), vec(b_in[l][SMALL_W:])
        s5m = [_s5_mats(s5_lam_re[l, dr], s5_lam_im[l, dr], s5_log_dt[l, dr], s5_b_re[l, dr], s5_b_im[l, dr],
                        s5_c_re[l, dr], s5_c_im[l, dr]) for dr in range(2)]
        wg, bg, s5d = w_glu[l].astype(BF16), vec(b_glu[l]), vec(s5_d[l])
        qn, kn = vec(qn_gain[l]), vec(kn_gain[l])
        wsp, bsp = w_sp[l].astype(BF16), b_sp[l].astype(F32)[:, :, None]
        w_brs = [w[l].astype(BF16) for w in (w_br_a, w_br_b, w_br_c, w_br_d)]
        wo = w_out[l].astype(BF16)
        wpqT = w_pq[l].T.astype(BF16)
        keys = peer_keys[l].astype(BF16)

        if h_lat is None:
            h_lat, h_ctx = _modulate(x, sc1, sh1), _modulate(ctx, sc1c, sh1c)

        proj = _matmul(h_lat, w_small, b_small, tm=512, tn=1792, out_dtype=BF16, name="in_proj")
        gates = _matmul(h_lat, w_gate, b_gate, tm=512, tn=2048, out_dtype=BF16, name="in_proj_gates")
        proj_c = _matmul(h_ctx, w_small, b_small, tm=512, tn=1792, out_dtype=BF16, name="in_proj_ctx")

        y_dir, y_dir_c = [], []
        for dr in range(2):
            yc_, hc_end = _s5_dir(proj_c, s5m[dr], jnp.zeros((2, S5_CH), F32), bool(dr))
            yl_, _ = _s5_dir(proj, s5m[dr], hc_end, bool(dr))
            y_dir.append(yl_)
            y_dir_c.append(yc_)
        y_a = _s5_out(proj, y_dir[0], y_dir[1], s5d, wg, bg)

        qg, kg, vge, qw, kw = _prep(proj, cosf, sinf, qn, kn, ATTN_SCALE * math.log2(math.e))
        qgc, kgc, vgce, qwc, kwc = _prep(proj_c, cos1, sin0, qn, kn, ATTN_SCALE)
        y_b = _flash(qg, kg, vge, kgc, vgce)
        y_c = _window(qw, kw, proj, kwc, proj_c, sink[l])

        y_d = _spatial_gate(proj, vec(sg_ln_g[l]), vec(sg_ln_b[l]), wsp, bsp)

        merged = _merge((y_a, y_b, y_c, y_d), w_brs, gates)
        x1, xmT = _out_ln(merged, wo, x, g1, vec(ln1_g[l]), vec(ln1_b[l]), sc2, sh2, alpha)
        qT = _matmul(wpqT, xmT, None, tm=D_MODEL, tn=1024, out_dtype=F32, name="peer_query")
        ffnT = _peer_dense(xmT, u_all, vT_all, l, _peer_topk(qT, keys))
        res = _ffn_ln(ffnT, x1, g2, vec(ln2_g[l]), vec(ln2_b[l]), scn, shn, alpha, need_ctx)
        x = res[0]

        if need_ctx:
            h_lat = res[1]
            gates_c = _matmul(h_ctx, w_gate, b_gate, tm=512, tn=2048, out_dtype=BF16, name="in_proj_gates_ctx")
            y_a_c = _s5_out(proj_c, y_dir_c[0], y_dir_c[1], s5d, wg, bg)
            y_b_c = _ctx_attn(qgc, kgc, proj_c, COL_VG, sink[l], False)
            y_c_c = _ctx_attn(qwc, kwc, proj_c, COL_VW, sink[l], True)
            y_d_c = _spatial_gate(proj_c, vec(sg_ln_g[l]), vec(sg_ln_b[l]), wsp, bsp)
            merged_c = _merge((y_a_c, y_b_c, y_c_c, y_d_c), w_brs, gates_c)
            c1, cmT = _out_ln(merged_c, wo, ctx, g1c, vec(ln1_g[l]), vec(ln1_b[l]), sc2c, sh2c, alpha)
            qTc = _matmul(wpqT, cmT, None, tm=D_MODEL, tn=1024, out_dtype=F32, name="peer_query_ctx")
            ffnTc = _peer_dense(cmT, u_all, vT_all, l, _peer_topk(qTc, keys))
            ctx, h_ctx = _ffn_ln(ffnTc, c1, g2c, vec(ln2_g[l]), vec(ln2_b[l]), scnc, shnc, alpha, True)
    return x[None]
```

```python
import functools
import math

import jax
import jax.numpy as jnp
import numpy as np
from jax import lax
from jax.experimental import pallas as pl
from jax.experimental.pallas import tpu as pltpu

F32, BF16 = jnp.float32, jnp.bfloat16

D_MODEL = 2048
GRID_W = 64
BLOCK = 128
HEAD_DIM = 128
AXIS_DIM = HEAD_DIM // 2
ROPE_THETA = 10000.0
ATTN_SCALE = HEAD_DIM ** -0.5
NEG_INF = -1e30
EPS = 1e-6
S5_GROUPS, S5_GROUP_CH, S5_STATE = 48, 16, 64
S5_WIDTH = S5_GROUPS * S5_GROUP_CH
S5_CH = S5_GROUPS * S5_STATE
S5_GB = 16
S5_NB = S5_GROUPS // S5_GB
S5_KB = S5_GB * S5_GROUP_CH
S5_PB = S5_GB * S5_STATE
GA_HEADS, GA_KV = 8, 2
WA_HEADS, WA_KV = 8, 2
KV_GROUP = GA_HEADS // GA_KV
SG_GROUPS, SG_GROUP_CH = 6, 128
SG_WIDTH = SG_GROUPS * SG_GROUP_CH
CHUNK = 128
N_BRANCH = 4
PEER_HEADS, PEER_NKEYS, PEER_QDIM, PEER_TOPK = 8, 128, 256, 16
PEER_HALF = PEER_QDIM // 2
PEER_EXPERTS = PEER_NKEYS * PEER_NKEYS

COL_A, COL_KG, COL_QG, COL_QW, COL_KW, COL_VG, COL_VW, COL_DU, COL_DV = 0, 768, 1024, 2048, 3072, 3328, 3584, 3840, 4608
SMALL_W = 5376
GATE_W = N_BRANCH * D_MODEL

VMEM_CAP_MB = 56


def _cp(ndims, vmem_mb=32):
    return pltpu.CompilerParams(dimension_semantics=("arbitrary",) * ndims,
                                vmem_limit_bytes=min(vmem_mb, VMEM_CAP_MB) << 20)


def _gelu(y):
    return 0.5 * y * (1.0 + lax.erf(y * (2.0 ** -0.5)))


def _ln(y, g, b):
    mu = jnp.mean(y, -1, keepdims=True)
    yc = y - mu
    var = jnp.mean(yc * yc, -1, keepdims=True)
    return yc * lax.rsqrt(var + EPS) * g + b


_NT = (((1,), (1,)), ((), ()))


def _ada_body(c_ref, w_ref, b_ref, o_ref):
    cnd = c_ref[...]
    s = (cnd * jax.nn.sigmoid(cnd)).astype(BF16)
    o_ref[...] = jnp.dot(s, w_ref[...].astype(BF16), preferred_element_type=F32) + b_ref[...]


def _ada(cond8, w_ada, b_ada, l):
    depth, d, n = w_ada.shape
    tn = 1024
    return pl.pallas_call(
        _ada_body, grid=(n // tn,),
        in_specs=[pl.BlockSpec((8, d), lambda j: (0, 0)),
                  pl.BlockSpec((None, d, tn), lambda j: (l, 0, j)),
                  pl.BlockSpec((None, 1, tn), lambda j: (l, 0, j))],
        out_specs=pl.BlockSpec((8, tn), lambda j: (0, j)),
        out_shape=jax.ShapeDtypeStruct((8, n), F32),
        compiler_params=_cp(1, 32), name="ada_mod")(cond8, w_ada, b_ada.reshape(depth, 1, n))


def _mod_body(x_ref, sc_ref, sh_ref, o_ref):
    o_ref[...] = (x_ref[...] * (1.0 + sc_ref[...]) + sh_ref[...]).astype(o_ref.dtype)


def _modulate(x, sc, sh):
    t, d = x.shape
    tm = min(512, t)
    return pl.pallas_call(
        _mod_body, grid=(t // tm,),
        in_specs=[pl.BlockSpec((tm, d), lambda i: (i, 0)),
                  pl.BlockSpec((1, d), lambda i: (0, 0)),
                  pl.BlockSpec((1, d), lambda i: (0, 0))],
        out_specs=pl.BlockSpec((tm, d), lambda i: (i, 0)),
        out_shape=jax.ShapeDtypeStruct((t, d), BF16),
        compiler_params=_cp(1, 24), name="modulate")(x, sc, sh)


def _mm_body(a_ref, w_ref, b_ref, o_ref):
    acc = jnp.dot(a_ref[...], w_ref[...], preferred_element_type=F32)
    o_ref[...] = (acc + b_ref[...]).astype(o_ref.dtype)


def _mm_nobias_body(a_ref, w_ref, o_ref):
    o_ref[...] = jnp.dot(a_ref[...], w_ref[...], preferred_element_type=F32).astype(o_ref.dtype)


def _matmul(a, w, b, *, tm, tn, out_dtype, name):
    m, k = a.shape
    n = w.shape[1]
    tm, tn = min(tm, m), min(tn, n)
    in_specs = [pl.BlockSpec((tm, k), lambda j, i: (i, 0)),
                pl.BlockSpec((k, tn), lambda j, i: (0, j))]
    args = [a, w]
    body = _mm_nobias_body
    if b is not None:
        in_specs.append(pl.BlockSpec((1, tn), lambda j, i: (0, j)))
        args.append(b)
        body = _mm_body
    osz = jnp.dtype(out_dtype).itemsize
    vmem = 2 * (tm * k * 2 + k * tn * 2 + tm * tn * osz) + tm * tn * 4
    return pl.pallas_call(
        body, grid=(n // tn, m // tm), in_specs=in_specs,
        out_specs=pl.BlockSpec((tm, tn), lambda j, i: (i, j)),
        out_shape=jax.ShapeDtypeStruct((m, n), out_dtype),
        compiler_params=_cp(2, (vmem >> 20) + 8), name=name)(*args)


S5_TC = 256
S5_SEG = 8
S5_L = S5_TC // S5_SEG


def _s5_body(u_ref, perm_ref, permT_ref, bre_ref, bim_ref, cre_ref, cim_ref, apow_ref, h0_ref, y_ref, hT_ref,
             xr, xi, carry, hin_r, hin_i, *, rev, seg_len):
    @pl.when(pl.program_id(0) == 0)
    def _():
        carry[...] = h0_ref[...]

    up = jnp.dot(perm_ref[...], u_ref[...], preferred_element_type=F32).astype(BF16)
    for j in range(S5_NB):
        uj = up[:, j * S5_KB:(j + 1) * S5_KB]
        xr[:, j * S5_PB:(j + 1) * S5_PB] = jnp.dot(uj, bre_ref[j], preferred_element_type=F32)
        xi[:, j * S5_PB:(j + 1) * S5_PB] = jnp.dot(uj, bim_ref[j], preferred_element_type=F32)

    seg_order = range(S5_SEG - 1, -1, -1) if rev else range(S5_SEG)
    for j in range(S5_NB):
        cs = slice(j * S5_PB, (j + 1) * S5_PB)
        ar = jnp.broadcast_to(apow_ref[0, 0:1, cs], (S5_SEG, S5_PB))
        ai = jnp.broadcast_to(apow_ref[1, 0:1, cs], (S5_SEG, S5_PB))

        def local_step(k, c, cs=cs, ar=ar, ai=ai):
            hr, hi = c
            rows = pl.ds(pl.multiple_of(k * S5_SEG, S5_SEG), S5_SEG)
            nr = ar * hr - ai * hi + xr[rows, cs]
            ni = ar * hi + ai * hr + xi[rows, cs]
            xr[rows, cs] = nr
            xi[rows, cs] = ni
            return nr, ni

        zero = jnp.zeros((S5_SEG, S5_PB), F32)
        er, ei = lax.fori_loop(0, seg_len, local_step, (zero, zero), unroll=4)

        alr, ali = apow_ref[0, seg_len - 1:seg_len, cs], apow_ref[1, seg_len - 1:seg_len, cs]
        hr, hi = carry[0:1, cs], carry[1:2, cs]
        for s in seg_order:
            hin_r[s:s + 1, cs] = hr
            hin_i[s:s + 1, cs] = hi
            hr, hi = (er[s:s + 1] + alr * hr - ali * hi, ei[s:s + 1] + alr * hi + ali * hr)
        carry[0:1, cs] = hr
        carry[1:2, cs] = hi

        def fix_step(k, c, cs=cs):
            rows = pl.ds(pl.multiple_of(k * S5_SEG, S5_SEG), S5_SEG)
            pr, pi = apow_ref[0, pl.ds(k, 1), cs], apow_ref[1, pl.ds(k, 1), cs]
            h_r, h_i = hin_r[:, cs], hin_i[:, cs]
            xr[rows, cs] += pr * h_r - pi * h_i
            xi[rows, cs] += pr * h_i + pi * h_r
            return c

        lax.fori_loop(0, seg_len, fix_step, 0, unroll=4)

    yp = jnp.concatenate(
        [jnp.dot(xr[:, j * S5_PB:(j + 1) * S5_PB].astype(BF16), cre_ref[j], preferred_element_type=F32)
         + jnp.dot(xi[:, j * S5_PB:(j + 1) * S5_PB].astype(BF16), cim_ref[j], preferred_element_type=F32)
         for j in range(S5_NB)], axis=1)
    hi = yp.astype(BF16)
    r1 = yp - hi.astype(F32)
    mid = r1.astype(BF16)
    lo = (r1 - mid.astype(F32)).astype(BF16)
    parts = jnp.dot(permT_ref[...], jnp.concatenate([hi, mid, lo], axis=1), preferred_element_type=F32)
    y_ref[...] = parts[:, :S5_WIDTH] + parts[:, S5_WIDTH:2 * S5_WIDTH] + parts[:, 2 * S5_WIDTH:]
    hT_ref[...] = carry[...]


def _s5_perm(rev, tc):
    seg_len = tc // S5_SEG
    p = np.zeros((tc, tc), np.float32)
    for k in range(seg_len):
        for s in range(S5_SEG):
            p[k * S5_SEG + s, s * seg_len + ((seg_len - 1 - k) if rev else k)] = 1.0
    return jnp.asarray(p, BF16), jnp.asarray(p.T, BF16)


def _s5_dir(proj, mats, h0, rev):
    t = proj.shape[0]
    tc = min(S5_TC, t)
    assert t % tc == 0 and tc % (8 * S5_SEG) == 0
    nt = t // tc
    bre, bim, cre, cim, apow = mats
    tmap = (lambda i: (nt - 1 - i, 0)) if rev else (lambda i: (i, 0))
    full3 = lambda i: (0, 0, 0)
    full2 = lambda i: (0, 0)
    return pl.pallas_call(
        functools.partial(_s5_body, rev=rev, seg_len=tc // S5_SEG), grid=(nt,),
        in_specs=[pl.BlockSpec((tc, S5_WIDTH), tmap),
                  pl.BlockSpec((tc, tc), full2), pl.BlockSpec((tc, tc), full2),
                  pl.BlockSpec((S5_NB, S5_KB, S5_PB), full3), pl.BlockSpec((S5_NB, S5_KB, S5_PB), full3),
                  pl.BlockSpec((S5_NB, S5_PB, S5_KB), full3), pl.BlockSpec((S5_NB, S5_PB, S5_KB), full3),
                  pl.BlockSpec((2, S5_L, S5_CH), full3), pl.BlockSpec((2, S5_CH), full2)],
        out_specs=[pl.BlockSpec((tc, S5_WIDTH), tmap), pl.BlockSpec((2, S5_CH), full2)],
        out_shape=[jax.ShapeDtypeStruct((t, S5_WIDTH), F32), jax.ShapeDtypeStruct((2, S5_CH), F32)],
        scratch_shapes=[pltpu.VMEM((tc, S5_CH), F32), pltpu.VMEM((tc, S5_CH), F32), pltpu.VMEM((2, S5_CH), F32),
                        pltpu.VMEM((S5_SEG, S5_CH), F32), pltpu.VMEM((S5_SEG, S5_CH), F32)],
        compiler_params=_cp(1, 48), name="s5_scan_bwd" if rev else "s5_scan_fwd")(
            proj, *_s5_perm(rev, tc), bre, bim, cre, cim, apow, h0)


def _s5_out_body(u_ref, yf_ref, yb_ref, d_ref, wg_ref, bg_ref, o_ref):
    y = u_ref[...].astype(F32) * d_ref[...] + yf_ref[...] + yb_ref[...]
    g = _gelu(y)
    z = jnp.dot(g.astype(BF16), wg_ref[...], preferred_element_type=F32) + bg_ref[...]
    o_ref[...] = (g * jax.nn.sigmoid(z)).astype(o_ref.dtype)


def _s5_out(proj, yf, yb, d, wg, bg):
    t = proj.shape[0]
    tm = min(512, t)
    row = lambda i: (i, 0)
    full = lambda i: (0, 0)
    return pl.pallas_call(
        _s5_out_body, grid=(t // tm,),
        in_specs=[pl.BlockSpec((tm, S5_WIDTH), row), pl.BlockSpec((tm, S5_WIDTH), row), pl.BlockSpec((tm, S5_WIDTH), row),
                  pl.BlockSpec((1, S5_WIDTH), full), pl.BlockSpec((S5_WIDTH, S5_WIDTH), full), pl.BlockSpec((1, S5_WIDTH), full)],
        out_specs=pl.BlockSpec((tm, S5_WIDTH), row),
        out_shape=jax.ShapeDtypeStruct((t, S5_WIDTH), BF16),
        compiler_params=_cp(1, 24), name="s5_glu")(proj, yf, yb, d, wg, bg)


def _s5_mats(lam_re, lam_im, log_dt, b_re, b_im, c_re, c_im):
    dt = jnp.exp(log_dt)[:, None]
    mag = jnp.exp(lam_re * dt)
    ar, ai = mag * jnp.cos(lam_im * dt), mag * jnp.sin(lam_im * dt)
    nr, ni = ar - 1.0, ai
    den = lam_re * lam_re + lam_im * lam_im
    cr = (nr * lam_re + ni * lam_im) / den
    ci = (ni * lam_re - nr * lam_im) / den
    fre = cr[:, :, None] * b_re - ci[:, :, None] * b_im
    fim = cr[:, :, None] * b_im + ci[:, :, None] * b_re
    eye = jnp.eye(S5_GB, dtype=F32)

    def blk_in(f):
        f = f.reshape(S5_NB, S5_GB, S5_STATE, S5_GROUP_CH)
        m = jnp.einsum('jgph,gk->jghkp', f, eye)
        return m.reshape(S5_NB, S5_KB, S5_PB).astype(BF16)

    def blk_out(c):
        c = c.reshape(S5_NB, S5_GB, S5_GROUP_CH, S5_STATE)
        m = jnp.einsum('jghp,gk->jgpkh', c, eye)
        return m.reshape(S5_NB, S5_PB, S5_KB).astype(BF16)

    kk = jnp.arange(1, S5_L + 1, dtype=F32)[:, None, None]
    magk, angk = jnp.exp(lam_re * dt * kk), lam_im * dt * kk
    is_re = lax.broadcasted_iota(jnp.int32, (2, S5_L, S5_CH), 0) == 0
    apow = jnp.where(is_re, (magk * jnp.cos(angk)).reshape(1, S5_L, S5_CH), (magk * jnp.sin(angk)).reshape(1, S5_L, S5_CH))
    return blk_in(fre), blk_in(fim), blk_out(c_re), blk_out(-c_im), apow


def _prep_body(qg_ref, kg_ref, vg_ref, qw_ref, kw_ref, cos_ref, sin_ref, qn_ref, kn_ref, oqg, okg, ovg, oqw, okw, *,
               qg_scale):
    cosf, sinf = cos_ref[...], sin_ref[...]
    lane = lax.broadcasted_iota(jnp.int32, cosf.shape, 1)
    low = (lane % AXIS_DIM) < (AXIS_DIM // 2)

    def rope(t):
        partner = jnp.where(low, pltpu.roll(t, HEAD_DIM - AXIS_DIM // 2, 1), pltpu.roll(t, AXIS_DIM // 2, 1))
        return t * cosf + partner * sinf

    def rms(t, g):
        return t * lax.rsqrt(jnp.mean(t * t, -1, keepdims=True) + EPS) * g

    for h in range(GA_HEADS):
        cs = slice(h * HEAD_DIM, (h + 1) * HEAD_DIM)
        oqg[:, cs] = (rope(rms(qg_ref[:, cs].astype(F32), qn_ref[...])) * qg_scale).astype(BF16)
        oqw[:, cs] = (rope(qw_ref[:, cs].astype(F32)) * ATTN_SCALE).astype(BF16)
    for h in range(GA_KV):
        cs = slice(h * HEAD_DIM, (h + 1) * HEAD_DIM)
        okg[:, cs] = rope(rms(kg_ref[:, cs].astype(F32), kn_ref[...])).astype(BF16)
        okw[:, cs] = rope(kw_ref[:, cs].astype(F32)).astype(BF16)
        ovg[:, 2 * h * HEAD_DIM:(2 * h + 1) * HEAD_DIM] = vg_ref[:, cs]
        ovg[:, (2 * h + 1) * HEAD_DIM:(2 * h + 2) * HEAD_DIM] = jnp.ones((vg_ref.shape[0], HEAD_DIM), BF16)


def _prep(proj, cosf, sinf, qn, kn, qg_scale):
    t = proj.shape[0]
    tr = min(256, t)
    qw_, kw_ = GA_HEADS * HEAD_DIM, GA_KV * HEAD_DIM
    row = lambda i: (i, 0)
    full = lambda i: (0, 0)
    return pl.pallas_call(
        functools.partial(_prep_body, qg_scale=qg_scale), grid=(t // tr,),
        in_specs=[pl.BlockSpec((tr, qw_), lambda i: (i, COL_QG // qw_)),
                  pl.BlockSpec((tr, kw_), lambda i: (i, COL_KG // kw_)),
                  pl.BlockSpec((tr, kw_), lambda i: (i, COL_VG // kw_)),
                  pl.BlockSpec((tr, qw_), lambda i: (i, COL_QW // qw_)),
                  pl.BlockSpec((tr, kw_), lambda i: (i, COL_KW // kw_)),
                  pl.BlockSpec((tr, HEAD_DIM), row), pl.BlockSpec((tr, HEAD_DIM), row),
                  pl.BlockSpec((1, HEAD_DIM), full), pl.BlockSpec((1, HEAD_DIM), full)],
        out_specs=[pl.BlockSpec((tr, qw_), row), pl.BlockSpec((tr, kw_), row), pl.BlockSpec((tr, 2 * kw_), row),
                   pl.BlockSpec((tr, qw_), row), pl.BlockSpec((tr, kw_), row)],
        out_shape=[jax.ShapeDtypeStruct((t, qw_), BF16), jax.ShapeDtypeStruct((t, kw_), BF16),
                   jax.ShapeDtypeStruct((t, 2 * kw_), BF16),
                   jax.ShapeDtypeStruct((t, qw_), BF16), jax.ShapeDtypeStruct((t, kw_), BF16)],
        compiler_params=_cp(1, 24), name="qk_prep")(proj, proj, proj, proj, proj, cosf, sinf, qn, kn)


def _rope_tables(n_lat):
    rows = n_lat // GRID_W
    inv = jnp.power(ROPE_THETA, -jnp.arange(0, AXIS_DIM, 2, dtype=F32) / AXIS_DIM)
    ang_r = jnp.arange(rows, dtype=F32)[:, None] * inv
    ang_c = jnp.arange(GRID_W, dtype=F32)[:, None] * inv
    rep = lambda a: jnp.repeat(a, GRID_W, axis=0)
    til = lambda a: jnp.tile(a, (rows, 1))
    cr, sr, cc, sc = rep(jnp.cos(ang_r)), rep(jnp.sin(ang_r)), til(jnp.cos(ang_c)), til(jnp.sin(ang_c))
    return jnp.concatenate([cr, cr, cc, cc], -1), jnp.concatenate([-sr, sr, -sc, sc], -1)


def _flash_body(q_ref, kc_ref, vc_ref, k_ref, v_ref, o_ref, qs, m_s, acc, *, tq, tk, nk):
    for g in range(KV_GROUP):
        qs[g * tq:(g + 1) * tq, :] = q_ref[:, g * HEAD_DIM:(g + 1) * HEAD_DIM]
    m_s[...] = jnp.full(m_s.shape, -jnp.inf, F32)
    acc[...] = jnp.zeros(acc.shape, F32)

    def update(k, v):
        for g in range(KV_GROUP):
            rs = slice(g * tq, (g + 1) * tq)
            s = lax.dot_general(qs[rs, :], k, _NT, preferred_element_type=F32)
            m_prev = m_s[rs, :]
            m_next = jnp.maximum(m_prev, jnp.max(s, axis=1, keepdims=True))
            alpha = jnp.exp2(m_prev - m_next)
            p = jnp.exp2(s - jnp.concatenate([m_next] * (s.shape[1] // HEAD_DIM), axis=1))
            acc[rs, :] = (jnp.concatenate([alpha, alpha], axis=1) * acc[rs, :]
                          + jnp.dot(p.astype(BF16), v, preferred_element_type=F32))
            m_s[rs, :] = m_next

    update(kc_ref[...], vc_ref[...])

    def body(j, carry):
        rows = pl.ds(pl.multiple_of(j * tk, tk), tk)
        update(k_ref[rows, :], v_ref[rows, :])
        return carry

    lax.fori_loop(0, nk, body, 0, unroll=FLASH_UNROLL)
    out = acc[:, :HEAD_DIM] / acc[:, HEAD_DIM:]
    for g in range(KV_GROUP):
        o_ref[:, g * HEAD_DIM:(g + 1) * HEAD_DIM] = out[g * tq:(g + 1) * tq, :].astype(o_ref.dtype)


FLASH_TK = 512
FLASH_UNROLL = 32


def _flash(q, kg, vge, kgc, vgce):
    t = q.shape[0]
    n_ctx = kgc.shape[0]
    tq = min(256, t)
    tk = min(FLASH_TK, t)
    gw = KV_GROUP * HEAD_DIM
    head = lambda h, i: (0, h)
    return pl.pallas_call(
        functools.partial(_flash_body, tq=tq, tk=tk, nk=t // tk), grid=(GA_KV, t // tq),
        in_specs=[pl.BlockSpec((tq, gw), lambda h, i: (i, h)),
                  pl.BlockSpec((n_ctx, HEAD_DIM), head), pl.BlockSpec((n_ctx, 2 * HEAD_DIM), head),
                  pl.BlockSpec((t, HEAD_DIM), head), pl.BlockSpec((t, 2 * HEAD_DIM), head)],
        out_specs=pl.BlockSpec((tq, gw), lambda h, i: (i, h)),
        out_shape=jax.ShapeDtypeStruct((t, GA_HEADS * HEAD_DIM), BF16),
        scratch_shapes=[pltpu.VMEM((KV_GROUP * tq, HEAD_DIM), BF16), pltpu.VMEM((KV_GROUP * tq, HEAD_DIM), F32),
                        pltpu.VMEM((KV_GROUP * tq, 2 * HEAD_DIM), F32)],
        compiler_params=_cp(2, 48), name="global_attn")(q, kgc, vgce, kg, vge)


WIN_QB = 8


def _win_body(q_ref, km_ref, k0_ref, kp_ref, vm_ref, v0_ref, vp_ref, kc_ref, vc_ref, sink_ref, o_ref, *, nb):
    kslab = jnp.concatenate([km_ref[...], k0_ref[...], kp_ref[...]], axis=0)
    vslab = jnp.concatenate([vm_ref[...], v0_ref[...], vp_ref[...]], axis=0)
    sink = sink_ref[0]
    shape = (KV_GROUP * BLOCK, 3 * BLOCK)
    r = lax.broadcasted_iota(jnp.int32, shape, 0) % BLOCK
    c = lax.broadcasted_iota(jnp.int32, shape, 1)
    cc = c % BLOCK
    for b in range(WIN_QB):
        i = pl.program_id(1) * WIN_QB + b
        rows = slice(b * BLOCK, (b + 1) * BLOCK)
        q = jnp.concatenate([q_ref[rows, g * HEAD_DIM:(g + 1) * HEAD_DIM] for g in range(KV_GROUP)], axis=0)
        kw, vw = kslab[b * BLOCK:(b + 3) * BLOCK], vslab[b * BLOCK:(b + 3) * BLOCK]
        s_w = lax.dot_general(q, kw, _NT, preferred_element_type=F32)
        iv = jnp.full(shape, i, jnp.int32)
        valid = (((c < BLOCK) & (cc >= r) & (iv >= 1)) | ((c >= BLOCK) & (c < 2 * BLOCK))
                 | ((c >= 2 * BLOCK) & (cc <= r) & (iv + 1 < nb)))
        s_w = jnp.where(valid, s_w, NEG_INF)
        s_c = lax.dot_general(q, kc_ref[...], _NT, preferred_element_type=F32)
        m = jnp.maximum(jnp.maximum(jnp.max(s_w, -1, keepdims=True), jnp.max(s_c, -1, keepdims=True)), sink)
        pw, pc = jnp.exp(s_w - m), jnp.exp(s_c - m)
        l = jnp.sum(pw, -1, keepdims=True) + jnp.sum(pc, -1, keepdims=True) + jnp.exp(sink - m)
        inv = 1.0 / l
        out = (jnp.dot((pc * inv).astype(BF16), vc_ref[...], preferred_element_type=F32)
               + jnp.dot((pw * inv).astype(BF16), vw, preferred_element_type=F32))
        for g in range(KV_GROUP):
            o_ref[rows, g * HEAD_DIM:(g + 1) * HEAD_DIM] = out[g * BLOCK:(g + 1) * BLOCK, :].astype(o_ref.dtype)


def _sink_rows(sink, rows_per_head):
    return jnp.repeat(sink.astype(F32).reshape(WA_KV, KV_GROUP), rows_per_head, axis=1)[:, :, None]


def _window(qw, kw, proj, kwc, proj_c, sink):
    t = qw.shape[0]
    n_ctx = kwc.shape[0]
    nb = t // BLOCK
    gw = KV_GROUP * HEAD_DIM
    vcol = COL_VW // HEAD_DIM
    assert nb % WIN_QB == 0
    edge = lambda f: pl.BlockSpec((BLOCK, HEAD_DIM), f)
    main = lambda f: pl.BlockSpec((WIN_QB * BLOCK, HEAD_DIM), f)
    before = lambda i: jnp.maximum(i * WIN_QB - 1, 0)
    after = lambda i: jnp.minimum((i + 1) * WIN_QB, nb - 1)
    return pl.pallas_call(
        functools.partial(_win_body, nb=nb), grid=(WA_KV, nb // WIN_QB),
        in_specs=[pl.BlockSpec((WIN_QB * BLOCK, gw), lambda h, i: (i, h)),
                  edge(lambda h, i: (before(i), h)), main(lambda h, i: (i, h)), edge(lambda h, i: (after(i), h)),
                  edge(lambda h, i: (before(i), vcol + h)), main(lambda h, i: (i, vcol + h)),
                  edge(lambda h, i: (after(i), vcol + h)),
                  pl.BlockSpec((n_ctx, HEAD_DIM), lambda h, i: (0, h)),
                  pl.BlockSpec((n_ctx, HEAD_DIM), lambda h, i: (0, vcol + h)),
                  pl.BlockSpec((1, KV_GROUP * BLOCK, 1), lambda h, i: (h, 0, 0))],
        out_specs=pl.BlockSpec((WIN_QB * BLOCK, gw), lambda h, i: (i, h)),
        out_shape=jax.ShapeDtypeStruct((t, WA_HEADS * HEAD_DIM), BF16),
        compiler_params=_cp(2, 24), name="window_attn")(
            qw, kw, kw, kw, proj, proj, proj, kwc, proj_c, _sink_rows(sink, BLOCK))


def _ctx_attn_body(q_ref, k_ref, v_ref, sink_ref, o_ref, *, n_ctx, use_sink):
    q = jnp.concatenate([q_ref[:, g * HEAD_DIM:(g + 1) * HEAD_DIM] for g in range(KV_GROUP)], axis=0)
    s = lax.dot_general(q, k_ref[...], _NT, preferred_element_type=F32)
    m = jnp.max(s, -1, keepdims=True)
    if use_sink:
        m = jnp.maximum(m, sink_ref[0])
    p = jnp.exp(s - m)
    l = jnp.sum(p, -1, keepdims=True)
    if use_sink:
        l = l + jnp.exp(sink_ref[0] - m)
    out = jnp.dot((p / l).astype(BF16), v_ref[...], preferred_element_type=F32)
    for g in range(KV_GROUP):
        o_ref[:, g * HEAD_DIM:(g + 1) * HEAD_DIM] = out[g * n_ctx:(g + 1) * n_ctx, :].astype(o_ref.dtype)


def _ctx_attn(q, k, proj_c, vcol0, sink, use_sink):
    n_ctx = q.shape[0]
    gw = KV_GROUP * HEAD_DIM
    vcol = vcol0 // HEAD_DIM
    return pl.pallas_call(
        functools.partial(_ctx_attn_body, n_ctx=n_ctx, use_sink=use_sink), grid=(GA_KV,),
        in_specs=[pl.BlockSpec((n_ctx, gw), lambda h: (0, h)),
                  pl.BlockSpec((n_ctx, HEAD_DIM), lambda h: (0, h)),
                  pl.BlockSpec((n_ctx, HEAD_DIM), lambda h: (0, vcol + h)),
                  pl.BlockSpec((1, KV_GROUP * n_ctx, 1), lambda h: (h, 0, 0))],
        out_specs=pl.BlockSpec((n_ctx, gw), lambda h: (0, h)),
        out_shape=jax.ShapeDtypeStruct((n_ctx, GA_HEADS * HEAD_DIM), BF16),
        compiler_params=_cp(1, 24), name="ctx_attn_sink" if use_sink else "ctx_attn")(
            q, k, proj_c, _sink_rows(sink, n_ctx))


def _sg_body(u_ref, v_ref, g_ref, b_ref, w_ref, bs_ref, o_ref, *, nch):
    for n in range(nch):
        rs = slice(n * CHUNK, (n + 1) * CHUNK)
        vn = _ln(v_ref[rs, :].astype(F32), g_ref[...], b_ref[...]).astype(BF16)
        for g in range(SG_GROUPS):
            cs = slice(g * SG_GROUP_CH, (g + 1) * SG_GROUP_CH)
            s = jnp.dot(w_ref[g], vn[:, cs], preferred_element_type=F32) + bs_ref[g]
            o_ref[rs, cs] = (u_ref[rs, cs].astype(F32) * s).astype(o_ref.dtype)


def _spatial_gate(proj, g_v, b_v, w_sp, b_sp):
    t = proj.shape[0]
    tm = min(512, t)
    full = lambda i: (0, 0)
    return pl.pallas_call(
        functools.partial(_sg_body, nch=tm // CHUNK), grid=(t // tm,),
        in_specs=[pl.BlockSpec((tm, SG_WIDTH), lambda i: (i, COL_DU // SG_WIDTH)),
                  pl.BlockSpec((tm, SG_WIDTH), lambda i: (i, COL_DV // SG_WIDTH)),
                  pl.BlockSpec((1, SG_WIDTH), full), pl.BlockSpec((1, SG_WIDTH), full),
                  pl.BlockSpec((SG_GROUPS, CHUNK, CHUNK), lambda i: (0, 0, 0)),
                  pl.BlockSpec((SG_GROUPS, CHUNK, 1), lambda i: (0, 0, 0))],
        out_specs=pl.BlockSpec((tm, SG_WIDTH), lambda i: (i, 0)),
        out_shape=jax.ShapeDtypeStruct((t, SG_WIDTH), BF16),
        compiler_params=_cp(1, 24), name="spatial_gate")(proj, proj, g_v, b_v, w_sp, b_sp)


def _merge_body(ya, yb, yc, yd, wa, wb, wc, wd, ga, gb, gc, gd, o_ref):
    def br(y, w, g):
        return jax.nn.sigmoid(g[...].astype(F32)) * jnp.dot(y[...], w[...], preferred_element_type=F32)
    o_ref[...] = (br(ya, wa, ga) + br(yb, wb, gb) + br(yc, wc, gc) + br(yd, wd, gd)).astype(o_ref.dtype)


def _merge(ys, ws, gates):
    t = ys[0].shape[0]
    tm, tn = min(512, t), 1024
    nn = D_MODEL // tn
    in_specs = [pl.BlockSpec((tm, y.shape[1]), lambda j, i: (i, 0)) for y in ys]
    in_specs += [pl.BlockSpec((w.shape[0], tn), lambda j, i: (0, j)) for w in ws]
    in_specs += [pl.BlockSpec((tm, tn), functools.partial(lambda j, i, b: (i, b * nn + j), b=b)) for b in range(N_BRANCH)]
    return pl.pallas_call(
        _merge_body, grid=(nn, t // tm), in_specs=in_specs,
        out_specs=pl.BlockSpec((tm, tn), lambda j, i: (i, j)),
        out_shape=jax.ShapeDtypeStruct((t, D_MODEL), BF16),
        compiler_params=_cp(2, 48), name="gated_merge")(*ys, *ws, gates, gates, gates, gates)


def _out_ln_body(mg_ref, w_ref, x_ref, g1_ref, lng_ref, lnb_ref, sc_ref, sh_ref, x1_ref, xmT_ref, *, alpha):
    o = jnp.dot(mg_ref[...], w_ref[...], preferred_element_type=F32)
    x1 = _ln(alpha * x_ref[...] + g1_ref[...] * o, lng_ref[...], lnb_ref[...])
    x1_ref[...] = x1
    xmT_ref[...] = (x1 * (1.0 + sc_ref[...]) + sh_ref[...]).T.astype(xmT_ref.dtype)


def _out_ln(merged, w_out, x, g1, lng, lnb, sc2, sh2, alpha):
    t, d = x.shape
    tm = min(256, t)
    row = lambda i: (i, 0)
    vec = pl.BlockSpec((1, d), lambda i: (0, 0))
    return pl.pallas_call(
        functools.partial(_out_ln_body, alpha=alpha), grid=(t // tm,),
        in_specs=[pl.BlockSpec((tm, d), row), pl.BlockSpec((d, d), lambda i: (0, 0)), pl.BlockSpec((tm, d), row),
                  vec, vec, vec, vec, vec],
        out_specs=[pl.BlockSpec((tm, d), row), pl.BlockSpec((d, tm), lambda i: (0, i))],
        out_shape=[jax.ShapeDtypeStruct((t, d), F32), jax.ShapeDtypeStruct((d, t), BF16)],
        compiler_params=_cp(1, 48), name="out_proj_ln")(merged, w_out, x, g1, lng, lnb, sc2, sh2)


def _peer_topk_body(q_ref, k_ref, s1_o, s2_o, g1_o, g2_o, tau_o):
    q = q_ref[...].astype(BF16)
    s1 = jnp.dot(k_ref[0], q[:PEER_HALF], preferred_element_type=F32)
    s2 = jnp.dot(k_ref[1], q[PEER_HALF:], preferred_element_type=F32)
    ninf = -jnp.inf

    def top_distinct(s):
        vals, tops, cnts = s, [], []
        for _ in range(PEER_TOPK):
            m = jnp.max(vals, axis=0, keepdims=True)
            eq = vals == m
            tops.append(m)
            cnts.append(jnp.sum(eq.astype(F32), axis=0, keepdims=True))
            vals = jnp.where(eq, ninf, vals)
        return jnp.concatenate(tops, 0), jnp.concatenate(cnts, 0)

    ta, na = top_distinct(s1)
    tb, nb = top_distinct(s2)
    nrow = [PEER_TOPK // (k + 1) for k in range(PEER_TOPK)]
    pad = -sum(nrow) % 8
    cand = jnp.concatenate([ta[k:k + 1] + tb[:nrow[k]] for k in range(PEER_TOPK)]
                           + [jnp.full((pad, ta.shape[1]), ninf, F32)], 0)
    mult = jnp.concatenate([na[k:k + 1] * nb[:nrow[k]] for k in range(PEER_TOPK)]
                           + [jnp.zeros((pad, ta.shape[1]), F32)], 0)
    vals = cand
    cnt = jnp.zeros_like(cand[0:1])
    tau = jnp.full_like(cand[0:1], ninf)
    for _ in range(PEER_TOPK):
        m = jnp.max(vals, axis=0, keepdims=True)
        eq = vals == m
        tau = jnp.where(cnt < PEER_TOPK, m, tau)
        cnt = cnt + jnp.sum(jnp.where(eq, mult, 0.0), axis=0, keepdims=True)
        vals = jnp.where(eq, ninf, vals)
    cmax = cand[0:1]
    z = jnp.sum(jnp.where(cand >= tau, mult * jnp.exp(cand - cmax), 0.0), axis=0, keepdims=True)
    g1 = jnp.exp(s1 - ta[0:1]) * (1.0 / z)
    g2 = jnp.exp(s2 - tb[0:1])
    for lc in range(s1.shape[1] // PEER_LANES):
        ls = slice(lc * PEER_LANES, (lc + 1) * PEER_LANES)
        s1_o[lc] = s1[:, ls]
        s2_o[lc] = s2[:, ls]
        g1_o[lc] = g1[:, ls]
        g2_o[lc] = g2[:, ls]
        tau_o[lc] = tau[:, ls]


PEER_LANES = 128
TOPK_TT = 1024


def _peer_topk(qT, keys):
    t = qT.shape[1]
    tt = min(TOPK_TT, t)
    nl = tt // PEER_LANES
    blk = pl.BlockSpec((None, nl, PEER_NKEYS, PEER_LANES), lambda j, h: (h, j, 0, 0))
    shp = jax.ShapeDtypeStruct((PEER_HEADS, t // PEER_LANES, PEER_NKEYS, PEER_LANES), F32)
    return pl.pallas_call(
        _peer_topk_body, grid=(t // tt, PEER_HEADS),
        in_specs=[pl.BlockSpec((PEER_QDIM, tt), lambda j, h: (h, j)),
                  pl.BlockSpec((None, 2, PEER_NKEYS, PEER_HALF), lambda j, h: (h, 0, 0, 0))],
        out_specs=[blk, blk, blk, blk, pl.BlockSpec((None, nl, 1, PEER_LANES), lambda j, h: (h, j, 0, 0))],
        out_shape=[shp, shp, shp, shp, jax.ShapeDtypeStruct((PEER_HEADS, t // PEER_LANES, 1, PEER_LANES), F32)],
        compiler_params=_cp(2, 32), name="peer_topk")(qT, keys)


PEER_EK = 512
PEER_SUBS = PEER_EK // PEER_NKEYS
PEER_PIECES = 8
PEER_KPIECES = 8


def _peer_dense_body(xT_ref, u_ref, vT_ref, s1_ref, s2_ref, g1_ref, g2_ref, tau_ref, o_ref, hbuf, awbuf, *, nk):
    g = pl.program_id(0)
    d = o_ref.shape[0]
    h0, h1, aw0, aw1 = hbuf.at[0], hbuf.at[1], awbuf.at[0], awbuf.at[1]

    @pl.when(g == 0)
    def _():
        hbuf[...] = jnp.zeros(hbuf.shape, F32)
        awbuf[...] = jnp.zeros(awbuf.shape, BF16)

    @pl.when((g <= 2) | ((g - 2) % nk == 0))
    def _():
        o_ref[...] = jnp.zeros(o_ref.shape, F32)

    def stages(h_cur, h_prev, aw_cur, aw_prev):
        base = (jnp.clip(g - 1, 0, pl.num_programs(0) - 3) % nk) * PEER_SUBS
        dq = d // PEER_PIECES

        def second_matmul(q):
            ds_ = slice(q * dq, (q + 1) * dq)
            o_ref[ds_, :] += jnp.dot(vT_ref[ds_, :], aw_cur[...], preferred_element_type=F32)

        kq = d // PEER_KPIECES

        def first_matmul(q):
            ks = slice(q * kq, (q + 1) * kq)
            part = jnp.dot(u_ref[:, ks], xT_ref[ks, :], preferred_element_type=F32)
            if q == 0:
                h_cur[...] = part
            else:
                h_cur[...] += part

        def gate_tile(ii, lc):
            i1 = pl.ds(base + ii, 1)
            w = None
            for h in range(PEER_HEADS):
                c = s2_ref[h, lc] + s1_ref[h, lc, i1, :]
                wh = jnp.where(c >= tau_ref[h, lc], g2_ref[h, lc] * g1_ref[h, lc, i1, :], 0.0)
                w = wh if w is None else w + wh
            rs = slice(ii * PEER_NKEYS, (ii + 1) * PEER_NKEYS)
            ls = slice(lc * PEER_LANES, (lc + 1) * PEER_LANES)
            aw_prev[rs, ls] = (_gelu(h_prev[rs, ls]) * w).astype(BF16)

        tiles = [(ii, lc) for ii in range(PEER_SUBS) for lc in range(o_ref.shape[1] // PEER_LANES)]
        per = -(-len(tiles) // (2 * PEER_PIECES))
        for q in range(PEER_PIECES):
            second_matmul(q)
            for tl in tiles[(2 * q) * per:(2 * q + 1) * per]:
                gate_tile(*tl)
            if q % (PEER_PIECES // PEER_KPIECES) == 0:
                first_matmul(q // (PEER_PIECES // PEER_KPIECES))
            for tl in tiles[(2 * q + 1) * per:(2 * q + 2) * per]:
                gate_tile(*tl)

    @pl.when(g % 2 == 0)
    def _():
        stages(h0, h1, aw0, aw1)

    @pl.when(g % 2 == 1)
    def _():
        stages(h1, h0, aw1, aw0)


def _peer_dense(xmT, u_all, vT_all, layer, tk):
    d, t = xmT.shape
    tt = min(512, t)
    nl = tt // PEER_LANES
    nk = PEER_EXPERTS // PEER_EK
    s1, s2, g1, g2, tau = tk
    last = (t // tt) * nk - 1
    pair = lambda g, lag: jnp.clip(g - lag, 0, last)
    sblk = pl.BlockSpec((PEER_HEADS, nl, PEER_NKEYS, PEER_LANES), lambda g: (0, pair(g, 1) // nk, 0, 0))
    return pl.pallas_call(
        functools.partial(_peer_dense_body, nk=nk), grid=(last + 3,),
        in_specs=[pl.BlockSpec((d, tt), lambda g: (0, pair(g, 0) // nk)),
                  pl.BlockSpec((None, PEER_EK, d), lambda g: (layer, pair(g, 0) % nk, 0)),
                  pl.BlockSpec((None, d, PEER_EK), lambda g: (layer, 0, pair(g, 2) % nk)),
                  sblk, sblk, sblk, sblk,
                  pl.BlockSpec((PEER_HEADS, nl, 1, PEER_LANES), lambda g: (0, pair(g, 1) // nk, 0, 0))],
        out_specs=pl.BlockSpec((d, tt), lambda g: (0, pair(g, 2) // nk)),
        out_shape=jax.ShapeDtypeStruct((d, t), F32),
        scratch_shapes=[pltpu.VMEM((2, PEER_EK, tt), F32), pltpu.VMEM((2, PEER_EK, tt), BF16)],
        compiler_params=_cp(1, 48), name="peer_dense")(xmT, u_all, vT_all, s1, s2, g1, g2, tau)


def _ffn_ln_body(fT_ref, x1_ref, g2_ref, lng_ref, lnb_ref, scn_ref, shn_ref, x2_ref, *rest, alpha):
    x2 = _ln(alpha * x1_ref[...] + g2_ref[...] * fT_ref[...].T, lng_ref[...], lnb_ref[...])
    x2_ref[...] = x2
    if rest:
        rest[0][...] = (x2 * (1.0 + scn_ref[...]) + shn_ref[...]).astype(rest[0].dtype)


def _ffn_ln(fT, x1, g2, lng, lnb, scn, shn, alpha, emit_h):
    t, d = x1.shape
    tm = min(256, t)
    row = pl.BlockSpec((tm, d), lambda i: (i, 0))
    vec = pl.BlockSpec((1, d), lambda i: (0, 0))
    out_specs = [row, row] if emit_h else [row]
    out_shape = [jax.ShapeDtypeStruct((t, d), F32)] + ([jax.ShapeDtypeStruct((t, d), BF16)] if emit_h else [])
    return pl.pallas_call(
        functools.partial(_ffn_ln_body, alpha=alpha), grid=(t // tm,),
        in_specs=[pl.BlockSpec((d, tm), lambda i: (0, i)), row, vec, vec, vec, vec, vec],
        out_specs=out_specs, out_shape=out_shape,
        compiler_params=_cp(1, 32), name="ffn_ln")(fT, x1, g2, lng, lnb, scn, shn)


def kernel(x, c, ctx, c_ctx, w_ada, b_ada, w_in, b_in, s5_lam_re, s5_lam_im, s5_log_dt, s5_b_re, s5_b_im, s5_c_re, s5_c_im, s5_d, w_glu, b_glu, qn_gain, kn_gain, sink, sg_ln_g, sg_ln_b, w_sp, b_sp, w_br_a, w_br_b, w_br_c, w_br_d, w_out, ln1_g, ln1_b, ln2_g, ln2_b, w_pq, peer_keys, peer_u, peer_v):
    depth = w_in.shape[0]
    bsz, n_lat, d = x.shape
    assert bsz == 1 and d == D_MODEL and n_lat % 512 == 0 and ctx.shape[1] % BLOCK == 0
    n_ctx = ctx.shape[1]
    alpha = (2 * depth) ** 0.25
    x, ctx = x[0], ctx[0]

    row8 = lax.broadcasted_iota(jnp.int32, (8, d), 0)
    cond8 = jnp.where(row8 == 0, c[0][None], jnp.where(row8 == 1, c_ctx[None], 0.0))
    ada = [_ada(cond8, w_ada, b_ada, l) for l in range(depth)]
    cosf, sinf = _rope_tables(n_lat)
    cos1, sin0 = jnp.ones((n_ctx, HEAD_DIM), F32), jnp.zeros((n_ctx, HEAD_DIM), F32)
    vec = lambda a: a.reshape(1, -1).astype(F32)

    ref_off = {'a': 0, 'qg': 768, 'kg': 1792, 'vg': 2048, 'qw': 2304, 'kw': 3328, 'vw': 3584, 'du': 3840, 'dv': 4608}
    ref_w = {'a': 768, 'qg': 1024, 'kg': 256, 'vg': 256, 'qw': 1024, 'kw': 256, 'vw': 256, 'du': 768, 'dv': 768}
    order = ['a', 'kg', 'qg', 'qw', 'kw', 'vg', 'vw', 'du', 'dv']
    perm = lambda a: jnp.concatenate([a[..., ref_off[n]:ref_off[n] + ref_w[n]] for n in order], -1)
    perm_idx = np.concatenate([np.arange(ref_off[n], ref_off[n] + ref_w[n]) for n in order]).astype(np.int32)

    u_all, vT_all = peer_u.astype(BF16), jnp.swapaxes(peer_v, 1, 2).astype(BF16)

    h_lat = None
    for l in range(depth):
        need_ctx = l < depth - 1
        mods = [[ada[l][r:r + 1, i * d:(i + 1) * d] for i in range(6)] for r in range(2)]
        (sh1, sc1, g1, sh2, sc2, g2), (sh1c, sc1c, g1c, sh2c, sc2c, g2c) = mods
        if need_ctx:
            nxt = [ada[l + 1][r:r + 1, 0:2 * d] for r in range(2)]
            (shn, scn), (shnc, scnc) = [(m[:, :d], m[:, d:]) for m in nxt]
        else:
            shn = scn = shnc = scnc = jnp.zeros((1, d), F32)

        w_small, b_small = perm(w_in[l][:, :SMALL_W]).astype(BF16), vec(jnp.take(b_in[l], perm_idx))
        w_gate, b_gate = w_in[l][:, SMALL_W:].astype(BF16), vec(b_in[l][SMALL_W:])
        s5m = [_s5_mats(s5_lam_re[l, dr], s5_lam_im[l, dr], s5_log_dt[l, dr], s5_b_re[l, dr], s5_b_im[l, dr],
                        s5_c_re[l, dr], s5_c_im[l, dr]) for dr in range(2)]
        wg, bg, s5d = w_glu[l].astype(BF16), vec(b_glu[l]), vec(s5_d[l])
        qn, kn = vec(qn_gain[l]), vec(kn_gain[l])
        wsp, bsp = w_sp[l].astype(BF16), b_sp[l].astype(F32)[:, :, None]
        w_brs = [w[l].astype(BF16) for w in (w_br_a, w_br_b, w_br_c, w_br_d)]
        wo = w_out[l].astype(BF16)
        wpqT = w_pq[l].T.astype(BF16)
        keys = peer_keys[l].astype(BF16)

        if h_lat is None:
            h_lat, h_ctx = _modulate(x, sc1, sh1), _modulate(ctx, sc1c, sh1c)

        proj = _matmul(h_lat, w_small, b_small, tm=1024, tn=1792, out_dtype=BF16, name="in_proj")
        gates = _matmul(h_lat, w_gate, b_gate, tm=1024, tn=2048, out_dtype=BF16, name="in_proj_gates")
        proj_c = _matmul(h_ctx, w_small, b_small, tm=512, tn=1792, out_dtype=BF16, name="in_proj_ctx")

        y_dir, y_dir_c = [], []
        for dr in range(2):
            yc_, hc_end = _s5_dir(proj_c, s5m[dr], jnp.zeros((2, S5_CH), F32), bool(dr))
            yl_, _ = _s5_dir(proj, s5m[dr], hc_end, bool(dr))
            y_dir.append(yl_)
            y_dir_c.append(yc_)
        y_a = _s5_out(proj, y_dir[0], y_dir[1], s5d, wg, bg)

        qg, kg, vge, qw, kw = _prep(proj, cosf, sinf, qn, kn, ATTN_SCALE * math.log2(math.e))
        qgc, kgc, vgce, qwc, kwc = _prep(proj_c, cos1, sin0, qn, kn, ATTN_SCALE)
        y_b = _flash(qg, kg, vge, kgc, vgce)
        y_c = _window(qw, kw, proj, kwc, proj_c, sink[l])

        y_d = _spatial_gate(proj, vec(sg_ln_g[l]), vec(sg_ln_b[l]), wsp, bsp)

        merged = _merge((y_a, y_b, y_c, y_d), w_brs, gates)
        x1, xmT = _out_ln(merged, wo, x, g1, vec(ln1_g[l]), vec(ln1_b[l]), sc2, sh2, alpha)
        qT = _matmul(wpqT, xmT, None, tm=D_MODEL, tn=1024, out_dtype=F32, name="peer_query")
        ffnT = _peer_dense(xmT, u_all, vT_all, l, _peer_topk(qT, keys))
        res = _ffn_ln(ffnT, x1, g2, vec(ln2_g[l]), vec(ln2_b[l]), scn, shn, alpha, need_ctx)
        x = res[0]

        if need_ctx:
            h_lat = res[1]
            gates_c = _matmul(h_ctx, w_gate, b_gate, tm=512, tn=2048, out_dtype=BF16, name="in_proj_gates_ctx")
            y_a_c = _s5_out(proj_c, y_dir_c[0], y_dir_c[1], s5d, wg, bg)
            y_b_c = _ctx_attn(qgc, kgc, proj_c, COL_VG, sink[l], False)
            y_c_c = _ctx_attn(qwc, kwc, proj_c, COL_VW, sink[l], True)
            y_d_c = _spatial_gate(proj_c, vec(sg_ln_g[l]), vec(sg_ln_b[l]), wsp, bsp)
            merged_c = _merge((y_a_c, y_b_c, y_c_c, y_d_c), w_brs, gates_c)
            c1, cmT = _out_ln(merged_c, wo, ctx, g1c, vec(ln1_g[l]), vec(ln1_b[l]), sc2c, sh2c, alpha)
            qTc = _matmul(wpqT, cmT, None, tm=D_MODEL, tn=1024, out_dtype=F32, name="peer_query_ctx")
            ffnTc = _peer_dense(cmT, u_all, vT_all, l, _peer_topk(qTc, keys))
            ctx, h_ctx = _ffn_ln(ffnTc, c1, g2c, vec(ln2_g[l]), vec(ln2_b[l]), scnc, shnc, alpha, True)
    return x[None]
```
